```python
import jax, jax.numpy as jnp
from jax import lax
import numpy as np

D_MODEL = 1024
BATCH = 2
SEQ = 8192
DEPTH = 1

HEAD_DIM = 64
ROT_FRACTION = 4
ROPE_THETA = 500000.0
DSA_HEADS = D_MODEL // 2 // HEAD_DIM
DSA_KV_RANK = 128
IDX_HEADS = 8
IDX_DIM = 32
DSA_TOPK_MAX = 256
NSA_HEADS = D_MODEL // 2 // HEAD_DIM
NSA_KV_GROUPS = 2
CMP_LEN = 32
CMP_STRIDE = 16
CMP_HIDDEN = 256
SEL_BLOCK = 64
SEL_COUNT = 16
WINDOW = 512
N_GROUPS = 4
EXPERTS_PER_GROUP = 8
N_EXPERTS = N_GROUPS * EXPERTS_PER_GROUP
EXPERT_TOPK = 2
D_EXPERT = 256
Q_BLOCK = 128
NORM_EPS = 1e-6
NEG_INF = -1e30
TINY = 1e-30
IN_PROJ_COLS = (DSA_HEADS * HEAD_DIM + DSA_KV_RANK + IDX_HEADS * IDX_DIM + IDX_HEADS + IDX_DIM
                + NSA_HEADS * HEAD_DIM + 6 * NSA_KV_GROUPS * HEAD_DIM + 3 * NSA_HEADS + 2 * D_MODEL)

kernel_name = 'hybrid_dsa_nsa_hmoe_block'


def _split_points():
    kvb = NSA_KV_GROUPS * HEAD_DIM
    sizes = [DSA_HEADS * HEAD_DIM, DSA_KV_RANK, IDX_HEADS * IDX_DIM, IDX_HEADS, IDX_DIM,
             NSA_HEADS * HEAD_DIM, kvb, kvb, kvb, kvb, kvb, kvb, 3 * NSA_HEADS, 2 * D_MODEL]
    return [int(v) for v in np.cumsum(sizes)[:-1]]


def rms_norm(x, g):
    xf = x.astype(jnp.float32)
    y = xf * lax.rsqrt(jnp.mean(xf * xf, axis=-1, keepdims=True) + NORM_EPS)
    return (y * g.astype(jnp.float32)).astype(x.dtype)


def partial_rotary(x, positions):
    rot = x.shape[-1] // ROT_FRACTION
    half = rot // 2
    inv_freq = ROPE_THETA ** (-jnp.arange(half, dtype=jnp.float32) / half)
    ang = positions.astype(jnp.float32)[..., None] * inv_freq
    ang = ang.reshape(ang.shape[:2] + (1,) * (x.ndim - 3) + (half,))
    cos = jnp.cos(ang).astype(x.dtype)
    sin = jnp.sin(ang).astype(x.dtype)
    x1, x2, rest = x[..., :half], x[..., half:rot], x[..., rot:]
    return jnp.concatenate([x1 * cos - x2 * sin, x2 * cos + x1 * sin, rest], axis=-1)


def masked_softmax(s, mask):
    s = jnp.where(mask, s.astype(jnp.float32), NEG_INF)
    p = jnp.exp(s - jnp.max(s, axis=-1, keepdims=True))
    p = jnp.where(mask, p, 0.0)
    return p / jnp.maximum(jnp.sum(p, axis=-1, keepdims=True), TINY)


def compress_blocks(k, pe, w1, w2):
    B, S, G, HD = k.shape
    r = CMP_LEN // CMP_STRIDE
    n_chunks = S // CMP_STRIDE
    chunks = k.reshape(B, n_chunks, CMP_STRIDE, G, HD)
    blocks = jnp.concatenate([chunks[:, i:n_chunks - r + 1 + i] for i in range(r)], axis=2)
    blocks = blocks + pe[:, None, :]
    flat = blocks.transpose(0, 1, 3, 2, 4).reshape(B, n_chunks - r + 1, G, CMP_LEN * HD)
    return jax.nn.silu(flat @ w1) @ w2


def token_mixer(h, positions, w_in, g_kv_latent, w_kv_up, pe_cmp_k, pe_cmp_v,
                w_cmp1_k, w_cmp2_k, w_cmp1_v, w_cmp2_v, w_up_a, w_up_b, w_out):
    B, S, _ = h.shape
    G, J, HD = NSA_KV_GROUPS, NSA_HEADS // NSA_KV_GROUPS, HEAD_DIM
    scale = HEAD_DIM ** -0.5
    topk_a = min(DSA_TOPK_MAX, S // 4)
    n_sel = min(SEL_COUNT, S // SEL_BLOCK)

    (q_a, kv_lat, iq, iw, ik, q_b, kc, vc, ks, vs, kw, vw, br_gate, mg) = jnp.split(
        h @ w_in, _split_points(), axis=-1)

    q_a = partial_rotary(q_a.reshape(B, S, DSA_HEADS, HD), positions)
    kv = rms_norm(kv_lat, g_kv_latent) @ w_kv_up
    k_a = partial_rotary(kv[..., :DSA_HEADS * HD].reshape(B, S, DSA_HEADS, HD), positions)
    v_a = kv[..., DSA_HEADS * HD:].reshape(B, S, DSA_HEADS, HD)
    iq = partial_rotary(iq.reshape(B, S, IDX_HEADS, IDX_DIM), positions)
    ik = partial_rotary(ik.reshape(B, S, 1, IDX_DIM), positions)[:, :, 0]

    q_b = partial_rotary(q_b.reshape(B, S, G, J, HD), positions)
    to_g = lambda t: t.reshape(B, S, G, HD)
    kc = compress_blocks(partial_rotary(to_g(kc), positions), pe_cmp_k, w_cmp1_k, w_cmp2_k)
    vc = compress_blocks(to_g(vc), pe_cmp_v, w_cmp1_v, w_cmp2_v)
    n_cmp = kc.shape[1]
    n_blk = S // SEL_BLOCK
    to_blocks = lambda t: t.reshape(B, n_blk, SEL_BLOCK, G, HD).transpose(0, 3, 1, 2, 4)
    ks_blocks = to_blocks(partial_rotary(to_g(ks), positions))
    vs_blocks = to_blocks(to_g(vs))
    pad = ((0, 0), (WINDOW, 0), (0, 0), (0, 0))
    kw_pad = jnp.pad(partial_rotary(to_g(kw), positions), pad)
    vw_pad = jnp.pad(to_g(vw), pad)
    br_gate = jax.nn.sigmoid(br_gate).reshape(B, S, G, J, 3)
    g_a, g_b = jnp.split(jax.nn.sigmoid(mg), 2, axis=-1)

    cmp_end = jnp.arange(n_cmp) * CMP_STRIDE + CMP_LEN - 1
    cmp_start = jnp.arange(n_cmp)[:, None] * CMP_STRIDE
    blk_start = jnp.arange(n_blk)[None, :] * SEL_BLOCK
    overlap = ((cmp_start <= blk_start + SEL_BLOCK - 1)
               & (cmp_start + CMP_LEN - 1 >= blk_start)).astype(jnp.float32)
    key_idx = jnp.arange(S)
    blk_ids = jnp.arange(n_blk)
    gather_rows = jax.vmap(lambda arr, idx: arr[idx])
    gather_blocks = jax.vmap(jax.vmap(lambda arr, idx: arr[idx]))

    def attend_block(q0):
        t = q0 + jnp.arange(Q_BLOCK)
        sl = lambda a: lax.dynamic_slice_in_dim(a, q0, Q_BLOCK, axis=1)
        logits = jnp.einsum('bqhd,bsd->bqhs', sl(iq), ik)
        idx_score = jnp.einsum('bqhs,bqh->bqs', jax.nn.relu(logits), sl(iw)).astype(jnp.float32)
        idx_score = jnp.where(key_idx[None, None, :] <= t[None, :, None], idx_score, NEG_INF)
        _, sel = lax.top_k(idx_score, topk_a)
        kg = gather_rows(k_a, sel)
        vg = gather_rows(v_a, sel)
        s_a = jnp.einsum('bqhd,bqkhd->bhqk', sl(q_a), kg) * scale
        p_a = masked_softmax(s_a, (sel <= t[None, :, None])[:, None])
        o_a = jnp.einsum('bhqk,bqkhd->bqhd', p_a.astype(vg.dtype), vg)

        qb = sl(q_b)
        s_c = jnp.einsum('bqgjd,bcgd->bgjqc', qb, kc) * scale
        p_c = masked_softmax(s_c, cmp_end[None, :] <= t[:, None])
        o_c = jnp.einsum('bgjqc,bcgd->bqgjd', p_c.astype(vc.dtype), vc)
        imp = jnp.einsum('bgjqc,cn->bgqn', p_c, overlap)
        cur = (t // SEL_BLOCK)[:, None]
        admissible = blk_ids[None, :] * SEL_BLOCK <= t[:, None]
        forced = (blk_ids[None, :] == 0) | (blk_ids[None, :] == cur) | (blk_ids[None, :] == cur - 1)
        sel_score = jnp.where(admissible & forced, -NEG_INF, jnp.where(admissible, imp, NEG_INF))
        _, bsel = lax.top_k(sel_score, n_sel)
        flat = bsel.reshape(B, G, Q_BLOCK * n_sel)
        ksg = gather_blocks(ks_blocks, flat).reshape(B, G, Q_BLOCK, n_sel * SEL_BLOCK, HD)
        vsg = gather_blocks(vs_blocks, flat).reshape(B, G, Q_BLOCK, n_sel * SEL_BLOCK, HD)
        tok = bsel[..., None] * SEL_BLOCK + jnp.arange(SEL_BLOCK)
        smask = (tok <= t[None, None, :, None, None]).reshape(B, G, 1, Q_BLOCK, n_sel * SEL_BLOCK)
        s_s = jnp.einsum('bqgjd,bgqmd->bgjqm', qb, ksg) * scale
        p_s = masked_softmax(s_s, smask)
        o_s = jnp.einsum('bgjqm,bgqmd->bqgjd', p_s.astype(vsg.dtype), vsg)
        kwb = lax.dynamic_slice_in_dim(kw_pad, q0, WINDOW + Q_BLOCK, axis=1)
        vwb = lax.dynamic_slice_in_dim(vw_pad, q0, WINDOW + Q_BLOCK, axis=1)
        spos = q0 - WINDOW + jnp.arange(WINDOW + Q_BLOCK)
        wmask = ((spos[None, :] <= t[:, None]) & (spos[None, :] > t[:, None] - WINDOW)
                 & (spos[None, :] >= 0))
        s_w = jnp.einsum('bqgjd,bkgd->bgjqk', qb, kwb) * scale
        p_w = masked_softmax(s_w, wmask)
        o_w = jnp.einsum('bgjqk,bkgd->bqgjd', p_w.astype(vwb.dtype), vwb)
        gb = sl(br_gate)
        o_b = gb[..., 0:1] * o_c + gb[..., 1:2] * o_s + gb[..., 2:3] * o_w
        return o_a, o_b

    o_a, o_b = lax.map(attend_block, jnp.arange(S // Q_BLOCK, dtype=jnp.int32) * Q_BLOCK)
    o_a = jnp.moveaxis(o_a, 0, 1).reshape(B, S, DSA_HEADS * HD)
    o_b = jnp.moveaxis(o_b, 0, 1).reshape(B, S, NSA_HEADS * HD)
    merged = g_a * (o_a @ w_up_a) + g_b * (o_b @ w_up_b)
    return merged @ w_out


def hier_moe(h, w_router_group, b_router_group, w_router_expert, b_router_expert,
             w_expert_gate, w_expert_up, w_expert_down):
    B, S, D = h.shape
    hf = h.reshape(B * S, D)
    lg = (hf @ w_router_group + b_router_group).astype(jnp.float32)
    pg = jax.nn.softmax(lg, axis=-1)
    gsel = jnp.argmax(lg, axis=-1)
    le = (hf @ w_router_expert + b_router_expert).astype(jnp.float32)
    le = le.reshape(B * S, N_GROUPS, EXPERTS_PER_GROUP)
    le_g = jnp.take_along_axis(le, gsel[:, None, None], axis=1)[:, 0]
    top_p, top_i = lax.top_k(jax.nn.softmax(le_g, axis=-1), EXPERT_TOPK)
    pg_sel = jnp.take_along_axis(pg, gsel[:, None], axis=1)
    wts = top_p / jnp.sum(top_p, axis=-1, keepdims=True) * pg_sel
    eidx = gsel[:, None] * EXPERTS_PER_GROUP + top_i
    comb = jnp.sum(jax.nn.one_hot(eidx, N_EXPERTS, dtype=jnp.float32) * wts[..., None], axis=1)
    comb = comb.astype(hf.dtype)
    out = jnp.zeros_like(hf)
    for e in range(N_EXPERTS):
        ye = (jax.nn.silu(hf @ w_expert_gate[e]) * (hf @ w_expert_up[e])) @ w_expert_down[e]
        out = out + comb[:, e:e + 1] * ye
    return out.reshape(B, S, D)


def setup_inputs(seed: int = 0) -> dict:
    key = jax.random.key(seed)
    k = jax.random.split(key, 28)
    f32 = jnp.float32
    L = DEPTH
    HD = HEAD_DIM

    def dense(kk, shape, fan_in, gain=1.0):
        return jax.random.normal(kk, shape, f32) * (gain * fan_in ** -0.5)

    def norm_gain(kk, shape):
        return 1.0 + 0.02 * jax.random.normal(kk, shape, f32)

    def small(kk, shape, s):
        return s * jax.random.normal(kk, shape, f32)

    return {
        'x': jax.random.normal(k[0], (BATCH, SEQ, D_MODEL), f32),
        'c': jax.random.normal(k[1], (BATCH, D_MODEL), f32),
        'positions': jnp.broadcast_to(jnp.arange(SEQ, dtype=jnp.int32)[None, :], (BATCH, SEQ)),
        'w_ada': dense(k[2], (L, D_MODEL, 6 * D_MODEL), D_MODEL, 0.5),
        'b_ada': small(k[3], (L, 6 * D_MODEL), 0.02),
        'g_norm1': norm_gain(k[4], (L, D_MODEL)),
        'w_in': dense(k[5], (L, D_MODEL, IN_PROJ_COLS), D_MODEL),
        'g_kv_latent': norm_gain(k[6], (L, DSA_KV_RANK)),
        'w_kv_up': dense(k[7], (L, DSA_KV_RANK, 2 * DSA_HEADS * HD), DSA_KV_RANK),
        'pe_cmp_k': small(k[8], (L, CMP_LEN, HD), 0.02),
        'pe_cmp_v': small(k[9], (L, CMP_LEN, HD), 0.02),
        'w_cmp1_k': dense(k[10], (L, CMP_LEN * HD, CMP_HIDDEN), CMP_LEN * HD),
        'w_cmp2_k': dense(k[11], (L, CMP_HIDDEN, HD), CMP_HIDDEN),
        'w_cmp1_v': dense(k[12], (L, CMP_LEN * HD, CMP_HIDDEN), CMP_LEN * HD),
        'w_cmp2_v': dense(k[13], (L, CMP_HIDDEN, HD), CMP_HIDDEN),
        'w_up_a': dense(k[14], (L, DSA_HEADS * HD, D_MODEL), DSA_HEADS * HD),
        'w_up_b': dense(k[15], (L, NSA_HEADS * HD, D_MODEL), NSA_HEADS * HD),
        'w_out': dense(k[16], (L, D_MODEL, D_MODEL), D_MODEL),
        'g_norm2': norm_gain(k[17], (L, D_MODEL)),
        'w_router_group': dense(k[18], (L, D_MODEL, N_GROUPS), D_MODEL),
        'b_router_group': small(k[19], (L, N_GROUPS), 0.01),
        'w_router_expert': dense(k[20], (L, D_MODEL, N_EXPERTS), D_MODEL),
        'b_router_expert': small(k[21], (L, N_EXPERTS), 0.01),
        'w_expert_gate': dense(k[22], (L, N_EXPERTS, D_MODEL, D_EXPERT), D_MODEL),
        'w_expert_up': dense(k[23], (L, N_EXPERTS, D_MODEL, D_EXPERT), D_MODEL),
        'w_expert_down': dense(k[24], (L, N_EXPERTS, D_EXPERT, D_MODEL), D_EXPERT),
        'g_final': norm_gain(k[25], (D_MODEL,)),
    }


def reference(x, c, positions, w_ada, b_ada, g_norm1, w_in, g_kv_latent, w_kv_up,
              pe_cmp_k, pe_cmp_v, w_cmp1_k, w_cmp2_k, w_cmp1_v, w_cmp2_v,
              w_up_a, w_up_b, w_out, g_norm2, w_router_group, b_router_group,
              w_router_expert, b_router_expert, w_expert_gate, w_expert_up,
              w_expert_down, g_final):
    cond = jax.nn.silu(c)
    for l in range(DEPTH):
        mod = cond @ w_ada[l] + b_ada[l]
        sh1, sc1, gt1, sh2, sc2, gt2 = jnp.split(mod[:, None, :], 6, axis=-1)
        h = rms_norm(x, g_norm1[l]) * (1.0 + sc1) + sh1
        x = x + gt1 * token_mixer(h, positions, w_in[l], g_kv_latent[l], w_kv_up[l],
                                  pe_cmp_k[l], pe_cmp_v[l], w_cmp1_k[l], w_cmp2_k[l],
                                  w_cmp1_v[l], w_cmp2_v[l], w_up_a[l], w_up_b[l], w_out[l])
        h = rms_norm(x, g_norm2[l]) * (1.0 + sc2) + sh2
        x = x + gt2 * hier_moe(h, w_router_group[l], b_router_group[l], w_router_expert[l],
                               b_router_expert[l], w_expert_gate[l], w_expert_up[l],
                               w_expert_down[l])
    return rms_norm(x, g_final)
```

```python
import functools

import numpy as np
import jax
import jax.numpy as jnp
from jax import lax
from jax.experimental import pallas as pl
from jax.experimental.pallas import tpu as pltpu

HEAD_DIM = 64
ROT_FRACTION = 4
ROPE_THETA = 500000.0
DSA_HEADS = 8
DSA_KV_RANK = 128
IDX_HEADS = 8
IDX_DIM = 32
DSA_TOPK_MAX = 256
NSA_HEADS = 8
NSA_KV_GROUPS = 2
NSA_REP = NSA_HEADS // NSA_KV_GROUPS
CMP_LEN = 32
CMP_STRIDE = 16
CMP_HIDDEN = 256
SEL_BLOCK = 64
SEL_COUNT = 16
WINDOW = 512
N_GROUPS = 4
EXPERTS_PER_GROUP = 8
N_EXPERTS = N_GROUPS * EXPERTS_PER_GROUP
D_EXPERT = 256
NORM_EPS = 1e-6
NEG_INF = -1e30
TINY = 1e-30
LOWEST = -3.0e38
FORCED = 1e30

LANES = 128
MXU_DTYPE = jnp.bfloat16
VMEM_LIMIT = 56 * 1024 * 1024

F32 = jnp.float32


def _cparams(sem):
    return pltpu.CompilerParams(dimension_semantics=sem, vmem_limit_bytes=VMEM_LIMIT)


def _dot(a, b):
    return jnp.dot(a.astype(MXU_DTYPE), b.astype(MXU_DTYPE), preferred_element_type=F32)


def _dot_nt(a, b):
    return lax.dot_general(a.astype(MXU_DTYPE), b.astype(MXU_DTYPE),
                           (((1,), (1,)), ((), ())), preferred_element_type=F32)


def _dot_split(a, b01):
    hi = a.astype(MXU_DTYPE)
    r1 = a - hi.astype(F32)
    mid = r1.astype(MXU_DTYPE)
    lo = (r1 - mid.astype(F32)).astype(MXU_DTYPE)
    b = b01.astype(MXU_DTYPE)
    d = lambda u: jnp.dot(u, b, preferred_element_type=F32)
    return d(hi) + d(mid) + d(lo)


def _rms(x, g):
    return x * lax.rsqrt(jnp.mean(x * x, axis=-1, keepdims=True) + NORM_EPS) * g


def _ada_kernel(c_ref, w_ref, b_ref, o_ref):
    c = c_ref[...]
    cond = c * jax.nn.sigmoid(c)
    o_ref[...] = _dot(cond, w_ref[...]) + b_ref[...]


def _ada_mod(c, w_ada, b_ada):
    B, D = c.shape
    n_out = w_ada.shape[1]
    rows = 8
    cp = jnp.zeros((rows, D), F32).at[:B].set(c)
    tn = 1024
    out = pl.pallas_call(
        _ada_kernel,
        grid=(n_out // tn,),
        in_specs=[pl.BlockSpec((rows, D), lambda j: (0, 0)),
                  pl.BlockSpec((D, tn), lambda j: (0, j)),
                  pl.BlockSpec((1, tn), lambda j: (0, j))],
        out_specs=pl.BlockSpec((rows, tn), lambda j: (0, j)),
        out_shape=jax.ShapeDtypeStruct((rows, n_out), F32),
        compiler_params=_cparams(("arbitrary",)),
    )(cp, w_ada, b_ada.reshape(1, n_out))
    return out[:B]


def _inproj_kernel(x_ref, g_ref, sc_ref, sh_ref, w_ref, o_ref):
    h = _rms(x_ref[...], g_ref[...]) * (1.0 + sc_ref[0]) + sh_ref[0]
    o_ref[...] = _dot(h, w_ref[...])


def _in_proj(x2d, g, sc, sh, w, S, tm=256):
    T, D = x2d.shape
    N = w.shape[1]
    per_b = S // tm
    return pl.pallas_call(
        _inproj_kernel,
        grid=(T // tm,),
        in_specs=[pl.BlockSpec((tm, D), lambda i: (i, 0)),
                  pl.BlockSpec((1, D), lambda i: (0, 0)),
                  pl.BlockSpec((1, 1, D), lambda i: (i // per_b, 0, 0)),
                  pl.BlockSpec((1, 1, D), lambda i: (i // per_b, 0, 0)),
                  pl.BlockSpec((D, N), lambda i: (0, 0))],
        out_specs=pl.BlockSpec((tm, N), lambda i: (i, 0)),
        out_shape=jax.ShapeDtypeStruct((T, N), F32),
        compiler_params=_cparams(("arbitrary",)),
    )(x2d, g.reshape(1, D), sc, sh, w)


def _kvup_kernel(l_ref, g_ref, w_ref, o_ref):
    o_ref[...] = _dot(_rms(l_ref[...], g_ref[...]), w_ref[...])


def _kv_up(proj, col_block, g, w, tm=512):
    T = proj.shape[0]
    R, N = w.shape
    return pl.pallas_call(
        _kvup_kernel,
        grid=(T // tm,),
        in_specs=[pl.BlockSpec((tm, R), lambda i: (i, col_block)),
                  pl.BlockSpec((1, R), lambda i: (0, 0)),
                  pl.BlockSpec((R, N), lambda i: (0, 0))],
        out_specs=pl.BlockSpec((tm, N), lambda i: (i, 0)),
        out_shape=jax.ShapeDtypeStruct((T, N), F32),
        compiler_params=_cparams(("arbitrary",)),
    )(proj, g.reshape(1, R), w)


def _cmp_kernel(f_ref, w1_ref, w2_ref, o_ref):
    hid = _dot(f_ref[...], w1_ref[...])
    hid = hid * jax.nn.sigmoid(hid)
    o_ref[...] = _dot(hid, w2_ref[...])


def _compress(tok, pe, w1, w2):
    B, S, G, HD = tok.shape
    r = CMP_LEN // CMP_STRIDE
    n_chunks = S // CMP_STRIDE
    n_cmp = n_chunks - r + 1
    chunks = tok.reshape(B, n_chunks, CMP_STRIDE, G, HD)
    blocks = jnp.concatenate([chunks[:, i:n_cmp + i] for i in range(r)], axis=2)
    blocks = blocks + pe[:, None, :]
    flat = blocks.transpose(0, 1, 3, 2, 4).reshape(B * n_cmp * G, CMP_LEN * HD)
    rows = flat.shape[0]
    tm = 512
    rows_p = -(-rows // tm) * tm
    flat = jnp.pad(flat, ((0, rows_p - rows), (0, 0))).astype(MXU_DTYPE)
    out = pl.pallas_call(
        _cmp_kernel,
        grid=(rows_p // tm,),
        in_specs=[pl.BlockSpec((tm, CMP_LEN * HD), lambda i: (i, 0)),
                  pl.BlockSpec((CMP_LEN * HD, CMP_HIDDEN), lambda i: (0, 0)),
                  pl.BlockSpec((CMP_HIDDEN, HD), lambda i: (0, 0))],
        out_specs=pl.BlockSpec((tm, HD), lambda i: (i, 0)),
        out_shape=jax.ShapeDtypeStruct((rows_p, HD), F32),
        compiler_params=_cparams(("arbitrary",)),
    )(flat, w1.astype(MXU_DTYPE), w2.astype(MXU_DTYPE))
    return out[:rows].reshape(B, n_cmp, G, HD)


def _select_kernel(iq_ref, iw_ref, ikt_ref, mask_ref, sc_ref, *, TQ, KC, S, K, NBIS):
    qi = pl.program_id(1)
    q0 = qi * TQ
    n_ch = (q0 + TQ + KC - 1) // KC
    n_all = S // KC
    ncg = KC // LANES
    t_col = q0 + lax.broadcasted_iota(jnp.int32, (TQ, 1), 0)
    lane_iota = lax.broadcasted_iota(jnp.int32, (TQ, LANES), 1)
    kf = float(K)
    SUB = 256 if KC % 256 == 0 else LANES

    def score_chunk(c, carry):
        for u in range(KC // SUB):
            off = pl.multiple_of(c * KC + u * SUB, SUB)
            ikt = ikt_ref[0, :, pl.ds(off, SUB)]
            acc = jnp.zeros((TQ, SUB), F32)
            for h in range(IDX_HEADS):
                lg = jnp.dot(iq_ref[0, h], ikt, preferred_element_type=F32)
                acc = acc + jnp.maximum(lg, 0.0) * iw_ref[0, h]
            kidx = off + lax.broadcasted_iota(jnp.int32, (TQ, SUB), 1)
            sc_ref[:, pl.ds(off, SUB)] = jnp.where(kidx <= t_col, acc, NEG_INF)
        return carry

    lax.fori_loop(0, n_ch, score_chunk, 0)

    def col(c, j):
        off = pl.multiple_of(c * KC + j * LANES, LANES)
        return sc_ref[:, pl.ds(off, LANES)], off

    def lane_pass(fn, init):
        def body(c, acc):
            for j in range(ncg):
                xs, off = col(c, j)
                acc = fn(acc, xs, off)
            return acc
        return lax.fori_loop(0, n_ch, body, init)

    def bcast(v):
        return jnp.broadcast_to(v, (TQ, LANES))

    def count_ge(thr):
        tb = bcast(thr)
        acc = lane_pass(lambda a, xs, off: a + jnp.where(xs >= tb, 1.0, 0.0),
                        jnp.zeros((TQ, LANES), F32))
        return jnp.sum(acc, axis=1, keepdims=True)

    def count_gt(thr):
        tb = bcast(thr)
        acc = lane_pass(lambda a, xs, off: a + jnp.where(xs > tb, 1.0, 0.0),
                        jnp.zeros((TQ, LANES), F32))
        return jnp.sum(acc, axis=1, keepdims=True)

    def max_where(bound, strict):
        bb = bcast(bound)
        def fn(a, xs, off):
            ok = (xs < bb) if strict else (xs <= bb)
            return jnp.maximum(a, jnp.where(ok, xs, LOWEST))
        acc = lane_pass(fn, jnp.full((TQ, LANES), LOWEST, F32))
        return jnp.max(acc, axis=1, keepdims=True)

    def minmax_fn(a, xs, off):
        mn, mx = a
        causal = (off + lane_iota) <= t_col
        return jnp.minimum(mn, jnp.where(causal, xs, -LOWEST)), jnp.maximum(mx, xs)

    mn, mx = lane_pass(minmax_fn, (jnp.full((TQ, LANES), -LOWEST, F32),
                                   jnp.full((TQ, LANES), LOWEST, F32)))
    lo = jnp.min(mn, axis=1, keepdims=True)
    hi = jnp.max(mx, axis=1, keepdims=True)

    def bisect(_, carry):
        lo, hi = carry
        mid = 0.5 * (lo + hi)
        ge = count_ge(mid) >= kf
        return jnp.where(ge, mid, lo), jnp.where(ge, hi, mid)

    lo, hi = lax.fori_loop(0, NBIS, bisect, (lo, hi))

    all_keys = jnp.where(t_col < K, 1.0, 0.0)
    v = max_where(hi, strict=False)
    done = jnp.maximum(all_keys, jnp.where(count_ge(v) >= kf, 1.0, 0.0))

    def peel_cond(st):
        return jnp.sum(st[1]) < float(TQ)

    def peel_body(st):
        v, done = st
        v = jnp.where(done > 0.5, v, max_where(v, strict=True))
        done = jnp.maximum(done, jnp.where(count_ge(v) >= kf, 1.0, 0.0))
        return v, done

    v, done = lax.while_loop(peel_cond, peel_body, (v, done))
    thr = jnp.where(all_keys > 0.5, LOWEST, v)
    n_ge = count_ge(thr)
    has_tie = jnp.max(jnp.where((n_ge > kf) & (all_keys < 0.5), 1.0, 0.0)) > 0.5

    def zero_tail():
        def body(c, carry):
            off = pl.multiple_of(c * KC, KC)
            mask_ref[0, :, pl.ds(off, KC)] = jnp.zeros((TQ, KC), mask_ref.dtype)
            return carry
        lax.fori_loop(n_ch, n_all, body, 0)

    @pl.when(jnp.logical_not(has_tie))
    def _():
        tb = bcast(thr)
        def body(c, carry):
            for j in range(ncg):
                xs, off = col(c, j)
                sel = (xs >= tb) & ((off + lane_iota) <= t_col)
                mask_ref[0, :, pl.ds(off, LANES)] = jnp.where(sel, 1.0, 0.0).astype(mask_ref.dtype)
            return carry
        lax.fori_loop(0, n_ch, body, 0)
        zero_tail()

    @pl.when(has_tie)
    def _():
        need = kf - count_gt(thr)
        r_i = lax.broadcasted_iota(jnp.int32, (KC, KC), 0)
        c_i = lax.broadcasted_iota(jnp.int32, (KC, KC), 1)
        upper = jnp.where(r_i <= c_i, 1.0, 0.0).astype(MXU_DTYPE)
        kc_iota = lax.broadcasted_iota(jnp.int32, (TQ, KC), 1)
        def body(c, seen):
            off = pl.multiple_of(c * KC, KC)
            xs = sc_ref[:, pl.ds(off, KC)]
            causal = (off + kc_iota) <= t_col
            eq = jnp.where((xs == thr) & causal, 1.0, 0.0)
            rank = jnp.dot(eq.astype(MXU_DTYPE), upper, preferred_element_type=F32) + seen
            sel = ((xs > thr) & causal) | ((eq > 0.5) & (rank <= need))
            mask_ref[0, :, pl.ds(off, KC)] = jnp.where(sel, 1.0, 0.0).astype(mask_ref.dtype)
            return seen + jnp.sum(eq, axis=1, keepdims=True)
        lax.fori_loop(0, n_ch, body, jnp.zeros((TQ, 1), F32))
        zero_tail()


def _dsa_select(iq, iw, ikt, K):
    B, H, S, DI = iq.shape
    TQ = 128
    KC = min(512, S)
    kern = functools.partial(_select_kernel, TQ=TQ, KC=KC, S=S, K=K, NBIS=20)
    return pl.pallas_call(
        kern,
        grid=(B, S // TQ),
        in_specs=[pl.BlockSpec((1, H, TQ, DI), lambda b, q: (b, 0, q, 0)),
                  pl.BlockSpec((1, H, TQ, 1), lambda b, q: (b, 0, q, 0)),
                  pl.BlockSpec((1, DI, S), lambda b, q: (b, 0, 0))],
        out_specs=pl.BlockSpec((1, TQ, S), lambda b, q: (b, q, 0)),
        out_shape=jax.ShapeDtypeStruct((B, S, S), jnp.bfloat16),
        scratch_shapes=[pltpu.VMEM((TQ, S), F32)],
        compiler_params=_cparams(("arbitrary", "arbitrary")),
    )(iq, iw, ikt)


def _cmp_attn_kernel(q_ref, kc_ref, vc_ref, o_ref, bm_ref, *, TQ, NCP, NBP, N_SEL):
    qi = pl.program_id(1)
    q0 = qi * TQ
    G, J = NSA_KV_GROUPS, NSA_REP
    row = lax.broadcasted_iota(jnp.int32, (J * TQ, NCP), 0)
    t_rows = q0 + (row & (TQ - 1))
    cmp_end = lax.broadcasted_iota(jnp.int32, (J * TQ, NCP), 1) * CMP_STRIDE + (CMP_LEN - 1)
    vis = cmp_end <= t_rows
    c_i = lax.broadcasted_iota(jnp.int32, (NCP, NBP), 0) * CMP_STRIDE
    n_i = lax.broadcasted_iota(jnp.int32, (NCP, NBP), 1) * SEL_BLOCK
    overlap = jnp.where((c_i <= n_i + SEL_BLOCK - 1) & (c_i + CMP_LEN - 1 >= n_i), 1.0, 0.0)
    blk = lax.broadcasted_iota(jnp.int32, (TQ, NBP), 1)
    tq = q0 + lax.broadcasted_iota(jnp.int32, (TQ, NBP), 0)
    cur = tq // SEL_BLOCK
    admissible = blk * SEL_BLOCK <= tq
    forced = (blk == 0) | (blk == cur) | (blk == cur - 1)
    for g in range(G):
        q = q_ref[0, g].reshape(J * TQ, HEAD_DIM)
        s = jnp.where(vis, _dot_nt(q, kc_ref[0, g]), NEG_INF)
        p = jnp.exp(s - jnp.max(s, axis=-1, keepdims=True))
        p = jnp.where(vis, p, 0.0)
        p = p / jnp.maximum(jnp.sum(p, axis=-1, keepdims=True), TINY)
        o = _dot(p, vc_ref[0, g])
        o_ref[0, g] = o.reshape(J, TQ, HEAD_DIM)
        psum = p[0:TQ]
        for j in range(1, J):
            psum = psum + p[j * TQ:(j + 1) * TQ]
        imp = _dot_split(psum, overlap)
        score = jnp.where(admissible & forced, FORCED, jnp.where(admissible, imp, NEG_INF))

        def pick(_, st):
            score, sel = st
            m = jnp.max(score, axis=-1, keepdims=True)
            first = jnp.min(jnp.where(score == m, blk, NBP), axis=-1, keepdims=True)
            hit = blk == first
            return jnp.where(hit, LOWEST, score), jnp.where(hit, 1.0, sel)

        _, sel = lax.fori_loop(0, N_SEL, pick, (score, jnp.zeros((TQ, NBP), F32)))
        bm_ref[0, g] = sel.astype(bm_ref.dtype)


def _cmp_attn(qb, kc, vc, n_sel, NBP):
    B, G, J, S, HD = qb.shape
    NCP = kc.shape[2]
    TQ = 256
    kern = functools.partial(_cmp_attn_kernel, TQ=TQ, NCP=NCP, NBP=NBP, N_SEL=n_sel)
    return pl.pallas_call(
        kern,
        grid=(B, S // TQ),
        in_specs=[pl.BlockSpec((1, G, J, TQ, HD), lambda b, q: (b, 0, 0, q, 0)),
                  pl.BlockSpec((1, G, NCP, HD), lambda b, q: (b, 0, 0, 0)),
                  pl.BlockSpec((1, G, NCP, HD), lambda b, q: (b, 0, 0, 0))],
        out_specs=[pl.BlockSpec((1, G, J, TQ, HD), lambda b, q: (b, 0, 0, q, 0)),
                   pl.BlockSpec((1, G, TQ, NBP), lambda b, q: (b, 0, q, 0))],
        out_shape=[jax.ShapeDtypeStruct((B, G, J, S, HD), F32),
                   jax.ShapeDtypeStruct((B, G, S, NBP), jnp.bfloat16)],
        compiler_params=_cparams(("arbitrary", "arbitrary")),
    )(qb, kc, vc)


def _flash_kernel(*refs, mode, G, J, TQ, TK, NWIN):
    if mode == "window":
        q_ref, k_ref, v_ref, o_ref, m_ref, l_ref, acc_ref = refs
        x_ref = None
    else:
        q_ref, k_ref, v_ref, x_ref, o_ref, m_ref, l_ref, acc_ref = refs
    qi = pl.program_id(1)
    kk = pl.program_id(2)
    q0 = qi * TQ
    last = (q0 + TQ - 1) // TK
    if mode == "window":
        ki = last - (NWIN - 1) + kk
        valid = ki >= 0
    else:
        ki = kk
        valid = kk <= last

    @pl.when(kk == 0)
    def _():
        m_ref[...] = jnp.full(m_ref.shape, NEG_INF, F32)
        l_ref[...] = jnp.zeros(l_ref.shape, F32)
        acc_ref[...] = jnp.zeros(acc_ref.shape, F32)

    @pl.when(valid)
    def _():
        k0 = ki * TK
        tq = q0 + lax.broadcasted_iota(jnp.int32, (TQ, TK), 0)
        kidx = k0 + lax.broadcasted_iota(jnp.int32, (TQ, TK), 1)
        if mode == "mask":
            base = x_ref[0].astype(F32) > 0.5
        elif mode == "window":
            base = (kidx <= tq) & (kidx > tq - WINDOW)
        else:
            base = kidx <= tq
            nbp = x_ref.shape[3]
            blk_of_key = (k0 + lax.broadcasted_iota(jnp.int32, (nbp, TK), 1)) // SEL_BLOCK
            expand = jnp.where(lax.broadcasted_iota(jnp.int32, (nbp, TK), 0) == blk_of_key, 1.0, 0.0)
            expand = expand.astype(MXU_DTYPE)

        def head(g, carry):
            if mode == "block":
                picked = jnp.dot(x_ref[0, g].astype(MXU_DTYPE), expand, preferred_element_type=F32)
                msk = base & (picked > 0.5)
            else:
                msk = base
            q = q_ref[0, g].reshape(J * TQ, HEAD_DIM)
            s = _dot_nt(q, k_ref[0, g]).reshape(J, TQ, TK)
            s = jnp.where(msk[None], s, NEG_INF).reshape(J * TQ, TK)
            m_prev = m_ref[g]
            m_new = jnp.maximum(m_prev, jnp.max(s, axis=-1, keepdims=True))
            alpha = jnp.exp(m_prev - m_new)
            p = jnp.exp(s - m_new).reshape(J, TQ, TK)
            p = jnp.where(msk[None], p, 0.0).reshape(J * TQ, TK)
            l_ref[g] = alpha * l_ref[g] + jnp.sum(p, axis=-1, keepdims=True)
            acc_ref[g] = alpha * acc_ref[g] + _dot(p, v_ref[0, g])
            m_ref[g] = m_new
            return carry

        lax.fori_loop(0, G, head, 0)

    @pl.when(kk == pl.num_programs(2) - 1)
    def _():
        for g in range(G):
            o = acc_ref[g] / jnp.maximum(l_ref[g], TINY)
            o_ref[0, g] = o.reshape(J, TQ, HEAD_DIM).astype(o_ref.dtype)


def _flash(q, k, v, extra, mode, TQ, TK):
    B, G, J, S, HD = q.shape
    nq = S // TQ
    if mode == "window":
        NWIN = -(-(WINDOW + TQ - 1) // TK) + (0 if (WINDOW + TQ - 1) % TK == 0 else 0)
        NWIN = (WINDOW - 1 + TK - 1) // TK + 1
        NWIN = min(NWIN, S // TK)
        nk = NWIN
        def kmap(b, qi, kk):
            return (b, 0, jnp.maximum((qi * TQ + TQ - 1) // TK - (NWIN - 1) + kk, 0), 0)
    else:
        NWIN = 0
        nk = S // TK
        def kmap(b, qi, kk):
            return (b, 0, jnp.minimum(kk, (qi * TQ + TQ - 1) // TK), 0)
    in_specs = [pl.BlockSpec((1, G, J, TQ, HD), lambda b, qi, kk: (b, 0, 0, qi, 0)),
                pl.BlockSpec((1, G, TK, HD), kmap),
                pl.BlockSpec((1, G, TK, HD), kmap)]
    args = [q, k, v]
    if mode == "mask":
        in_specs.append(pl.BlockSpec(
            (1, TQ, TK), lambda b, qi, kk: (b, qi, jnp.minimum(kk, (qi * TQ + TQ - 1) // TK))))
        args.append(extra)
    elif mode == "block":
        nbp = extra.shape[3]
        in_specs.append(pl.BlockSpec((1, G, TQ, nbp), lambda b, qi, kk: (b, 0, qi, 0)))
        args.append(extra)
    kern = functools.partial(_flash_kernel, mode=mode, G=G, J=J, TQ=TQ, TK=TK, NWIN=NWIN)
    return pl.pallas_call(
        kern,
        grid=(B, nq, nk),
        in_specs=in_specs,
        out_specs=pl.BlockSpec((1, G, J, TQ, HD), lambda b, qi, kk: (b, 0, 0, qi, 0)),
        out_shape=jax.ShapeDtypeStruct((B, G, J, S, HD), F32),
        scratch_shapes=[pltpu.VMEM((G, J * TQ, 1), F32),
                        pltpu.VMEM((G, J * TQ, 1), F32),
                        pltpu.VMEM((G, J * TQ, HD), F32)],
        compiler_params=_cparams(("arbitrary", "arbitrary", "arbitrary")),
    )(*args)


def _outproj_kernel(x_ref, oa_ref, ob_ref, ga_ref, gb_ref, wa_ref, wb_ref, wo_ref, gt_ref, o_ref):
    ua = _dot(oa_ref[...], wa_ref[...])
    ub = _dot(ob_ref[...], wb_ref[...])
    merged = jax.nn.sigmoid(ga_ref[...]) * ua + jax.nn.sigmoid(gb_ref[...]) * ub
    o_ref[...] = x_ref[...] + gt_ref[0] * _dot(merged, wo_ref[...])


def _out_proj(x2d, oa, ob, proj, ga_blk, gb_blk, wa, wb, wo, gt, S, tm=256):
    T, D = x2d.shape
    HA = oa.shape[1]
    per_b = S // tm
    return pl.pallas_call(
        _outproj_kernel,
        grid=(T // tm,),
        in_specs=[pl.BlockSpec((tm, D), lambda i: (i, 0)),
                  pl.BlockSpec((tm, HA), lambda i: (i, 0)),
                  pl.BlockSpec((tm, HA), lambda i: (i, 0)),
                  pl.BlockSpec((tm, D), lambda i: (i, ga_blk)),
                  pl.BlockSpec((tm, D), lambda i: (i, gb_blk)),
                  pl.BlockSpec((HA, D), lambda i: (0, 0)),
                  pl.BlockSpec((HA, D), lambda i: (0, 0)),
                  pl.BlockSpec((D, D), lambda i: (0, 0)),
                  pl.BlockSpec((1, 1, D), lambda i: (i // per_b, 0, 0))],
        out_specs=pl.BlockSpec((tm, D), lambda i: (i, 0)),
        out_shape=jax.ShapeDtypeStruct((T, D), F32),
        compiler_params=_cparams(("arbitrary",)),
    )(x2d, oa, ob, proj, proj, wa, wb, wo, gt)


def _moe_kernel(x_ref, g2_ref, sc_ref, sh_ref, gt_ref, wr_ref, br_ref, wg_ref, wu_ref, wd_ref,
                gf_ref, o_ref, h_ref, comb_ref, acc_ref, *, TM):
    e = pl.program_id(1)
    lane = lax.broadcasted_iota(jnp.int32, (TM, LANES), 1)

    @pl.when(e == 0)
    def _():
        h = _rms(x_ref[...], g2_ref[...]) * (1.0 + sc_ref[0]) + sh_ref[0]
        h_ref[...] = h.astype(h_ref.dtype)
        h_hi = h.astype(MXU_DTYPE)
        h_lo = (h - h_hi.astype(F32)).astype(MXU_DTYPE)
        w = wr_ref[...]
        w_hi = w.astype(MXU_DTYPE)
        w_lo = (w - w_hi.astype(F32)).astype(MXU_DTYPE)
        d = lambda a, b: jnp.dot(a, b, preferred_element_type=F32)
        logits = d(h_hi, w_hi) + d(h_hi, w_lo) + d(h_lo, w_hi) + br_ref[...]
        is_e = lane < N_EXPERTS
        is_g = (lane >= N_EXPERTS) & (lane < N_EXPERTS + N_GROUPS)
        lg = jnp.where(is_g, logits, LOWEST)
        mg = jnp.max(lg, axis=-1, keepdims=True)
        gsel = jnp.min(jnp.where(is_g & (lg == mg), lane - N_EXPERTS, N_GROUPS), axis=-1, keepdims=True)
        pg_sel = 1.0 / jnp.sum(jnp.where(is_g, jnp.exp(lg - mg), 0.0), axis=-1, keepdims=True)
        in_grp = is_e & ((lane // EXPERTS_PER_GROUP) == gsel)
        le = jnp.where(in_grp, logits, LOWEST)
        me = jnp.max(le, axis=-1, keepdims=True)
        ex = jnp.where(in_grp, jnp.exp(le - me), 0.0)
        pe = ex / jnp.sum(ex, axis=-1, keepdims=True)
        pe = jnp.where(in_grp, pe, -1.0)
        p1 = jnp.max(pe, axis=-1, keepdims=True)
        i1 = jnp.min(jnp.where(pe == p1, lane, LANES), axis=-1, keepdims=True)
        pe2 = jnp.where(lane == i1, -1.0, pe)
        p2 = jnp.max(pe2, axis=-1, keepdims=True)
        i2 = jnp.min(jnp.where(pe2 == p2, lane, LANES), axis=-1, keepdims=True)
        tot = p1 + p2
        comb = jnp.where(lane == i1, p1 / tot * pg_sel, 0.0) + jnp.where(lane == i2, p2 / tot * pg_sel, 0.0)
        comb_ref[...] = comb
        acc_ref[...] = jnp.zeros(acc_ref.shape, F32)

    h = h_ref[...]
    a = jnp.dot(h, wg_ref[0], preferred_element_type=F32)
    u = jnp.dot(h, wu_ref[0], preferred_element_type=F32)
    y = _dot(a * jax.nn.sigmoid(a) * u, wd_ref[0])
    w_e = jnp.sum(jnp.where(lane == e, comb_ref[...], 0.0), axis=-1, keepdims=True)
    acc_ref[...] += w_e * y

    @pl.when(e == pl.num_programs(1) - 1)
    def _():
        x2 = x_ref[...] + gt_ref[0] * acc_ref[...]
        o_ref[...] = _rms(x2, gf_ref[...])


def _moe(x1, g2, sc, sh, gt, wr, br, wg, wu, wd, gf, S, tm=512):
    T, D = x1.shape
    E, _, DE = wg.shape
    per_b = S // tm
    kern = functools.partial(_moe_kernel, TM=tm)
    return pl.pallas_call(
        kern,
        grid=(T // tm, E),
        in_specs=[pl.BlockSpec((tm, D), lambda i, e: (i, 0)),
                  pl.BlockSpec((1, D), lambda i, e: (0, 0)),
                  pl.BlockSpec((1, 1, D), lambda i, e: (i // per_b, 0, 0)),
                  pl.BlockSpec((1, 1, D), lambda i, e: (i // per_b, 0, 0)),
                  pl.BlockSpec((1, 1, D), lambda i, e: (i // per_b, 0, 0)),
                  pl.BlockSpec((D, LANES), lambda i, e: (0, 0)),
                  pl.BlockSpec((1, LANES), lambda i, e: (0, 0)),
                  pl.BlockSpec((1, D, DE), lambda i, e: (e, 0, 0)),
                  pl.BlockSpec((1, D, DE), lambda i, e: (e, 0, 0)),
                  pl.BlockSpec((1, DE, D), lambda i, e: (e, 0, 0)),
                  pl.BlockSpec((1, D), lambda i, e: (0, 0))],
        out_specs=pl.BlockSpec((tm, D), lambda i, e: (i, 0)),
        out_shape=jax.ShapeDtypeStruct((T, D), F32),
        scratch_shapes=[pltpu.VMEM((tm, D), MXU_DTYPE),
                        pltpu.VMEM((tm, LANES), F32),
                        pltpu.VMEM((tm, D), F32)],
        compiler_params=_cparams(("arbitrary", "arbitrary")),
    )(x1, g2.reshape(1, D), sc, sh, gt, wr, br, wg, wu, wd, gf.reshape(1, D))


def _rotary(x, positions):
    rot = x.shape[-1] // ROT_FRACTION
    half = rot // 2
    inv_freq = ROPE_THETA ** (-jnp.arange(half, dtype=F32) / half)
    ang = positions.astype(F32)[..., None] * inv_freq
    ang = ang.reshape(ang.shape[:2] + (1,) * (x.ndim - 3) + (half,))
    cos, sin = jnp.cos(ang), jnp.sin(ang)
    x1, x2, rest = x[..., :half], x[..., half:rot], x[..., rot:]
    return jnp.concatenate([x1 * cos - x2 * sin, x2 * cos + x1 * sin, rest], axis=-1)


def _layer(x, mod, positions, g_norm1, w_in, g_kv_latent, w_kv_up, pe_cmp_k, pe_cmp_v,
           w_cmp1_k, w_cmp2_k, w_cmp1_v, w_cmp2_v, w_up_a, w_up_b, w_out, g_norm2,
           w_router_group, b_router_group, w_router_expert, b_router_expert,
           w_expert_gate, w_expert_up, w_expert_down, g_out):
    B, S, D = x.shape
    T = B * S
    HD, G, J = HEAD_DIM, NSA_KV_GROUPS, NSA_REP
    HA = DSA_HEADS * HD
    kvb = G * HD
    scale = HD ** -0.5
    topk_a = min(DSA_TOPK_MAX, S // 4)
    n_sel = min(SEL_COUNT, S // SEL_BLOCK)
    mod6 = mod.reshape(B, 6, 1, D)
    sh1, sc1, gt1, sh2, sc2, gt2 = (mod6[:, i] for i in range(6))

    sizes = [HA, DSA_KV_RANK, IDX_HEADS * IDX_DIM, IDX_HEADS, IDX_DIM, NSA_HEADS * HD,
             kvb, kvb, kvb, kvb, kvb, kvb, 3 * NSA_HEADS, 2 * D]
    starts = np.concatenate([[0], np.cumsum(sizes)]).astype(int)
    seg = lambda i: w_in[:, starts[i]:starts[i + 1]]
    order = [13, 0, 5, 1, 2, 6, 7, 8, 9, 10, 11, 3, 4, 12]
    w_cat = jnp.concatenate([seg(i) for i in order], axis=1)
    n_cols = w_cat.shape[1]
    n_pad = -(-n_cols // 256) * 256
    w_cat = jnp.pad(w_cat, ((0, 0), (0, n_pad - n_cols))).astype(MXU_DTYPE)
    offs = {}
    pos = 0
    for i in order:
        offs[i] = pos
        pos += sizes[i]

    x2d = x.reshape(T, D)
    proj = _in_proj(x2d, g_norm1, sc1, sh1, w_cat, S)
    take = lambda i: proj[:, offs[i]:offs[i] + sizes[i]].reshape(B, S, sizes[i])

    q_a = _rotary(take(0).reshape(B, S, DSA_HEADS, HD), positions) * scale
    q_a = q_a.astype(MXU_DTYPE).transpose(0, 2, 1, 3)[:, :, None]
    kv = _kv_up(proj, offs[1] // DSA_KV_RANK, g_kv_latent, w_kv_up.astype(MXU_DTYPE)).reshape(B, S, 2 * HA)
    k_a = _rotary(kv[..., :HA].reshape(B, S, DSA_HEADS, HD), positions)
    k_a = k_a.astype(MXU_DTYPE).transpose(0, 2, 1, 3)
    v_a = kv[..., HA:].reshape(B, S, DSA_HEADS, HD).astype(MXU_DTYPE).transpose(0, 2, 1, 3)
    iq = _rotary(take(2).reshape(B, S, IDX_HEADS, IDX_DIM), positions)
    iq = iq.astype(MXU_DTYPE).transpose(0, 2, 1, 3)
    iw = take(3).transpose(0, 2, 1)[..., None]
    ik = _rotary(take(4).reshape(B, S, 1, IDX_DIM), positions)[:, :, 0]
    ikt = ik.astype(MXU_DTYPE).transpose(0, 2, 1)

    q_b = _rotary(take(5).reshape(B, S, G, J, HD), positions) * scale
    q_b = q_b.astype(MXU_DTYPE).transpose(0, 2, 3, 1, 4)
    to_g = lambda i: take(i).reshape(B, S, G, HD)
    kc = _compress(_rotary(to_g(6), positions), pe_cmp_k, w_cmp1_k, w_cmp2_k)
    vc = _compress(to_g(7), pe_cmp_v, w_cmp1_v, w_cmp2_v)
    n_cmp = kc.shape[1]
    ncp = -(-(n_cmp + 1) // LANES) * LANES
    pad_c = lambda t: jnp.pad(t, ((0, 0), (0, ncp - n_cmp), (0, 0), (0, 0))).astype(MXU_DTYPE).transpose(0, 2, 1, 3)
    kc, vc = pad_c(kc), pad_c(vc)
    heads_first = lambda t: t.astype(MXU_DTYPE).transpose(0, 2, 1, 3)
    ks = heads_first(_rotary(to_g(8), positions))
    vs = heads_first(to_g(9))
    kw = heads_first(_rotary(to_g(10), positions))
    vw = heads_first(to_g(11))
    br = jax.nn.sigmoid(take(12)).reshape(B, S, G, J, 3)

    sel_mask = _dsa_select(iq, iw, ikt, topk_a)
    TA = min(512, S)
    o_a = _flash(q_a, k_a, v_a, sel_mask, "mask", TA, TA)
    o_a = o_a[:, :, 0].transpose(0, 2, 1, 3).reshape(T, HA)

    nbp = -(-(S // SEL_BLOCK) // LANES) * LANES
    o_c, blk_mask = _cmp_attn(q_b, kc, vc, n_sel, nbp)
    TB = min(256, S)
    o_s = _flash(q_b, ks, vs, blk_mask, "block", TB, min(512, S))
    o_w = _flash(q_b, kw, vw, None, "window", TB, TB)
    to_tok = lambda t: t.transpose(0, 3, 1, 2, 4)
    o_b = br[..., 0:1] * to_tok(o_c) + br[..., 1:2] * to_tok(o_s) + br[..., 2:3] * to_tok(o_w)
    o_b = o_b.reshape(T, NSA_HEADS * HD)

    x1 = _out_proj(x2d, o_a.astype(MXU_DTYPE), o_b.astype(MXU_DTYPE), proj,
                   offs[13] // D, offs[13] // D + 1,
                   w_up_a.astype(MXU_DTYPE), w_up_b.astype(MXU_DTYPE), w_out.astype(MXU_DTYPE), gt1, S)

    wr = jnp.concatenate([w_router_expert, w_router_group], axis=1)
    wr = jnp.pad(wr, ((0, 0), (0, LANES - wr.shape[1])))
    brt = jnp.concatenate([b_router_expert, b_router_group])
    brt = jnp.pad(brt, (0, LANES - brt.shape[0])).reshape(1, LANES)
    out = _moe(x1, g_norm2, sc2, sh2, gt2, wr, brt, w_expert_gate.astype(MXU_DTYPE),
               w_expert_up.astype(MXU_DTYPE), w_expert_down.astype(MXU_DTYPE), g_out, S)
    return out.reshape(B, S, D)


def kernel(x, c, positions, w_ada, b_ada, g_norm1, w_in, g_kv_latent, w_kv_up, pe_cmp_k, pe_cmp_v,
           w_cmp1_k, w_cmp2_k, w_cmp1_v, w_cmp2_v, w_up_a, w_up_b, w_out, g_norm2, w_router_group,
           b_router_group, w_router_expert, b_router_expert, w_expert_gate, w_expert_up,
           w_expert_down, g_final):
    depth = w_ada.shape[0]
    assert depth == 1, "the fused final norm assumes a single layer"
    mod = _ada_mod(c, w_ada[0], b_ada[0])
    return _layer(x, mod, positions, g_norm1[0], w_in[0], g_kv_latent[0], w_kv_up[0], pe_cmp_k[0],
                  pe_cmp_v[0], w_cmp1_k[0], w_cmp2_k[0], w_cmp1_v[0], w_cmp2_v[0], w_up_a[0],
                  w_up_b[0], w_out[0], g_norm2[0], w_router_group[0], b_router_group[0],
                  w_router_expert[0], b_router_expert[0], w_expert_gate[0], w_expert_up[0],
                  w_expert_down[0], g_final)
```

```python
import functools

import numpy as np
import jax
import jax.numpy as jnp
from jax import lax
from jax.experimental import pallas as pl
from jax.experimental.pallas import tpu as pltpu

HEAD_DIM = 64
ROT_FRACTION = 4
ROPE_THETA = 500000.0
DSA_HEADS = 8
DSA_KV_RANK = 128
IDX_HEADS = 8
IDX_DIM = 32
DSA_TOPK_MAX = 256
NSA_HEADS = 8
NSA_KV_GROUPS = 2
NSA_REP = NSA_HEADS // NSA_KV_GROUPS
CMP_LEN = 32
CMP_STRIDE = 16
CMP_HIDDEN = 256
SEL_BLOCK = 64
SEL_COUNT = 16
WINDOW = 512
N_GROUPS = 4
EXPERTS_PER_GROUP = 8
N_EXPERTS = N_GROUPS * EXPERTS_PER_GROUP
D_EXPERT = 256
NORM_EPS = 1e-6
NEG_INF = -1e30
TINY = 1e-30
LOWEST = -3.0e38
FORCED = 1e30

LANES = 128
SUBLANES = 8
MXU_DTYPE = jnp.bfloat16
VMEM_LIMIT = 56 * 1024 * 1024

F32 = jnp.float32


def _cparams(sem):
    return pltpu.CompilerParams(dimension_semantics=sem, vmem_limit_bytes=VMEM_LIMIT)


def _dot(a, b):
    return jnp.dot(a.astype(MXU_DTYPE), b.astype(MXU_DTYPE), preferred_element_type=F32)


def _dot_exact_lhs(a01, b):
    hi = b.astype(MXU_DTYPE)
    r1 = b - hi.astype(F32)
    mid = r1.astype(MXU_DTYPE)
    lo = (r1 - mid.astype(F32)).astype(MXU_DTYPE)
    a = a01.astype(MXU_DTYPE)
    d = lambda u: jnp.dot(a, u, preferred_element_type=F32)
    return d(hi) + d(mid) + d(lo)


def _rms(x, g):
    return x * lax.rsqrt(jnp.mean(x * x, axis=-1, keepdims=True) + NORM_EPS) * g


def _ada_kernel(c_ref, w_ref, b_ref, o_ref):
    c = c_ref[...]
    cond = c * jax.nn.sigmoid(c)
    o_ref[...] = _dot(cond, w_ref[...]) + b_ref[...]


def _ada_mod(c, w_ada, b_ada):
    B, D = c.shape
    n_out = w_ada.shape[1]
    rows = SUBLANES
    cp = jnp.zeros((rows, D), F32).at[:B].set(c)
    tn = 1024
    out = pl.pallas_call(
        _ada_kernel,
        grid=(n_out // tn,),
        in_specs=[pl.BlockSpec((rows, D), lambda j: (0, 0)),
                  pl.BlockSpec((D, tn), lambda j: (0, j)),
                  pl.BlockSpec((1, tn), lambda j: (0, j))],
        out_specs=pl.BlockSpec((rows, tn), lambda j: (0, j)),
        out_shape=jax.ShapeDtypeStruct((rows, n_out), F32),
        compiler_params=_cparams(("arbitrary",)),
    )(cp, w_ada, b_ada.reshape(1, n_out))
    return out[:B]


def _inproj_kernel(x_ref, g_ref, sc_ref, sh_ref, w_ref, o_ref):
    h = _rms(x_ref[...], g_ref[...]) * (1.0 + sc_ref[0]) + sh_ref[0]
    o_ref[...] = _dot(h, w_ref[...])


def _in_proj(x2d, g, sc, sh, w, S, tm=256):
    T, D = x2d.shape
    N = w.shape[1]
    per_b = S // tm
    return pl.pallas_call(
        _inproj_kernel,
        grid=(T // tm,),
        in_specs=[pl.BlockSpec((tm, D), lambda i: (i, 0)),
                  pl.BlockSpec((1, D), lambda i: (0, 0)),
                  pl.BlockSpec((1, 1, D), lambda i: (i // per_b, 0, 0)),
                  pl.BlockSpec((1, 1, D), lambda i: (i // per_b, 0, 0)),
                  pl.BlockSpec((D, N), lambda i: (0, 0))],
        out_specs=pl.BlockSpec((tm, N), lambda i: (i, 0)),
        out_shape=jax.ShapeDtypeStruct((T, N), F32),
        compiler_params=_cparams(("arbitrary",)),
    )(x2d, g.reshape(1, D), sc, sh, w)


def _kvup_kernel(l_ref, g_ref, w_ref, o_ref):
    o_ref[...] = _dot(_rms(l_ref[...], g_ref[...]), w_ref[...])


def _kv_up(proj, col_block, g, w, tm=512):
    T = proj.shape[0]
    R, N = w.shape
    return pl.pallas_call(
        _kvup_kernel,
        grid=(T // tm,),
        in_specs=[pl.BlockSpec((tm, R), lambda i: (i, col_block)),
                  pl.BlockSpec((1, R), lambda i: (0, 0)),
                  pl.BlockSpec((R, N), lambda i: (0, 0))],
        out_specs=pl.BlockSpec((tm, N), lambda i: (i, 0)),
        out_shape=jax.ShapeDtypeStruct((T, N), F32),
        compiler_params=_cparams(("arbitrary",)),
    )(proj, g.reshape(1, R), w)


def _cmp_kernel(f_ref, w1_ref, w2_ref, o_ref):
    hid = _dot(f_ref[...], w1_ref[...])
    hid = hid * jax.nn.sigmoid(hid)
    o_ref[...] = _dot(hid, w2_ref[...])


def _compress(tok, pe, w1, w2):
    B, S, G, HD = tok.shape
    r = CMP_LEN // CMP_STRIDE
    n_chunks = S // CMP_STRIDE
    n_cmp = n_chunks - r + 1
    chunks = tok.reshape(B, n_chunks, CMP_STRIDE, G, HD)
    blocks = jnp.concatenate([chunks[:, i:n_cmp + i] for i in range(r)], axis=2)
    blocks = blocks + pe[:, None, :]
    flat = blocks.transpose(0, 1, 3, 2, 4).reshape(B * n_cmp * G, CMP_LEN * HD)
    rows = flat.shape[0]
    tm = 512
    rows_p = -(-rows // tm) * tm
    flat = jnp.pad(flat, ((0, rows_p - rows), (0, 0))).astype(MXU_DTYPE)
    out = pl.pallas_call(
        _cmp_kernel,
        grid=(rows_p // tm,),
        in_specs=[pl.BlockSpec((tm, CMP_LEN * HD), lambda i: (i, 0)),
                  pl.BlockSpec((CMP_LEN * HD, CMP_HIDDEN), lambda i: (0, 0)),
                  pl.BlockSpec((CMP_HIDDEN, HD), lambda i: (0, 0))],
        out_specs=pl.BlockSpec((tm, HD), lambda i: (i, 0)),
        out_shape=jax.ShapeDtypeStruct((rows_p, HD), F32),
        compiler_params=_cparams(("arbitrary",)),
    )(flat, w1.astype(MXU_DTYPE), w2.astype(MXU_DTYPE))
    return out[:rows].reshape(B, n_cmp, G, HD)


def _select_kernel(ik_ref, iqt_ref, iwt_ref, bias_ref, sc_ref, *, TQ, KC, S, K, NBIS):
    qi = pl.program_id(1)
    q0 = qi * TQ
    n_ch = (q0 + TQ + KC - 1) // KC
    n_all = S // KC
    t_row = q0 + lax.broadcasted_iota(jnp.int32, (1, TQ), 1)
    key_iota = lax.broadcasted_iota(jnp.int32, (KC, TQ), 0)
    kf = float(K)
    SUB = LANES
    sub_iota = lax.broadcasted_iota(jnp.int32, (SUB, TQ), 0)

    def score_chunk(c, carry):
        for u in range(KC // SUB):
            off = pl.multiple_of(c * KC + u * SUB, SUB)
            ikc = ik_ref[0, pl.ds(off, SUB), :]
            acc = jnp.zeros((SUB, TQ), F32)
            for h in range(IDX_HEADS):
                lg = jnp.dot(ikc, iqt_ref[0, h], preferred_element_type=F32)
                acc = acc + jnp.maximum(lg, 0.0) * iwt_ref[0, h:h + 1, :]
            sc_ref[pl.ds(off, SUB), :] = jnp.where(off + sub_iota <= t_row, acc, NEG_INF)
        return carry

    lax.fori_loop(0, n_ch, score_chunk, 0)

    def chunk(c):
        off = pl.multiple_of(c * KC, KC)
        return sc_ref[pl.ds(off, KC), :], off

    def fold(x, op):
        return op(x.reshape(KC // SUBLANES, SUBLANES, TQ), axis=0)

    def key_pass(fn, init):
        def body(c, acc):
            xs, off = chunk(c)
            return fn(acc, xs, off)
        return lax.fori_loop(0, n_ch, body, init)

    def count_ge(thr):
        acc = key_pass(lambda a, xs, off: a + fold(jnp.where(xs >= thr, 1.0, 0.0), jnp.sum),
                       jnp.zeros((SUBLANES, TQ), F32))
        return jnp.sum(acc, axis=0, keepdims=True)

    def count_gt(thr):
        acc = key_pass(lambda a, xs, off: a + fold(jnp.where(xs > thr, 1.0, 0.0), jnp.sum),
                       jnp.zeros((SUBLANES, TQ), F32))
        return jnp.sum(acc, axis=0, keepdims=True)

    def max_where(bound, strict):
        def fn(a, xs, off):
            ok = (xs < bound) if strict else (xs <= bound)
            return jnp.maximum(a, fold(jnp.where(ok, xs, LOWEST), jnp.max))
        acc = key_pass(fn, jnp.full((SUBLANES, TQ), LOWEST, F32))
        return jnp.max(acc, axis=0, keepdims=True)

    def minmax_fn(a, xs, off):
        mn, mx = a
        causal = (off + key_iota) <= t_row
        return (jnp.minimum(mn, fold(jnp.where(causal, xs, -LOWEST), jnp.min)),
                jnp.maximum(mx, fold(xs, jnp.max)))

    mn, mx = key_pass(minmax_fn, (jnp.full((SUBLANES, TQ), -LOWEST, F32),
                                  jnp.full((SUBLANES, TQ), LOWEST, F32)))
    lo = jnp.min(mn, axis=0, keepdims=True)
    hi = jnp.max(mx, axis=0, keepdims=True)

    def bisect(_, carry):
        lo, hi = carry
        mid = 0.5 * (lo + hi)
        ge = count_ge(mid) >= kf
        return jnp.where(ge, mid, lo), jnp.where(ge, hi, mid)

    lo, hi = lax.fori_loop(0, NBIS, bisect, (lo, hi))

    all_keys = jnp.where(t_row < K, 1.0, 0.0)
    v = max_where(hi, strict=False)
    done = jnp.maximum(all_keys, jnp.where(count_ge(v) >= kf, 1.0, 0.0))

    def peel_cond(st):
        return jnp.sum(st[1]) < float(TQ)

    def peel_body(st):
        v, done = st
        v = jnp.where(done > 0.5, v, max_where(v, strict=True))
        done = jnp.maximum(done, jnp.where(count_ge(v) >= kf, 1.0, 0.0))
        return v, done

    v, done = lax.while_loop(peel_cond, peel_body, (v, done))
    thr = jnp.where(all_keys > 0.5, LOWEST, v)
    n_ge = count_ge(thr)
    has_tie = jnp.max(jnp.where((n_ge > kf) & (all_keys < 0.5), 1.0, 0.0)) > 0.5

    def write(off, sel):
        bias_ref[0, pl.ds(off, KC), :] = jnp.where(sel, 0.0, NEG_INF).astype(bias_ref.dtype)

    def fill_tail():
        def body(c, carry):
            write(pl.multiple_of(c * KC, KC), jnp.zeros((KC, TQ), jnp.bool_))
            return carry
        lax.fori_loop(n_ch, n_all, body, 0)

    @pl.when(jnp.logical_not(has_tie))
    def _():
        def body(c, carry):
            xs, off = chunk(c)
            write(off, (xs >= thr) & ((off + key_iota) <= t_row))
            return carry
        lax.fori_loop(0, n_ch, body, 0)
        fill_tail()

    @pl.when(has_tie)
    def _():
        need = kf - count_gt(thr)
        r_i = lax.broadcasted_iota(jnp.int32, (KC, KC), 0)
        c_i = lax.broadcasted_iota(jnp.int32, (KC, KC), 1)
        lower = jnp.where(c_i <= r_i, 1.0, 0.0).astype(MXU_DTYPE)
        def body(c, seen):
            xs, off = chunk(c)
            causal = (off + key_iota) <= t_row
            eq = jnp.where((xs == thr) & causal, 1.0, 0.0)
            rank = jnp.dot(lower, eq.astype(MXU_DTYPE), preferred_element_type=F32) + seen
            write(off, ((xs > thr) & causal) | ((eq > 0.5) & (rank <= need)))
            return seen + jnp.sum(eq, axis=0, keepdims=True)
        lax.fori_loop(0, n_ch, body, jnp.zeros((1, TQ), F32))
        fill_tail()


def _dsa_select(ik, iqt, iwt, K):
    B, S, DI = ik.shape
    H = iqt.shape[1]
    TQ = min(256, S)
    KC = min(256, S)
    kern = functools.partial(_select_kernel, TQ=TQ, KC=KC, S=S, K=K, NBIS=20)
    return pl.pallas_call(
        kern,
        grid=(B, S // TQ),
        in_specs=[pl.BlockSpec((1, S, DI), lambda b, q: (b, 0, 0)),
                  pl.BlockSpec((1, H, DI, TQ), lambda b, q: (b, 0, 0, q)),
                  pl.BlockSpec((1, H, TQ), lambda b, q: (b, 0, q))],
        out_specs=pl.BlockSpec((1, S, TQ), lambda b, q: (b, 0, q)),
        out_shape=jax.ShapeDtypeStruct((B, S, S), jnp.bfloat16),
        scratch_shapes=[pltpu.VMEM((S, TQ), F32)],
        compiler_params=_cparams(("arbitrary", "arbitrary")),
    )(ik, iqt, iwt)


def _cmp_attn_kernel(qt_ref, kc_ref, vct_ref, o_ref, bm_ref, *, TQ, NCP, NBP, N_SEL):
    qi = pl.program_id(1)
    q0 = qi * TQ
    G, J = NSA_KV_GROUPS, NSA_REP
    N = J * TQ
    t_lane = q0 + (lax.broadcasted_iota(jnp.int32, (NCP, N), 1) & (TQ - 1))
    cmp_end = lax.broadcasted_iota(jnp.int32, (NCP, N), 0) * CMP_STRIDE + (CMP_LEN - 1)
    vis = cmp_end <= t_lane
    n_i = lax.broadcasted_iota(jnp.int32, (NBP, NCP), 0) * SEL_BLOCK
    c_i = lax.broadcasted_iota(jnp.int32, (NBP, NCP), 1) * CMP_STRIDE
    overlap_t = jnp.where((c_i <= n_i + SEL_BLOCK - 1) & (c_i + CMP_LEN - 1 >= n_i), 1.0, 0.0)
    blk = lax.broadcasted_iota(jnp.int32, (NBP, TQ), 0)
    tq = q0 + lax.broadcasted_iota(jnp.int32, (NBP, TQ), 1)
    cur = tq // SEL_BLOCK
    admissible = blk * SEL_BLOCK <= tq
    forced = (blk == 0) | (blk == cur) | (blk == cur - 1)
    for g in range(G):
        s = jnp.where(vis, _dot(kc_ref[0, g], qt_ref[0, g, 0]), NEG_INF)
        p = jnp.exp(s - jnp.max(s, axis=0, keepdims=True))
        p = jnp.where(vis, p, 0.0)
        p = p / jnp.maximum(jnp.sum(p, axis=0, keepdims=True), TINY)
        o_ref[0, g, 0] = _dot(vct_ref[0, g], p)
        psum = p[:, 0:TQ]
        for j in range(1, J):
            psum = psum + p[:, j * TQ:(j + 1) * TQ]
        imp = _dot_exact_lhs(overlap_t, psum)
        score = jnp.where(admissible & forced, FORCED, jnp.where(admissible, imp, NEG_INF))

        def pick(_, st):
            score, sel = st
            m = jnp.max(score, axis=0, keepdims=True)
            first = jnp.min(jnp.where(score == m, blk, NBP), axis=0, keepdims=True)
            hit = blk == first
            return jnp.where(hit, LOWEST, score), jnp.where(hit, 1.0, sel)

        _, sel = lax.fori_loop(0, N_SEL, pick, (score, jnp.zeros((NBP, TQ), F32)))
        bm_ref[0, g] = sel.astype(bm_ref.dtype)


def _cmp_attn(qt, kc, vct, n_sel, NBP, TQ):
    B, G, nq, HD, N = qt.shape
    NCP = kc.shape[2]
    S = nq * TQ
    kern = functools.partial(_cmp_attn_kernel, TQ=TQ, NCP=NCP, NBP=NBP, N_SEL=n_sel)
    return pl.pallas_call(
        kern,
        grid=(B, nq),
        in_specs=[pl.BlockSpec((1, G, 1, HD, N), lambda b, q: (b, 0, q, 0, 0)),
                  pl.BlockSpec((1, G, NCP, HD), lambda b, q: (b, 0, 0, 0)),
                  pl.BlockSpec((1, G, HD, NCP), lambda b, q: (b, 0, 0, 0))],
        out_specs=[pl.BlockSpec((1, G, 1, HD, N), lambda b, q: (b, 0, q, 0, 0)),
                   pl.BlockSpec((1, G, NBP, TQ), lambda b, q: (b, 0, 0, q))],
        out_shape=[jax.ShapeDtypeStruct((B, G, nq, HD, N), F32),
                   jax.ShapeDtypeStruct((B, G, NBP, S), jnp.bfloat16)],
        compiler_params=_cparams(("arbitrary", "arbitrary")),
    )(qt, kc, vct)


def _flash_kernel(*refs, mode, G, J, TQ, TK, NWIN):
    if mode == "window":
        qt_ref, k_ref, vt_ref, o_ref, m_ref, l_ref, acc_ref = refs
        x_ref = None
    else:
        qt_ref, k_ref, vt_ref, x_ref, o_ref, m_ref, l_ref, acc_ref = refs
    qi = pl.program_id(1)
    kk = pl.program_id(2)
    q0 = qi * TQ
    last = (q0 + TQ - 1) // TK
    if mode == "window":
        ki = last - (NWIN - 1) + kk
        valid = ki >= 0
    else:
        ki = kk
        valid = kk <= last

    @pl.when(kk == 0)
    def _():
        m_ref[...] = jnp.full(m_ref.shape, NEG_INF, F32)
        l_ref[...] = jnp.zeros(l_ref.shape, F32)
        acc_ref[...] = jnp.zeros(acc_ref.shape, F32)

    @pl.when(valid)
    def _():
        k0 = ki * TK
        kidx = k0 + lax.broadcasted_iota(jnp.int32, (TK, TQ), 0)
        tq = q0 + lax.broadcasted_iota(jnp.int32, (TK, TQ), 1)
        if mode == "mask":
            bias = x_ref[0].astype(F32)
        elif mode == "window":
            bias = jnp.where((kidx <= tq) & (kidx > tq - WINDOW), 0.0, NEG_INF)
        else:
            causal = kidx <= tq
            nbp = x_ref.shape[2]
            blk_of_key = (k0 + lax.broadcasted_iota(jnp.int32, (TK, nbp), 0)) // SEL_BLOCK
            expand_t = jnp.where(lax.broadcasted_iota(jnp.int32, (TK, nbp), 1) == blk_of_key, 1.0, 0.0)
            expand_t = expand_t.astype(MXU_DTYPE)

        for g in range(G):
            if mode == "block":
                picked = jnp.dot(expand_t, x_ref[0, g].astype(MXU_DTYPE), preferred_element_type=F32)
                b = jnp.where(causal & (picked > 0.5), 0.0, NEG_INF)
            else:
                b = bias
            if J > 1:
                b = jnp.concatenate([b] * J, axis=1)
            s = _dot(k_ref[0, g], qt_ref[0, g, 0]) + b
            m_prev = m_ref[g]
            m_new = jnp.maximum(m_prev, jnp.max(s, axis=0, keepdims=True))
            alpha = jnp.exp(m_prev - m_new)
            p = jnp.exp(s - m_new)
            l_ref[g] = alpha * l_ref[g] + jnp.sum(p, axis=0, keepdims=True)
            acc_ref[g] = alpha * acc_ref[g] + _dot(vt_ref[0, g], p)
            m_ref[g] = m_new

    @pl.when(kk == pl.num_programs(2) - 1)
    def _():
        for g in range(G):
            o_ref[0, g, 0] = (acc_ref[g] / jnp.maximum(l_ref[g], TINY)).astype(o_ref.dtype)


def _flash(qt, k, vt, extra, mode, TQ, TK):
    B, G, nq, HD, N = qt.shape
    J = N // TQ
    S = nq * TQ
    last_of = lambda qi: (qi * TQ + TQ - 1) // TK
    if mode == "window":
        NWIN = min((WINDOW - 1 + TK - 1) // TK + 1, S // TK)
        nk = NWIN
        kidx_of = lambda qi, kk: jnp.maximum(last_of(qi) - (NWIN - 1) + kk, 0)
    else:
        NWIN = 0
        nk = S // TK
        kidx_of = lambda qi, kk: jnp.minimum(kk, last_of(qi))
    in_specs = [pl.BlockSpec((1, G, 1, HD, N), lambda b, qi, kk: (b, 0, qi, 0, 0)),
                pl.BlockSpec((1, G, TK, HD), lambda b, qi, kk: (b, 0, kidx_of(qi, kk), 0)),
                pl.BlockSpec((1, G, HD, TK), lambda b, qi, kk: (b, 0, 0, kidx_of(qi, kk)))]
    args = [qt, k, vt]
    if mode == "mask":
        in_specs.append(pl.BlockSpec((1, TK, TQ), lambda b, qi, kk: (b, kidx_of(qi, kk), qi)))
        args.append(extra)
    elif mode == "block":
        nbp = extra.shape[2]
        in_specs.append(pl.BlockSpec((1, G, nbp, TQ), lambda b, qi, kk: (b, 0, 0, qi)))
        args.append(extra)
    kern = functools.partial(_flash_kernel, mode=mode, G=G, J=J, TQ=TQ, TK=TK, NWIN=NWIN)
    return pl.pallas_call(
        kern,
        grid=(B, nq, nk),
        in_specs=in_specs,
        out_specs=pl.BlockSpec((1, G, 1, HD, N), lambda b, qi, kk: (b, 0, qi, 0, 0)),
        out_shape=jax.ShapeDtypeStruct((B, G, nq, HD, N), F32),
        scratch_shapes=[pltpu.VMEM((G, 1, N), F32),
                        pltpu.VMEM((G, 1, N), F32),
                        pltpu.VMEM((G, HD, N), F32)],
        compiler_params=_cparams(("arbitrary", "arbitrary", "arbitrary")),
    )(*args)


def _outproj_kernel(x_ref, oa_ref, ob_ref, ga_ref, gb_ref, wa_ref, wb_ref, wo_ref, gt_ref, o_ref):
    ua = _dot(oa_ref[...], wa_ref[...])
    ub = _dot(ob_ref[...], wb_ref[...])
    merged = jax.nn.sigmoid(ga_ref[...]) * ua + jax.nn.sigmoid(gb_ref[...]) * ub
    o_ref[...] = x_ref[...] + gt_ref[0] * _dot(merged, wo_ref[...])


def _out_proj(x2d, oa, ob, proj, ga_blk, gb_blk, wa, wb, wo, gt, S, tm=256):
    T, D = x2d.shape
    HA = oa.shape[1]
    per_b = S // tm
    return pl.pallas_call(
        _outproj_kernel,
        grid=(T // tm,),
        in_specs=[pl.BlockSpec((tm, D), lambda i: (i, 0)),
                  pl.BlockSpec((tm, HA), lambda i: (i, 0)),
                  pl.BlockSpec((tm, HA), lambda i: (i, 0)),
                  pl.BlockSpec((tm, D), lambda i: (i, ga_blk)),
                  pl.BlockSpec((tm, D), lambda i: (i, gb_blk)),
                  pl.BlockSpec((HA, D), lambda i: (0, 0)),
                  pl.BlockSpec((HA, D), lambda i: (0, 0)),
                  pl.BlockSpec((D, D), lambda i: (0, 0)),
                  pl.BlockSpec((1, 1, D), lambda i: (i // per_b, 0, 0))],
        out_specs=pl.BlockSpec((tm, D), lambda i: (i, 0)),
        out_shape=jax.ShapeDtypeStruct((T, D), F32),
        compiler_params=_cparams(("arbitrary",)),
    )(x2d, oa, ob, proj, proj, wa, wb, wo, gt)


def _moe_kernel(x_ref, g2_ref, sc_ref, sh_ref, gt_ref, wr_ref, br_ref, wg_ref, wu_ref, wd_ref,
                gf_ref, o_ref, h_ref, comb_ref, acc_ref, *, TM):
    e = pl.program_id(1)
    lane = lax.broadcasted_iota(jnp.int32, (TM, LANES), 1)

    @pl.when(e == 0)
    def _():
        h = _rms(x_ref[...], g2_ref[...]) * (1.0 + sc_ref[0]) + sh_ref[0]
        h_ref[...] = h.astype(h_ref.dtype)
        h_hi = h.astype(MXU_DTYPE)
        h_lo = (h - h_hi.astype(F32)).astype(MXU_DTYPE)
        w = wr_ref[...]
        w_hi = w.astype(MXU_DTYPE)
        w_lo = (w - w_hi.astype(F32)).astype(MXU_DTYPE)
        d = lambda a, b: jnp.dot(a, b, preferred_element_type=F32)
        logits = d(h_hi, w_hi) + d(h_hi, w_lo) + d(h_lo, w_hi) + br_ref[...]
        is_e = lane < N_EXPERTS
        is_g = (lane >= N_EXPERTS) & (lane < N_EXPERTS + N_GROUPS)
        lg = jnp.where(is_g, logits, LOWEST)
        mg = jnp.max(lg, axis=-1, keepdims=True)
        gsel = jnp.min(jnp.where(is_g & (lg == mg), lane - N_EXPERTS, N_GROUPS), axis=-1, keepdims=True)
        pg_sel = 1.0 / jnp.sum(jnp.where(is_g, jnp.exp(lg - mg), 0.0), axis=-1, keepdims=True)
        in_grp = is_e & ((lane // EXPERTS_PER_GROUP) == gsel)
        le = jnp.where(in_grp, logits, LOWEST)
        me = jnp.max(le, axis=-1, keepdims=True)
        ex = jnp.where(in_grp, jnp.exp(le - me), 0.0)
        pe = ex / jnp.sum(ex, axis=-1, keepdims=True)
        pe = jnp.where(in_grp, pe, -1.0)
        p1 = jnp.max(pe, axis=-1, keepdims=True)
        i1 = jnp.min(jnp.where(pe == p1, lane, LANES), axis=-1, keepdims=True)
        pe2 = jnp.where(lane == i1, -1.0, pe)
        p2 = jnp.max(pe2, axis=-1, keepdims=True)
        i2 = jnp.min(jnp.where(pe2 == p2, lane, LANES), axis=-1, keepdims=True)
        tot = p1 + p2
        comb = jnp.where(lane == i1, p1 / tot * pg_sel, 0.0) + jnp.where(lane == i2, p2 / tot * pg_sel, 0.0)
        comb_ref[...] = comb
        acc_ref[...] = jnp.zeros(acc_ref.shape, F32)

    h = h_ref[...]
    a = jnp.dot(h, wg_ref[0], preferred_element_type=F32)
    u = jnp.dot(h, wu_ref[0], preferred_element_type=F32)
    y = _dot(a * jax.nn.sigmoid(a) * u, wd_ref[0])
    w_e = jnp.sum(jnp.where(lane == e, comb_ref[...], 0.0), axis=-1, keepdims=True)
    acc_ref[...] += w_e * y

    @pl.when(e == pl.num_programs(1) - 1)
    def _():
        x2 = x_ref[...] + gt_ref[0] * acc_ref[...]
        o_ref[...] = _rms(x2, gf_ref[...])


def _moe(x1, g2, sc, sh, gt, wr, br, wg, wu, wd, gf, S, tm=512):
    T, D = x1.shape
    E, _, DE = wg.shape
    per_b = S // tm
    kern = functools.partial(_moe_kernel, TM=tm)
    return pl.pallas_call(
        kern,
        grid=(T // tm, E),
        in_specs=[pl.BlockSpec((tm, D), lambda i, e: (i, 0)),
                  pl.BlockSpec((1, D), lambda i, e: (0, 0)),
                  pl.BlockSpec((1, 1, D), lambda i, e: (i // per_b, 0, 0)),
                  pl.BlockSpec((1, 1, D), lambda i, e: (i // per_b, 0, 0)),
                  pl.BlockSpec((1, 1, D), lambda i, e: (i // per_b, 0, 0)),
                  pl.BlockSpec((D, LANES), lambda i, e: (0, 0)),
                  pl.BlockSpec((1, LANES), lambda i, e: (0, 0)),
                  pl.BlockSpec((1, D, DE), lambda i, e: (e, 0, 0)),
                  pl.BlockSpec((1, D, DE), lambda i, e: (e, 0, 0)),
                  pl.BlockSpec((1, DE, D), lambda i, e: (e, 0, 0)),
                  pl.BlockSpec((1, D), lambda i, e: (0, 0))],
        out_specs=pl.BlockSpec((tm, D), lambda i, e: (i, 0)),
        out_shape=jax.ShapeDtypeStruct((T, D), F32),
        scratch_shapes=[pltpu.VMEM((tm, D), MXU_DTYPE),
                        pltpu.VMEM((tm, LANES), F32),
                        pltpu.VMEM((tm, D), F32)],
        compiler_params=_cparams(("arbitrary", "arbitrary")),
    )(x1, g2.reshape(1, D), sc, sh, gt, wr, br, wg, wu, wd, gf.reshape(1, D))


def _rotary(x, positions):
    rot = x.shape[-1] // ROT_FRACTION
    half = rot // 2
    inv_freq = ROPE_THETA ** (-jnp.arange(half, dtype=F32) / half)
    ang = positions.astype(F32)[..., None] * inv_freq
    ang = ang.reshape(ang.shape[:2] + (1,) * (x.ndim - 3) + (half,))
    cos, sin = jnp.cos(ang), jnp.sin(ang)
    x1, x2, rest = x[..., :half], x[..., half:rot], x[..., rot:]
    return jnp.concatenate([x1 * cos - x2 * sin, x2 * cos + x1 * sin, rest], axis=-1)


def _q_tiles(q, TQ):
    B, S, G, J, HD = q.shape
    q = q.reshape(B, S // TQ, TQ, G, J, HD).transpose(0, 3, 1, 5, 4, 2)
    return q.reshape(B, G, S // TQ, HD, J * TQ)


def _from_q_tiles(o, J):
    B, G, nq, HD, N = o.shape
    TQ = N // J
    o = o.reshape(B, G, nq, HD, J, TQ).transpose(0, 2, 5, 1, 4, 3)
    return o.reshape(B, nq * TQ, G, J, HD)


def _layer(x, mod, positions, g_norm1, w_in, g_kv_latent, w_kv_up, pe_cmp_k, pe_cmp_v,
           w_cmp1_k, w_cmp2_k, w_cmp1_v, w_cmp2_v, w_up_a, w_up_b, w_out, g_norm2,
           w_router_group, b_router_group, w_router_expert, b_router_expert,
           w_expert_gate, w_expert_up, w_expert_down, g_out):
    B, S, D = x.shape
    T = B * S
    HD, G, J = HEAD_DIM, NSA_KV_GROUPS, NSA_REP
    HA = DSA_HEADS * HD
    kvb = G * HD
    scale = HD ** -0.5
    topk_a = min(DSA_TOPK_MAX, S // 4)
    n_sel = min(SEL_COUNT, S // SEL_BLOCK)
    mod6 = mod.reshape(B, 6, 1, D)
    sh1, sc1, gt1, sh2, sc2, gt2 = (mod6[:, i] for i in range(6))

    sizes = [HA, DSA_KV_RANK, IDX_HEADS * IDX_DIM, IDX_HEADS, IDX_DIM, NSA_HEADS * HD,
             kvb, kvb, kvb, kvb, kvb, kvb, 3 * NSA_HEADS, 2 * D]
    starts = np.concatenate([[0], np.cumsum(sizes)]).astype(int)
    seg = lambda i: w_in[:, starts[i]:starts[i + 1]]
    order = [13, 0, 5, 1, 2, 6, 7, 8, 9, 10, 11, 3, 4, 12]
    w_cat = jnp.concatenate([seg(i) for i in order], axis=1)
    n_cols = w_cat.shape[1]
    n_pad = -(-n_cols // 256) * 256
    w_cat = jnp.pad(w_cat, ((0, 0), (0, n_pad - n_cols))).astype(MXU_DTYPE)
    offs = {}
    pos = 0
    for i in order:
        offs[i] = pos
        pos += sizes[i]

    x2d = x.reshape(T, D)
    proj = _in_proj(x2d, g_norm1, sc1, sh1, w_cat, S)
    take = lambda i: proj[:, offs[i]:offs[i] + sizes[i]].reshape(B, S, sizes[i])
    heads_first = lambda t: t.astype(MXU_DTYPE).transpose(0, 2, 1, 3)
    heads_first_t = lambda t: t.astype(MXU_DTYPE).transpose(0, 2, 3, 1)

    TA = min(512, S)
    q_a = _rotary(take(0).reshape(B, S, DSA_HEADS, HD), positions) * scale
    qt_a = _q_tiles(q_a.astype(MXU_DTYPE)[:, :, :, None], TA)
    kv = _kv_up(proj, offs[1] // DSA_KV_RANK, g_kv_latent, w_kv_up.astype(MXU_DTYPE)).reshape(B, S, 2 * HA)
    k_a = heads_first(_rotary(kv[..., :HA].reshape(B, S, DSA_HEADS, HD), positions))
    vt_a = heads_first_t(kv[..., HA:].reshape(B, S, DSA_HEADS, HD))
    iq = _rotary(take(2).reshape(B, S, IDX_HEADS, IDX_DIM), positions)
    iqt = heads_first_t(iq)
    iwt = take(3).transpose(0, 2, 1)
    ik = _rotary(take(4).reshape(B, S, 1, IDX_DIM), positions)[:, :, 0].astype(MXU_DTYPE)

    TB = min(256, S)
    q_b = _rotary(take(5).reshape(B, S, G, J, HD), positions) * scale
    qt_b = _q_tiles(q_b.astype(MXU_DTYPE), TB)
    to_g = lambda i: take(i).reshape(B, S, G, HD)
    kc = _compress(_rotary(to_g(6), positions), pe_cmp_k, w_cmp1_k, w_cmp2_k)
    vc = _compress(to_g(7), pe_cmp_v, w_cmp1_v, w_cmp2_v)
    n_cmp = kc.shape[1]
    ncp = -(-(n_cmp + 1) // LANES) * LANES
    pad_c = lambda t: jnp.pad(t, ((0, 0), (0, ncp - n_cmp), (0, 0), (0, 0)))
    kc, vct = heads_first(pad_c(kc)), heads_first_t(pad_c(vc))
    ks = heads_first(_rotary(to_g(8), positions))
    vst = heads_first_t(to_g(9))
    kw = heads_first(_rotary(to_g(10), positions))
    vwt = heads_first_t(to_g(11))
    br = jax.nn.sigmoid(take(12)).reshape(B, S, G, J, 3)

    sel_bias = _dsa_select(ik, iqt, iwt, topk_a)
    o_a = _flash(qt_a, k_a, vt_a, sel_bias, "mask", TA, TA)
    o_a = _from_q_tiles(o_a, 1).reshape(T, HA)

    nbp = -(-(S // SEL_BLOCK) // LANES) * LANES
    o_c, blk_mask = _cmp_attn(qt_b, kc, vct, n_sel, nbp, TB)
    o_s = _flash(qt_b, ks, vst, blk_mask, "block", TB, min(512, S))
    o_w = _flash(qt_b, kw, vwt, None, "window", TB, TB)
    o_b = (br[..., 0:1] * _from_q_tiles(o_c, J) + br[..., 1:2] * _from_q_tiles(o_s, J)
           + br[..., 2:3] * _from_q_tiles(o_w, J))
    o_b = o_b.reshape(T, NSA_HEADS * HD)

    x1 = _out_proj(x2d, o_a.astype(MXU_DTYPE), o_b.astype(MXU_DTYPE), proj,
                   offs[13] // D, offs[13] // D + 1,
                   w_up_a.astype(MXU_DTYPE), w_up_b.astype(MXU_DTYPE), w_out.astype(MXU_DTYPE), gt1, S)

    wr = jnp.concatenate([w_router_expert, w_router_group], axis=1)
    wr = jnp.pad(wr, ((0, 0), (0, LANES - wr.shape[1])))
    brt = jnp.concatenate([b_router_expert, b_router_group])
    brt = jnp.pad(brt, (0, LANES - brt.shape[0])).reshape(1, LANES)
    out = _moe(x1, g_norm2, sc2, sh2, gt2, wr, brt, w_expert_gate.astype(MXU_DTYPE),
               w_expert_up.astype(MXU_DTYPE), w_expert_down.astype(MXU_DTYPE), g_out, S)
    return out.reshape(B, S, D)


def kernel(x, c, positions, w_ada, b_ada, g_norm1, w_in, g_kv_latent, w_kv_up, pe_cmp_k, pe_cmp_v,
           w_cmp1_k, w_cmp2_k, w_cmp1_v, w_cmp2_v, w_up_a, w_up_b, w_out, g_norm2, w_router_group,
           b_router_group, w_router_expert, b_router_expert, w_expert_gate, w_expert_up,
           w_expert_down, g_final):
    depth = w_ada.shape[0]
    assert depth == 1, "the fused final norm assumes a single layer"
    mod = _ada_mod(c, w_ada[0], b_ada[0])
    return _layer(x, mod, positions, g_norm1[0], w_in[0], g_kv_latent[0], w_kv_up[0], pe_cmp_k[0],
                  pe_cmp_v[0], w_cmp1_k[0], w_cmp2_k[0], w_cmp1_v[0], w_cmp2_v[0], w_up_a[0],
                  w_up_b[0], w_out[0], g_norm2[0], w_router_group[0], b_router_group[0],
                  w_router_expert[0], b_router_expert[0], w_expert_gate[0], w_expert_up[0],
                  w_expert_down[0], g_final)
```

```python
import functools

import numpy as np
import jax
import jax.numpy as jnp
from jax import lax
from jax.experimental import pallas as pl
from jax.experimental.pallas import tpu as pltpu

HEAD_DIM = 64
ROT_FRACTION = 4
ROPE_THETA = 500000.0
DSA_HEADS = 8
DSA_KV_RANK = 128
IDX_HEADS = 8
IDX_DIM = 32
DSA_TOPK_MAX = 256
NSA_HEADS = 8
NSA_KV_GROUPS = 2
NSA_REP = NSA_HEADS // NSA_KV_GROUPS
CMP_LEN = 32
CMP_STRIDE = 16
CMP_HIDDEN = 256
SEL_BLOCK = 64
SEL_COUNT = 16
WINDOW = 512
N_GROUPS = 4
EXPERTS_PER_GROUP = 8
N_EXPERTS = N_GROUPS * EXPERTS_PER_GROUP
D_EXPERT = 256
NORM_EPS = 1e-6
NEG_INF = -1e30
TINY = 1e-30
LOWEST = -3.0e38
FORCED = 1e30

LANES = 128
SUBLANES = 8
MXU_DTYPE = jnp.bfloat16
VMEM_LIMIT = 56 * 1024 * 1024

F32 = jnp.float32


def _cparams(sem):
    return pltpu.CompilerParams(dimension_semantics=sem, vmem_limit_bytes=VMEM_LIMIT)


def _dot(a, b):
    return jnp.dot(a.astype(MXU_DTYPE), b.astype(MXU_DTYPE), preferred_element_type=F32)


def _dot_exact_lhs(a01, b):
    hi = b.astype(MXU_DTYPE)
    r1 = b - hi.astype(F32)
    mid = r1.astype(MXU_DTYPE)
    lo = (r1 - mid.astype(F32)).astype(MXU_DTYPE)
    a = a01.astype(MXU_DTYPE)
    d = lambda u: jnp.dot(a, u, preferred_element_type=F32)
    return d(hi) + d(mid) + d(lo)


def _rms(x, g):
    return x * lax.rsqrt(jnp.mean(x * x, axis=-1, keepdims=True) + NORM_EPS) * g


def _ada_kernel(c_ref, w_ref, b_ref, o_ref):
    c = c_ref[...]
    cond = c * jax.nn.sigmoid(c)
    o_ref[...] = _dot(cond, w_ref[...]) + b_ref[...]


def _ada_mod(c, w_ada, b_ada):
    B, D = c.shape
    n_out = w_ada.shape[1]
    rows = SUBLANES
    cp = jnp.zeros((rows, D), F32).at[:B].set(c)
    tn = 1024
    out = pl.pallas_call(
        _ada_kernel,
        grid=(n_out // tn,),
        in_specs=[pl.BlockSpec((rows, D), lambda j: (0, 0)),
                  pl.BlockSpec((D, tn), lambda j: (0, j)),
                  pl.BlockSpec((1, tn), lambda j: (0, j))],
        out_specs=pl.BlockSpec((rows, tn), lambda j: (0, j)),
        out_shape=jax.ShapeDtypeStruct((rows, n_out), F32),
        compiler_params=_cparams(("arbitrary",)),
    )(cp, w_ada, b_ada.reshape(1, n_out))
    return out[:B]


KVB = NSA_KV_GROUPS * HEAD_DIM
ROW_SEGS = (("mg", 2048), ("lat", DSA_KV_RANK), ("kc", KVB), ("vc", KVB), ("ks", KVB), ("kw", KVB),
            ("ik", LANES))
COL_SEGS = (("qa", DSA_HEADS * HEAD_DIM), ("qb", NSA_HEADS * HEAD_DIM), ("iq", IDX_HEADS * IDX_DIM),
            ("vs", KVB), ("vw", KVB), ("lat", DSA_KV_RANK), ("iw", IDX_HEADS), ("br", 3 * NSA_HEADS))


def _seg_offsets(segs):
    out, pos = {}, 0
    for name, n in segs:
        out[name] = pos
        pos += n
    return out, pos


ROW_OFF, ROW_COLS = _seg_offsets(ROW_SEGS)
COL_OFF, COL_ROWS = _seg_offsets(COL_SEGS)


def _rot_lanes(x, c, sa, sb, half):
    outs = []
    for j in range(x.shape[1] // LANES):
        xs = x[:, j * LANES:(j + 1) * LANES]
        outs.append(xs * c + pltpu.roll(xs, half, 1) * sa + pltpu.roll(xs, LANES - half, 1) * sb)
    return outs[0] if len(outs) == 1 else jnp.concatenate(outs, axis=1)


def _prep_kernel(x_ref, xt_ref, g_ref, sc_ref, sh_ref, gc_ref, scc_ref, shc_ref, wr_ref, wc_ref,
                 gkv_ref, gkvc_ref, wkk_ref, wkv_ref, c64_ref, sa64_ref, sb64_ref,
                 c32_ref, sa32_ref, sb32_ref, cos_ref, sin_ref, c32t_ref, s32t_ref,
                 mg_ref, qa_ref, ka_ref, va_ref, iq_ref, iw_ref, ik_ref, qb_ref, kcvc_ref,
                 ks_ref, kw_ref, vs_ref, vw_ref, br_ref):
    HD = HEAD_DIM
    half = HD // ROT_FRACTION // 2
    scale = HD ** -0.5
    h = _rms(x_ref[...], g_ref[...]) * (1.0 + sc_ref[0]) + sh_ref[0]
    pr = _dot(h, wr_ref[...])
    row = lambda name, n: pr[:, ROW_OFF[name]:ROW_OFF[name] + n]
    mg_ref[...] = row("mg", 2048)
    c64, sa64, sb64 = c64_ref[...], sa64_ref[...], sb64_ref[...]
    ka = _rot_lanes(_dot(_rms(row("lat", DSA_KV_RANK), gkv_ref[...]), wkk_ref[...]), c64, sa64, sb64, half)
    for hh in range(DSA_HEADS):
        ka_ref[0, hh] = ka[:, hh * HD:(hh + 1) * HD].astype(ka_ref.dtype)
    kcvc_ref[0] = jnp.concatenate([_rot_lanes(row("kc", KVB), c64, sa64, sb64, half), row("vc", KVB)], axis=1)
    ks = _rot_lanes(row("ks", KVB), c64, sa64, sb64, half)
    kw = _rot_lanes(row("kw", KVB), c64, sa64, sb64, half)
    for g in range(NSA_KV_GROUPS):
        ks_ref[0, g] = ks[:, g * HD:(g + 1) * HD].astype(ks_ref.dtype)
        kw_ref[0, g] = kw[:, g * HD:(g + 1) * HD].astype(kw_ref.dtype)
    ik = _rot_lanes(row("ik", LANES), c32_ref[...], sa32_ref[...], sb32_ref[...], IDX_DIM // ROT_FRACTION // 2)
    ik_ref[0] = ik[:, :IDX_DIM].astype(ik_ref.dtype)
    xt = xt_ref[0]
    ht = xt * lax.rsqrt(jnp.mean(xt * xt, axis=0, keepdims=True) + NORM_EPS) * gc_ref[...]
    ht = ht * (1.0 + scc_ref[0]) + shc_ref[0]
    pc = _dot(wc_ref[...], ht)
    col = lambda name, n: pc[COL_OFF[name]:COL_OFF[name] + n]
    cos, sin = cos_ref[0], sin_ref[0]

    def rot_rows(blk):
        x1, x2 = blk[0:half], blk[half:2 * half]
        return jnp.concatenate([x1 * cos - x2 * sin, x2 * cos + x1 * sin, blk[2 * half:]], axis=0)

    qa, qb = col("qa", DSA_HEADS * HD), col("qb", NSA_HEADS * HD)
    for hh in range(DSA_HEADS):
        qa_ref[0, hh] = (rot_rows(qa[hh * HD:(hh + 1) * HD]) * scale).astype(qa_ref.dtype)
    for hh in range(NSA_HEADS):
        qb_ref[0, hh] = (rot_rows(qb[hh * HD:(hh + 1) * HD]) * scale).astype(qb_ref.dtype)
    iq = col("iq", IDX_HEADS * IDX_DIM)
    c32t, s32t = c32t_ref[0], s32t_ref[0]
    for hh in range(IDX_HEADS):
        blk = iq[hh * IDX_DIM:(hh + 1) * IDX_DIM]
        top = blk[0:SUBLANES]
        top = top * c32t + pltpu.roll(top, SUBLANES // 2, 0) * s32t
        iq_ref[0, hh] = jnp.concatenate([top, blk[SUBLANES:]], axis=0).astype(iq_ref.dtype)
    vs, vw = col("vs", KVB), col("vw", KVB)
    for g in range(NSA_KV_GROUPS):
        vs_ref[0, g] = vs[g * HD:(g + 1) * HD].astype(vs_ref.dtype)
        vw_ref[0, g] = vw[g * HD:(g + 1) * HD].astype(vw_ref.dtype)
    lat = col("lat", DSA_KV_RANK)
    lat = lat * lax.rsqrt(jnp.mean(lat * lat, axis=0, keepdims=True) + NORM_EPS) * gkvc_ref[...]
    va = _dot(wkv_ref[...], lat)
    for hh in range(DSA_HEADS):
        va_ref[0, hh] = va[hh * HD:(hh + 1) * HD].astype(va_ref.dtype)
    iw_ref[0] = col("iw", IDX_HEADS)
    br_ref[0] = jax.nn.sigmoid(col("br", 3 * NSA_HEADS))


def _prep(x, mod_sc, mod_sh, g_norm1, w_in, g_kv, w_kv_up, positions, tm=256):
    B, S, D = x.shape
    T = B * S
    HD, HA, G = HEAD_DIM, DSA_HEADS, NSA_KV_GROUPS
    per_b = S // tm
    sizes = [HA * HD, DSA_KV_RANK, IDX_HEADS * IDX_DIM, IDX_HEADS, IDX_DIM, NSA_HEADS * HD,
             KVB, KVB, KVB, KVB, KVB, KVB, 3 * NSA_HEADS, 2 * D]
    names = ["qa", "lat", "iq", "iw", "ik", "qb", "kc", "vc", "ks", "vs", "kw", "vw", "br", "mg"]
    starts = dict(zip(names, np.concatenate([[0], np.cumsum(sizes)[:-1]]).astype(int)))
    width = dict(zip(names, sizes))
    def seg(name, n):
        w = w_in[:, starts[name]:starts[name] + width[name]]
        return jnp.pad(w, ((0, 0), (0, n - width[name])))
    w_row = jnp.concatenate([seg(n, k) for n, k in ROW_SEGS], axis=1).astype(MXU_DTYPE)
    w_col = jnp.concatenate([seg(n, k) for n, k in COL_SEGS], axis=1).T.astype(MXU_DTYPE)
    w_kk = w_kv_up[:, :HA * HD].astype(MXU_DTYPE)
    w_kv = w_kv_up[:, HA * HD:].T.astype(MXU_DTYPE)
    wide = lambda v: jnp.broadcast_to(v[..., None], v.shape + (tm,))

    def tables(dim):
        half = dim // ROT_FRACTION // 2
        inv_freq = ROPE_THETA ** (-jnp.arange(half, dtype=F32) / half)
        ang = positions.astype(F32)[..., None] * inv_freq
        return jnp.cos(ang), jnp.sin(ang), half
    def lane_tables(dim):
        cos, sin, half = tables(dim)
        reps = LANES // dim
        z = jnp.zeros((B, S, dim - 2 * half), F32)
        c = jnp.concatenate([cos, cos, z + 1.0], axis=-1)
        sa = jnp.concatenate([jnp.zeros_like(sin), sin, z], axis=-1)
        sb = jnp.concatenate([-sin, jnp.zeros_like(sin), z], axis=-1)
        return [jnp.tile(t, (1, 1, reps)).reshape(T, LANES) for t in (c, sa, sb)]
    c64, sa64, sb64 = lane_tables(HD)
    c32, sa32, sb32 = lane_tables(IDX_DIM)
    cos64, sin64, _ = tables(HD)
    cos_t, sin_t = cos64.transpose(0, 2, 1), sin64.transpose(0, 2, 1)
    cos32, sin32, _ = tables(IDX_DIM)
    c32t = jnp.concatenate([cos32, cos32], axis=-1).transpose(0, 2, 1)
    s32t = jnp.concatenate([-sin32, sin32], axis=-1).transpose(0, 2, 1)

    row_blk = lambda n: pl.BlockSpec((tm, n), lambda i: (i, 0))
    const = lambda shape: pl.BlockSpec(shape, lambda i: (0,) * len(shape))
    per_batch = lambda shape: pl.BlockSpec((1,) + shape, lambda i: (i // per_b,) + (0,) * len(shape))
    tok_cols = lambda rows: pl.BlockSpec((1, rows, tm), lambda i: (i // per_b, 0, i % per_b))
    heads_cols = lambda h, rows: pl.BlockSpec((1, h, rows, tm), lambda i: (i // per_b, 0, 0, i % per_b))
    heads_rows = lambda h, n: pl.BlockSpec((1, h, tm, n), lambda i: (i // per_b, 0, i % per_b, 0))
    bf = MXU_DTYPE
    out_shape = [jax.ShapeDtypeStruct((T, 2 * D), F32),
                 jax.ShapeDtypeStruct((B, HA, HD, S), bf),
                 jax.ShapeDtypeStruct((B, HA, S, HD), bf),
                 jax.ShapeDtypeStruct((B, HA, HD, S), bf),
                 jax.ShapeDtypeStruct((B, IDX_HEADS, IDX_DIM, S), bf),
                 jax.ShapeDtypeStruct((B, IDX_HEADS, S), F32),
                 jax.ShapeDtypeStruct((B, S, IDX_DIM), bf),
                 jax.ShapeDtypeStruct((B, NSA_HEADS, HD, S), bf),
                 jax.ShapeDtypeStruct((B, S, 2 * KVB), F32),
                 jax.ShapeDtypeStruct((B, G, S, HD), bf),
                 jax.ShapeDtypeStruct((B, G, S, HD), bf),
                 jax.ShapeDtypeStruct((B, G, HD, S), bf),
                 jax.ShapeDtypeStruct((B, G, HD, S), bf),
                 jax.ShapeDtypeStruct((B, 3 * NSA_HEADS, S), F32)]
    out_specs = [row_blk(2 * D), heads_cols(HA, HD), heads_rows(HA, HD), heads_cols(HA, HD),
                 heads_cols(IDX_HEADS, IDX_DIM), tok_cols(IDX_HEADS),
                 pl.BlockSpec((1, tm, IDX_DIM), lambda i: (i // per_b, i % per_b, 0)),
                 heads_cols(NSA_HEADS, HD),
                 pl.BlockSpec((1, tm, 2 * KVB), lambda i: (i // per_b, i % per_b, 0)),
                 heads_rows(G, HD), heads_rows(G, HD), heads_cols(G, HD), heads_cols(G, HD),
                 tok_cols(3 * NSA_HEADS)]
    half64 = HD // ROT_FRACTION // 2
    in_specs = [row_blk(D), tok_cols(D), const((1, D)), per_batch((1, D)), per_batch((1, D)),
                const((D, tm)), per_batch((D, tm)), per_batch((D, tm)),
                const((D, ROW_COLS)), const((COL_ROWS, D)),
                const((1, DSA_KV_RANK)), const((DSA_KV_RANK, tm)),
                const((DSA_KV_RANK, HA * HD)), const((HA * HD, DSA_KV_RANK)),
                row_blk(LANES), row_blk(LANES), row_blk(LANES), row_blk(LANES), row_blk(LANES), row_blk(LANES),
                tok_cols(half64), tok_cols(half64), tok_cols(SUBLANES), tok_cols(SUBLANES)]
    return pl.pallas_call(
        _prep_kernel,
        grid=(T // tm,),
        in_specs=in_specs,
        out_specs=out_specs,
        out_shape=out_shape,
        compiler_params=_cparams(("arbitrary",)),
    )(x.reshape(T, D), x.transpose(0, 2, 1), g_norm1.reshape(1, D), mod_sc, mod_sh,
      wide(g_norm1), wide(mod_sc[:, 0]), wide(mod_sh[:, 0]), w_row, w_col,
      g_kv.reshape(1, DSA_KV_RANK), wide(g_kv), w_kk, w_kv,
      c64, sa64, sb64, c32, sa32, sb32, cos_t, sin_t, c32t, s32t)


def _cmp_kernel(f_ref, w1_ref, w2_ref, o_ref):
    hid = _dot(f_ref[...], w1_ref[...])
    hid = hid * jax.nn.sigmoid(hid)
    o_ref[...] = _dot(hid, w2_ref[...])


def _compress(tok, pe, w1, w2):
    B, S, G, HD = tok.shape
    r = CMP_LEN // CMP_STRIDE
    n_chunks = S // CMP_STRIDE
    n_cmp = n_chunks - r + 1
    chunks = tok.reshape(B, n_chunks, CMP_STRIDE, G, HD)
    blocks = jnp.concatenate([chunks[:, i:n_cmp + i] for i in range(r)], axis=2)
    blocks = blocks + pe[:, None, :]
    flat = blocks.transpose(0, 1, 3, 2, 4).reshape(B * n_cmp * G, CMP_LEN * HD)
    rows = flat.shape[0]
    tm = 512
    rows_p = -(-rows // tm) * tm
    flat = jnp.pad(flat, ((0, rows_p - rows), (0, 0))).astype(MXU_DTYPE)
    out = pl.pallas_call(
        _cmp_kernel,
        grid=(rows_p // tm,),
        in_specs=[pl.BlockSpec((tm, CMP_LEN * HD), lambda i: (i, 0)),
                  pl.BlockSpec((CMP_LEN * HD, CMP_HIDDEN), lambda i: (0, 0)),
                  pl.BlockSpec((CMP_HIDDEN, HD), lambda i: (0, 0))],
        out_specs=pl.BlockSpec((tm, HD), lambda i: (i, 0)),
        out_shape=jax.ShapeDtypeStruct((rows_p, HD), F32),
        compiler_params=_cparams(("arbitrary",)),
    )(flat, w1.astype(MXU_DTYPE), w2.astype(MXU_DTYPE))
    return out[:rows].reshape(B, n_cmp, G, HD)


def _select_kernel(ik_ref, iqt_ref, iwt_ref, bias_ref, sc_ref, *, TQ, KC, S, K, NBIS):
    qi = pl.program_id(1)
    q0 = qi * TQ
    n_ch = (q0 + TQ + KC - 1) // KC
    n_all = S // KC
    t_row = q0 + lax.broadcasted_iota(jnp.int32, (1, TQ), 1)
    key_iota = lax.broadcasted_iota(jnp.int32, (KC, TQ), 0)
    kf = float(K)
    SUB = LANES
    sub_iota = lax.broadcasted_iota(jnp.int32, (SUB, TQ), 0)

    def score_chunk(c, carry):
        for u in range(KC // SUB):
            off = pl.multiple_of(c * KC + u * SUB, SUB)
            ikc = ik_ref[0, pl.ds(off, SUB), :]
            acc = jnp.zeros((SUB, TQ), F32)
            for h in range(IDX_HEADS):
                lg = jnp.dot(ikc, iqt_ref[0, h], preferred_element_type=F32)
                acc = acc + jnp.maximum(lg, 0.0) * iwt_ref[0, h:h + 1, :]
            sc_ref[pl.ds(off, SUB), :] = jnp.where(off + sub_iota <= t_row, acc, NEG_INF)
        return carry

    lax.fori_loop(0, n_ch, score_chunk, 0)

    def chunk(c):
        off = pl.multiple_of(c * KC, KC)
        return sc_ref[pl.ds(off, KC), :], off

    def fold(x, op):
        return op(x.reshape(KC // SUBLANES, SUBLANES, TQ), axis=0)

    def key_pass(fn, init):
        def body(c, acc):
            xs, off = chunk(c)
            return fn(acc, xs, off)
        return lax.fori_loop(0, n_ch, body, init)

    def count_ge(thr):
        acc = key_pass(lambda a, xs, off: a + fold(jnp.where(xs >= thr, 1.0, 0.0), jnp.sum),
                       jnp.zeros((SUBLANES, TQ), F32))
        return jnp.sum(acc, axis=0, keepdims=True)

    def count_gt(thr):
        acc = key_pass(lambda a, xs, off: a + fold(jnp.where(xs > thr, 1.0, 0.0), jnp.sum),
                       jnp.zeros((SUBLANES, TQ), F32))
        return jnp.sum(acc, axis=0, keepdims=True)

    def max_where(bound, strict):
        def fn(a, xs, off):
            ok = (xs < bound) if strict else (xs <= bound)
            return jnp.maximum(a, fold(jnp.where(ok, xs, LOWEST), jnp.max))
        acc = key_pass(fn, jnp.full((SUBLANES, TQ), LOWEST, F32))
        return jnp.max(acc, axis=0, keepdims=True)

    def minmax_fn(a, xs, off):
        mn, mx = a
        causal = (off + key_iota) <= t_row
        return (jnp.minimum(mn, fold(jnp.where(causal, xs, -LOWEST), jnp.min)),
                jnp.maximum(mx, fold(xs, jnp.max)))

    mn, mx = key_pass(minmax_fn, (jnp.full((SUBLANES, TQ), -LOWEST, F32),
                                  jnp.full((SUBLANES, TQ), LOWEST, F32)))
    lo = jnp.min(mn, axis=0, keepdims=True)
    hi = jnp.max(mx, axis=0, keepdims=True)

    def bisect(_, carry):
        lo, hi = carry
        mid = 0.5 * (lo + hi)
        ge = count_ge(mid) >= kf
        return jnp.where(ge, mid, lo), jnp.where(ge, hi, mid)

    lo, hi = lax.fori_loop(0, NBIS, bisect, (lo, hi))

    all_keys = jnp.where(t_row < K, 1.0, 0.0)
    v = max_where(hi, strict=False)
    done = jnp.maximum(all_keys, jnp.where(count_ge(v) >= kf, 1.0, 0.0))

    def peel_cond(st):
        return jnp.sum(st[1]) < float(TQ)

    def peel_body(st):
        v, done = st
        v = jnp.where(done > 0.5, v, max_where(v, strict=True))
        done = jnp.maximum(done, jnp.where(count_ge(v) >= kf, 1.0, 0.0))
        return v, done

    v, done = lax.while_loop(peel_cond, peel_body, (v, done))
    thr = jnp.where(all_keys > 0.5, LOWEST, v)
    n_ge = count_ge(thr)
    has_tie = jnp.max(jnp.where((n_ge > kf) & (all_keys < 0.5), 1.0, 0.0)) > 0.5

    def write(off, sel):
        bias_ref[0, pl.ds(off, KC), :] = jnp.where(sel, 0.0, NEG_INF).astype(bias_ref.dtype)

    def fill_tail():
        def body(c, carry):
            write(pl.multiple_of(c * KC, KC), jnp.zeros((KC, TQ), jnp.bool_))
            return carry
        lax.fori_loop(n_ch, n_all, body, 0)

    @pl.when(jnp.logical_not(has_tie))
    def _():
        def body(c, carry):
            xs, off = chunk(c)
            write(off, (xs >= thr) & ((off + key_iota) <= t_row))
            return carry
        lax.fori_loop(0, n_ch, body, 0)
        fill_tail()

    @pl.when(has_tie)
    def _():
        need = kf - count_gt(thr)
        r_i = lax.broadcasted_iota(jnp.int32, (KC, KC), 0)
        c_i = lax.broadcasted_iota(jnp.int32, (KC, KC), 1)
        lower = jnp.where(c_i <= r_i, 1.0, 0.0).astype(MXU_DTYPE)
        def body(c, seen):
            xs, off = chunk(c)
            causal = (off + key_iota) <= t_row
            eq = jnp.where((xs == thr) & causal, 1.0, 0.0)
            rank = jnp.dot(lower, eq.astype(MXU_DTYPE), preferred_element_type=F32) + seen
            write(off, ((xs > thr) & causal) | ((eq > 0.5) & (rank <= need)))
            return seen + jnp.sum(eq, axis=0, keepdims=True)
        lax.fori_loop(0, n_ch, body, jnp.zeros((1, TQ), F32))
        fill_tail()


def _dsa_select(ik, iqt, iwt, K):
    B, S, DI = ik.shape
    H = iqt.shape[1]
    TQ = min(256, S)
    KC = min(256, S)
    kern = functools.partial(_select_kernel, TQ=TQ, KC=KC, S=S, K=K, NBIS=20)
    return pl.pallas_call(
        kern,
        grid=(B, S // TQ),
        in_specs=[pl.BlockSpec((1, S, DI), lambda b, q: (b, 0, 0)),
                  pl.BlockSpec((1, H, DI, TQ), lambda b, q: (b, 0, 0, q)),
                  pl.BlockSpec((1, H, TQ), lambda b, q: (b, 0, q))],
        out_specs=pl.BlockSpec((1, S, TQ), lambda b, q: (b, 0, q)),
        out_shape=jax.ShapeDtypeStruct((B, S, S), jnp.bfloat16),
        scratch_shapes=[pltpu.VMEM((S, TQ), F32)],
        compiler_params=_cparams(("arbitrary", "arbitrary")),
    )(ik, iqt, iwt)


def _cmp_attn_kernel(qt_ref, kc_ref, vct_ref, o_ref, bm_ref, *, TQ, NCP, NBP, N_SEL):
    qi = pl.program_id(1)
    q0 = qi * TQ
    G, J = NSA_KV_GROUPS, NSA_REP
    N = J * TQ
    t_lane = q0 + (lax.broadcasted_iota(jnp.int32, (NCP, N), 1) & (TQ - 1))
    cmp_end = lax.broadcasted_iota(jnp.int32, (NCP, N), 0) * CMP_STRIDE + (CMP_LEN - 1)
    vis = cmp_end <= t_lane
    n_i = lax.broadcasted_iota(jnp.int32, (NBP, NCP), 0) * SEL_BLOCK
    c_i = lax.broadcasted_iota(jnp.int32, (NBP, NCP), 1) * CMP_STRIDE
    overlap_t = jnp.where((c_i <= n_i + SEL_BLOCK - 1) & (c_i + CMP_LEN - 1 >= n_i), 1.0, 0.0)
    blk = lax.broadcasted_iota(jnp.int32, (NBP, TQ), 0)
    tq = q0 + lax.broadcasted_iota(jnp.int32, (NBP, TQ), 1)
    cur = tq // SEL_BLOCK
    admissible = blk * SEL_BLOCK <= tq
    forced = (blk == 0) | (blk == cur) | (blk == cur - 1)
    for g in range(G):
        qt = jnp.concatenate([qt_ref[0, g * J + j] for j in range(J)], axis=1)
        s = jnp.where(vis, _dot(kc_ref[0, g], qt), NEG_INF)
        p = jnp.exp(s - jnp.max(s, axis=0, keepdims=True))
        p = jnp.where(vis, p, 0.0)
        p = p / jnp.maximum(jnp.sum(p, axis=0, keepdims=True), TINY)
        o = _dot(vct_ref[0, g], p)
        for j in range(J):
            o_ref[0, g * J + j] = o[:, j * TQ:(j + 1) * TQ]
        psum = p[:, 0:TQ]
        for j in range(1, J):
            psum = psum + p[:, j * TQ:(j + 1) * TQ]
        imp = _dot_exact_lhs(overlap_t, psum)
        score = jnp.where(admissible & forced, FORCED, jnp.where(admissible, imp, NEG_INF))

        def pick(_, st):
            score, sel = st
            m = jnp.max(score, axis=0, keepdims=True)
            first = jnp.min(jnp.where(score == m, blk, NBP), axis=0, keepdims=True)
            hit = blk == first
            return jnp.where(hit, LOWEST, score), jnp.where(hit, 1.0, sel)

        _, sel = lax.fori_loop(0, N_SEL, pick, (score, jnp.zeros((NBP, TQ), F32)))
        bm_ref[0, g] = sel.astype(bm_ref.dtype)


def _cmp_attn(qt, kc, vct, n_sel, NBP, TQ):
    B, H, HD, S = qt.shape
    G, NCP = kc.shape[1], kc.shape[2]
    kern = functools.partial(_cmp_attn_kernel, TQ=TQ, NCP=NCP, NBP=NBP, N_SEL=n_sel)
    return pl.pallas_call(
        kern,
        grid=(B, S // TQ),
        in_specs=[pl.BlockSpec((1, H, HD, TQ), lambda b, q: (b, 0, 0, q)),
                  pl.BlockSpec((1, G, NCP, HD), lambda b, q: (b, 0, 0, 0)),
                  pl.BlockSpec((1, G, HD, NCP), lambda b, q: (b, 0, 0, 0))],
        out_specs=[pl.BlockSpec((1, H, HD, TQ), lambda b, q: (b, 0, 0, q)),
                   pl.BlockSpec((1, G, NBP, TQ), lambda b, q: (b, 0, 0, q))],
        out_shape=[jax.ShapeDtypeStruct((B, H, HD, S), F32),
                   jax.ShapeDtypeStruct((B, G, NBP, S), jnp.bfloat16)],
        compiler_params=_cparams(("arbitrary", "arbitrary")),
    )(qt, kc, vct)


def _flash_kernel(*refs, mode, G, J, TQ, TK, NWIN):
    if mode == "window":
        qt_ref, k_ref, vt_ref, o_ref, m_ref, l_ref, acc_ref = refs
        x_ref = None
    else:
        qt_ref, k_ref, vt_ref, x_ref, o_ref, m_ref, l_ref, acc_ref = refs
    qi = pl.program_id(1)
    kk = pl.program_id(2)
    q0 = qi * TQ
    last = (q0 + TQ - 1) // TK
    if mode == "window":
        ki = last - (NWIN - 1) + kk
        valid = ki >= 0
    else:
        ki = kk
        valid = kk <= last

    @pl.when(kk == 0)
    def _():
        m_ref[...] = jnp.full(m_ref.shape, NEG_INF, F32)
        l_ref[...] = jnp.zeros(l_ref.shape, F32)
        acc_ref[...] = jnp.zeros(acc_ref.shape, F32)

    @pl.when(valid)
    def _():
        k0 = ki * TK
        kidx = k0 + lax.broadcasted_iota(jnp.int32, (TK, TQ), 0)
        tq = q0 + lax.broadcasted_iota(jnp.int32, (TK, TQ), 1)
        if mode == "mask":
            bias = x_ref[0].astype(F32)
        elif mode == "window":
            bias = jnp.where((kidx <= tq) & (kidx > tq - WINDOW), 0.0, NEG_INF)
        else:
            causal = kidx <= tq
            nbp = x_ref.shape[2]
            blk_of_key = (k0 + lax.broadcasted_iota(jnp.int32, (TK, nbp), 0)) // SEL_BLOCK
            expand_t = jnp.where(lax.broadcasted_iota(jnp.int32, (TK, nbp), 1) == blk_of_key, 1.0, 0.0)
            expand_t = expand_t.astype(MXU_DTYPE)

        for g in range(G):
            if mode == "block":
                picked = jnp.dot(expand_t, x_ref[0, g].astype(MXU_DTYPE), preferred_element_type=F32)
                b = jnp.where(causal & (picked > 0.5), 0.0, NEG_INF)
            else:
                b = bias
            if J > 1:
                b = jnp.concatenate([b] * J, axis=1)
                qt = jnp.concatenate([qt_ref[0, g * J + j] for j in range(J)], axis=1)
            else:
                qt = qt_ref[0, g]
            s = _dot(k_ref[0, g], qt) + b
            m_prev = m_ref[g]
            m_new = jnp.maximum(m_prev, jnp.max(s, axis=0, keepdims=True))
            alpha = jnp.exp(m_prev - m_new)
            p = jnp.exp(s - m_new)
            l_ref[g] = alpha * l_ref[g] + jnp.sum(p, axis=0, keepdims=True)
            acc_ref[g] = alpha * acc_ref[g] + _dot(vt_ref[0, g], p)
            m_ref[g] = m_new

    @pl.when(kk == pl.num_programs(2) - 1)
    def _():
        for g in range(G):
            o = acc_ref[g] / jnp.maximum(l_ref[g], TINY)
            for j in range(J):
                o_ref[0, g * J + j] = o[:, j * TQ:(j + 1) * TQ].astype(o_ref.dtype)


def _flash(qt, k, vt, extra, mode, TQ, TK, out_dtype):
    B, H, HD, S = qt.shape
    G = k.shape[1]
    J = H // G
    N = J * TQ
    nq = S // TQ
    last_of = lambda qi: (qi * TQ + TQ - 1) // TK
    if mode == "window":
        NWIN = min((WINDOW - 1 + TK - 1) // TK + 1, S // TK)
        nk = NWIN
        kidx_of = lambda qi, kk: jnp.maximum(last_of(qi) - (NWIN - 1) + kk, 0)
    else:
        NWIN = 0
        nk = S // TK
        kidx_of = lambda qi, kk: jnp.minimum(kk, last_of(qi))
    in_specs = [pl.BlockSpec((1, H, HD, TQ), lambda b, qi, kk: (b, 0, 0, qi)),
                pl.BlockSpec((1, G, TK, HD), lambda b, qi, kk: (b, 0, kidx_of(qi, kk), 0)),
                pl.BlockSpec((1, G, HD, TK), lambda b, qi, kk: (b, 0, 0, kidx_of(qi, kk)))]
    args = [qt, k, vt]
    if mode == "mask":
        in_specs.append(pl.BlockSpec((1, TK, TQ), lambda b, qi, kk: (b, kidx_of(qi, kk), qi)))
        args.append(extra)
    elif mode == "block":
        nbp = extra.shape[2]
        in_specs.append(pl.BlockSpec((1, G, nbp, TQ), lambda b, qi, kk: (b, 0, 0, qi)))
        args.append(extra)
    kern = functools.partial(_flash_kernel, mode=mode, G=G, J=J, TQ=TQ, TK=TK, NWIN=NWIN)
    return pl.pallas_call(
        kern,
        grid=(B, nq, nk),
        in_specs=in_specs,
        out_specs=pl.BlockSpec((1, H, HD, TQ), lambda b, qi, kk: (b, 0, 0, qi)),
        out_shape=jax.ShapeDtypeStruct((B, H, HD, S), out_dtype),
        scratch_shapes=[pltpu.VMEM((G, 1, N), F32),
                        pltpu.VMEM((G, 1, N), F32),
                        pltpu.VMEM((G, HD, N), F32)],
        compiler_params=_cparams(("arbitrary", "arbitrary", "arbitrary")),
    )(*args)


def _dot_tn(a_t, b):
    return lax.dot_general(a_t.astype(MXU_DTYPE), b.astype(MXU_DTYPE),
                           (((0,), (0,)), ((), ())), preferred_element_type=F32)


def _outproj_kernel(x_ref, oa_ref, oc_ref, os_ref, ow_ref, br_ref, mg_ref, wa_ref, wb_ref, wo_ref,
                    gt_ref, o_ref):
    HD = HEAD_DIM
    D = x_ref.shape[1]
    br = br_ref[0]
    parts = []
    for hh in range(NSA_HEADS):
        parts.append(br[3 * hh:3 * hh + 1] * oc_ref[0, hh] + br[3 * hh + 1:3 * hh + 2] * os_ref[0, hh]
                     + br[3 * hh + 2:3 * hh + 3] * ow_ref[0, hh])
    ob_t = jnp.concatenate(parts, axis=0)
    oa_t = jnp.concatenate([oa_ref[0, hh] for hh in range(DSA_HEADS)], axis=0)
    ua = _dot_tn(oa_t, wa_ref[...])
    ub = _dot_tn(ob_t, wb_ref[...])
    mg = mg_ref[...]
    merged = jax.nn.sigmoid(mg[:, :D]) * ua + jax.nn.sigmoid(mg[:, D:]) * ub
    o_ref[...] = x_ref[...] + gt_ref[0] * _dot(merged, wo_ref[...])


def _out_proj(x2d, oa, oc, os_, ow, br, mg, wa, wb, wo, gt, S, tm=256):
    T, D = x2d.shape
    B, H, HD, _ = oa.shape
    per_b = S // tm
    heads_cols = pl.BlockSpec((1, H, HD, tm), lambda i: (i // per_b, 0, 0, i % per_b))
    return pl.pallas_call(
        _outproj_kernel,
        grid=(T // tm,),
        in_specs=[pl.BlockSpec((tm, D), lambda i: (i, 0)),
                  heads_cols, heads_cols, heads_cols, heads_cols,
                  pl.BlockSpec((1, br.shape[1], tm), lambda i: (i // per_b, 0, i % per_b)),
                  pl.BlockSpec((tm, 2 * D), lambda i: (i, 0)),
                  pl.BlockSpec((H * HD, D), lambda i: (0, 0)),
                  pl.BlockSpec((H * HD, D), lambda i: (0, 0)),
                  pl.BlockSpec((D, D), lambda i: (0, 0)),
                  pl.BlockSpec((1, 1, D), lambda i: (i // per_b, 0, 0))],
        out_specs=pl.BlockSpec((tm, D), lambda i: (i, 0)),
        out_shape=jax.ShapeDtypeStruct((T, D), F32),
        compiler_params=_cparams(("arbitrary",)),
    )(x2d, oa, oc, os_, ow, br, mg, wa, wb, wo, gt)


def _moe_kernel(x_ref, g2_ref, sc_ref, sh_ref, gt_ref, wr_ref, br_ref, wg_ref, wu_ref, wd_ref,
                gf_ref, o_ref, h_ref, comb_ref, acc_ref, *, TM):
    e = pl.program_id(1)
    lane = lax.broadcasted_iota(jnp.int32, (TM, LANES), 1)

    @pl.when(e == 0)
    def _():
        h = _rms(x_ref[...], g2_ref[...]) * (1.0 + sc_ref[0]) + sh_ref[0]
        h_ref[...] = h.astype(h_ref.dtype)
        h_hi = h.astype(MXU_DTYPE)
        h_lo = (h - h_hi.astype(F32)).astype(MXU_DTYPE)
        w = wr_ref[...]
        w_hi = w.astype(MXU_DTYPE)
        w_lo = (w - w_hi.astype(F32)).astype(MXU_DTYPE)
        d = lambda a, b: jnp.dot(a, b, preferred_element_type=F32)
        logits = d(h_hi, w_hi) + d(h_hi, w_lo) + d(h_lo, w_hi) + br_ref[...]
        is_e = lane < N_EXPERTS
        is_g = (lane >= N_EXPERTS) & (lane < N_EXPERTS + N_GROUPS)
        lg = jnp.where(is_g, logits, LOWEST)
        mg = jnp.max(lg, axis=-1, keepdims=True)
        gsel = jnp.min(jnp.where(is_g & (lg == mg), lane - N_EXPERTS, N_GROUPS), axis=-1, keepdims=True)
        pg_sel = 1.0 / jnp.sum(jnp.where(is_g, jnp.exp(lg - mg), 0.0), axis=-1, keepdims=True)
        in_grp = is_e & ((lane // EXPERTS_PER_GROUP) == gsel)
        le = jnp.where(in_grp, logits, LOWEST)
        me = jnp.max(le, axis=-1, keepdims=True)
        ex = jnp.where(in_grp, jnp.exp(le - me), 0.0)
        pe = ex / jnp.sum(ex, axis=-1, keepdims=True)
        pe = jnp.where(in_grp, pe, -1.0)
        p1 = jnp.max(pe, axis=-1, keepdims=True)
        i1 = jnp.min(jnp.where(pe == p1, lane, LANES), axis=-1, keepdims=True)
        pe2 = jnp.where(lane == i1, -1.0, pe)
        p2 = jnp.max(pe2, axis=-1, keepdims=True)
        i2 = jnp.min(jnp.where(pe2 == p2, lane, LANES), axis=-1, keepdims=True)
        tot = p1 + p2
        comb = jnp.where(lane == i1, p1 / tot * pg_sel, 0.0) + jnp.where(lane == i2, p2 / tot * pg_sel, 0.0)
        comb_ref[...] = comb
        acc_ref[...] = jnp.zeros(acc_ref.shape, F32)

    h = h_ref[...]
    a = jnp.dot(h, wg_ref[0], preferred_element_type=F32)
    u = jnp.dot(h, wu_ref[0], preferred_element_type=F32)
    y = _dot(a * jax.nn.sigmoid(a) * u, wd_ref[0])
    w_e = jnp.sum(jnp.where(lane == e, comb_ref[...], 0.0), axis=-1, keepdims=True)
    acc_ref[...] += w_e * y

    @pl.when(e == pl.num_programs(1) - 1)
    def _():
        x2 = x_ref[...] + gt_ref[0] * acc_ref[...]
        o_ref[...] = _rms(x2, gf_ref[...])


def _moe(x1, g2, sc, sh, gt, wr, br, wg, wu, wd, gf, S, tm=512):
    T, D = x1.shape
    E, _, DE = wg.shape
    per_b = S // tm
    kern = functools.partial(_moe_kernel, TM=tm)
    return pl.pallas_call(
        kern,
        grid=(T // tm, E),
        in_specs=[pl.BlockSpec((tm, D), lambda i, e: (i, 0)),
                  pl.BlockSpec((1, D), lambda i, e: (0, 0)),
                  pl.BlockSpec((1, 1, D), lambda i, e: (i // per_b, 0, 0)),
                  pl.BlockSpec((1, 1, D), lambda i, e: (i // per_b, 0, 0)),
                  pl.BlockSpec((1, 1, D), lambda i, e: (i // per_b, 0, 0)),
                  pl.BlockSpec((D, LANES), lambda i, e: (0, 0)),
                  pl.BlockSpec((1, LANES), lambda i, e: (0, 0)),
                  pl.BlockSpec((1, D, DE), lambda i, e: (e, 0, 0)),
                  pl.BlockSpec((1, D, DE), lambda i, e: (e, 0, 0)),
                  pl.BlockSpec((1, DE, D), lambda i, e: (e, 0, 0)),
                  pl.BlockSpec((1, D), lambda i, e: (0, 0))],
        out_specs=pl.BlockSpec((tm, D), lambda i, e: (i, 0)),
        out_shape=jax.ShapeDtypeStruct((T, D), F32),
        scratch_shapes=[pltpu.VMEM((tm, D), MXU_DTYPE),
                        pltpu.VMEM((tm, LANES), F32),
                        pltpu.VMEM((tm, D), F32)],
        compiler_params=_cparams(("arbitrary", "arbitrary")),
    )(x1, g2.reshape(1, D), sc, sh, gt, wr, br, wg, wu, wd, gf.reshape(1, D))


def _layer(x, mod, positions, g_norm1, w_in, g_kv_latent, w_kv_up, pe_cmp_k, pe_cmp_v,
           w_cmp1_k, w_cmp2_k, w_cmp1_v, w_cmp2_v, w_up_a, w_up_b, w_out, g_norm2,
           w_router_group, b_router_group, w_router_expert, b_router_expert,
           w_expert_gate, w_expert_up, w_expert_down, g_out):
    B, S, D = x.shape
    T = B * S
    HD, G, J = HEAD_DIM, NSA_KV_GROUPS, NSA_REP
    HA = DSA_HEADS * HD
    kvb = G * HD
    scale = HD ** -0.5
    topk_a = min(DSA_TOPK_MAX, S // 4)
    n_sel = min(SEL_COUNT, S // SEL_BLOCK)
    mod6 = mod.reshape(B, 6, 1, D)
    sh1, sc1, gt1, sh2, sc2, gt2 = (mod6[:, i] for i in range(6))

    x2d = x.reshape(T, D)
    (mg, qt_a, k_a, vt_a, iqt, iwt, ik, qt_b, kcvc, ks, kw, vst, vwt, br) = _prep(
        x, sc1, sh1, g_norm1, w_in, g_kv_latent, w_kv_up, positions)

    kc = _compress(kcvc[..., :kvb].reshape(B, S, G, HD), pe_cmp_k, w_cmp1_k, w_cmp2_k)
    vc = _compress(kcvc[..., kvb:].reshape(B, S, G, HD), pe_cmp_v, w_cmp1_v, w_cmp2_v)
    n_cmp = kc.shape[1]
    ncp = -(-(n_cmp + 1) // LANES) * LANES
    pad_c = lambda t: jnp.pad(t, ((0, 0), (0, ncp - n_cmp), (0, 0), (0, 0))).astype(MXU_DTYPE)
    kc, vct = pad_c(kc).transpose(0, 2, 1, 3), pad_c(vc).transpose(0, 2, 3, 1)

    TA = min(512, S)
    sel_bias = _dsa_select(ik, iqt, iwt, topk_a)
    o_a = _flash(qt_a, k_a, vt_a, sel_bias, "mask", TA, TA, MXU_DTYPE)

    TB = min(256, S)
    nbp = -(-(S // SEL_BLOCK) // LANES) * LANES
    o_c, blk_mask = _cmp_attn(qt_b, kc, vct, n_sel, nbp, TB)
    o_s = _flash(qt_b, ks, vst, blk_mask, "block", TB, min(512, S), F32)
    o_w = _flash(qt_b, kw, vwt, None, "window", TB, TB, F32)

    x1 = _out_proj(x2d, o_a, o_c, o_s, o_w, br, mg, w_up_a.astype(MXU_DTYPE),
                   w_up_b.astype(MXU_DTYPE), w_out.astype(MXU_DTYPE), gt1, S)

    wr = jnp.concatenate([w_router_expert, w_router_group], axis=1)
    wr = jnp.pad(wr, ((0, 0), (0, LANES - wr.shape[1])))
    brt = jnp.concatenate([b_router_expert, b_router_group])
    brt = jnp.pad(brt, (0, LANES - brt.shape[0])).reshape(1, LANES)
    out = _moe(x1, g_norm2, sc2, sh2, gt2, wr, brt, w_expert_gate.astype(MXU_DTYPE),
               w_expert_up.astype(MXU_DTYPE), w_expert_down.astype(MXU_DTYPE), g_out, S)
    return out.reshape(B, S, D)


def kernel(x, c, positions, w_ada, b_ada, g_norm1, w_in, g_kv_latent, w_kv_up, pe_cmp_k, pe_cmp_v,
           w_cmp1_k, w_cmp2_k, w_cmp1_v, w_cmp2_v, w_up_a, w_up_b, w_out, g_norm2, w_router_group,
           b_router_group, w_router_expert, b_router_expert, w_expert_gate, w_expert_up,
           w_expert_down, g_final):
    depth = w_ada.shape[0]
    assert depth == 1, "the fused final norm assumes a single layer"
    mod = _ada_mod(c, w_ada[0], b_ada[0])
    return _layer(x, mod, positions, g_norm1[0], w_in[0], g_kv_latent[0], w_kv_up[0], pe_cmp_k[0],
                  pe_cmp_v[0], w_cmp1_k[0], w_cmp2_k[0], w_cmp1_v[0], w_cmp2_v[0], w_up_a[0],
                  w_up_b[0], w_out[0], g_norm2[0], w_router_group[0], b_router_group[0],
                  w_router_expert[0], b_router_expert[0], w_expert_gate[0], w_expert_up[0],
                  w_expert_down[0], g_final)
```

```python
import functools

import numpy as np
import jax
import jax.numpy as jnp
from jax import lax
from jax.experimental import pallas as pl
from jax.experimental.pallas import tpu as pltpu

HEAD_DIM = 64
ROT_FRACTION = 4
ROPE_THETA = 500000.0
DSA_HEADS = 8
DSA_KV_RANK = 128
IDX_HEADS = 8
IDX_DIM = 32
DSA_TOPK_MAX = 256
NSA_HEADS = 8
NSA_KV_GROUPS = 2
NSA_REP = NSA_HEADS // NSA_KV_GROUPS
CMP_LEN = 32
CMP_STRIDE = 16
CMP_HIDDEN = 256
SEL_BLOCK = 64
SEL_COUNT = 16
WINDOW = 512
N_GROUPS = 4
EXPERTS_PER_GROUP = 8
N_EXPERTS = N_GROUPS * EXPERTS_PER_GROUP
D_EXPERT = 256
NORM_EPS = 1e-6
NEG_INF = -1e30
TINY = 1e-30
LOWEST = -3.0e38
FORCED = 1e30

LANES = 128
SUBLANES = 8
ONES_ROWS = 16
MXU_DTYPE = jnp.bfloat16
VMEM_LIMIT = 56 * 1024 * 1024

F32 = jnp.float32


def _cparams(sem):
    return pltpu.CompilerParams(dimension_semantics=sem, vmem_limit_bytes=VMEM_LIMIT)


def _dot(a, b):
    return jnp.dot(a.astype(MXU_DTYPE), b.astype(MXU_DTYPE), preferred_element_type=F32)


def _dot_exact_lhs(a01, b):
    hi = b.astype(MXU_DTYPE)
    r1 = b - hi.astype(F32)
    mid = r1.astype(MXU_DTYPE)
    lo = (r1 - mid.astype(F32)).astype(MXU_DTYPE)
    a = a01.astype(MXU_DTYPE)
    d = lambda u: jnp.dot(a, u, preferred_element_type=F32)
    return d(hi) + d(mid) + d(lo)


def _rms(x, g):
    return x * lax.rsqrt(jnp.mean(x * x, axis=-1, keepdims=True) + NORM_EPS) * g


def _ada_kernel(c_ref, w_ref, b_ref, o_ref):
    c = c_ref[...]
    cond = c * jax.nn.sigmoid(c)
    o_ref[...] = _dot(cond, w_ref[...]) + b_ref[...]


def _ada_mod(c, w_ada, b_ada):
    B, D = c.shape
    n_out = w_ada.shape[1]
    rows = SUBLANES
    cp = jnp.zeros((rows, D), F32).at[:B].set(c)
    tn = 1024
    out = pl.pallas_call(
        _ada_kernel,
        grid=(n_out // tn,),
        in_specs=[pl.BlockSpec((rows, D), lambda j: (0, 0)),
                  pl.BlockSpec((D, tn), lambda j: (0, j)),
                  pl.BlockSpec((1, tn), lambda j: (0, j))],
        out_specs=pl.BlockSpec((rows, tn), lambda j: (0, j)),
        out_shape=jax.ShapeDtypeStruct((rows, n_out), F32),
        compiler_params=_cparams(("arbitrary",)),
    )(cp, w_ada, b_ada.reshape(1, n_out))
    return out[:B]


KVB = NSA_KV_GROUPS * HEAD_DIM
ROW_SEGS = (("mg", 2048), ("lat", DSA_KV_RANK), ("kc", KVB), ("vc", KVB), ("ks", KVB), ("kw", KVB),
            ("ik", LANES))
COL_SEGS = (("qa", DSA_HEADS * HEAD_DIM), ("qb", NSA_HEADS * HEAD_DIM), ("iq", IDX_HEADS * IDX_DIM),
            ("vs", KVB), ("vw", KVB), ("lat", DSA_KV_RANK), ("iw", IDX_HEADS), ("br", 3 * NSA_HEADS))


def _seg_offsets(segs):
    out, pos = {}, 0
    for name, n in segs:
        out[name] = pos
        pos += n
    return out, pos


ROW_OFF, ROW_COLS = _seg_offsets(ROW_SEGS)
COL_OFF, COL_ROWS = _seg_offsets(COL_SEGS)


def _rot_lanes(x, c, sa, sb, half):
    outs = []
    for j in range(x.shape[1] // LANES):
        xs = x[:, j * LANES:(j + 1) * LANES]
        outs.append(xs * c + pltpu.roll(xs, half, 1) * sa + pltpu.roll(xs, LANES - half, 1) * sb)
    return outs[0] if len(outs) == 1 else jnp.concatenate(outs, axis=1)


def _prep_kernel(x_ref, xt_ref, g_ref, sc_ref, sh_ref, gc_ref, scc_ref, shc_ref, wr_ref, wc_ref,
                 gkv_ref, gkvc_ref, wkk_ref, wkv_ref, c64_ref, sa64_ref, sb64_ref,
                 c32_ref, sa32_ref, sb32_ref, cos_ref, sin_ref, c32t_ref, s32t_ref,
                 mg_ref, qa_ref, ka_ref, va_ref, iq_ref, iw_ref, ik_ref, qb_ref, kcvc_ref,
                 ks_ref, kw_ref, vs_ref, vw_ref, br_ref):
    HD = HEAD_DIM
    half = HD // ROT_FRACTION // 2
    scale = HD ** -0.5
    h = _rms(x_ref[...], g_ref[...]) * (1.0 + sc_ref[0]) + sh_ref[0]
    pr = _dot(h, wr_ref[...])
    row = lambda name, n: pr[:, ROW_OFF[name]:ROW_OFF[name] + n]
    mg_ref[...] = row("mg", 2048)
    c64, sa64, sb64 = c64_ref[...], sa64_ref[...], sb64_ref[...]
    ka = _rot_lanes(_dot(_rms(row("lat", DSA_KV_RANK), gkv_ref[...]), wkk_ref[...]), c64, sa64, sb64, half)
    for hh in range(DSA_HEADS):
        ka_ref[0, hh] = ka[:, hh * HD:(hh + 1) * HD].astype(ka_ref.dtype)
    kcvc_ref[0] = jnp.concatenate([_rot_lanes(row("kc", KVB), c64, sa64, sb64, half), row("vc", KVB)], axis=1)
    ks = _rot_lanes(row("ks", KVB), c64, sa64, sb64, half)
    kw = _rot_lanes(row("kw", KVB), c64, sa64, sb64, half)
    for g in range(NSA_KV_GROUPS):
        ks_ref[0, g] = ks[:, g * HD:(g + 1) * HD].astype(ks_ref.dtype)
        kw_ref[0, g] = kw[:, g * HD:(g + 1) * HD].astype(kw_ref.dtype)
    ik = _rot_lanes(row("ik", LANES), c32_ref[...], sa32_ref[...], sb32_ref[...], IDX_DIM // ROT_FRACTION // 2)
    ik_ref[0] = ik[:, :IDX_DIM].astype(ik_ref.dtype)
    xt = xt_ref[0]
    ht = xt * lax.rsqrt(jnp.mean(xt * xt, axis=0, keepdims=True) + NORM_EPS) * gc_ref[...]
    ht = ht * (1.0 + scc_ref[0]) + shc_ref[0]
    pc = _dot(wc_ref[...], ht)
    col = lambda name, n: pc[COL_OFF[name]:COL_OFF[name] + n]
    cos, sin = cos_ref[0], sin_ref[0]

    def rot_rows(blk):
        x1, x2 = blk[0:half], blk[half:2 * half]
        return jnp.concatenate([x1 * cos - x2 * sin, x2 * cos + x1 * sin, blk[2 * half:]], axis=0)

    qa, qb = col("qa", DSA_HEADS * HD), col("qb", NSA_HEADS * HD)
    for hh in range(DSA_HEADS):
        qa_ref[0, hh] = (rot_rows(qa[hh * HD:(hh + 1) * HD]) * scale).astype(qa_ref.dtype)
    for hh in range(NSA_HEADS):
        qb_ref[0, hh] = (rot_rows(qb[hh * HD:(hh + 1) * HD]) * scale).astype(qb_ref.dtype)
    iq = col("iq", IDX_HEADS * IDX_DIM)
    c32t, s32t = c32t_ref[0], s32t_ref[0]
    for hh in range(IDX_HEADS):
        blk = iq[hh * IDX_DIM:(hh + 1) * IDX_DIM]
        top = blk[0:SUBLANES]
        top = top * c32t + pltpu.roll(top, SUBLANES // 2, 0) * s32t
        iq_ref[0, hh] = jnp.concatenate([top, blk[SUBLANES:]], axis=0).astype(iq_ref.dtype)
    ones = jnp.ones((ONES_ROWS, xt.shape[1]), F32)
    with_ones = lambda v: jnp.concatenate([v, ones], axis=0)
    vs, vw = col("vs", KVB), col("vw", KVB)
    for g in range(NSA_KV_GROUPS):
        vs_ref[0, g] = with_ones(vs[g * HD:(g + 1) * HD]).astype(vs_ref.dtype)
        vw_ref[0, g] = with_ones(vw[g * HD:(g + 1) * HD]).astype(vw_ref.dtype)
    lat = col("lat", DSA_KV_RANK)
    lat = lat * lax.rsqrt(jnp.mean(lat * lat, axis=0, keepdims=True) + NORM_EPS) * gkvc_ref[...]
    va = _dot(wkv_ref[...], lat)
    for hh in range(DSA_HEADS):
        va_ref[0, hh] = with_ones(va[hh * HD:(hh + 1) * HD]).astype(va_ref.dtype)
    iw_ref[0] = col("iw", IDX_HEADS)
    br_ref[0] = jax.nn.sigmoid(col("br", 3 * NSA_HEADS))


def _prep(x, mod_sc, mod_sh, g_norm1, w_in, g_kv, w_kv_up, positions, tm=256):
    B, S, D = x.shape
    T = B * S
    HD, HA, G = HEAD_DIM, DSA_HEADS, NSA_KV_GROUPS
    per_b = S // tm
    sizes = [HA * HD, DSA_KV_RANK, IDX_HEADS * IDX_DIM, IDX_HEADS, IDX_DIM, NSA_HEADS * HD,
             KVB, KVB, KVB, KVB, KVB, KVB, 3 * NSA_HEADS, 2 * D]
    names = ["qa", "lat", "iq", "iw", "ik", "qb", "kc", "vc", "ks", "vs", "kw", "vw", "br", "mg"]
    starts = dict(zip(names, np.concatenate([[0], np.cumsum(sizes)[:-1]]).astype(int)))
    width = dict(zip(names, sizes))
    def seg(name, n):
        w = w_in[:, starts[name]:starts[name] + width[name]]
        return jnp.pad(w, ((0, 0), (0, n - width[name])))
    w_row = jnp.concatenate([seg(n, k) for n, k in ROW_SEGS], axis=1).astype(MXU_DTYPE)
    w_col = jnp.concatenate([seg(n, k) for n, k in COL_SEGS], axis=1).T.astype(MXU_DTYPE)
    w_kk = w_kv_up[:, :HA * HD].astype(MXU_DTYPE)
    w_kv = w_kv_up[:, HA * HD:].T.astype(MXU_DTYPE)
    wide = lambda v: jnp.broadcast_to(v[..., None], v.shape + (tm,))

    def tables(dim):
        half = dim // ROT_FRACTION // 2
        inv_freq = ROPE_THETA ** (-jnp.arange(half, dtype=F32) / half)
        ang = positions.astype(F32)[..., None] * inv_freq
        return jnp.cos(ang), jnp.sin(ang), half
    def lane_tables(dim):
        cos, sin, half = tables(dim)
        reps = LANES // dim
        z = jnp.zeros((B, S, dim - 2 * half), F32)
        c = jnp.concatenate([cos, cos, z + 1.0], axis=-1)
        sa = jnp.concatenate([jnp.zeros_like(sin), sin, z], axis=-1)
        sb = jnp.concatenate([-sin, jnp.zeros_like(sin), z], axis=-1)
        return [jnp.tile(t, (1, 1, reps)).reshape(T, LANES) for t in (c, sa, sb)]
    c64, sa64, sb64 = lane_tables(HD)
    c32, sa32, sb32 = lane_tables(IDX_DIM)
    cos64, sin64, _ = tables(HD)
    cos_t, sin_t = cos64.transpose(0, 2, 1), sin64.transpose(0, 2, 1)
    cos32, sin32, _ = tables(IDX_DIM)
    c32t = jnp.concatenate([cos32, cos32], axis=-1).transpose(0, 2, 1)
    s32t = jnp.concatenate([-sin32, sin32], axis=-1).transpose(0, 2, 1)

    row_blk = lambda n: pl.BlockSpec((tm, n), lambda i: (i, 0))
    const = lambda shape: pl.BlockSpec(shape, lambda i: (0,) * len(shape))
    per_batch = lambda shape: pl.BlockSpec((1,) + shape, lambda i: (i // per_b,) + (0,) * len(shape))
    tok_cols = lambda rows: pl.BlockSpec((1, rows, tm), lambda i: (i // per_b, 0, i % per_b))
    heads_cols = lambda h, rows: pl.BlockSpec((1, h, rows, tm), lambda i: (i // per_b, 0, 0, i % per_b))
    heads_rows = lambda h, n: pl.BlockSpec((1, h, tm, n), lambda i: (i // per_b, 0, i % per_b, 0))
    bf = MXU_DTYPE
    out_shape = [jax.ShapeDtypeStruct((T, 2 * D), F32),
                 jax.ShapeDtypeStruct((B, HA, HD, S), bf),
                 jax.ShapeDtypeStruct((B, HA, S, HD), bf),
                 jax.ShapeDtypeStruct((B, HA, HD + ONES_ROWS, S), bf),
                 jax.ShapeDtypeStruct((B, IDX_HEADS, IDX_DIM, S), bf),
                 jax.ShapeDtypeStruct((B, IDX_HEADS, S), F32),
                 jax.ShapeDtypeStruct((B, S, IDX_DIM), bf),
                 jax.ShapeDtypeStruct((B, NSA_HEADS, HD, S), bf),
                 jax.ShapeDtypeStruct((B, S, 2 * KVB), F32),
                 jax.ShapeDtypeStruct((B, G, S, HD), bf),
                 jax.ShapeDtypeStruct((B, G, S, HD), bf),
                 jax.ShapeDtypeStruct((B, G, HD + ONES_ROWS, S), bf),
                 jax.ShapeDtypeStruct((B, G, HD + ONES_ROWS, S), bf),
                 jax.ShapeDtypeStruct((B, 3 * NSA_HEADS, S), F32)]
    out_specs = [row_blk(2 * D), heads_cols(HA, HD), heads_rows(HA, HD), heads_cols(HA, HD + ONES_ROWS),
                 heads_cols(IDX_HEADS, IDX_DIM), tok_cols(IDX_HEADS),
                 pl.BlockSpec((1, tm, IDX_DIM), lambda i: (i // per_b, i % per_b, 0)),
                 heads_cols(NSA_HEADS, HD),
                 pl.BlockSpec((1, tm, 2 * KVB), lambda i: (i // per_b, i % per_b, 0)),
                 heads_rows(G, HD), heads_rows(G, HD), heads_cols(G, HD + ONES_ROWS),
                 heads_cols(G, HD + ONES_ROWS),
                 tok_cols(3 * NSA_HEADS)]
    half64 = HD // ROT_FRACTION // 2
    in_specs = [row_blk(D), tok_cols(D), const((1, D)), per_batch((1, D)), per_batch((1, D)),
                const((D, tm)), per_batch((D, tm)), per_batch((D, tm)),
                const((D, ROW_COLS)), const((COL_ROWS, D)),
                const((1, DSA_KV_RANK)), const((DSA_KV_RANK, tm)),
                const((DSA_KV_RANK, HA * HD)), const((HA * HD, DSA_KV_RANK)),
                row_blk(LANES), row_blk(LANES), row_blk(LANES), row_blk(LANES), row_blk(LANES), row_blk(LANES),
                tok_cols(half64), tok_cols(half64), tok_cols(SUBLANES), tok_cols(SUBLANES)]
    return pl.pallas_call(
        _prep_kernel,
        grid=(T // tm,),
        in_specs=in_specs,
        out_specs=out_specs,
        out_shape=out_shape,
        compiler_params=_cparams(("arbitrary",)),
    )(x.reshape(T, D), x.transpose(0, 2, 1), g_norm1.reshape(1, D), mod_sc, mod_sh,
      wide(g_norm1), wide(mod_sc[:, 0]), wide(mod_sh[:, 0]), w_row, w_col,
      g_kv.reshape(1, DSA_KV_RANK), wide(g_kv), w_kk, w_kv,
      c64, sa64, sb64, c32, sa32, sb32, cos_t, sin_t, c32t, s32t)


def _cmp_kernel(f_ref, w1_ref, w2_ref, o_ref):
    hid = _dot(f_ref[...], w1_ref[...])
    hid = hid * jax.nn.sigmoid(hid)
    o_ref[...] = _dot(hid, w2_ref[...])


def _compress(tok, pe, w1, w2):
    B, S, G, HD = tok.shape
    r = CMP_LEN // CMP_STRIDE
    n_chunks = S // CMP_STRIDE
    n_cmp = n_chunks - r + 1
    chunks = tok.reshape(B, n_chunks, CMP_STRIDE, G, HD)
    blocks = jnp.concatenate([chunks[:, i:n_cmp + i] for i in range(r)], axis=2)
    blocks = blocks + pe[:, None, :]
    flat = blocks.transpose(0, 1, 3, 2, 4).reshape(B * n_cmp * G, CMP_LEN * HD)
    rows = flat.shape[0]
    tm = 512
    rows_p = -(-rows // tm) * tm
    flat = jnp.pad(flat, ((0, rows_p - rows), (0, 0))).astype(MXU_DTYPE)
    out = pl.pallas_call(
        _cmp_kernel,
        grid=(rows_p // tm,),
        in_specs=[pl.BlockSpec((tm, CMP_LEN * HD), lambda i: (i, 0)),
                  pl.BlockSpec((CMP_LEN * HD, CMP_HIDDEN), lambda i: (0, 0)),
                  pl.BlockSpec((CMP_HIDDEN, HD), lambda i: (0, 0))],
        out_specs=pl.BlockSpec((tm, HD), lambda i: (i, 0)),
        out_shape=jax.ShapeDtypeStruct((rows_p, HD), F32),
        compiler_params=_cparams(("arbitrary",)),
    )(flat, w1.astype(MXU_DTYPE), w2.astype(MXU_DTYPE))
    return out[:rows].reshape(B, n_cmp, G, HD)


def _select_kernel(ik_ref, iqt_ref, iwt_ref, bias_ref, sc_ref, *, TQ, KC, S, K, NBIS):
    qi = pl.program_id(1)
    q0 = qi * TQ
    n_ch = (q0 + TQ + KC - 1) // KC
    n_all = S // KC
    t_row = q0 + lax.broadcasted_iota(jnp.int32, (1, TQ), 1)
    key_iota = lax.broadcasted_iota(jnp.int32, (KC, TQ), 0)
    kf = float(K)
    SUB = LANES
    sub_iota = lax.broadcasted_iota(jnp.int32, (SUB, TQ), 0)

    def score_chunk(c, carry):
        for u in range(KC // SUB):
            off = pl.multiple_of(c * KC + u * SUB, SUB)
            ikc = ik_ref[0, pl.ds(off, SUB), :]
            acc = jnp.zeros((SUB, TQ), F32)
            for h in range(IDX_HEADS):
                lg = jnp.dot(ikc, iqt_ref[0, h], preferred_element_type=F32)
                acc = acc + jnp.maximum(lg, 0.0) * iwt_ref[0, h:h + 1, :]
            sc_ref[pl.ds(off, SUB), :] = jnp.where(off + sub_iota <= t_row, acc, NEG_INF)
        return carry

    lax.fori_loop(0, n_ch, score_chunk, 0)

    def chunk(c):
        off = pl.multiple_of(c * KC, KC)
        return sc_ref[pl.ds(off, KC), :], off

    def fold(x, op):
        return op(x.reshape(KC // SUBLANES, SUBLANES, TQ), axis=0)

    def key_pass(fn, init):
        def body(c, acc):
            xs, off = chunk(c)
            return fn(acc, xs, off)
        return lax.fori_loop(0, n_ch, body, init)

    def count_ge(thr):
        acc = key_pass(lambda a, xs, off: a + fold(jnp.where(xs >= thr, 1.0, 0.0), jnp.sum),
                       jnp.zeros((SUBLANES, TQ), F32))
        return jnp.sum(acc, axis=0, keepdims=True)

    def count_gt(thr):
        acc = key_pass(lambda a, xs, off: a + fold(jnp.where(xs > thr, 1.0, 0.0), jnp.sum),
                       jnp.zeros((SUBLANES, TQ), F32))
        return jnp.sum(acc, axis=0, keepdims=True)

    def max_where(bound, strict):
        def fn(a, xs, off):
            ok = (xs < bound) if strict else (xs <= bound)
            return jnp.maximum(a, fold(jnp.where(ok, xs, LOWEST), jnp.max))
        acc = key_pass(fn, jnp.full((SUBLANES, TQ), LOWEST, F32))
        return jnp.max(acc, axis=0, keepdims=True)

    def minmax_fn(a, xs, off):
        mn, mx = a
        causal = (off + key_iota) <= t_row
        return (jnp.minimum(mn, fold(jnp.where(causal, xs, -LOWEST), jnp.min)),
                jnp.maximum(mx, fold(xs, jnp.max)))

    mn, mx = key_pass(minmax_fn, (jnp.full((SUBLANES, TQ), -LOWEST, F32),
                                  jnp.full((SUBLANES, TQ), LOWEST, F32)))
    lo = jnp.min(mn, axis=0, keepdims=True)
    hi = jnp.max(mx, axis=0, keepdims=True)

    def bisect(_, carry):
        lo, hi = carry
        mid = 0.5 * (lo + hi)
        ge = count_ge(mid) >= kf
        return jnp.where(ge, mid, lo), jnp.where(ge, hi, mid)

    lo, hi = lax.fori_loop(0, NBIS, bisect, (lo, hi))

    all_keys = jnp.where(t_row < K, 1.0, 0.0)
    v = max_where(hi, strict=False)
    done = jnp.maximum(all_keys, jnp.where(count_ge(v) >= kf, 1.0, 0.0))

    def peel_cond(st):
        return jnp.sum(st[1]) < float(TQ)

    def peel_body(st):
        v, done = st
        v = jnp.where(done > 0.5, v, max_where(v, strict=True))
        done = jnp.maximum(done, jnp.where(count_ge(v) >= kf, 1.0, 0.0))
        return v, done

    v, done = lax.while_loop(peel_cond, peel_body, (v, done))
    thr = jnp.where(all_keys > 0.5, LOWEST, v)
    n_ge = count_ge(thr)
    has_tie = jnp.max(jnp.where((n_ge > kf) & (all_keys < 0.5), 1.0, 0.0)) > 0.5

    def write(off, sel):
        bias_ref[0, pl.ds(off, KC), :] = jnp.where(sel, 0.0, NEG_INF).astype(bias_ref.dtype)

    def fill_tail():
        def body(c, carry):
            write(pl.multiple_of(c * KC, KC), jnp.zeros((KC, TQ), jnp.bool_))
            return carry
        lax.fori_loop(n_ch, n_all, body, 0)

    @pl.when(jnp.logical_not(has_tie))
    def _():
        def body(c, carry):
            xs, off = chunk(c)
            write(off, (xs >= thr) & ((off + key_iota) <= t_row))
            return carry
        lax.fori_loop(0, n_ch, body, 0)
        fill_tail()

    @pl.when(has_tie)
    def _():
        need = kf - count_gt(thr)
        r_i = lax.broadcasted_iota(jnp.int32, (KC, KC), 0)
        c_i = lax.broadcasted_iota(jnp.int32, (KC, KC), 1)
        lower = jnp.where(c_i <= r_i, 1.0, 0.0).astype(MXU_DTYPE)
        def body(c, seen):
            xs, off = chunk(c)
            causal = (off + key_iota) <= t_row
            eq = jnp.where((xs == thr) & causal, 1.0, 0.0)
            rank = jnp.dot(lower, eq.astype(MXU_DTYPE), preferred_element_type=F32) + seen
            write(off, ((xs > thr) & causal) | ((eq > 0.5) & (rank <= need)))
            return seen + jnp.sum(eq, axis=0, keepdims=True)
        lax.fori_loop(0, n_ch, body, jnp.zeros((1, TQ), F32))
        fill_tail()


def _dsa_select(ik, iqt, iwt, K):
    B, S, DI = ik.shape
    H = iqt.shape[1]
    TQ = min(256, S)
    KC = min(256, S)
    kern = functools.partial(_select_kernel, TQ=TQ, KC=KC, S=S, K=K, NBIS=20)
    return pl.pallas_call(
        kern,
        grid=(B, S // TQ),
        in_specs=[pl.BlockSpec((1, S, DI), lambda b, q: (b, 0, 0)),
                  pl.BlockSpec((1, H, DI, TQ), lambda b, q: (b, 0, 0, q)),
                  pl.BlockSpec((1, H, TQ), lambda b, q: (b, 0, q))],
        out_specs=pl.BlockSpec((1, S, TQ), lambda b, q: (b, 0, q)),
        out_shape=jax.ShapeDtypeStruct((B, S, S), jnp.bfloat16),
        scratch_shapes=[pltpu.VMEM((S, TQ), F32)],
        compiler_params=_cparams(("arbitrary", "arbitrary")),
    )(ik, iqt, iwt)


def _cmp_attn_kernel(qt_ref, kc_ref, vct_ref, o_ref, bm_ref, *, TQ, NCP, NBP, N_SEL):
    qi = pl.program_id(1)
    q0 = qi * TQ
    G, J = NSA_KV_GROUPS, NSA_REP
    N = J * TQ
    t_lane = q0 + (lax.broadcasted_iota(jnp.int32, (NCP, N), 1) & (TQ - 1))
    cmp_end = lax.broadcasted_iota(jnp.int32, (NCP, N), 0) * CMP_STRIDE + (CMP_LEN - 1)
    vis = cmp_end <= t_lane
    n_i = lax.broadcasted_iota(jnp.int32, (NBP, NCP), 0) * SEL_BLOCK
    c_i = lax.broadcasted_iota(jnp.int32, (NBP, NCP), 1) * CMP_STRIDE
    overlap_t = jnp.where((c_i <= n_i + SEL_BLOCK - 1) & (c_i + CMP_LEN - 1 >= n_i), 1.0, 0.0)
    blk = lax.broadcasted_iota(jnp.int32, (NBP, TQ), 0)
    tq = q0 + lax.broadcasted_iota(jnp.int32, (NBP, TQ), 1)
    cur = tq // SEL_BLOCK
    admissible = blk * SEL_BLOCK <= tq
    forced = (blk == 0) | (blk == cur) | (blk == cur - 1)
    for g in range(G):
        qt = jnp.concatenate([qt_ref[0, g * J + j] for j in range(J)], axis=1)
        s = jnp.where(vis, _dot(kc_ref[0, g], qt), NEG_INF)
        p = jnp.exp(s - jnp.max(s, axis=0, keepdims=True))
        p = jnp.where(vis, p, 0.0)
        p = p / jnp.maximum(jnp.sum(p, axis=0, keepdims=True), TINY)
        o = _dot(vct_ref[0, g], p)
        for j in range(J):
            o_ref[0, g * J + j] = o[:, j * TQ:(j + 1) * TQ]
        psum = p[:, 0:TQ]
        for j in range(1, J):
            psum = psum + p[:, j * TQ:(j + 1) * TQ]
        imp = _dot_exact_lhs(overlap_t, psum)
        score = jnp.where(admissible & forced, FORCED, jnp.where(admissible, imp, NEG_INF))

        def pick(_, st):
            score, sel = st
            m = jnp.max(score, axis=0, keepdims=True)
            first = jnp.min(jnp.where(score == m, blk, NBP), axis=0, keepdims=True)
            hit = blk == first
            return jnp.where(hit, LOWEST, score), jnp.where(hit, 1.0, sel)

        _, sel = lax.fori_loop(0, N_SEL, pick, (score, jnp.zeros((NBP, TQ), F32)))
        bm_ref[0, g] = sel.astype(bm_ref.dtype)


def _cmp_attn(qt, kc, vct, n_sel, NBP, TQ):
    B, H, HD, S = qt.shape
    G, NCP = kc.shape[1], kc.shape[2]
    kern = functools.partial(_cmp_attn_kernel, TQ=TQ, NCP=NCP, NBP=NBP, N_SEL=n_sel)
    return pl.pallas_call(
        kern,
        grid=(B, S // TQ),
        in_specs=[pl.BlockSpec((1, H, HD, TQ), lambda b, q: (b, 0, 0, q)),
                  pl.BlockSpec((1, G, NCP, HD), lambda b, q: (b, 0, 0, 0)),
                  pl.BlockSpec((1, G, HD, NCP), lambda b, q: (b, 0, 0, 0))],
        out_specs=[pl.BlockSpec((1, H, HD, TQ), lambda b, q: (b, 0, 0, q)),
                   pl.BlockSpec((1, G, NBP, TQ), lambda b, q: (b, 0, 0, q))],
        out_shape=[jax.ShapeDtypeStruct((B, H, HD, S), F32),
                   jax.ShapeDtypeStruct((B, G, NBP, S), jnp.bfloat16)],
        compiler_params=_cparams(("arbitrary", "arbitrary")),
    )(qt, kc, vct)


def _flash_kernel(*refs, mode, G, J, TQ, TK, NWIN):
    if mode == "window":
        qt_ref, k_ref, vt_ref, o_ref, m_ref, acc_ref = refs
        x_ref = None
    else:
        qt_ref, k_ref, vt_ref, x_ref, o_ref, m_ref, acc_ref = refs
    HD = HEAD_DIM
    qi = pl.program_id(1)
    kk = pl.program_id(2)
    q0 = qi * TQ
    last = (q0 + TQ - 1) // TK
    if mode == "window":
        ki = last - (NWIN - 1) + kk
        valid = ki >= 0
    else:
        ki = kk
        valid = kk <= last

    @pl.when(kk == 0)
    def _():
        m_ref[...] = jnp.full(m_ref.shape, NEG_INF, F32)
        acc_ref[...] = jnp.zeros(acc_ref.shape, F32)

    @pl.when(valid)
    def _():
        k0 = ki * TK
        kidx = k0 + lax.broadcasted_iota(jnp.int32, (TK, TQ), 0)
        tq = q0 + lax.broadcasted_iota(jnp.int32, (TK, TQ), 1)
        if mode == "mask":
            bias = x_ref[0].astype(F32)
        elif mode == "window":
            bias = jnp.where((kidx <= tq) & (kidx > tq - WINDOW), 0.0, NEG_INF)
        else:
            causal = kidx <= tq
            nbp = x_ref.shape[2]
            blk_of_key = (k0 + lax.broadcasted_iota(jnp.int32, (TK, nbp), 0)) // SEL_BLOCK
            expand_t = jnp.where(lax.broadcasted_iota(jnp.int32, (TK, nbp), 1) == blk_of_key, 1.0, 0.0)
            expand_t = expand_t.astype(MXU_DTYPE)

        for g in range(G):
            if mode == "block":
                picked = jnp.dot(expand_t, x_ref[0, g].astype(MXU_DTYPE), preferred_element_type=F32)
                b = jnp.where(causal & (picked > 0.5), 0.0, NEG_INF)
            else:
                b = bias
            if J > 1:
                b = jnp.concatenate([b] * J, axis=1)
                qt = jnp.concatenate([qt_ref[0, g * J + j] for j in range(J)], axis=1)
            else:
                qt = qt_ref[0, g]
            s = _dot(k_ref[0, g], qt) + b
            m_prev = m_ref[g]
            m_new = jnp.maximum(m_prev, jnp.max(s, axis=0, keepdims=True))
            alpha = jnp.exp(m_prev - m_new)
            p = jnp.exp((s - m_new).astype(MXU_DTYPE))
            acc_ref[g] = alpha * acc_ref[g] + _dot(vt_ref[0, g], p)
            m_ref[g] = m_new

    @pl.when(kk == pl.num_programs(2) - 1)
    def _():
        for g in range(G):
            o = acc_ref[g, 0:HD] / jnp.maximum(acc_ref[g, HD:HD + 1], TINY)
            for j in range(J):
                o_ref[0, g * J + j] = o[:, j * TQ:(j + 1) * TQ].astype(o_ref.dtype)


def _flash(qt, k, vt, extra, mode, TQ, TK, out_dtype):
    B, H, HD, S = qt.shape
    G = k.shape[1]
    J = H // G
    N = J * TQ
    nq = S // TQ
    last_of = lambda qi: (qi * TQ + TQ - 1) // TK
    if mode == "window":
        NWIN = min((WINDOW - 1 + TK - 1) // TK + 1, S // TK)
        nk = NWIN
        kidx_of = lambda qi, kk: jnp.maximum(last_of(qi) - (NWIN - 1) + kk, 0)
    else:
        NWIN = 0
        nk = S // TK
        kidx_of = lambda qi, kk: jnp.minimum(kk, last_of(qi))
    in_specs = [pl.BlockSpec((1, H, HD, TQ), lambda b, qi, kk: (b, 0, 0, qi)),
                pl.BlockSpec((1, G, TK, HD), lambda b, qi, kk: (b, 0, kidx_of(qi, kk), 0)),
                pl.BlockSpec((1, G, HD + ONES_ROWS, TK), lambda b, qi, kk: (b, 0, 0, kidx_of(qi, kk)))]
    args = [qt, k, vt]
    if mode == "mask":
        in_specs.append(pl.BlockSpec((1, TK, TQ), lambda b, qi, kk: (b, kidx_of(qi, kk), qi)))
        args.append(extra)
    elif mode == "block":
        nbp = extra.shape[2]
        in_specs.append(pl.BlockSpec((1, G, nbp, TQ), lambda b, qi, kk: (b, 0, 0, qi)))
        args.append(extra)
    kern = functools.partial(_flash_kernel, mode=mode, G=G, J=J, TQ=TQ, TK=TK, NWIN=NWIN)
    return pl.pallas_call(
        kern,
        grid=(B, nq, nk),
        in_specs=in_specs,
        out_specs=pl.BlockSpec((1, H, HD, TQ), lambda b, qi, kk: (b, 0, 0, qi)),
        out_shape=jax.ShapeDtypeStruct((B, H, HD, S), out_dtype),
        scratch_shapes=[pltpu.VMEM((G, 1, N), F32),
                        pltpu.VMEM((G, HD + ONES_ROWS, N), F32)],
        compiler_params=_cparams(("arbitrary", "arbitrary", "arbitrary")),
    )(*args)


def _dot_tn(a_t, b):
    return lax.dot_general(a_t.astype(MXU_DTYPE), b.astype(MXU_DTYPE),
                           (((0,), (0,)), ((), ())), preferred_element_type=F32)


def _outproj_kernel(x_ref, oa_ref, oc_ref, os_ref, ow_ref, br_ref, mg_ref, wa_ref, wb_ref, wo_ref,
                    gt_ref, o_ref):
    HD = HEAD_DIM
    D = x_ref.shape[1]
    br = br_ref[0]
    parts = []
    for hh in range(NSA_HEADS):
        parts.append(br[3 * hh:3 * hh + 1] * oc_ref[0, hh] + br[3 * hh + 1:3 * hh + 2] * os_ref[0, hh]
                     + br[3 * hh + 2:3 * hh + 3] * ow_ref[0, hh])
    ob_t = jnp.concatenate(parts, axis=0)
    oa_t = jnp.concatenate([oa_ref[0, hh] for hh in range(DSA_HEADS)], axis=0)
    ua = _dot_tn(oa_t, wa_ref[...])
    ub = _dot_tn(ob_t, wb_ref[...])
    mg = mg_ref[...]
    merged = jax.nn.sigmoid(mg[:, :D]) * ua + jax.nn.sigmoid(mg[:, D:]) * ub
    o_ref[...] = x_ref[...] + gt_ref[0] * _dot(merged, wo_ref[...])


def _out_proj(x2d, oa, oc, os_, ow, br, mg, wa, wb, wo, gt, S, tm=256):
    T, D = x2d.shape
    B, H, HD, _ = oa.shape
    per_b = S // tm
    heads_cols = pl.BlockSpec((1, H, HD, tm), lambda i: (i // per_b, 0, 0, i % per_b))
    return pl.pallas_call(
        _outproj_kernel,
        grid=(T // tm,),
        in_specs=[pl.BlockSpec((tm, D), lambda i: (i, 0)),
                  heads_cols, heads_cols, heads_cols, heads_cols,
                  pl.BlockSpec((1, br.shape[1], tm), lambda i: (i // per_b, 0, i % per_b)),
                  pl.BlockSpec((tm, 2 * D), lambda i: (i, 0)),
                  pl.BlockSpec((H * HD, D), lambda i: (0, 0)),
                  pl.BlockSpec((H * HD, D), lambda i: (0, 0)),
                  pl.BlockSpec((D, D), lambda i: (0, 0)),
                  pl.BlockSpec((1, 1, D), lambda i: (i // per_b, 0, 0))],
        out_specs=pl.BlockSpec((tm, D), lambda i: (i, 0)),
        out_shape=jax.ShapeDtypeStruct((T, D), F32),
        compiler_params=_cparams(("arbitrary",)),
    )(x2d, oa, oc, os_, ow, br, mg, wa, wb, wo, gt)


def _moe_kernel(x_ref, g2_ref, sc_ref, sh_ref, gt_ref, wr_ref, br_ref, wg_ref, wu_ref, wd_ref,
                gf_ref, o_ref, h_ref, comb_ref, acc_ref, *, TM):
    e = pl.program_id(1)
    lane = lax.broadcasted_iota(jnp.int32, (TM, LANES), 1)

    @pl.when(e == 0)
    def _():
        h = _rms(x_ref[...], g2_ref[...]) * (1.0 + sc_ref[0]) + sh_ref[0]
        h_ref[...] = h.astype(h_ref.dtype)
        h_hi = h.astype(MXU_DTYPE)
        h_lo = (h - h_hi.astype(F32)).astype(MXU_DTYPE)
        w = wr_ref[...]
        w_hi = w.astype(MXU_DTYPE)
        w_lo = (w - w_hi.astype(F32)).astype(MXU_DTYPE)
        d = lambda a, b: jnp.dot(a, b, preferred_element_type=F32)
        logits = d(h_hi, w_hi) + d(h_hi, w_lo) + d(h_lo, w_hi) + br_ref[...]
        is_e = lane < N_EXPERTS
        is_g = (lane >= N_EXPERTS) & (lane < N_EXPERTS + N_GROUPS)
        lg = jnp.where(is_g, logits, LOWEST)
        mg = jnp.max(lg, axis=-1, keepdims=True)
        gsel = jnp.min(jnp.where(is_g & (lg == mg), lane - N_EXPERTS, N_GROUPS), axis=-1, keepdims=True)
        pg_sel = 1.0 / jnp.sum(jnp.where(is_g, jnp.exp(lg - mg), 0.0), axis=-1, keepdims=True)
        in_grp = is_e & ((lane // EXPERTS_PER_GROUP) == gsel)
        le = jnp.where(in_grp, logits, LOWEST)
        me = jnp.max(le, axis=-1, keepdims=True)
        ex = jnp.where(in_grp, jnp.exp(le - me), 0.0)
        pe = ex / jnp.sum(ex, axis=-1, keepdims=True)
        pe = jnp.where(in_grp, pe, -1.0)
        p1 = jnp.max(pe, axis=-1, keepdims=True)
        i1 = jnp.min(jnp.where(pe == p1, lane, LANES), axis=-1, keepdims=True)
        pe2 = jnp.where(lane == i1, -1.0, pe)
        p2 = jnp.max(pe2, axis=-1, keepdims=True)
        i2 = jnp.min(jnp.where(pe2 == p2, lane, LANES), axis=-1, keepdims=True)
        tot = p1 + p2
        comb = jnp.where(lane == i1, p1 / tot * pg_sel, 0.0) + jnp.where(lane == i2, p2 / tot * pg_sel, 0.0)
        comb_ref[...] = comb
        acc_ref[...] = jnp.zeros(acc_ref.shape, F32)

    h = h_ref[...]
    a = jnp.dot(h, wg_ref[0], preferred_element_type=F32)
    u = jnp.dot(h, wu_ref[0], preferred_element_type=F32)
    y = _dot(a * jax.nn.sigmoid(a) * u, wd_ref[0])
    w_e = jnp.sum(jnp.where(lane == e, comb_ref[...], 0.0), axis=-1, keepdims=True)
    acc_ref[...] += w_e * y

    @pl.when(e == pl.num_programs(1) - 1)
    def _():
        x2 = x_ref[...] + gt_ref[0] * acc_ref[...]
        o_ref[...] = _rms(x2, gf_ref[...])


def _moe(x1, g2, sc, sh, gt, wr, br, wg, wu, wd, gf, S, tm=1024):
    T, D = x1.shape
    E, _, DE = wg.shape
    per_b = S // tm
    kern = functools.partial(_moe_kernel, TM=tm)
    return pl.pallas_call(
        kern,
        grid=(T // tm, E),
        in_specs=[pl.BlockSpec((tm, D), lambda i, e: (i, 0)),
                  pl.BlockSpec((1, D), lambda i, e: (0, 0)),
                  pl.BlockSpec((1, 1, D), lambda i, e: (i // per_b, 0, 0)),
                  pl.BlockSpec((1, 1, D), lambda i, e: (i // per_b, 0, 0)),
                  pl.BlockSpec((1, 1, D), lambda i, e: (i // per_b, 0, 0)),
                  pl.BlockSpec((D, LANES), lambda i, e: (0, 0)),
                  pl.BlockSpec((1, LANES), lambda i, e: (0, 0)),
                  pl.BlockSpec((1, D, DE), lambda i, e: (e, 0, 0)),
                  pl.BlockSpec((1, D, DE), lambda i, e: (e, 0, 0)),
                  pl.BlockSpec((1, DE, D), lambda i, e: (e, 0, 0)),
                  pl.BlockSpec((1, D), lambda i, e: (0, 0))],
        out_specs=pl.BlockSpec((tm, D), lambda i, e: (i, 0)),
        out_shape=jax.ShapeDtypeStruct((T, D), F32),
        scratch_shapes=[pltpu.VMEM((tm, D), MXU_DTYPE),
                        pltpu.VMEM((tm, LANES), F32),
                        pltpu.VMEM((tm, D), F32)],
        compiler_params=_cparams(("arbitrary", "arbitrary")),
    )(x1, g2.reshape(1, D), sc, sh, gt, wr, br, wg, wu, wd, gf.reshape(1, D))


def _layer(x, mod, positions, g_norm1, w_in, g_kv_latent, w_kv_up, pe_cmp_k, pe_cmp_v,
           w_cmp1_k, w_cmp2_k, w_cmp1_v, w_cmp2_v, w_up_a, w_up_b, w_out, g_norm2,
           w_router_group, b_router_group, w_router_expert, b_router_expert,
           w_expert_gate, w_expert_up, w_expert_down, g_out):
    B, S, D = x.shape
    T = B * S
    HD, G, J = HEAD_DIM, NSA_KV_GROUPS, NSA_REP
    HA = DSA_HEADS * HD
    kvb = G * HD
    scale = HD ** -0.5
    topk_a = min(DSA_TOPK_MAX, S // 4)
    n_sel = min(SEL_COUNT, S // SEL_BLOCK)
    mod6 = mod.reshape(B, 6, 1, D)
    sh1, sc1, gt1, sh2, sc2, gt2 = (mod6[:, i] for i in range(6))

    x2d = x.reshape(T, D)
    (mg, qt_a, k_a, vt_a, iqt, iwt, ik, qt_b, kcvc, ks, kw, vst, vwt, br) = _prep(
        x, sc1, sh1, g_norm1, w_in, g_kv_latent, w_kv_up, positions)

    kc = _compress(kcvc[..., :kvb].reshape(B, S, G, HD), pe_cmp_k, w_cmp1_k, w_cmp2_k)
    vc = _compress(kcvc[..., kvb:].reshape(B, S, G, HD), pe_cmp_v, w_cmp1_v, w_cmp2_v)
    n_cmp = kc.shape[1]
    ncp = -(-(n_cmp + 1) // LANES) * LANES
    pad_c = lambda t: jnp.pad(t, ((0, 0), (0, ncp - n_cmp), (0, 0), (0, 0))).astype(MXU_DTYPE)
    kc, vct = pad_c(kc).transpose(0, 2, 1, 3), pad_c(vc).transpose(0, 2, 3, 1)

    TA = min(512, S)
    sel_bias = _dsa_select(ik, iqt, iwt, topk_a)
    o_a = _flash(qt_a, k_a, vt_a, sel_bias, "mask", TA, TA, MXU_DTYPE)

    TB = min(256, S)
    nbp = -(-(S // SEL_BLOCK) // LANES) * LANES
    o_c, blk_mask = _cmp_attn(qt_b, kc, vct, n_sel, nbp, TB)
    o_s = _flash(qt_b, ks, vst, blk_mask, "block", TB, min(512, S), F32)
    o_w = _flash(qt_b, kw, vwt, None, "window", TB, TB, F32)

    x1 = _out_proj(x2d, o_a, o_c, o_s, o_w, br, mg, w_up_a.astype(MXU_DTYPE),
                   w_up_b.astype(MXU_DTYPE), w_out.astype(MXU_DTYPE), gt1, S)

    wr = jnp.concatenate([w_router_expert, w_router_group], axis=1)
    wr = jnp.pad(wr, ((0, 0), (0, LANES - wr.shape[1])))
    brt = jnp.concatenate([b_router_expert, b_router_group])
    brt = jnp.pad(brt, (0, LANES - brt.shape[0])).reshape(1, LANES)
    out = _moe(x1, g_norm2, sc2, sh2, gt2, wr, brt, w_expert_gate.astype(MXU_DTYPE),
               w_expert_up.astype(MXU_DTYPE), w_expert_down.astype(MXU_DTYPE), g_out, S)
    return out.reshape(B, S, D)


def kernel(x, c, positions, w_ada, b_ada, g_norm1, w_in, g_kv_latent, w_kv_up, pe_cmp_k, pe_cmp_v,
           w_cmp1_k, w_cmp2_k, w_cmp1_v, w_cmp2_v, w_up_a, w_up_b, w_out, g_norm2, w_router_group,
           b_router_group, w_router_expert, b_router_expert, w_expert_gate, w_expert_up,
           w_expert_down, g_final):
    depth = w_ada.shape[0]
    assert depth == 1, "the fused final norm assumes a single layer"
    mod = _ada_mod(c, w_ada[0], b_ada[0])
    return _layer(x, mod, positions, g_norm1[0], w_in[0], g_kv_latent[0], w_kv_up[0], pe_cmp_k[0],
                  pe_cmp_v[0], w_cmp1_k[0], w_cmp2_k[0], w_cmp1_v[0], w_cmp2_v[0], w_up_a[0],
                  w_up_b[0], w_out[0], g_norm2[0], w_router_group[0], b_router_group[0],
                  w_router_expert[0], b_router_expert[0], w_expert_gate[0], w_expert_up[0],
                  w_expert_down[0], g_final)
```

```python
import functools

import numpy as np
import jax
import jax.numpy as jnp
from jax import lax
from jax.experimental import pallas as pl
from jax.experimental.pallas import tpu as pltpu

HEAD_DIM = 64
ROT_FRACTION = 4
ROPE_THETA = 500000.0
DSA_HEADS = 8
DSA_KV_RANK = 128
IDX_HEADS = 8
IDX_DIM = 32
DSA_TOPK_MAX = 256
NSA_HEADS = 8
NSA_KV_GROUPS = 2
NSA_REP = NSA_HEADS // NSA_KV_GROUPS
CMP_LEN = 32
CMP_STRIDE = 16
CMP_HIDDEN = 256
SEL_BLOCK = 64
SEL_COUNT = 16
WINDOW = 512
N_GROUPS = 4
EXPERTS_PER_GROUP = 8
N_EXPERTS = N_GROUPS * EXPERTS_PER_GROUP
D_EXPERT = 256
NORM_EPS = 1e-6
NEG_INF = -1e30
TINY = 1e-30
LOWEST = -3.0e38
FORCED = 1e30

LANES = 128
SUBLANES = 8
ONES_ROWS = 16
MXU_DTYPE = jnp.bfloat16
VMEM_LIMIT = 56 * 1024 * 1024

F32 = jnp.float32


def _cparams(sem):
    return pltpu.CompilerParams(dimension_semantics=sem, vmem_limit_bytes=VMEM_LIMIT)


def _dot(a, b):
    return jnp.dot(a.astype(MXU_DTYPE), b.astype(MXU_DTYPE), preferred_element_type=F32)


def _dot_exact_lhs(a01, b):
    hi = b.astype(MXU_DTYPE)
    r1 = b - hi.astype(F32)
    mid = r1.astype(MXU_DTYPE)
    lo = (r1 - mid.astype(F32)).astype(MXU_DTYPE)
    a = a01.astype(MXU_DTYPE)
    d = lambda u: jnp.dot(a, u, preferred_element_type=F32)
    return d(hi) + d(mid) + d(lo)


def _rms(x, g):
    return x * lax.rsqrt(jnp.mean(x * x, axis=-1, keepdims=True) + NORM_EPS) * g


def _ada_kernel(c_ref, w_ref, b_ref, o_ref):
    c = c_ref[...]
    cond = c * jax.nn.sigmoid(c)
    o_ref[...] = _dot(cond, w_ref[...]) + b_ref[...]


def _ada_mod(c, w_ada, b_ada):
    B, D = c.shape
    n_out = w_ada.shape[1]
    rows = SUBLANES
    cp = jnp.zeros((rows, D), F32).at[:B].set(c)
    tn = 1024
    out = pl.pallas_call(
        _ada_kernel,
        grid=(n_out // tn,),
        in_specs=[pl.BlockSpec((rows, D), lambda j: (0, 0)),
                  pl.BlockSpec((D, tn), lambda j: (0, j)),
                  pl.BlockSpec((1, tn), lambda j: (0, j))],
        out_specs=pl.BlockSpec((rows, tn), lambda j: (0, j)),
        out_shape=jax.ShapeDtypeStruct((rows, n_out), F32),
        compiler_params=_cparams(("arbitrary",)),
    )(cp, w_ada, b_ada.reshape(1, n_out))
    return out[:B]


KVB = NSA_KV_GROUPS * HEAD_DIM
ROW_SEGS = (("mg", 2048), ("lat", DSA_KV_RANK), ("kc", KVB), ("vc", KVB), ("ks", KVB), ("kw", KVB),
            ("ik", LANES))
COL_SEGS = (("qa", DSA_HEADS * HEAD_DIM), ("qb", NSA_HEADS * HEAD_DIM), ("iq", IDX_HEADS * IDX_DIM),
            ("vs", KVB), ("vw", KVB), ("lat", DSA_KV_RANK), ("iw", IDX_HEADS), ("br", 3 * NSA_HEADS))


def _seg_offsets(segs):
    out, pos = {}, 0
    for name, n in segs:
        out[name] = pos
        pos += n
    return out, pos


ROW_OFF, ROW_COLS = _seg_offsets(ROW_SEGS)
COL_OFF, COL_ROWS = _seg_offsets(COL_SEGS)


def _rot_lanes(x, c, sa, sb, half):
    outs = []
    for j in range(x.shape[1] // LANES):
        xs = x[:, j * LANES:(j + 1) * LANES]
        outs.append(xs * c + pltpu.roll(xs, half, 1) * sa + pltpu.roll(xs, LANES - half, 1) * sb)
    return outs[0] if len(outs) == 1 else jnp.concatenate(outs, axis=1)


def _prep_kernel(x_ref, xt_ref, g_ref, sc_ref, sh_ref, gc_ref, scc_ref, shc_ref, wr_ref, wc_ref,
                 gkv_ref, gkvc_ref, wkk_ref, wkv_ref, c64_ref, sa64_ref, sb64_ref,
                 c32_ref, sa32_ref, sb32_ref, cos_ref, sin_ref, c32t_ref, s32t_ref,
                 mg_ref, qa_ref, ka_ref, va_ref, iq_ref, iw_ref, ik_ref, qb_ref, kcvc_ref,
                 ks_ref, kw_ref, vs_ref, vw_ref, br_ref):
    HD = HEAD_DIM
    half = HD // ROT_FRACTION // 2
    scale = HD ** -0.5
    h = _rms(x_ref[...], g_ref[...]) * (1.0 + sc_ref[0]) + sh_ref[0]
    pr = _dot(h, wr_ref[...])
    row = lambda name, n: pr[:, ROW_OFF[name]:ROW_OFF[name] + n]
    mg_ref[...] = row("mg", 2048)
    c64, sa64, sb64 = c64_ref[...], sa64_ref[...], sb64_ref[...]
    ka = _rot_lanes(_dot(_rms(row("lat", DSA_KV_RANK), gkv_ref[...]), wkk_ref[...]), c64, sa64, sb64, half)
    for hh in range(DSA_HEADS):
        ka_ref[0, hh] = ka[:, hh * HD:(hh + 1) * HD].astype(ka_ref.dtype)
    kcvc_ref[0] = jnp.concatenate([_rot_lanes(row("kc", KVB), c64, sa64, sb64, half), row("vc", KVB)], axis=1)
    ks = _rot_lanes(row("ks", KVB), c64, sa64, sb64, half)
    kw = _rot_lanes(row("kw", KVB), c64, sa64, sb64, half)
    for g in range(NSA_KV_GROUPS):
        ks_ref[0, g] = ks[:, g * HD:(g + 1) * HD].astype(ks_ref.dtype)
        kw_ref[0, g] = kw[:, g * HD:(g + 1) * HD].astype(kw_ref.dtype)
    ik = _rot_lanes(row("ik", LANES), c32_ref[...], sa32_ref[...], sb32_ref[...], IDX_DIM // ROT_FRACTION // 2)
    ik_ref[0] = ik[:, :IDX_DIM].astype(ik_ref.dtype)
    xt = xt_ref[0]
    ht = xt * lax.rsqrt(jnp.mean(xt * xt, axis=0, keepdims=True) + NORM_EPS) * gc_ref[...]
    ht = ht * (1.0 + scc_ref[0]) + shc_ref[0]
    pc = _dot(wc_ref[...], ht)
    col = lambda name, n: pc[COL_OFF[name]:COL_OFF[name] + n]
    cos, sin = cos_ref[0], sin_ref[0]

    def rot_rows(blk):
        x1, x2 = blk[0:half], blk[half:2 * half]
        return jnp.concatenate([x1 * cos - x2 * sin, x2 * cos + x1 * sin, blk[2 * half:]], axis=0)

    qa, qb = col("qa", DSA_HEADS * HD), col("qb", NSA_HEADS * HD)
    for hh in range(DSA_HEADS):
        qa_ref[0, hh] = (rot_rows(qa[hh * HD:(hh + 1) * HD]) * scale).astype(qa_ref.dtype)
    for hh in range(NSA_HEADS):
        qb_ref[0, hh] = (rot_rows(qb[hh * HD:(hh + 1) * HD]) * scale).astype(qb_ref.dtype)
    iq = col("iq", IDX_HEADS * IDX_DIM)
    c32t, s32t = c32t_ref[0], s32t_ref[0]
    for hh in range(IDX_HEADS):
        blk = iq[hh * IDX_DIM:(hh + 1) * IDX_DIM]
        top = blk[0:SUBLANES]
        top = top * c32t + pltpu.roll(top, SUBLANES // 2, 0) * s32t
        iq_ref[0, hh] = jnp.concatenate([top, blk[SUBLANES:]], axis=0).astype(iq_ref.dtype)
    ones = jnp.ones((ONES_ROWS, xt.shape[1]), F32)
    with_ones = lambda v: jnp.concatenate([v, ones], axis=0)
    vs, vw = col("vs", KVB), col("vw", KVB)
    for g in range(NSA_KV_GROUPS):
        vs_ref[0, g] = with_ones(vs[g * HD:(g + 1) * HD]).astype(vs_ref.dtype)
        vw_ref[0, g] = with_ones(vw[g * HD:(g + 1) * HD]).astype(vw_ref.dtype)
    lat = col("lat", DSA_KV_RANK)
    lat = lat * lax.rsqrt(jnp.mean(lat * lat, axis=0, keepdims=True) + NORM_EPS) * gkvc_ref[...]
    va = _dot(wkv_ref[...], lat)
    for hh in range(DSA_HEADS):
        va_ref[0, hh] = with_ones(va[hh * HD:(hh + 1) * HD]).astype(va_ref.dtype)
    iw_ref[0] = col("iw", IDX_HEADS)
    br_ref[0] = jax.nn.sigmoid(col("br", 3 * NSA_HEADS))


def _prep(x, mod_sc, mod_sh, g_norm1, w_in, g_kv, w_kv_up, positions, tm=256):
    B, S, D = x.shape
    T = B * S
    HD, HA, G = HEAD_DIM, DSA_HEADS, NSA_KV_GROUPS
    per_b = S // tm
    sizes = [HA * HD, DSA_KV_RANK, IDX_HEADS * IDX_DIM, IDX_HEADS, IDX_DIM, NSA_HEADS * HD,
             KVB, KVB, KVB, KVB, KVB, KVB, 3 * NSA_HEADS, 2 * D]
    names = ["qa", "lat", "iq", "iw", "ik", "qb", "kc", "vc", "ks", "vs", "kw", "vw", "br", "mg"]
    starts = dict(zip(names, np.concatenate([[0], np.cumsum(sizes)[:-1]]).astype(int)))
    width = dict(zip(names, sizes))
    def seg(name, n):
        w = w_in[:, starts[name]:starts[name] + width[name]]
        return jnp.pad(w, ((0, 0), (0, n - width[name])))
    w_row = jnp.concatenate([seg(n, k) for n, k in ROW_SEGS], axis=1).astype(MXU_DTYPE)
    w_col = jnp.concatenate([seg(n, k) for n, k in COL_SEGS], axis=1).T.astype(MXU_DTYPE)
    w_kk = w_kv_up[:, :HA * HD].astype(MXU_DTYPE)
    w_kv = w_kv_up[:, HA * HD:].T.astype(MXU_DTYPE)
    wide = lambda v: jnp.broadcast_to(v[..., None], v.shape + (tm,))

    def tables(dim):
        half = dim // ROT_FRACTION // 2
        inv_freq = ROPE_THETA ** (-jnp.arange(half, dtype=F32) / half)
        ang = positions.astype(F32)[..., None] * inv_freq
        return jnp.cos(ang), jnp.sin(ang), half
    def lane_tables(dim):
        cos, sin, half = tables(dim)
        reps = LANES // dim
        z = jnp.zeros((B, S, dim - 2 * half), F32)
        c = jnp.concatenate([cos, cos, z + 1.0], axis=-1)
        sa = jnp.concatenate([jnp.zeros_like(sin), sin, z], axis=-1)
        sb = jnp.concatenate([-sin, jnp.zeros_like(sin), z], axis=-1)
        return [jnp.tile(t, (1, 1, reps)).reshape(T, LANES) for t in (c, sa, sb)]
    c64, sa64, sb64 = lane_tables(HD)
    c32, sa32, sb32 = lane_tables(IDX_DIM)
    cos64, sin64, _ = tables(HD)
    cos_t, sin_t = cos64.transpose(0, 2, 1), sin64.transpose(0, 2, 1)
    cos32, sin32, _ = tables(IDX_DIM)
    c32t = jnp.concatenate([cos32, cos32], axis=-1).transpose(0, 2, 1)
    s32t = jnp.concatenate([-sin32, sin32], axis=-1).transpose(0, 2, 1)

    row_blk = lambda n: pl.BlockSpec((tm, n), lambda i: (i, 0))
    const = lambda shape: pl.BlockSpec(shape, lambda i: (0,) * len(shape))
    per_batch = lambda shape: pl.BlockSpec((1,) + shape, lambda i: (i // per_b,) + (0,) * len(shape))
    tok_cols = lambda rows: pl.BlockSpec((1, rows, tm), lambda i: (i // per_b, 0, i % per_b))
    heads_cols = lambda h, rows: pl.BlockSpec((1, h, rows, tm), lambda i: (i // per_b, 0, 0, i % per_b))
    heads_rows = lambda h, n: pl.BlockSpec((1, h, tm, n), lambda i: (i // per_b, 0, i % per_b, 0))
    bf = MXU_DTYPE
    out_shape = [jax.ShapeDtypeStruct((T, 2 * D), F32),
                 jax.ShapeDtypeStruct((B, HA, HD, S), bf),
                 jax.ShapeDtypeStruct((B, HA, S, HD), bf),
                 jax.ShapeDtypeStruct((B, HA, HD + ONES_ROWS, S), bf),
                 jax.ShapeDtypeStruct((B, IDX_HEADS, IDX_DIM, S), bf),
                 jax.ShapeDtypeStruct((B, IDX_HEADS, S), F32),
                 jax.ShapeDtypeStruct((B, S, IDX_DIM), bf),
                 jax.ShapeDtypeStruct((B, NSA_HEADS, HD, S), bf),
                 jax.ShapeDtypeStruct((B, S, 2 * KVB), F32),
                 jax.ShapeDtypeStruct((B, G, S, HD), bf),
                 jax.ShapeDtypeStruct((B, G, S, HD), bf),
                 jax.ShapeDtypeStruct((B, G, HD + ONES_ROWS, S), bf),
                 jax.ShapeDtypeStruct((B, G, HD + ONES_ROWS, S), bf),
                 jax.ShapeDtypeStruct((B, 3 * NSA_HEADS, S), F32)]
    out_specs = [row_blk(2 * D), heads_cols(HA, HD), heads_rows(HA, HD), heads_cols(HA, HD + ONES_ROWS),
                 heads_cols(IDX_HEADS, IDX_DIM), tok_cols(IDX_HEADS),
                 pl.BlockSpec((1, tm, IDX_DIM), lambda i: (i // per_b, i % per_b, 0)),
                 heads_cols(NSA_HEADS, HD),
                 pl.BlockSpec((1, tm, 2 * KVB), lambda i: (i // per_b, i % per_b, 0)),
                 heads_rows(G, HD), heads_rows(G, HD), heads_cols(G, HD + ONES_ROWS),
                 heads_cols(G, HD + ONES_ROWS),
                 tok_cols(3 * NSA_HEADS)]
    half64 = HD // ROT_FRACTION // 2
    in_specs = [row_blk(D), tok_cols(D), const((1, D)), per_batch((1, D)), per_batch((1, D)),
                const((D, tm)), per_batch((D, tm)), per_batch((D, tm)),
                const((D, ROW_COLS)), const((COL_ROWS, D)),
                const((1, DSA_KV_RANK)), const((DSA_KV_RANK, tm)),
                const((DSA_KV_RANK, HA * HD)), const((HA * HD, DSA_KV_RANK)),
                row_blk(LANES), row_blk(LANES), row_blk(LANES), row_blk(LANES), row_blk(LANES), row_blk(LANES),
                tok_cols(half64), tok_cols(half64), tok_cols(SUBLANES), tok_cols(SUBLANES)]
    return pl.pallas_call(
        _prep_kernel,
        grid=(T // tm,),
        in_specs=in_specs,
        out_specs=out_specs,
        out_shape=out_shape,
        compiler_params=_cparams(("arbitrary",)),
    )(x.reshape(T, D), x.transpose(0, 2, 1), g_norm1.reshape(1, D), mod_sc, mod_sh,
      wide(g_norm1), wide(mod_sc[:, 0]), wide(mod_sh[:, 0]), w_row, w_col,
      g_kv.reshape(1, DSA_KV_RANK), wide(g_kv), w_kk, w_kv,
      c64, sa64, sb64, c32, sa32, sb32, cos_t, sin_t, c32t, s32t)


def _cmp_kernel(f_ref, w1_ref, w2_ref, o_ref):
    hid = _dot(f_ref[...], w1_ref[...])
    hid = hid * jax.nn.sigmoid(hid)
    o_ref[...] = _dot(hid, w2_ref[...])


def _compress(tok, pe, w1, w2):
    B, S, G, HD = tok.shape
    r = CMP_LEN // CMP_STRIDE
    n_chunks = S // CMP_STRIDE
    n_cmp = n_chunks - r + 1
    chunks = tok.reshape(B, n_chunks, CMP_STRIDE, G, HD)
    blocks = jnp.concatenate([chunks[:, i:n_cmp + i] for i in range(r)], axis=2)
    blocks = blocks + pe[:, None, :]
    flat = blocks.transpose(0, 1, 3, 2, 4).reshape(B * n_cmp * G, CMP_LEN * HD)
    rows = flat.shape[0]
    tm = 512
    rows_p = -(-rows // tm) * tm
    flat = jnp.pad(flat, ((0, rows_p - rows), (0, 0))).astype(MXU_DTYPE)
    out = pl.pallas_call(
        _cmp_kernel,
        grid=(rows_p // tm,),
        in_specs=[pl.BlockSpec((tm, CMP_LEN * HD), lambda i: (i, 0)),
                  pl.BlockSpec((CMP_LEN * HD, CMP_HIDDEN), lambda i: (0, 0)),
                  pl.BlockSpec((CMP_HIDDEN, HD), lambda i: (0, 0))],
        out_specs=pl.BlockSpec((tm, HD), lambda i: (i, 0)),
        out_shape=jax.ShapeDtypeStruct((rows_p, HD), F32),
        compiler_params=_cparams(("arbitrary",)),
    )(flat, w1.astype(MXU_DTYPE), w2.astype(MXU_DTYPE))
    return out[:rows].reshape(B, n_cmp, G, HD)


def _select_kernel(ik_ref, iqt_ref, iwt_ref, bias_ref, sc_ref, *, TQ, KC, S, K, NBIS):
    qi = pl.program_id(1)
    q0 = qi * TQ
    n_ch = (q0 + TQ + KC - 1) // KC
    n_all = S // KC
    t_row = q0 + lax.broadcasted_iota(jnp.int32, (1, TQ), 1)
    key_iota = lax.broadcasted_iota(jnp.int32, (KC, TQ), 0)
    kf = float(K)
    SUB = LANES
    sub_iota = lax.broadcasted_iota(jnp.int32, (SUB, TQ), 0)

    def score_chunk(c, carry):
        for u in range(KC // SUB):
            off = pl.multiple_of(c * KC + u * SUB, SUB)
            ikc = ik_ref[0, pl.ds(off, SUB), :]
            acc = jnp.zeros((SUB, TQ), F32)
            for h in range(IDX_HEADS):
                lg = jnp.dot(ikc, iqt_ref[0, h], preferred_element_type=F32)
                acc = acc + jnp.maximum(lg, 0.0) * iwt_ref[0, h:h + 1, :]
            sc_ref[pl.ds(off, SUB), :] = jnp.where(off + sub_iota <= t_row, acc, NEG_INF)
        return carry

    lax.fori_loop(0, n_ch, score_chunk, 0)

    def chunk(c):
        off = pl.multiple_of(c * KC, KC)
        return sc_ref[pl.ds(off, KC), :], off

    def fold(x, op):
        return op(x.reshape(KC // SUBLANES, SUBLANES, TQ), axis=0)

    def key_pass(fn, init):
        def body(c, acc):
            xs, off = chunk(c)
            return fn(acc, xs, off)
        return lax.fori_loop(0, n_ch, body, init)

    def count_ge(thr):
        acc = key_pass(lambda a, xs, off: a + fold(jnp.where(xs >= thr, 1.0, 0.0), jnp.sum),
                       jnp.zeros((SUBLANES, TQ), F32))
        return jnp.sum(acc, axis=0, keepdims=True)

    def count_gt(thr):
        acc = key_pass(lambda a, xs, off: a + fold(jnp.where(xs > thr, 1.0, 0.0), jnp.sum),
                       jnp.zeros((SUBLANES, TQ), F32))
        return jnp.sum(acc, axis=0, keepdims=True)

    def max_where(bound, strict):
        def fn(a, xs, off):
            ok = (xs < bound) if strict else (xs <= bound)
            return jnp.maximum(a, fold(jnp.where(ok, xs, LOWEST), jnp.max))
        acc = key_pass(fn, jnp.full((SUBLANES, TQ), LOWEST, F32))
        return jnp.max(acc, axis=0, keepdims=True)

    def minmax_fn(a, xs, off):
        mn, mx = a
        causal = (off + key_iota) <= t_row
        return (jnp.minimum(mn, fold(jnp.where(causal, xs, -LOWEST), jnp.min)),
                jnp.maximum(mx, fold(xs, jnp.max)))

    mn, mx = key_pass(minmax_fn, (jnp.full((SUBLANES, TQ), -LOWEST, F32),
                                  jnp.full((SUBLANES, TQ), LOWEST, F32)))
    lo = jnp.min(mn, axis=0, keepdims=True)
    hi = jnp.max(mx, axis=0, keepdims=True)

    def bisect(_, carry):
        lo, hi = carry
        mid = 0.5 * (lo + hi)
        ge = count_ge(mid) >= kf
        return jnp.where(ge, mid, lo), jnp.where(ge, hi, mid)

    lo, hi = lax.fori_loop(0, NBIS, bisect, (lo, hi))

    all_keys = jnp.where(t_row < K, 1.0, 0.0)
    v = max_where(hi, strict=False)
    done = jnp.maximum(all_keys, jnp.where(count_ge(v) >= kf, 1.0, 0.0))

    def peel_cond(st):
        return jnp.sum(st[1]) < float(TQ)

    def peel_body(st):
        v, done = st
        v = jnp.where(done > 0.5, v, max_where(v, strict=True))
        done = jnp.maximum(done, jnp.where(count_ge(v) >= kf, 1.0, 0.0))
        return v, done

    v, done = lax.while_loop(peel_cond, peel_body, (v, done))
    thr = jnp.where(all_keys > 0.5, LOWEST, v)
    n_ge = count_ge(thr)
    has_tie = jnp.max(jnp.where((n_ge > kf) & (all_keys < 0.5), 1.0, 0.0)) > 0.5

    def write(off, sel):
        bias_ref[0, pl.ds(off, KC), :] = jnp.where(sel, 0.0, NEG_INF).astype(bias_ref.dtype)

    def fill_tail():
        def body(c, carry):
            write(pl.multiple_of(c * KC, KC), jnp.zeros((KC, TQ), jnp.bool_))
            return carry
        lax.fori_loop(n_ch, n_all, body, 0)

    @pl.when(jnp.logical_not(has_tie))
    def _():
        def body(c, carry):
            xs, off = chunk(c)
            write(off, (xs >= thr) & ((off + key_iota) <= t_row))
            return carry
        lax.fori_loop(0, n_ch, body, 0)
        fill_tail()

    @pl.when(has_tie)
    def _():
        need = kf - count_gt(thr)
        r_i = lax.broadcasted_iota(jnp.int32, (KC, KC), 0)
        c_i = lax.broadcasted_iota(jnp.int32, (KC, KC), 1)
        lower = jnp.where(c_i <= r_i, 1.0, 0.0).astype(MXU_DTYPE)
        def body(c, seen):
            xs, off = chunk(c)
            causal = (off + key_iota) <= t_row
            eq = jnp.where((xs == thr) & causal, 1.0, 0.0)
            rank = jnp.dot(lower, eq.astype(MXU_DTYPE), preferred_element_type=F32) + seen
            write(off, ((xs > thr) & causal) | ((eq > 0.5) & (rank <= need)))
            return seen + jnp.sum(eq, axis=0, keepdims=True)
        lax.fori_loop(0, n_ch, body, jnp.zeros((1, TQ), F32))
        fill_tail()


def _dsa_select(ik, iqt, iwt, K):
    B, S, DI = ik.shape
    H = iqt.shape[1]
    TQ = min(256, S)
    KC = min(256, S)
    kern = functools.partial(_select_kernel, TQ=TQ, KC=KC, S=S, K=K, NBIS=20)
    return pl.pallas_call(
        kern,
        grid=(B, S // TQ),
        in_specs=[pl.BlockSpec((1, S, DI), lambda b, q: (b, 0, 0)),
                  pl.BlockSpec((1, H, DI, TQ), lambda b, q: (b, 0, 0, q)),
                  pl.BlockSpec((1, H, TQ), lambda b, q: (b, 0, q))],
        out_specs=pl.BlockSpec((1, S, TQ), lambda b, q: (b, 0, q)),
        out_shape=jax.ShapeDtypeStruct((B, S, S), jnp.bfloat16),
        scratch_shapes=[pltpu.VMEM((S, TQ), F32)],
        compiler_params=_cparams(("arbitrary", "arbitrary")),
    )(ik, iqt, iwt)


def _cmp_attn_kernel(qt_ref, kc_ref, vct_ref, o_ref, bm_ref, *, TQ, NCP, NBP, N_SEL):
    qi = pl.program_id(1)
    q0 = qi * TQ
    G, J = NSA_KV_GROUPS, NSA_REP
    N = J * TQ
    t_lane = q0 + (lax.broadcasted_iota(jnp.int32, (NCP, N), 1) & (TQ - 1))
    cmp_end = lax.broadcasted_iota(jnp.int32, (NCP, N), 0) * CMP_STRIDE + (CMP_LEN - 1)
    vis = cmp_end <= t_lane
    n_i = lax.broadcasted_iota(jnp.int32, (NBP, NCP), 0) * SEL_BLOCK
    c_i = lax.broadcasted_iota(jnp.int32, (NBP, NCP), 1) * CMP_STRIDE
    overlap_t = jnp.where((c_i <= n_i + SEL_BLOCK - 1) & (c_i + CMP_LEN - 1 >= n_i), 1.0, 0.0)
    blk = lax.broadcasted_iota(jnp.int32, (NBP, TQ), 0)
    tq = q0 + lax.broadcasted_iota(jnp.int32, (NBP, TQ), 1)
    cur = tq // SEL_BLOCK
    admissible = blk * SEL_BLOCK <= tq
    forced = (blk == 0) | (blk == cur) | (blk == cur - 1)
    for g in range(G):
        qt = jnp.concatenate([qt_ref[0, g * J + j] for j in range(J)], axis=1)
        s = jnp.where(vis, _dot(kc_ref[0, g], qt), NEG_INF)
        p = jnp.exp(s - jnp.max(s, axis=0, keepdims=True))
        p = jnp.where(vis, p, 0.0)
        p = p / jnp.maximum(jnp.sum(p, axis=0, keepdims=True), TINY)
        o = _dot(vct_ref[0, g], p)
        for j in range(J):
            o_ref[0, g * J + j] = o[:, j * TQ:(j + 1) * TQ]
        psum = p[:, 0:TQ]
        for j in range(1, J):
            psum = psum + p[:, j * TQ:(j + 1) * TQ]
        imp = _dot_exact_lhs(overlap_t, psum)
        score = jnp.where(admissible & forced, FORCED, jnp.where(admissible, imp, NEG_INF))

        def pick(_, st):
            score, sel = st
            m = jnp.max(score, axis=0, keepdims=True)
            first = jnp.min(jnp.where(score == m, blk, NBP), axis=0, keepdims=True)
            hit = blk == first
            return jnp.where(hit, LOWEST, score), jnp.where(hit, 1.0, sel)

        _, sel = lax.fori_loop(0, N_SEL, pick, (score, jnp.zeros((NBP, TQ), F32)))
        bm_ref[0, g] = sel.astype(bm_ref.dtype)


def _cmp_attn(qt, kc, vct, n_sel, NBP, TQ):
    B, H, HD, S = qt.shape
    G, NCP = kc.shape[1], kc.shape[2]
    kern = functools.partial(_cmp_attn_kernel, TQ=TQ, NCP=NCP, NBP=NBP, N_SEL=n_sel)
    return pl.pallas_call(
        kern,
        grid=(B, S // TQ),
        in_specs=[pl.BlockSpec((1, H, HD, TQ), lambda b, q: (b, 0, 0, q)),
                  pl.BlockSpec((1, G, NCP, HD), lambda b, q: (b, 0, 0, 0)),
                  pl.BlockSpec((1, G, HD, NCP), lambda b, q: (b, 0, 0, 0))],
        out_specs=[pl.BlockSpec((1, H, HD, TQ), lambda b, q: (b, 0, 0, q)),
                   pl.BlockSpec((1, G, NBP, TQ), lambda b, q: (b, 0, 0, q))],
        out_shape=[jax.ShapeDtypeStruct((B, H, HD, S), F32),
                   jax.ShapeDtypeStruct((B, G, NBP, S), jnp.bfloat16)],
        compiler_params=_cparams(("arbitrary", "arbitrary")),
    )(qt, kc, vct)


def _flash_kernel(*refs, mode, G, J, TQ, TK, NWIN):
    if mode == "window":
        qt_ref, k_ref, vt_ref, o_ref, m_ref, acc_ref = refs
        x_ref = None
    else:
        qt_ref, k_ref, vt_ref, x_ref, o_ref, m_ref, acc_ref = refs
    HD = HEAD_DIM
    qi = pl.program_id(1)
    kk = pl.program_id(2)
    q0 = qi * TQ
    last = (q0 + TQ - 1) // TK
    if mode == "window":
        ki = last - (NWIN - 1) + kk
        valid = ki >= 0
    else:
        ki = kk
        valid = kk <= last

    @pl.when(kk == 0)
    def _():
        m_ref[...] = jnp.full(m_ref.shape, NEG_INF, F32)
        acc_ref[...] = jnp.zeros(acc_ref.shape, F32)

    @pl.when(valid)
    def _():
        k0 = ki * TK
        kidx = k0 + lax.broadcasted_iota(jnp.int32, (TK, TQ), 0)
        tq = q0 + lax.broadcasted_iota(jnp.int32, (TK, TQ), 1)
        if mode == "mask":
            bias = x_ref[0].astype(F32)
        elif mode == "window":
            bias = jnp.where((kidx <= tq) & (kidx > tq - WINDOW), 0.0, NEG_INF)
        else:
            causal = kidx <= tq
            nbp = x_ref.shape[2]
            blk_of_key = (k0 + lax.broadcasted_iota(jnp.int32, (TK, nbp), 0)) // SEL_BLOCK
            expand_t = jnp.where(lax.broadcasted_iota(jnp.int32, (TK, nbp), 1) == blk_of_key, 1.0, 0.0)
            expand_t = expand_t.astype(MXU_DTYPE)

        for g in range(G):
            if mode == "block":
                picked = jnp.dot(expand_t, x_ref[0, g].astype(MXU_DTYPE), preferred_element_type=F32)
                b = jnp.where(causal & (picked > 0.5), 0.0, NEG_INF)
            else:
                b = bias
            if J > 1:
                b = jnp.concatenate([b] * J, axis=1)
                qt = jnp.concatenate([qt_ref[0, g * J + j] for j in range(J)], axis=1)
            else:
                qt = qt_ref[0, g]
            s = _dot(k_ref[0, g], qt) + b
            m_prev = m_ref[g]
            m_new = jnp.maximum(m_prev, jnp.max(s, axis=0, keepdims=True))
            alpha = jnp.exp(m_prev - m_new)
            p = jnp.exp((s - m_new).astype(MXU_DTYPE))
            acc_ref[g] = alpha * acc_ref[g] + _dot(vt_ref[0, g], p)
            m_ref[g] = m_new

    @pl.when(kk == pl.num_programs(2) - 1)
    def _():
        for g in range(G):
            o = acc_ref[g, 0:HD] / jnp.maximum(acc_ref[g, HD:HD + 1], TINY)
            for j in range(J):
                o_ref[0, g * J + j] = o[:, j * TQ:(j + 1) * TQ].astype(o_ref.dtype)


def _flash(qt, k, vt, extra, mode, TQ, TK, out_dtype):
    B, H, HD, S = qt.shape
    G = k.shape[1]
    J = H // G
    N = J * TQ
    nq = S // TQ
    last_of = lambda qi: (qi * TQ + TQ - 1) // TK
    if mode == "window":
        NWIN = min((WINDOW - 1 + TK - 1) // TK + 1, S // TK)
        nk = NWIN
        kidx_of = lambda qi, kk: jnp.maximum(last_of(qi) - (NWIN - 1) + kk, 0)
    else:
        NWIN = 0
        nk = S // TK
        kidx_of = lambda qi, kk: jnp.minimum(kk, last_of(qi))
    in_specs = [pl.BlockSpec((1, H, HD, TQ), lambda b, qi, kk: (b, 0, 0, qi)),
                pl.BlockSpec((1, G, TK, HD), lambda b, qi, kk: (b, 0, kidx_of(qi, kk), 0)),
                pl.BlockSpec((1, G, HD + ONES_ROWS, TK), lambda b, qi, kk: (b, 0, 0, kidx_of(qi, kk)))]
    args = [qt, k, vt]
    if mode == "mask":
        in_specs.append(pl.BlockSpec((1, TK, TQ), lambda b, qi, kk: (b, kidx_of(qi, kk), qi)))
        args.append(extra)
    elif mode == "block":
        nbp = extra.shape[2]
        in_specs.append(pl.BlockSpec((1, G, nbp, TQ), lambda b, qi, kk: (b, 0, 0, qi)))
        args.append(extra)
    kern = functools.partial(_flash_kernel, mode=mode, G=G, J=J, TQ=TQ, TK=TK, NWIN=NWIN)
    return pl.pallas_call(
        kern,
        grid=(B, nq, nk),
        in_specs=in_specs,
        out_specs=pl.BlockSpec((1, H, HD, TQ), lambda b, qi, kk: (b, 0, 0, qi)),
        out_shape=jax.ShapeDtypeStruct((B, H, HD, S), out_dtype),
        scratch_shapes=[pltpu.VMEM((G, 1, N), F32),
                        pltpu.VMEM((G, HD + ONES_ROWS, N), F32)],
        compiler_params=_cparams(("arbitrary", "arbitrary", "arbitrary")),
    )(*args)


def _dot_tn(a_t, b):
    return lax.dot_general(a_t.astype(MXU_DTYPE), b.astype(MXU_DTYPE),
                           (((0,), (0,)), ((), ())), preferred_element_type=F32)


def _route(h, wr, br):
    tm = h.shape[0]
    lane = lax.broadcasted_iota(jnp.int32, (tm, LANES), 1)
    h_hi = h.astype(MXU_DTYPE)
    h_lo = (h - h_hi.astype(F32)).astype(MXU_DTYPE)
    w_hi = wr.astype(MXU_DTYPE)
    w_lo = (wr - w_hi.astype(F32)).astype(MXU_DTYPE)
    d = lambda a, b: jnp.dot(a, b, preferred_element_type=F32)
    logits = d(h_hi, w_hi) + d(h_hi, w_lo) + d(h_lo, w_hi) + br
    is_e = lane < N_EXPERTS
    is_g = (lane >= N_EXPERTS) & (lane < N_EXPERTS + N_GROUPS)
    lg = jnp.where(is_g, logits, LOWEST)
    mg = jnp.max(lg, axis=-1, keepdims=True)
    gsel = jnp.min(jnp.where(is_g & (lg == mg), lane - N_EXPERTS, N_GROUPS), axis=-1, keepdims=True)
    pg_sel = 1.0 / jnp.sum(jnp.where(is_g, jnp.exp(lg - mg), 0.0), axis=-1, keepdims=True)
    in_grp = is_e & ((lane // EXPERTS_PER_GROUP) == gsel)
    le = jnp.where(in_grp, logits, LOWEST)
    me = jnp.max(le, axis=-1, keepdims=True)
    ex = jnp.where(in_grp, jnp.exp(le - me), 0.0)
    pe = ex / jnp.sum(ex, axis=-1, keepdims=True)
    pe = jnp.where(in_grp, pe, -1.0)
    p1 = jnp.max(pe, axis=-1, keepdims=True)
    i1 = jnp.min(jnp.where(pe == p1, lane, LANES), axis=-1, keepdims=True)
    pe2 = jnp.where(lane == i1, -1.0, pe)
    p2 = jnp.max(pe2, axis=-1, keepdims=True)
    i2 = jnp.min(jnp.where(pe2 == p2, lane, LANES), axis=-1, keepdims=True)
    tot = p1 + p2
    comb = jnp.where(lane == i1, p1 / tot * pg_sel, 0.0) + jnp.where(lane == i2, p2 / tot * pg_sel, 0.0)
    return jnp.where(lane == N_EXPERTS, gsel.astype(F32), comb)


def _outproj_kernel(x_ref, oa_ref, oc_ref, os_ref, ow_ref, br_ref, mg_ref, wa_ref, wb_ref, wo_ref,
                    gt_ref, g2_ref, sc_ref, sh_ref, wr_ref, brt_ref, o_ref, h_ref, route_ref):
    HD = HEAD_DIM
    D = x_ref.shape[1]
    br = br_ref[0]
    parts = []
    for hh in range(NSA_HEADS):
        parts.append(br[3 * hh:3 * hh + 1] * oc_ref[0, hh] + br[3 * hh + 1:3 * hh + 2] * os_ref[0, hh]
                     + br[3 * hh + 2:3 * hh + 3] * ow_ref[0, hh])
    ob_t = jnp.concatenate(parts, axis=0)
    oa_t = jnp.concatenate([oa_ref[0, hh] for hh in range(DSA_HEADS)], axis=0)
    ua = _dot_tn(oa_t, wa_ref[...])
    ub = _dot_tn(ob_t, wb_ref[...])
    mg = mg_ref[...]
    merged = jax.nn.sigmoid(mg[:, :D]) * ua + jax.nn.sigmoid(mg[:, D:]) * ub
    x1 = x_ref[...] + gt_ref[0] * _dot(merged, wo_ref[...])
    o_ref[...] = x1
    h = _rms(x1, g2_ref[...]) * (1.0 + sc_ref[0]) + sh_ref[0]
    h_ref[...] = h
    route_ref[...] = _route(h, wr_ref[...], brt_ref[...])


def _out_proj(x2d, oa, oc, os_, ow, br, mg, wa, wb, wo, gt, g2, sc2, sh2, wr, brt, S, tm=256):
    T, D = x2d.shape
    B, H, HD, _ = oa.shape
    per_b = S // tm
    heads_cols = pl.BlockSpec((1, H, HD, tm), lambda i: (i // per_b, 0, 0, i % per_b))
    per_batch = pl.BlockSpec((1, 1, D), lambda i: (i // per_b, 0, 0))
    rows = lambda n: pl.BlockSpec((tm, n), lambda i: (i, 0))
    const = lambda a, b: pl.BlockSpec((a, b), lambda i: (0, 0))
    return pl.pallas_call(
        _outproj_kernel,
        grid=(T // tm,),
        in_specs=[rows(D), heads_cols, heads_cols, heads_cols, heads_cols,
                  pl.BlockSpec((1, br.shape[1], tm), lambda i: (i // per_b, 0, i % per_b)),
                  rows(2 * D), const(H * HD, D), const(H * HD, D), const(D, D), per_batch,
                  const(1, D), per_batch, per_batch, const(D, LANES), const(1, LANES)],
        out_specs=[rows(D), rows(D), rows(LANES)],
        out_shape=[jax.ShapeDtypeStruct((T, D), F32), jax.ShapeDtypeStruct((T, D), F32),
                   jax.ShapeDtypeStruct((T, LANES), F32)],
        compiler_params=_cparams(("arbitrary",)),
    )(x2d, oa, oc, os_, ow, br, mg, wa, wb, wo, gt, g2.reshape(1, D), sc2, sh2, wr, brt)


def _row_copy(src_hbm, t, dst, r, sem):
    return pltpu.make_async_copy(src_hbm.at[pl.ds(t, 1)], dst.at[pl.ds(r, 1)], sem)


def _start_rows(idx_ref, base, r0, n, src_hbm, dst, sem):
    for r in range(n):
        _row_copy(src_hbm, idx_ref[base + r0 + r], dst, r0 + r, sem).start()


def _wait_rows(src_hbm, dst, n, sem):
    pltpu.make_async_copy(src_hbm.at[pl.ds(0, n)], dst.at[pl.ds(0, n)], sem).wait()


def _experts_kernel(tg_ref, src_ref, h_hbm, r_ref, wg_ref, wu_ref, wd_ref, o_ref,
                    hbuf, acc_ref, sems, *, TM):
    i = pl.program_id(0)
    e = pl.program_id(1)
    n_tiles = pl.num_programs(0)
    EPG = EXPERTS_PER_GROUP
    part = TM // EPG
    slot = i % 2

    @pl.when((i == 0) & (e == 0))
    def _():
        def start(c, carry):
            _start_rows(src_ref, 0, pl.multiple_of(c * part, part), part, h_hbm, hbuf.at[0], sems.at[0])
            return carry
        lax.fori_loop(0, EPG, start, 0)

    @pl.when(e == 0)
    def _():
        _wait_rows(h_hbm, hbuf.at[slot], TM, sems.at[slot])
        acc_ref[...] = jnp.zeros(acc_ref.shape, F32)

    @pl.when(i + 1 < n_tiles)
    def _():
        _start_rows(src_ref, (i + 1) * TM, e * part, part, h_hbm, hbuf.at[1 - slot], sems.at[1 - slot])

    h = hbuf[slot].astype(MXU_DTYPE)
    a = jnp.dot(h, wg_ref[0], preferred_element_type=F32)
    u = jnp.dot(h, wu_ref[0], preferred_element_type=F32)
    y = _dot(a * jax.nn.sigmoid(a) * u, wd_ref[0])
    lane = lax.broadcasted_iota(jnp.int32, (TM, LANES), 1)
    expert = tg_ref[i] * EPG + e
    w_e = jnp.sum(jnp.where(lane == expert, r_ref[...], 0.0), axis=-1, keepdims=True)
    acc_ref[...] += w_e * y

    @pl.when(e == pl.num_programs(1) - 1)
    def _():
        o_ref[...] = acc_ref[...]


def _experts(h2, route, tile_group, src, wg, wu, wd, TM):
    T, D = h2.shape
    DE = wg.shape[2]
    P = src.shape[0]
    EPG = EXPERTS_PER_GROUP
    w_idx = lambda i, e, tg, src: (tg[i] * EPG + e, 0, 0)
    kern = functools.partial(_experts_kernel, TM=TM)
    grid_spec = pltpu.PrefetchScalarGridSpec(
        num_scalar_prefetch=2,
        grid=(P // TM, EPG),
        in_specs=[pl.BlockSpec(memory_space=pl.ANY),
                  pl.BlockSpec((TM, LANES), lambda i, e, tg, src: (i, 0)),
                  pl.BlockSpec((1, D, DE), w_idx),
                  pl.BlockSpec((1, D, DE), w_idx),
                  pl.BlockSpec((1, DE, D), w_idx)],
        out_specs=pl.BlockSpec((TM, D), lambda i, e, tg, src: (i, 0)),
        scratch_shapes=[pltpu.VMEM((2, TM, D), F32),
                        pltpu.VMEM((TM, D), F32),
                        pltpu.SemaphoreType.DMA((2,))])
    return pl.pallas_call(
        kern,
        grid_spec=grid_spec,
        out_shape=jax.ShapeDtypeStruct((P, D), F32),
        compiler_params=_cparams(("arbitrary", "arbitrary")),
    )(tile_group, src, h2, route, wg, wu, wd)


def _combine_kernel(pos_ref, x_ref, y_hbm, gt_ref, gf_ref, o_ref, ybuf, sem, *, TM):
    part = 64
    def start(c, carry):
        _start_rows(pos_ref, pl.program_id(0) * TM, pl.multiple_of(c * part, part), part, y_hbm, ybuf,
                    sem.at[0])
        return carry
    lax.fori_loop(0, TM // part, start, 0)
    _wait_rows(y_hbm, ybuf, TM, sem.at[0])
    o_ref[...] = _rms(x_ref[...] + gt_ref[0] * ybuf[...], gf_ref[...])


def _combine(x1, ys, pos, gt, gf, S, tm=256):
    T, D = x1.shape
    per_b = S // tm
    grid_spec = pltpu.PrefetchScalarGridSpec(
        num_scalar_prefetch=1,
        grid=(T // tm,),
        in_specs=[pl.BlockSpec((tm, D), lambda i, pos: (i, 0)),
                  pl.BlockSpec(memory_space=pl.ANY),
                  pl.BlockSpec((1, 1, D), lambda i, pos: (i // per_b, 0, 0)),
                  pl.BlockSpec((1, D), lambda i, pos: (0, 0))],
        out_specs=pl.BlockSpec((tm, D), lambda i, pos: (i, 0)),
        scratch_shapes=[pltpu.VMEM((tm, D), F32), pltpu.SemaphoreType.DMA((1,))])
    return pl.pallas_call(
        functools.partial(_combine_kernel, TM=tm),
        grid_spec=grid_spec,
        out_shape=jax.ShapeDtypeStruct((T, D), F32),
        compiler_params=_cparams(("arbitrary",)),
    )(pos, x1, ys, gt, gf.reshape(1, D))


def _routed_moe(x1, h2, route, gt, gf, wg, wu, wd, S, TM=512):
    T, D = x1.shape
    gsel = route[:, N_EXPERTS].astype(jnp.int32)
    onehot = (gsel[:, None] == jnp.arange(N_GROUPS, dtype=jnp.int32)[None, :]).astype(jnp.int32)
    csum = jnp.cumsum(onehot, axis=0)
    rank = jnp.take_along_axis(csum, gsel[:, None], axis=1)[:, 0] - 1
    padded = -(-csum[-1] // TM) * TM
    ends = jnp.cumsum(padded)
    starts = ends - padded
    pos = (jnp.take(starts, gsel) + rank).astype(jnp.int32)
    P = T + N_GROUPS * TM
    src = jnp.zeros((P,), jnp.int32).at[pos].set(jnp.arange(T, dtype=jnp.int32))
    tile_start = jnp.arange(P // TM, dtype=jnp.int32) * TM
    tile_group = jnp.minimum(jnp.sum(tile_start[:, None] >= ends[None, :], axis=1), N_GROUPS - 1)
    ys = _experts(h2, jnp.take(route, src, axis=0), tile_group.astype(jnp.int32), src, wg, wu, wd, TM)
    return _combine(x1, ys, pos, gt, gf, S)


def _layer(x, mod, positions, g_norm1, w_in, g_kv_latent, w_kv_up, pe_cmp_k, pe_cmp_v,
           w_cmp1_k, w_cmp2_k, w_cmp1_v, w_cmp2_v, w_up_a, w_up_b, w_out, g_norm2,
           w_router_group, b_router_group, w_router_expert, b_router_expert,
           w_expert_gate, w_expert_up, w_expert_down, g_out):
    B, S, D = x.shape
    T = B * S
    HD, G, J = HEAD_DIM, NSA_KV_GROUPS, NSA_REP
    HA = DSA_HEADS * HD
    kvb = G * HD
    scale = HD ** -0.5
    topk_a = min(DSA_TOPK_MAX, S // 4)
    n_sel = min(SEL_COUNT, S // SEL_BLOCK)
    mod6 = mod.reshape(B, 6, 1, D)
    sh1, sc1, gt1, sh2, sc2, gt2 = (mod6[:, i] for i in range(6))

    x2d = x.reshape(T, D)
    (mg, qt_a, k_a, vt_a, iqt, iwt, ik, qt_b, kcvc, ks, kw, vst, vwt, br) = _prep(
        x, sc1, sh1, g_norm1, w_in, g_kv_latent, w_kv_up, positions)

    kc = _compress(kcvc[..., :kvb].reshape(B, S, G, HD), pe_cmp_k, w_cmp1_k, w_cmp2_k)
    vc = _compress(kcvc[..., kvb:].reshape(B, S, G, HD), pe_cmp_v, w_cmp1_v, w_cmp2_v)
    n_cmp = kc.shape[1]
    ncp = -(-(n_cmp + 1) // LANES) * LANES
    pad_c = lambda t: jnp.pad(t, ((0, 0), (0, ncp - n_cmp), (0, 0), (0, 0))).astype(MXU_DTYPE)
    kc, vct = pad_c(kc).transpose(0, 2, 1, 3), pad_c(vc).transpose(0, 2, 3, 1)

    TA = min(512, S)
    sel_bias = _dsa_select(ik, iqt, iwt, topk_a)
    o_a = _flash(qt_a, k_a, vt_a, sel_bias, "mask", TA, TA, MXU_DTYPE)

    TB = min(256, S)
    nbp = -(-(S // SEL_BLOCK) // LANES) * LANES
    o_c, blk_mask = _cmp_attn(qt_b, kc, vct, n_sel, nbp, TB)
    o_s = _flash(qt_b, ks, vst, blk_mask, "block", TB, min(512, S), F32)
    o_w = _flash(qt_b, kw, vwt, None, "window", TB, TB, F32)

    wr = jnp.concatenate([w_router_expert, w_router_group], axis=1)
    wr = jnp.pad(wr, ((0, 0), (0, LANES - wr.shape[1])))
    brt = jnp.concatenate([b_router_expert, b_router_group])
    brt = jnp.pad(brt, (0, LANES - brt.shape[0])).reshape(1, LANES)
    x1, h2, route = _out_proj(x2d, o_a, o_c, o_s, o_w, br, mg, w_up_a.astype(MXU_DTYPE),
                              w_up_b.astype(MXU_DTYPE), w_out.astype(MXU_DTYPE), gt1,
                              g_norm2, sc2, sh2, wr, brt, S)

    out = _routed_moe(x1, h2, route, gt2, g_out, w_expert_gate.astype(MXU_DTYPE),
                      w_expert_up.astype(MXU_DTYPE), w_expert_down.astype(MXU_DTYPE), S)
    return out.reshape(B, S, D)


def kernel(x, c, positions, w_ada, b_ada, g_norm1, w_in, g_kv_latent, w_kv_up, pe_cmp_k, pe_cmp_v,
           w_cmp1_k, w_cmp2_k, w_cmp1_v, w_cmp2_v, w_up_a, w_up_b, w_out, g_norm2, w_router_group,
           b_router_group, w_router_expert, b_router_expert, w_expert_gate, w_expert_up,
           w_expert_down, g_final):
    depth = w_ada.shape[0]
    assert depth == 1, "the fused final norm assumes a single layer"
    mod = _ada_mod(c, w_ada[0], b_ada[0])
    return _layer(x, mod, positions, g_norm1[0], w_in[0], g_kv_latent[0], w_kv_up[0], pe_cmp_k[0],
                  pe_cmp_v[0], w_cmp1_k[0], w_cmp2_k[0], w_cmp1_v[0], w_cmp2_v[0], w_up_a[0],
                  w_up_b[0], w_out[0], g_norm2[0], w_router_group[0], b_router_group[0],
                  w_router_expert[0], b_router_expert[0], w_expert_gate[0], w_expert_up[0],
                  w_expert_down[0], g_final)
```

```python
import functools

import numpy as np
import jax
import jax.numpy as jnp
from jax import lax
from jax.experimental import pallas as pl
from jax.experimental.pallas import tpu as pltpu

HEAD_DIM = 64
ROT_FRACTION = 4
ROPE_THETA = 500000.0
DSA_HEADS = 8
DSA_KV_RANK = 128
IDX_HEADS = 8
IDX_DIM = 32
DSA_TOPK_MAX = 256
NSA_HEADS = 8
NSA_KV_GROUPS = 2
NSA_REP = NSA_HEADS // NSA_KV_GROUPS
CMP_LEN = 32
CMP_STRIDE = 16
CMP_HIDDEN = 256
SEL_BLOCK = 64
SEL_COUNT = 16
WINDOW = 512
N_GROUPS = 4
EXPERTS_PER_GROUP = 8
N_EXPERTS = N_GROUPS * EXPERTS_PER_GROUP
D_EXPERT = 256
NORM_EPS = 1e-6
NEG_INF = -1e30
TINY = 1e-30
LOWEST = -3.0e38
FORCED = 1e30

LANES = 128
SUBLANES = 8
ONES_ROWS = 16
MXU_DTYPE = jnp.bfloat16
VMEM_LIMIT = 56 * 1024 * 1024

F32 = jnp.float32


def _cparams(sem):
    return pltpu.CompilerParams(dimension_semantics=sem, vmem_limit_bytes=VMEM_LIMIT)


def _dot(a, b):
    return jnp.dot(a.astype(MXU_DTYPE), b.astype(MXU_DTYPE), preferred_element_type=F32)


def _dot_exact_lhs(a01, b):
    hi = b.astype(MXU_DTYPE)
    r1 = b - hi.astype(F32)
    mid = r1.astype(MXU_DTYPE)
    lo = (r1 - mid.astype(F32)).astype(MXU_DTYPE)
    a = a01.astype(MXU_DTYPE)
    d = lambda u: jnp.dot(a, u, preferred_element_type=F32)
    return d(hi) + d(mid) + d(lo)


def _rms(x, g):
    return x * lax.rsqrt(jnp.mean(x * x, axis=-1, keepdims=True) + NORM_EPS) * g


def _ada_kernel(c_ref, w_ref, b_ref, o_ref):
    c = c_ref[...]
    cond = c * jax.nn.sigmoid(c)
    o_ref[...] = _dot(cond, w_ref[...]) + b_ref[...]


def _ada_mod(c, w_ada, b_ada):
    B, D = c.shape
    n_out = w_ada.shape[1]
    rows = SUBLANES
    cp = jnp.zeros((rows, D), F32).at[:B].set(c)
    tn = 1024
    out = pl.pallas_call(
        _ada_kernel,
        grid=(n_out // tn,),
        in_specs=[pl.BlockSpec((rows, D), lambda j: (0, 0)),
                  pl.BlockSpec((D, tn), lambda j: (0, j)),
                  pl.BlockSpec((1, tn), lambda j: (0, j))],
        out_specs=pl.BlockSpec((rows, tn), lambda j: (0, j)),
        out_shape=jax.ShapeDtypeStruct((rows, n_out), F32),
        compiler_params=_cparams(("arbitrary",)),
    )(cp, w_ada, b_ada.reshape(1, n_out))
    return out[:B]


KVB = NSA_KV_GROUPS * HEAD_DIM
ROW_SEGS = (("mg", 2048), ("lat", DSA_KV_RANK), ("kc", KVB), ("vc", KVB), ("ks", KVB), ("kw", KVB),
            ("ik", LANES))
COL_SEGS = (("qa", DSA_HEADS * HEAD_DIM), ("qb", NSA_HEADS * HEAD_DIM), ("iq", IDX_HEADS * IDX_DIM),
            ("vs", KVB), ("vw", KVB), ("lat", DSA_KV_RANK), ("iw", IDX_HEADS), ("br", 3 * NSA_HEADS))


def _seg_offsets(segs):
    out, pos = {}, 0
    for name, n in segs:
        out[name] = pos
        pos += n
    return out, pos


ROW_OFF, ROW_COLS = _seg_offsets(ROW_SEGS)
COL_OFF, COL_ROWS = _seg_offsets(COL_SEGS)


def _rot_lanes(x, c, sa, sb, half):
    outs = []
    for j in range(x.shape[1] // LANES):
        xs = x[:, j * LANES:(j + 1) * LANES]
        outs.append(xs * c + pltpu.roll(xs, half, 1) * sa + pltpu.roll(xs, LANES - half, 1) * sb)
    return outs[0] if len(outs) == 1 else jnp.concatenate(outs, axis=1)


def _prep_kernel(x_ref, xt_ref, g_ref, sc_ref, sh_ref, gc_ref, scc_ref, shc_ref, wr_ref, wc_ref,
                 gkv_ref, gkvc_ref, wkk_ref, wkv_ref, c64_ref, sa64_ref, sb64_ref,
                 c32_ref, sa32_ref, sb32_ref, cos_ref, sin_ref, c32t_ref, s32t_ref,
                 mg_ref, qa_ref, ka_ref, va_ref, iq_ref, iw_ref, ik_ref, qb_ref, kcvc_ref,
                 ks_ref, kw_ref, vs_ref, vw_ref, br_ref):
    HD = HEAD_DIM
    half = HD // ROT_FRACTION // 2
    scale = HD ** -0.5
    h = _rms(x_ref[...], g_ref[...]) * (1.0 + sc_ref[0]) + sh_ref[0]
    pr = _dot(h, wr_ref[...])
    row = lambda name, n: pr[:, ROW_OFF[name]:ROW_OFF[name] + n]
    mg_ref[...] = row("mg", 2048)
    c64, sa64, sb64 = c64_ref[...], sa64_ref[...], sb64_ref[...]
    ka = _rot_lanes(_dot(_rms(row("lat", DSA_KV_RANK), gkv_ref[...]), wkk_ref[...]), c64, sa64, sb64, half)
    for hh in range(DSA_HEADS):
        ka_ref[0, hh] = ka[:, hh * HD:(hh + 1) * HD].astype(ka_ref.dtype)
    kcvc_ref[0] = jnp.concatenate([_rot_lanes(row("kc", KVB), c64, sa64, sb64, half), row("vc", KVB)], axis=1)
    ks = _rot_lanes(row("ks", KVB), c64, sa64, sb64, half)
    kw = _rot_lanes(row("kw", KVB), c64, sa64, sb64, half)
    for g in range(NSA_KV_GROUPS):
        ks_ref[0, g] = ks[:, g * HD:(g + 1) * HD].astype(ks_ref.dtype)
        kw_ref[0, g] = kw[:, g * HD:(g + 1) * HD].astype(kw_ref.dtype)
    ik = _rot_lanes(row("ik", LANES), c32_ref[...], sa32_ref[...], sb32_ref[...], IDX_DIM // ROT_FRACTION // 2)
    ik_ref[0] = ik[:, :IDX_DIM].astype(ik_ref.dtype)
    xt = xt_ref[0]
    ht = xt * lax.rsqrt(jnp.mean(xt * xt, axis=0, keepdims=True) + NORM_EPS) * gc_ref[...]
    ht = ht * (1.0 + scc_ref[0]) + shc_ref[0]
    pc = _dot(wc_ref[...], ht)
    col = lambda name, n: pc[COL_OFF[name]:COL_OFF[name] + n]
    cos, sin = cos_ref[0], sin_ref[0]

    def rot_rows(blk):
        x1, x2 = blk[0:half], blk[half:2 * half]
        return jnp.concatenate([x1 * cos - x2 * sin, x2 * cos + x1 * sin, blk[2 * half:]], axis=0)

    qa, qb = col("qa", DSA_HEADS * HD), col("qb", NSA_HEADS * HD)
    for hh in range(DSA_HEADS):
        qa_ref[0, hh] = (rot_rows(qa[hh * HD:(hh + 1) * HD]) * scale).astype(qa_ref.dtype)
    for hh in range(NSA_HEADS):
        qb_ref[0, hh] = (rot_rows(qb[hh * HD:(hh + 1) * HD]) * scale).astype(qb_ref.dtype)
    iq = col("iq", IDX_HEADS * IDX_DIM)
    c32t, s32t = c32t_ref[0], s32t_ref[0]
    for hh in range(IDX_HEADS):
        blk = iq[hh * IDX_DIM:(hh + 1) * IDX_DIM]
        top = blk[0:SUBLANES]
        top = top * c32t + pltpu.roll(top, SUBLANES // 2, 0) * s32t
        iq_ref[0, hh] = jnp.concatenate([top, blk[SUBLANES:]], axis=0).astype(iq_ref.dtype)
    ones = jnp.ones((ONES_ROWS, xt.shape[1]), F32)
    with_ones = lambda v: jnp.concatenate([v, ones], axis=0)
    vs, vw = col("vs", KVB), col("vw", KVB)
    for g in range(NSA_KV_GROUPS):
        vs_ref[0, g] = with_ones(vs[g * HD:(g + 1) * HD]).astype(vs_ref.dtype)
        vw_ref[0, g] = with_ones(vw[g * HD:(g + 1) * HD]).astype(vw_ref.dtype)
    lat = col("lat", DSA_KV_RANK)
    lat = lat * lax.rsqrt(jnp.mean(lat * lat, axis=0, keepdims=True) + NORM_EPS) * gkvc_ref[...]
    va = _dot(wkv_ref[...], lat)
    for hh in range(DSA_HEADS):
        va_ref[0, hh] = with_ones(va[hh * HD:(hh + 1) * HD]).astype(va_ref.dtype)
    iw_ref[0] = col("iw", IDX_HEADS)
    br_ref[0] = jax.nn.sigmoid(col("br", 3 * NSA_HEADS))


def _prep(x, mod_sc, mod_sh, g_norm1, w_in, g_kv, w_kv_up, positions, tm=256):
    B, S, D = x.shape
    T = B * S
    HD, HA, G = HEAD_DIM, DSA_HEADS, NSA_KV_GROUPS
    per_b = S // tm
    sizes = [HA * HD, DSA_KV_RANK, IDX_HEADS * IDX_DIM, IDX_HEADS, IDX_DIM, NSA_HEADS * HD,
             KVB, KVB, KVB, KVB, KVB, KVB, 3 * NSA_HEADS, 2 * D]
    names = ["qa", "lat", "iq", "iw", "ik", "qb", "kc", "vc", "ks", "vs", "kw", "vw", "br", "mg"]
    starts = dict(zip(names, np.concatenate([[0], np.cumsum(sizes)[:-1]]).astype(int)))
    width = dict(zip(names, sizes))
    def seg(name, n):
        w = w_in[:, starts[name]:starts[name] + width[name]]
        return jnp.pad(w, ((0, 0), (0, n - width[name])))
    w_row = jnp.concatenate([seg(n, k) for n, k in ROW_SEGS], axis=1).astype(MXU_DTYPE)
    w_col = jnp.concatenate([seg(n, k) for n, k in COL_SEGS], axis=1).T.astype(MXU_DTYPE)
    w_kk = w_kv_up[:, :HA * HD].astype(MXU_DTYPE)
    w_kv = w_kv_up[:, HA * HD:].T.astype(MXU_DTYPE)
    wide = lambda v: jnp.broadcast_to(v[..., None], v.shape + (tm,))

    def tables(dim):
        half = dim // ROT_FRACTION // 2
        inv_freq = ROPE_THETA ** (-jnp.arange(half, dtype=F32) / half)
        ang = positions.astype(F32)[..., None] * inv_freq
        return jnp.cos(ang), jnp.sin(ang), half
    def lane_tables(dim):
        cos, sin, half = tables(dim)
        reps = LANES // dim
        z = jnp.zeros((B, S, dim - 2 * half), F32)
        c = jnp.concatenate([cos, cos, z + 1.0], axis=-1)
        sa = jnp.concatenate([jnp.zeros_like(sin), sin, z], axis=-1)
        sb = jnp.concatenate([-sin, jnp.zeros_like(sin), z], axis=-1)
        return [jnp.tile(t, (1, 1, reps)).reshape(T, LANES) for t in (c, sa, sb)]
    c64, sa64, sb64 = lane_tables(HD)
    c32, sa32, sb32 = lane_tables(IDX_DIM)
    cos64, sin64, _ = tables(HD)
    cos_t, sin_t = cos64.transpose(0, 2, 1), sin64.transpose(0, 2, 1)
    cos32, sin32, _ = tables(IDX_DIM)
    c32t = jnp.concatenate([cos32, cos32], axis=-1).transpose(0, 2, 1)
    s32t = jnp.concatenate([-sin32, sin32], axis=-1).transpose(0, 2, 1)

    row_blk = lambda n: pl.BlockSpec((tm, n), lambda i: (i, 0))
    const = lambda shape: pl.BlockSpec(shape, lambda i: (0,) * len(shape))
    per_batch = lambda shape: pl.BlockSpec((1,) + shape, lambda i: (i // per_b,) + (0,) * len(shape))
    tok_cols = lambda rows: pl.BlockSpec((1, rows, tm), lambda i: (i // per_b, 0, i % per_b))
    heads_cols = lambda h, rows: pl.BlockSpec((1, h, rows, tm), lambda i: (i // per_b, 0, 0, i % per_b))
    heads_rows = lambda h, n: pl.BlockSpec((1, h, tm, n), lambda i: (i // per_b, 0, i % per_b, 0))
    bf = MXU_DTYPE
    out_shape = [jax.ShapeDtypeStruct((T, 2 * D), F32),
                 jax.ShapeDtypeStruct((B, HA, HD, S), bf),
                 jax.ShapeDtypeStruct((B, HA, S, HD), bf),
                 jax.ShapeDtypeStruct((B, HA, HD + ONES_ROWS, S), bf),
                 jax.ShapeDtypeStruct((B, IDX_HEADS, IDX_DIM, S), bf),
                 jax.ShapeDtypeStruct((B, IDX_HEADS, S), F32),
                 jax.ShapeDtypeStruct((B, S, IDX_DIM), bf),
                 jax.ShapeDtypeStruct((B, NSA_HEADS, HD, S), bf),
                 jax.ShapeDtypeStruct((B, S, 2 * KVB), F32),
                 jax.ShapeDtypeStruct((B, G, S, HD), bf),
                 jax.ShapeDtypeStruct((B, G, S, HD), bf),
                 jax.ShapeDtypeStruct((B, G, HD + ONES_ROWS, S), bf),
                 jax.ShapeDtypeStruct((B, G, HD + ONES_ROWS, S), bf),
                 jax.ShapeDtypeStruct((B, 3 * NSA_HEADS, S), F32)]
    out_specs = [row_blk(2 * D), heads_cols(HA, HD), heads_rows(HA, HD), heads_cols(HA, HD + ONES_ROWS),
                 heads_cols(IDX_HEADS, IDX_DIM), tok_cols(IDX_HEADS),
                 pl.BlockSpec((1, tm, IDX_DIM), lambda i: (i // per_b, i % per_b, 0)),
                 heads_cols(NSA_HEADS, HD),
                 pl.BlockSpec((1, tm, 2 * KVB), lambda i: (i // per_b, i % per_b, 0)),
                 heads_rows(G, HD), heads_rows(G, HD), heads_cols(G, HD + ONES_ROWS),
                 heads_cols(G, HD + ONES_ROWS),
                 tok_cols(3 * NSA_HEADS)]
    half64 = HD // ROT_FRACTION // 2
    in_specs = [row_blk(D), tok_cols(D), const((1, D)), per_batch((1, D)), per_batch((1, D)),
                const((D, tm)), per_batch((D, tm)), per_batch((D, tm)),
                const((D, ROW_COLS)), const((COL_ROWS, D)),
                const((1, DSA_KV_RANK)), const((DSA_KV_RANK, tm)),
                const((DSA_KV_RANK, HA * HD)), const((HA * HD, DSA_KV_RANK)),
                row_blk(LANES), row_blk(LANES), row_blk(LANES), row_blk(LANES), row_blk(LANES), row_blk(LANES),
                tok_cols(half64), tok_cols(half64), tok_cols(SUBLANES), tok_cols(SUBLANES)]
    return pl.pallas_call(
        _prep_kernel,
        grid=(T // tm,),
        in_specs=in_specs,
        out_specs=out_specs,
        out_shape=out_shape,
        compiler_params=_cparams(("arbitrary",)),
    )(x.reshape(T, D), x.transpose(0, 2, 1), g_norm1.reshape(1, D), mod_sc, mod_sh,
      wide(g_norm1), wide(mod_sc[:, 0]), wide(mod_sh[:, 0]), w_row, w_col,
      g_kv.reshape(1, DSA_KV_RANK), wide(g_kv), w_kk, w_kv,
      c64, sa64, sb64, c32, sa32, sb32, cos_t, sin_t, c32t, s32t)


def _cmp_kernel(f_ref, w1_ref, w2_ref, o_ref):
    hid = _dot(f_ref[...], w1_ref[...])
    hid = hid * jax.nn.sigmoid(hid)
    o_ref[...] = _dot(hid, w2_ref[...])


def _compress(tok, pe, w1, w2):
    B, S, G, HD = tok.shape
    r = CMP_LEN // CMP_STRIDE
    n_chunks = S // CMP_STRIDE
    n_cmp = n_chunks - r + 1
    chunks = tok.reshape(B, n_chunks, CMP_STRIDE, G, HD)
    blocks = jnp.concatenate([chunks[:, i:n_cmp + i] for i in range(r)], axis=2)
    blocks = blocks + pe[:, None, :]
    flat = blocks.transpose(0, 1, 3, 2, 4).reshape(B * n_cmp * G, CMP_LEN * HD)
    rows = flat.shape[0]
    tm = 512
    rows_p = -(-rows // tm) * tm
    flat = jnp.pad(flat, ((0, rows_p - rows), (0, 0))).astype(MXU_DTYPE)
    out = pl.pallas_call(
        _cmp_kernel,
        grid=(rows_p // tm,),
        in_specs=[pl.BlockSpec((tm, CMP_LEN * HD), lambda i: (i, 0)),
                  pl.BlockSpec((CMP_LEN * HD, CMP_HIDDEN), lambda i: (0, 0)),
                  pl.BlockSpec((CMP_HIDDEN, HD), lambda i: (0, 0))],
        out_specs=pl.BlockSpec((tm, HD), lambda i: (i, 0)),
        out_shape=jax.ShapeDtypeStruct((rows_p, HD), F32),
        compiler_params=_cparams(("arbitrary",)),
    )(flat, w1.astype(MXU_DTYPE), w2.astype(MXU_DTYPE))
    return out[:rows].reshape(B, n_cmp, G, HD)


def _select_kernel(ik_ref, iqt_ref, iwt_ref, bias_ref, sc_ref, *, TQ, KC, S, K, NBIS):
    qi = pl.program_id(1)
    q0 = qi * TQ
    n_ch = (q0 + TQ + KC - 1) // KC
    n_all = S // KC
    t_row = q0 + lax.broadcasted_iota(jnp.int32, (1, TQ), 1)
    key_iota = lax.broadcasted_iota(jnp.int32, (KC, TQ), 0)
    kf = float(K)
    SUB = LANES
    sub_iota = lax.broadcasted_iota(jnp.int32, (SUB, TQ), 0)

    def score_chunk(c, carry):
        mn, mx = carry
        for u in range(KC // SUB):
            off = pl.multiple_of(c * KC + u * SUB, SUB)
            ikc = ik_ref[0, pl.ds(off, SUB), :]
            acc = jnp.zeros((SUB, TQ), F32)
            for h in range(IDX_HEADS):
                lg = jnp.dot(ikc, iqt_ref[0, h], preferred_element_type=F32)
                acc = acc + jnp.maximum(lg, 0.0) * iwt_ref[0, h:h + 1, :]
            causal = off + sub_iota <= t_row
            sc_ref[pl.ds(off, SUB), :] = jnp.where(causal, acc, NEG_INF)
            rows = lambda x: x.reshape(SUB // SUBLANES, SUBLANES, TQ)
            mn = jnp.minimum(mn, jnp.min(rows(jnp.where(causal, acc, -LOWEST)), axis=0))
            mx = jnp.maximum(mx, jnp.max(rows(jnp.where(causal, acc, LOWEST)), axis=0))
        return mn, mx

    mn, mx = lax.fori_loop(0, n_ch, score_chunk, (jnp.full((SUBLANES, TQ), -LOWEST, F32),
                                                  jnp.full((SUBLANES, TQ), LOWEST, F32)))
    lo = jnp.min(mn, axis=0, keepdims=True)
    hi = jnp.max(mx, axis=0, keepdims=True)

    def chunk(c):
        off = pl.multiple_of(c * KC, KC)
        return sc_ref[pl.ds(off, KC), :], off

    AR = 4 * SUBLANES

    def fold(x, op):
        return op(x.reshape(KC // AR, AR, TQ), axis=0)

    def key_pass(fn, init):
        def body(c, acc):
            xs, off = chunk(c)
            return fn(acc, xs, off)
        return lax.fori_loop(0, n_ch, body, init)

    def count_ge(thr):
        acc = key_pass(lambda a, xs, off: a + fold(jnp.where(xs >= thr, 1.0, 0.0), jnp.sum),
                       jnp.zeros((AR, TQ), F32))
        return jnp.sum(acc, axis=0, keepdims=True)

    def count_gt(thr):
        acc = key_pass(lambda a, xs, off: a + fold(jnp.where(xs > thr, 1.0, 0.0), jnp.sum),
                       jnp.zeros((AR, TQ), F32))
        return jnp.sum(acc, axis=0, keepdims=True)

    def max_where(bound):
        fn = lambda a, xs, off: jnp.maximum(a, fold(jnp.where(xs <= bound, xs, LOWEST), jnp.max))
        acc = key_pass(fn, jnp.full((AR, TQ), LOWEST, F32))
        return jnp.max(acc, axis=0, keepdims=True)

    def count_and_next(v):
        def fn(a, xs, off):
            cnt, nxt = a
            return (cnt + fold(jnp.where(xs >= v, 1.0, 0.0), jnp.sum),
                    jnp.maximum(nxt, fold(jnp.where(xs < v, xs, LOWEST), jnp.max)))
        cnt, nxt = key_pass(fn, (jnp.zeros((AR, TQ), F32), jnp.full((AR, TQ), LOWEST, F32)))
        return jnp.sum(cnt, axis=0, keepdims=True), jnp.max(nxt, axis=0, keepdims=True)

    def bisect(_, carry):
        lo, hi = carry
        mid = 0.5 * (lo + hi)
        ge = count_ge(mid) >= kf
        return jnp.where(ge, mid, lo), jnp.where(ge, hi, mid)

    lo, hi = lax.fori_loop(0, NBIS, bisect, (lo, hi))

    all_keys = jnp.where(t_row < K, 1.0, 0.0)
    v = max_where(hi)
    n_ge, nxt = count_and_next(v)
    done = jnp.maximum(all_keys, jnp.where(n_ge >= kf, 1.0, 0.0))

    def peel_cond(st):
        return jnp.sum(st[1]) < float(TQ)

    def peel_body(st):
        v, done, n_ge, nxt = st
        v = jnp.where(done > 0.5, v, nxt)
        cnt, nxt = count_and_next(v)
        n_ge = jnp.where(done > 0.5, n_ge, cnt)
        done = jnp.maximum(done, jnp.where(cnt >= kf, 1.0, 0.0))
        return v, done, n_ge, nxt

    v, done, n_ge, _ = lax.while_loop(peel_cond, peel_body, (v, done, n_ge, nxt))
    thr = jnp.where(all_keys > 0.5, LOWEST, v)
    has_tie = jnp.max(jnp.where((n_ge > kf) & (all_keys < 0.5), 1.0, 0.0)) > 0.5

    def write(off, sel):
        bias_ref[0, pl.ds(off, KC), :] = jnp.where(sel, 0.0, NEG_INF).astype(bias_ref.dtype)

    def fill_tail():
        def body(c, carry):
            write(pl.multiple_of(c * KC, KC), jnp.zeros((KC, TQ), jnp.bool_))
            return carry
        lax.fori_loop(n_ch, n_all, body, 0)

    @pl.when(jnp.logical_not(has_tie))
    def _():
        def body(c, carry):
            xs, off = chunk(c)
            write(off, (xs >= thr) & ((off + key_iota) <= t_row))
            return carry
        lax.fori_loop(0, n_ch, body, 0)
        fill_tail()

    @pl.when(has_tie)
    def _():
        need = kf - count_gt(thr)
        r_i = lax.broadcasted_iota(jnp.int32, (KC, KC), 0)
        c_i = lax.broadcasted_iota(jnp.int32, (KC, KC), 1)
        lower = jnp.where(c_i <= r_i, 1.0, 0.0).astype(MXU_DTYPE)
        def body(c, seen):
            xs, off = chunk(c)
            causal = (off + key_iota) <= t_row
            eq = jnp.where((xs == thr) & causal, 1.0, 0.0)
            rank = jnp.dot(lower, eq.astype(MXU_DTYPE), preferred_element_type=F32) + seen
            write(off, ((xs > thr) & causal) | ((eq > 0.5) & (rank <= need)))
            return seen + jnp.sum(eq, axis=0, keepdims=True)
        lax.fori_loop(0, n_ch, body, jnp.zeros((1, TQ), F32))
        fill_tail()


def _dsa_select(ik, iqt, iwt, K):
    B, S, DI = ik.shape
    H = iqt.shape[1]
    TQ = min(256, S)
    KC = min(256, S)
    kern = functools.partial(_select_kernel, TQ=TQ, KC=KC, S=S, K=K, NBIS=16)
    return pl.pallas_call(
        kern,
        grid=(B, S // TQ),
        in_specs=[pl.BlockSpec((1, S, DI), lambda b, q: (b, 0, 0)),
                  pl.BlockSpec((1, H, DI, TQ), lambda b, q: (b, 0, 0, q)),
                  pl.BlockSpec((1, H, TQ), lambda b, q: (b, 0, q))],
        out_specs=pl.BlockSpec((1, S, TQ), lambda b, q: (b, 0, q)),
        out_shape=jax.ShapeDtypeStruct((B, S, S), jnp.bfloat16),
        scratch_shapes=[pltpu.VMEM((S, TQ), F32)],
        compiler_params=_cparams(("arbitrary", "arbitrary")),
    )(ik, iqt, iwt)


def _cmp_attn_kernel(qt_ref, kc_ref, vct_ref, o_ref, bm_ref, *, TQ, NCP, NBP, N_SEL):
    qi = pl.program_id(1)
    q0 = qi * TQ
    G, J = NSA_KV_GROUPS, NSA_REP
    N = J * TQ
    t_lane = q0 + (lax.broadcasted_iota(jnp.int32, (NCP, N), 1) & (TQ - 1))
    cmp_end = lax.broadcasted_iota(jnp.int32, (NCP, N), 0) * CMP_STRIDE + (CMP_LEN - 1)
    vis = cmp_end <= t_lane
    n_i = lax.broadcasted_iota(jnp.int32, (NBP, NCP), 0) * SEL_BLOCK
    c_i = lax.broadcasted_iota(jnp.int32, (NBP, NCP), 1) * CMP_STRIDE
    overlap_t = jnp.where((c_i <= n_i + SEL_BLOCK - 1) & (c_i + CMP_LEN - 1 >= n_i), 1.0, 0.0)
    blk = lax.broadcasted_iota(jnp.int32, (NBP, TQ), 0)
    tq = q0 + lax.broadcasted_iota(jnp.int32, (NBP, TQ), 1)
    cur = tq // SEL_BLOCK
    admissible = blk * SEL_BLOCK <= tq
    forced = (blk == 0) | (blk == cur) | (blk == cur - 1)
    for g in range(G):
        qt = jnp.concatenate([qt_ref[0, g * J + j] for j in range(J)], axis=1)
        s = jnp.where(vis, _dot(kc_ref[0, g], qt), NEG_INF)
        p = jnp.exp(s - jnp.max(s, axis=0, keepdims=True))
        p = jnp.where(vis, p, 0.0)
        p = p / jnp.maximum(jnp.sum(p, axis=0, keepdims=True), TINY)
        o = _dot(vct_ref[0, g], p)
        for j in range(J):
            o_ref[0, g * J + j] = o[:, j * TQ:(j + 1) * TQ]
        psum = p[:, 0:TQ]
        for j in range(1, J):
            psum = psum + p[:, j * TQ:(j + 1) * TQ]
        imp = _dot_exact_lhs(overlap_t, psum)
        score = jnp.where(admissible & forced, FORCED, jnp.where(admissible, imp, NEG_INF))

        def pick(_, st):
            score, sel = st
            m = jnp.max(score, axis=0, keepdims=True)
            first = jnp.min(jnp.where(score == m, blk, NBP), axis=0, keepdims=True)
            hit = blk == first
            return jnp.where(hit, LOWEST, score), jnp.where(hit, 1.0, sel)

        _, sel = lax.fori_loop(0, N_SEL, pick, (score, jnp.zeros((NBP, TQ), F32)))
        bm_ref[0, g] = sel.astype(bm_ref.dtype)


def _cmp_attn(qt, kc, vct, n_sel, NBP, TQ):
    B, H, HD, S = qt.shape
    G, NCP = kc.shape[1], kc.shape[2]
    kern = functools.partial(_cmp_attn_kernel, TQ=TQ, NCP=NCP, NBP=NBP, N_SEL=n_sel)
    return pl.pallas_call(
        kern,
        grid=(B, S // TQ),
        in_specs=[pl.BlockSpec((1, H, HD, TQ), lambda b, q: (b, 0, 0, q)),
                  pl.BlockSpec((1, G, NCP, HD), lambda b, q: (b, 0, 0, 0)),
                  pl.BlockSpec((1, G, HD, NCP), lambda b, q: (b, 0, 0, 0))],
        out_specs=[pl.BlockSpec((1, H, HD, TQ), lambda b, q: (b, 0, 0, q)),
                   pl.BlockSpec((1, G, NBP, TQ), lambda b, q: (b, 0, 0, q))],
        out_shape=[jax.ShapeDtypeStruct((B, H, HD, S), F32),
                   jax.ShapeDtypeStruct((B, G, NBP, S), jnp.bfloat16)],
        compiler_params=_cparams(("arbitrary", "arbitrary")),
    )(qt, kc, vct)


def _flash_kernel(*refs, mode, G, J, TQ, TK, NWIN):
    if mode == "window":
        qt_ref, k_ref, vt_ref, o_ref, m_ref, acc_ref = refs
        x_ref = None
    else:
        qt_ref, k_ref, vt_ref, x_ref, o_ref, m_ref, acc_ref = refs
    HD = HEAD_DIM
    qi = pl.program_id(1)
    kk = pl.program_id(2)
    q0 = qi * TQ
    last = (q0 + TQ - 1) // TK
    if mode == "window":
        ki = last - (NWIN - 1) + kk
        valid = ki >= 0
    else:
        ki = kk
        valid = kk <= last

    @pl.when(kk == 0)
    def _():
        m_ref[...] = jnp.full(m_ref.shape, NEG_INF, F32)
        acc_ref[...] = jnp.zeros(acc_ref.shape, F32)

    @pl.when(valid)
    def _():
        k0 = ki * TK
        kidx = k0 + lax.broadcasted_iota(jnp.int32, (TK, TQ), 0)
        tq = q0 + lax.broadcasted_iota(jnp.int32, (TK, TQ), 1)
        if mode == "mask":
            bias = x_ref[0].astype(F32)
        elif mode == "window":
            bias = jnp.where((kidx <= tq) & (kidx > tq - WINDOW), 0.0, NEG_INF)
        else:
            causal = kidx <= tq
            nbp = x_ref.shape[2]
            blk_of_key = (k0 + lax.broadcasted_iota(jnp.int32, (TK, nbp), 0)) // SEL_BLOCK
            expand_t = jnp.where(lax.broadcasted_iota(jnp.int32, (TK, nbp), 1) == blk_of_key, 1.0, 0.0)
            expand_t = expand_t.astype(MXU_DTYPE)

        for g in range(G):
            if mode == "block":
                picked = jnp.dot(expand_t, x_ref[0, g].astype(MXU_DTYPE), preferred_element_type=F32)
                b = jnp.where(causal & (picked > 0.5), 0.0, NEG_INF)
            else:
                b = bias
            if J > 1:
                b = jnp.concatenate([b] * J, axis=1)
                qt = jnp.concatenate([qt_ref[0, g * J + j] for j in range(J)], axis=1)
            else:
                qt = qt_ref[0, g]
            s = _dot(k_ref[0, g], qt) + b
            m_prev = m_ref[g]
            m_new = jnp.maximum(m_prev, jnp.max(s, axis=0, keepdims=True))
            alpha = jnp.exp(m_prev - m_new)
            p = jnp.exp((s - m_new).astype(MXU_DTYPE))
            acc_ref[g] = alpha * acc_ref[g] + _dot(vt_ref[0, g], p)
            m_ref[g] = m_new

    @pl.when(kk == pl.num_programs(2) - 1)
    def _():
        for g in range(G):
            o = acc_ref[g, 0:HD] / jnp.maximum(acc_ref[g, HD:HD + 1], TINY)
            for j in range(J):
                o_ref[0, g * J + j] = o[:, j * TQ:(j + 1) * TQ].astype(o_ref.dtype)


def _flash(qt, k, vt, extra, mode, TQ, TK, out_dtype):
    B, H, HD, S = qt.shape
    G = k.shape[1]
    J = H // G
    N = J * TQ
    nq = S // TQ
    last_of = lambda qi: (qi * TQ + TQ - 1) // TK
    if mode == "window":
        NWIN = min((WINDOW - 1 + TK - 1) // TK + 1, S // TK)
        nk = NWIN
        kidx_of = lambda qi, kk: jnp.maximum(last_of(qi) - (NWIN - 1) + kk, 0)
    else:
        NWIN = 0
        nk = S // TK
        kidx_of = lambda qi, kk: jnp.minimum(kk, last_of(qi))
    in_specs = [pl.BlockSpec((1, H, HD, TQ), lambda b, qi, kk: (b, 0, 0, qi)),
                pl.BlockSpec((1, G, TK, HD), lambda b, qi, kk: (b, 0, kidx_of(qi, kk), 0)),
                pl.BlockSpec((1, G, HD + ONES_ROWS, TK), lambda b, qi, kk: (b, 0, 0, kidx_of(qi, kk)))]
    args = [qt, k, vt]
    if mode == "mask":
        in_specs.append(pl.BlockSpec((1, TK, TQ), lambda b, qi, kk: (b, kidx_of(qi, kk), qi)))
        args.append(extra)
    elif mode == "block":
        nbp = extra.shape[2]
        in_specs.append(pl.BlockSpec((1, G, nbp, TQ), lambda b, qi, kk: (b, 0, 0, qi)))
        args.append(extra)
    kern = functools.partial(_flash_kernel, mode=mode, G=G, J=J, TQ=TQ, TK=TK, NWIN=NWIN)
    return pl.pallas_call(
        kern,
        grid=(B, nq, nk),
        in_specs=in_specs,
        out_specs=pl.BlockSpec((1, H, HD, TQ), lambda b, qi, kk: (b, 0, 0, qi)),
        out_shape=jax.ShapeDtypeStruct((B, H, HD, S), out_dtype),
        scratch_shapes=[pltpu.VMEM((G, 1, N), F32),
                        pltpu.VMEM((G, HD + ONES_ROWS, N), F32)],
        compiler_params=_cparams(("arbitrary", "arbitrary", "arbitrary")),
    )(*args)


def _dot_tn(a_t, b):
    return lax.dot_general(a_t.astype(MXU_DTYPE), b.astype(MXU_DTYPE),
                           (((0,), (0,)), ((), ())), preferred_element_type=F32)


def _route(h, wr, br):
    tm = h.shape[0]
    lane = lax.broadcasted_iota(jnp.int32, (tm, LANES), 1)
    h_hi = h.astype(MXU_DTYPE)
    h_lo = (h - h_hi.astype(F32)).astype(MXU_DTYPE)
    w_hi = wr.astype(MXU_DTYPE)
    w_lo = (wr - w_hi.astype(F32)).astype(MXU_DTYPE)
    d = lambda a, b: jnp.dot(a, b, preferred_element_type=F32)
    logits = d(h_hi, w_hi) + d(h_hi, w_lo) + d(h_lo, w_hi) + br
    is_e = lane < N_EXPERTS
    is_g = (lane >= N_EXPERTS) & (lane < N_EXPERTS + N_GROUPS)
    lg = jnp.where(is_g, logits, LOWEST)
    mg = jnp.max(lg, axis=-1, keepdims=True)
    gsel = jnp.min(jnp.where(is_g & (lg == mg), lane - N_EXPERTS, N_GROUPS), axis=-1, keepdims=True)
    pg_sel = 1.0 / jnp.sum(jnp.where(is_g, jnp.exp(lg - mg), 0.0), axis=-1, keepdims=True)
    in_grp = is_e & ((lane // EXPERTS_PER_GROUP) == gsel)
    le = jnp.where(in_grp, logits, LOWEST)
    me = jnp.max(le, axis=-1, keepdims=True)
    ex = jnp.where(in_grp, jnp.exp(le - me), 0.0)
    pe = ex / jnp.sum(ex, axis=-1, keepdims=True)
    pe = jnp.where(in_grp, pe, -1.0)
    p1 = jnp.max(pe, axis=-1, keepdims=True)
    i1 = jnp.min(jnp.where(pe == p1, lane, LANES), axis=-1, keepdims=True)
    pe2 = jnp.where(lane == i1, -1.0, pe)
    p2 = jnp.max(pe2, axis=-1, keepdims=True)
    i2 = jnp.min(jnp.where(pe2 == p2, lane, LANES), axis=-1, keepdims=True)
    tot = p1 + p2
    comb = jnp.where(lane == i1, p1 / tot * pg_sel, 0.0) + jnp.where(lane == i2, p2 / tot * pg_sel, 0.0)
    return jnp.where(lane == N_EXPERTS, gsel.astype(F32), comb)


def _outproj_kernel(x_ref, oa_ref, oc_ref, os_ref, ow_ref, br_ref, mg_ref, wa_ref, wb_ref, wo_ref,
                    gt_ref, g2_ref, sc_ref, sh_ref, wr_ref, brt_ref, o_ref, h_ref, route_ref):
    HD = HEAD_DIM
    D = x_ref.shape[1]
    br = br_ref[0]
    parts = []
    for hh in range(NSA_HEADS):
        parts.append(br[3 * hh:3 * hh + 1] * oc_ref[0, hh] + br[3 * hh + 1:3 * hh + 2] * os_ref[0, hh]
                     + br[3 * hh + 2:3 * hh + 3] * ow_ref[0, hh])
    ob_t = jnp.concatenate(parts, axis=0)
    oa_t = jnp.concatenate([oa_ref[0, hh] for hh in range(DSA_HEADS)], axis=0)
    ua = _dot_tn(oa_t, wa_ref[...])
    ub = _dot_tn(ob_t, wb_ref[...])
    mg = mg_ref[...]
    merged = jax.nn.sigmoid(mg[:, :D]) * ua + jax.nn.sigmoid(mg[:, D:]) * ub
    x1 = x_ref[...] + gt_ref[0] * _dot(merged, wo_ref[...])
    o_ref[...] = x1
    h = _rms(x1, g2_ref[...]) * (1.0 + sc_ref[0]) + sh_ref[0]
    h_ref[...] = h
    route_ref[...] = _route(h, wr_ref[...], brt_ref[...])


def _out_proj(x2d, oa, oc, os_, ow, br, mg, wa, wb, wo, gt, g2, sc2, sh2, wr, brt, S, tm=256):
    T, D = x2d.shape
    B, H, HD, _ = oa.shape
    per_b = S // tm
    heads_cols = pl.BlockSpec((1, H, HD, tm), lambda i: (i // per_b, 0, 0, i % per_b))
    per_batch = pl.BlockSpec((1, 1, D), lambda i: (i // per_b, 0, 0))
    rows = lambda n: pl.BlockSpec((tm, n), lambda i: (i, 0))
    const = lambda a, b: pl.BlockSpec((a, b), lambda i: (0, 0))
    return pl.pallas_call(
        _outproj_kernel,
        grid=(T // tm,),
        in_specs=[rows(D), heads_cols, heads_cols, heads_cols, heads_cols,
                  pl.BlockSpec((1, br.shape[1], tm), lambda i: (i // per_b, 0, i % per_b)),
                  rows(2 * D), const(H * HD, D), const(H * HD, D), const(D, D), per_batch,
                  const(1, D), per_batch, per_batch, const(D, LANES), const(1, LANES)],
        out_specs=[rows(D), rows(D), rows(LANES)],
        out_shape=[jax.ShapeDtypeStruct((T, D), F32), jax.ShapeDtypeStruct((T, D), F32),
                   jax.ShapeDtypeStruct((T, LANES), F32)],
        compiler_params=_cparams(("arbitrary",)),
    )(x2d, oa, oc, os_, ow, br, mg, wa, wb, wo, gt, g2.reshape(1, D), sc2, sh2, wr, brt)


def _row_copy(src_hbm, t, dst, r, sem):
    return pltpu.make_async_copy(src_hbm.at[pl.ds(t, 1)], dst.at[pl.ds(r, 1)], sem)


def _start_rows(idx_ref, base, r0, n, src_hbm, dst, sem):
    for r in range(n):
        _row_copy(src_hbm, idx_ref[base + r0 + r], dst, r0 + r, sem).start()


def _wait_rows(src_hbm, dst, n, sem):
    pltpu.make_async_copy(src_hbm.at[pl.ds(0, n)], dst.at[pl.ds(0, n)], sem).wait()


def _experts_kernel(tg_ref, src_ref, h_hbm, r_ref, wg_ref, wu_ref, wd_ref, o_ref,
                    hbuf, acc_ref, sems, *, TM):
    i = pl.program_id(0)
    e = pl.program_id(1)
    n_tiles = pl.num_programs(0)
    EPG = EXPERTS_PER_GROUP
    part = TM // EPG
    slot = i % 2

    @pl.when((i == 0) & (e == 0))
    def _():
        def start(c, carry):
            _start_rows(src_ref, 0, pl.multiple_of(c * part, part), part, h_hbm, hbuf.at[0], sems.at[0])
            return carry
        lax.fori_loop(0, EPG, start, 0)

    @pl.when(e == 0)
    def _():
        _wait_rows(h_hbm, hbuf.at[slot], TM, sems.at[slot])
        acc_ref[...] = jnp.zeros(acc_ref.shape, F32)

    @pl.when(i + 1 < n_tiles)
    def _():
        _start_rows(src_ref, (i + 1) * TM, e * part, part, h_hbm, hbuf.at[1 - slot], sems.at[1 - slot])

    h = hbuf[slot].astype(MXU_DTYPE)
    a = jnp.dot(h, wg_ref[0], preferred_element_type=F32)
    u = jnp.dot(h, wu_ref[0], preferred_element_type=F32)
    y = _dot(a * jax.nn.sigmoid(a) * u, wd_ref[0])
    lane = lax.broadcasted_iota(jnp.int32, (TM, LANES), 1)
    expert = tg_ref[i] * EPG + e
    w_e = jnp.sum(jnp.where(lane == expert, r_ref[...], 0.0), axis=-1, keepdims=True)
    acc_ref[...] += w_e * y

    @pl.when(e == pl.num_programs(1) - 1)
    def _():
        o_ref[...] = acc_ref[...]


def _experts(h2, route, tile_group, src, wg, wu, wd, TM):
    T, D = h2.shape
    DE = wg.shape[2]
    P = src.shape[0]
    EPG = EXPERTS_PER_GROUP
    w_idx = lambda i, e, tg, src: (tg[i] * EPG + e, 0, 0)
    kern = functools.partial(_experts_kernel, TM=TM)
    grid_spec = pltpu.PrefetchScalarGridSpec(
        num_scalar_prefetch=2,
        grid=(P // TM, EPG),
        in_specs=[pl.BlockSpec(memory_space=pl.ANY),
                  pl.BlockSpec((TM, LANES), lambda i, e, tg, src: (i, 0)),
                  pl.BlockSpec((1, D, DE), w_idx),
                  pl.BlockSpec((1, D, DE), w_idx),
                  pl.BlockSpec((1, DE, D), w_idx)],
        out_specs=pl.BlockSpec((TM, D), lambda i, e, tg, src: (i, 0)),
        scratch_shapes=[pltpu.VMEM((2, TM, D), F32),
                        pltpu.VMEM((TM, D), F32),
                        pltpu.SemaphoreType.DMA((2,))])
    return pl.pallas_call(
        kern,
        grid_spec=grid_spec,
        out_shape=jax.ShapeDtypeStruct((P, D), F32),
        compiler_params=_cparams(("arbitrary", "arbitrary")),
    )(tile_group, src, h2, route, wg, wu, wd)


def _combine_kernel(pos_ref, x_ref, y_hbm, gt_ref, gf_ref, o_ref, ybuf, sem, *, TM):
    part = 64
    def start(c, carry):
        _start_rows(pos_ref, pl.program_id(0) * TM, pl.multiple_of(c * part, part), part, y_hbm, ybuf,
                    sem.at[0])
        return carry
    lax.fori_loop(0, TM // part, start, 0)
    _wait_rows(y_hbm, ybuf, TM, sem.at[0])
    o_ref[...] = _rms(x_ref[...] + gt_ref[0] * ybuf[...], gf_ref[...])


def _combine(x1, ys, pos, gt, gf, S, tm=256):
    T, D = x1.shape
    per_b = S // tm
    grid_spec = pltpu.PrefetchScalarGridSpec(
        num_scalar_prefetch=1,
        grid=(T // tm,),
        in_specs=[pl.BlockSpec((tm, D), lambda i, pos: (i, 0)),
                  pl.BlockSpec(memory_space=pl.ANY),
                  pl.BlockSpec((1, 1, D), lambda i, pos: (i // per_b, 0, 0)),
                  pl.BlockSpec((1, D), lambda i, pos: (0, 0))],
        out_specs=pl.BlockSpec((tm, D), lambda i, pos: (i, 0)),
        scratch_shapes=[pltpu.VMEM((tm, D), F32), pltpu.SemaphoreType.DMA((1,))])
    return pl.pallas_call(
        functools.partial(_combine_kernel, TM=tm),
        grid_spec=grid_spec,
        out_shape=jax.ShapeDtypeStruct((T, D), F32),
        compiler_params=_cparams(("arbitrary",)),
    )(pos, x1, ys, gt, gf.reshape(1, D))


def _routed_moe(x1, h2, route, gt, gf, wg, wu, wd, S, TM=512):
    T, D = x1.shape
    gsel = route[:, N_EXPERTS].astype(jnp.int32)
    onehot = (gsel[:, None] == jnp.arange(N_GROUPS, dtype=jnp.int32)[None, :]).astype(jnp.int32)
    csum = jnp.cumsum(onehot, axis=0)
    rank = jnp.take_along_axis(csum, gsel[:, None], axis=1)[:, 0] - 1
    padded = -(-csum[-1] // TM) * TM
    ends = jnp.cumsum(padded)
    starts = ends - padded
    pos = (jnp.take(starts, gsel) + rank).astype(jnp.int32)
    P = T + N_GROUPS * TM
    src = jnp.zeros((P,), jnp.int32).at[pos].set(jnp.arange(T, dtype=jnp.int32))
    tile_start = jnp.arange(P // TM, dtype=jnp.int32) * TM
    tile_group = jnp.minimum(jnp.sum(tile_start[:, None] >= ends[None, :], axis=1), N_GROUPS - 1)
    ys = _experts(h2, jnp.take(route, src, axis=0), tile_group.astype(jnp.int32), src, wg, wu, wd, TM)
    return _combine(x1, ys, pos, gt, gf, S)


def _layer(x, mod, positions, g_norm1, w_in, g_kv_latent, w_kv_up, pe_cmp_k, pe_cmp_v,
           w_cmp1_k, w_cmp2_k, w_cmp1_v, w_cmp2_v, w_up_a, w_up_b, w_out, g_norm2,
           w_router_group, b_router_group, w_router_expert, b_router_expert,
           w_expert_gate, w_expert_up, w_expert_down, g_out):
    B, S, D = x.shape
    T = B * S
    HD, G, J = HEAD_DIM, NSA_KV_GROUPS, NSA_REP
    HA = DSA_HEADS * HD
    kvb = G * HD
    scale = HD ** -0.5
    topk_a = min(DSA_TOPK_MAX, S // 4)
    n_sel = min(SEL_COUNT, S // SEL_BLOCK)
    mod6 = mod.reshape(B, 6, 1, D)
    sh1, sc1, gt1, sh2, sc2, gt2 = (mod6[:, i] for i in range(6))

    x2d = x.reshape(T, D)
    (mg, qt_a, k_a, vt_a, iqt, iwt, ik, qt_b, kcvc, ks, kw, vst, vwt, br) = _prep(
        x, sc1, sh1, g_norm1, w_in, g_kv_latent, w_kv_up, positions)

    kc = _compress(kcvc[..., :kvb].reshape(B, S, G, HD), pe_cmp_k, w_cmp1_k, w_cmp2_k)
    vc = _compress(kcvc[..., kvb:].reshape(B, S, G, HD), pe_cmp_v, w_cmp1_v, w_cmp2_v)
    n_cmp = kc.shape[1]
    ncp = -(-(n_cmp + 1) // LANES) * LANES
    pad_c = lambda t: jnp.pad(t, ((0, 0), (0, ncp - n_cmp), (0, 0), (0, 0))).astype(MXU_DTYPE)
    kc, vct = pad_c(kc).transpose(0, 2, 1, 3), pad_c(vc).transpose(0, 2, 3, 1)

    TA = min(1024, S)
    sel_bias = _dsa_select(ik, iqt, iwt, topk_a)
    o_a = _flash(qt_a, k_a, vt_a, sel_bias, "mask", TA, min(1024, S), MXU_DTYPE)

    TB = min(512, S)
    nbp = -(-(S // SEL_BLOCK) // LANES) * LANES
    o_c, blk_mask = _cmp_attn(qt_b, kc, vct, n_sel, nbp, TB)
    o_s = _flash(qt_b, ks, vst, blk_mask, "block", TB, min(1024, S), F32)
    o_w = _flash(qt_b, kw, vwt, None, "window", TB, TB, F32)

    wr = jnp.concatenate([w_router_expert, w_router_group], axis=1)
    wr = jnp.pad(wr, ((0, 0), (0, LANES - wr.shape[1])))
    brt = jnp.concatenate([b_router_expert, b_router_group])
    brt = jnp.pad(brt, (0, LANES - brt.shape[0])).reshape(1, LANES)
    x1, h2, route = _out_proj(x2d, o_a, o_c, o_s, o_w, br, mg, w_up_a.astype(MXU_DTYPE),
                              w_up_b.astype(MXU_DTYPE), w_out.astype(MXU_DTYPE), gt1,
                              g_norm2, sc2, sh2, wr, brt, S)

    out = _routed_moe(x1, h2, route, gt2, g_out, w_expert_gate.astype(MXU_DTYPE),
                      w_expert_up.astype(MXU_DTYPE), w_expert_down.astype(MXU_DTYPE), S)
    return out.reshape(B, S, D)


def kernel(x, c, positions, w_ada, b_ada, g_norm1, w_in, g_kv_latent, w_kv_up, pe_cmp_k, pe_cmp_v,
           w_cmp1_k, w_cmp2_k, w_cmp1_v, w_cmp2_v, w_up_a, w_up_b, w_out, g_norm2, w_router_group,
           b_router_group, w_router_expert, b_router_expert, w_expert_gate, w_expert_up,
           w_expert_down, g_final):
    depth = w_ada.shape[0]
    assert depth == 1, "the fused final norm assumes a single layer"
    mod = _ada_mod(c, w_ada[0], b_ada[0])
    return _layer(x, mod, positions, g_norm1[0], w_in[0], g_kv_latent[0], w_kv_up[0], pe_cmp_k[0],
                  pe_cmp_v[0], w_cmp1_k[0], w_cmp2_k[0], w_cmp1_v[0], w_cmp2_v[0], w_up_a[0],
                  w_up_b[0], w_out[0], g_norm2[0], w_router_group[0], b_router_group[0],
                  w_router_expert[0], b_router_expert[0], w_expert_gate[0], w_expert_up[0],
                  w_expert_down[0], g_final)
```

```python
import functools
from typing import NamedTuple

import numpy as np
import jax
import jax.numpy as jnp
from jax import lax
from jax.experimental import pallas as pl
from jax.experimental.pallas import tpu as pltpu

HEAD_DIM = 64
ROT_FRACTION = 4
ROPE_THETA = 500000.0
DSA_HEADS = 8
DSA_KV_RANK = 128
IDX_HEADS = 8
IDX_DIM = 32
DSA_TOPK_MAX = 256
NSA_HEADS = 8
NSA_KV_GROUPS = 2
NSA_REP = NSA_HEADS // NSA_KV_GROUPS
CMP_LEN = 32
CMP_STRIDE = 16
CMP_HIDDEN = 256
SEL_BLOCK = 64
SEL_COUNT = 16
WINDOW = 512
N_GROUPS = 4
EXPERTS_PER_GROUP = 8
N_EXPERTS = N_GROUPS * EXPERTS_PER_GROUP
D_EXPERT = 256
NORM_EPS = 1e-6
NEG_INF = -1e30
TINY = 1e-30
LOWEST = -3.0e38
FORCED = 1e30

LANES = 128
SUBLANES = 8
ONES_ROWS = 16
MXU_DTYPE = jnp.bfloat16
VMEM_LIMIT = 56 * 1024 * 1024

F32 = jnp.float32
SELECT_BISECTIONS = 16


class _Tiles(NamedTuple):
    rows: int
    select_q: int
    select_k: int
    mask_q: int
    mask_k: int
    nsa_q: int
    block_k: int
    moe_rows: int


def _tiles(S):
    cap = lambda n: min(n, S)
    return _Tiles(rows=cap(256), select_q=cap(256), select_k=cap(256), mask_q=cap(1024),
                  mask_k=cap(1024), nsa_q=cap(512), block_k=cap(1024), moe_rows=512)


def _cparams(sem):
    return pltpu.CompilerParams(dimension_semantics=sem, vmem_limit_bytes=VMEM_LIMIT)


def _dot(a, b):
    return jnp.dot(a.astype(MXU_DTYPE), b.astype(MXU_DTYPE), preferred_element_type=F32)


def _dot_exact_lhs(a01, b):
    hi = b.astype(MXU_DTYPE)
    r1 = b - hi.astype(F32)
    mid = r1.astype(MXU_DTYPE)
    lo = (r1 - mid.astype(F32)).astype(MXU_DTYPE)
    a = a01.astype(MXU_DTYPE)
    d = lambda u: jnp.dot(a, u, preferred_element_type=F32)
    return d(hi) + d(mid) + d(lo)


def _rms(x, g):
    return x * lax.rsqrt(jnp.mean(x * x, axis=-1, keepdims=True) + NORM_EPS) * g


def _ada_kernel(c_ref, w_ref, b_ref, o_ref):
    c = c_ref[...]
    cond = c * jax.nn.sigmoid(c)
    o_ref[...] = _dot(cond, w_ref[...]) + b_ref[...]


def _ada_mod(c, w_ada, b_ada):
    B, D = c.shape
    n_out = w_ada.shape[1]
    rows = SUBLANES
    cp = jnp.zeros((rows, D), F32).at[:B].set(c)
    tn = 1024
    out = pl.pallas_call(
        _ada_kernel,
        grid=(n_out // tn,),
        in_specs=[pl.BlockSpec((rows, D), lambda j: (0, 0)),
                  pl.BlockSpec((D, tn), lambda j: (0, j)),
                  pl.BlockSpec((1, tn), lambda j: (0, j))],
        out_specs=pl.BlockSpec((rows, tn), lambda j: (0, j)),
        out_shape=jax.ShapeDtypeStruct((rows, n_out), F32),
        compiler_params=_cparams(("arbitrary",)),
    )(cp, w_ada, b_ada.reshape(1, n_out))
    return out[:B]


KVB = NSA_KV_GROUPS * HEAD_DIM
ROW_SEGS = (("mg", 2048), ("lat", DSA_KV_RANK), ("kc", KVB), ("vc", KVB), ("ks", KVB), ("kw", KVB),
            ("ik", LANES))
COL_SEGS = (("qa", DSA_HEADS * HEAD_DIM), ("qb", NSA_HEADS * HEAD_DIM), ("iq", IDX_HEADS * IDX_DIM),
            ("vs", KVB), ("vw", KVB), ("lat", DSA_KV_RANK), ("iw", IDX_HEADS), ("br", 3 * NSA_HEADS))


def _seg_offsets(segs):
    out, pos = {}, 0
    for name, n in segs:
        out[name] = pos
        pos += n
    return out, pos


ROW_OFF, ROW_COLS = _seg_offsets(ROW_SEGS)
COL_OFF, COL_ROWS = _seg_offsets(COL_SEGS)


def _rot_lanes(x, c, sa, sb, half):
    outs = []
    for j in range(x.shape[1] // LANES):
        xs = x[:, j * LANES:(j + 1) * LANES]
        outs.append(xs * c + pltpu.roll(xs, half, 1) * sa + pltpu.roll(xs, LANES - half, 1) * sb)
    return outs[0] if len(outs) == 1 else jnp.concatenate(outs, axis=1)


def _prep_kernel(x_ref, xt_ref, g_ref, sc_ref, sh_ref, gc_ref, scc_ref, shc_ref, wr_ref, wc_ref,
                 gkv_ref, gkvc_ref, wkk_ref, wkv_ref, c64_ref, sa64_ref, sb64_ref,
                 c32_ref, sa32_ref, sb32_ref, cos_ref, sin_ref, c32t_ref, s32t_ref,
                 mg_ref, qa_ref, ka_ref, va_ref, iq_ref, iw_ref, ik_ref, qb_ref, kcvc_ref,
                 ks_ref, kw_ref, vs_ref, vw_ref, br_ref):
    HD = HEAD_DIM
    half = HD // ROT_FRACTION // 2
    scale = HD ** -0.5
    h = _rms(x_ref[...], g_ref[...]) * (1.0 + sc_ref[0]) + sh_ref[0]
    pr = _dot(h, wr_ref[...])
    row = lambda name, n: pr[:, ROW_OFF[name]:ROW_OFF[name] + n]
    mg_ref[...] = row("mg", 2048)
    c64, sa64, sb64 = c64_ref[...], sa64_ref[...], sb64_ref[...]
    ka = _rot_lanes(_dot(_rms(row("lat", DSA_KV_RANK), gkv_ref[...]), wkk_ref[...]), c64, sa64, sb64, half)
    for hh in range(DSA_HEADS):
        ka_ref[0, hh] = ka[:, hh * HD:(hh + 1) * HD].astype(ka_ref.dtype)
    kcvc_ref[0] = jnp.concatenate([_rot_lanes(row("kc", KVB), c64, sa64, sb64, half), row("vc", KVB)], axis=1)
    ks = _rot_lanes(row("ks", KVB), c64, sa64, sb64, half)
    kw = _rot_lanes(row("kw", KVB), c64, sa64, sb64, half)
    for g in range(NSA_KV_GROUPS):
        ks_ref[0, g] = ks[:, g * HD:(g + 1) * HD].astype(ks_ref.dtype)
        kw_ref[0, g] = kw[:, g * HD:(g + 1) * HD].astype(kw_ref.dtype)
    ik = _rot_lanes(row("ik", LANES), c32_ref[...], sa32_ref[...], sb32_ref[...], IDX_DIM // ROT_FRACTION // 2)
    ik_ref[0] = ik[:, :IDX_DIM].astype(ik_ref.dtype)
    xt = xt_ref[0]
    ht = xt * lax.rsqrt(jnp.mean(xt * xt, axis=0, keepdims=True) + NORM_EPS) * gc_ref[...]
    ht = ht * (1.0 + scc_ref[0]) + shc_ref[0]
    pc = _dot(wc_ref[...], ht)
    col = lambda name, n: pc[COL_OFF[name]:COL_OFF[name] + n]
    cos, sin = cos_ref[0], sin_ref[0]

    def rot_rows(blk):
        x1, x2 = blk[0:half], blk[half:2 * half]
        return jnp.concatenate([x1 * cos - x2 * sin, x2 * cos + x1 * sin, blk[2 * half:]], axis=0)

    qa, qb = col("qa", DSA_HEADS * HD), col("qb", NSA_HEADS * HD)
    for hh in range(DSA_HEADS):
        qa_ref[0, hh] = (rot_rows(qa[hh * HD:(hh + 1) * HD]) * scale).astype(qa_ref.dtype)
    for hh in range(NSA_HEADS):
        qb_ref[0, hh] = (rot_rows(qb[hh * HD:(hh + 1) * HD]) * scale).astype(qb_ref.dtype)
    iq = col("iq", IDX_HEADS * IDX_DIM)
    c32t, s32t = c32t_ref[0], s32t_ref[0]
    for hh in range(IDX_HEADS):
        blk = iq[hh * IDX_DIM:(hh + 1) * IDX_DIM]
        top = blk[0:SUBLANES]
        top = top * c32t + pltpu.roll(top, SUBLANES // 2, 0) * s32t
        iq_ref[0, hh] = jnp.concatenate([top, blk[SUBLANES:]], axis=0).astype(iq_ref.dtype)
    ones = jnp.ones((ONES_ROWS, xt.shape[1]), F32)
    with_ones = lambda v: jnp.concatenate([v, ones], axis=0)
    vs, vw = col("vs", KVB), col("vw", KVB)
    for g in range(NSA_KV_GROUPS):
        vs_ref[0, g] = with_ones(vs[g * HD:(g + 1) * HD]).astype(vs_ref.dtype)
        vw_ref[0, g] = with_ones(vw[g * HD:(g + 1) * HD]).astype(vw_ref.dtype)
    lat = col("lat", DSA_KV_RANK)
    lat = lat * lax.rsqrt(jnp.mean(lat * lat, axis=0, keepdims=True) + NORM_EPS) * gkvc_ref[...]
    va = _dot(wkv_ref[...], lat)
    for hh in range(DSA_HEADS):
        va_ref[0, hh] = with_ones(va[hh * HD:(hh + 1) * HD]).astype(va_ref.dtype)
    iw_ref[0] = col("iw", IDX_HEADS)
    br_ref[0] = jax.nn.sigmoid(col("br", 3 * NSA_HEADS))


def _prep(x, mod_sc, mod_sh, g_norm1, w_in, g_kv, w_kv_up, positions, tm):
    B, S, D = x.shape
    T = B * S
    HD, HA, G = HEAD_DIM, DSA_HEADS, NSA_KV_GROUPS
    per_b = S // tm
    sizes = [HA * HD, DSA_KV_RANK, IDX_HEADS * IDX_DIM, IDX_HEADS, IDX_DIM, NSA_HEADS * HD,
             KVB, KVB, KVB, KVB, KVB, KVB, 3 * NSA_HEADS, 2 * D]
    names = ["qa", "lat", "iq", "iw", "ik", "qb", "kc", "vc", "ks", "vs", "kw", "vw", "br", "mg"]
    starts = dict(zip(names, np.concatenate([[0], np.cumsum(sizes)[:-1]]).astype(int)))
    width = dict(zip(names, sizes))
    def seg(name, n):
        w = w_in[:, starts[name]:starts[name] + width[name]]
        return jnp.pad(w, ((0, 0), (0, n - width[name])))
    w_row = jnp.concatenate([seg(n, k) for n, k in ROW_SEGS], axis=1).astype(MXU_DTYPE)
    w_col = jnp.concatenate([seg(n, k) for n, k in COL_SEGS], axis=1).T.astype(MXU_DTYPE)
    w_kk = w_kv_up[:, :HA * HD].astype(MXU_DTYPE)
    w_kv = w_kv_up[:, HA * HD:].T.astype(MXU_DTYPE)
    wide = lambda v: jnp.broadcast_to(v[..., None], v.shape + (tm,))

    def tables(dim):
        half = dim // ROT_FRACTION // 2
        inv_freq = ROPE_THETA ** (-jnp.arange(half, dtype=F32) / half)
        ang = positions.astype(F32)[..., None] * inv_freq
        return jnp.cos(ang), jnp.sin(ang), half
    def lane_tables(dim):
        cos, sin, half = tables(dim)
        reps = LANES // dim
        z = jnp.zeros((B, S, dim - 2 * half), F32)
        c = jnp.concatenate([cos, cos, z + 1.0], axis=-1)
        sa = jnp.concatenate([jnp.zeros_like(sin), sin, z], axis=-1)
        sb = jnp.concatenate([-sin, jnp.zeros_like(sin), z], axis=-1)
        return [jnp.tile(t, (1, 1, reps)).reshape(T, LANES) for t in (c, sa, sb)]
    c64, sa64, sb64 = lane_tables(HD)
    c32, sa32, sb32 = lane_tables(IDX_DIM)
    cos64, sin64, _ = tables(HD)
    cos_t, sin_t = cos64.transpose(0, 2, 1), sin64.transpose(0, 2, 1)
    cos32, sin32, _ = tables(IDX_DIM)
    c32t = jnp.concatenate([cos32, cos32], axis=-1).transpose(0, 2, 1)
    s32t = jnp.concatenate([-sin32, sin32], axis=-1).transpose(0, 2, 1)

    row_blk = lambda n: pl.BlockSpec((tm, n), lambda i: (i, 0))
    const = lambda shape: pl.BlockSpec(shape, lambda i: (0,) * len(shape))
    per_batch = lambda shape: pl.BlockSpec((1,) + shape, lambda i: (i // per_b,) + (0,) * len(shape))
    tok_cols = lambda rows: pl.BlockSpec((1, rows, tm), lambda i: (i // per_b, 0, i % per_b))
    heads_cols = lambda h, rows: pl.BlockSpec((1, h, rows, tm), lambda i: (i // per_b, 0, 0, i % per_b))
    heads_rows = lambda h, n: pl.BlockSpec((1, h, tm, n), lambda i: (i // per_b, 0, i % per_b, 0))
    bf = MXU_DTYPE
    out_shape = [jax.ShapeDtypeStruct((T, 2 * D), F32),
                 jax.ShapeDtypeStruct((B, HA, HD, S), bf),
                 jax.ShapeDtypeStruct((B, HA, S, HD), bf),
                 jax.ShapeDtypeStruct((B, HA, HD + ONES_ROWS, S), bf),
                 jax.ShapeDtypeStruct((B, IDX_HEADS, IDX_DIM, S), bf),
                 jax.ShapeDtypeStruct((B, IDX_HEADS, S), F32),
                 jax.ShapeDtypeStruct((B, S, IDX_DIM), bf),
                 jax.ShapeDtypeStruct((B, NSA_HEADS, HD, S), bf),
                 jax.ShapeDtypeStruct((B, S, 2 * KVB), F32),
                 jax.ShapeDtypeStruct((B, G, S, HD), bf),
                 jax.ShapeDtypeStruct((B, G, S, HD), bf),
                 jax.ShapeDtypeStruct((B, G, HD + ONES_ROWS, S), bf),
                 jax.ShapeDtypeStruct((B, G, HD + ONES_ROWS, S), bf),
                 jax.ShapeDtypeStruct((B, 3 * NSA_HEADS, S), F32)]
    out_specs = [row_blk(2 * D), heads_cols(HA, HD), heads_rows(HA, HD), heads_cols(HA, HD + ONES_ROWS),
                 heads_cols(IDX_HEADS, IDX_DIM), tok_cols(IDX_HEADS),
                 pl.BlockSpec((1, tm, IDX_DIM), lambda i: (i // per_b, i % per_b, 0)),
                 heads_cols(NSA_HEADS, HD),
                 pl.BlockSpec((1, tm, 2 * KVB), lambda i: (i // per_b, i % per_b, 0)),
                 heads_rows(G, HD), heads_rows(G, HD), heads_cols(G, HD + ONES_ROWS),
                 heads_cols(G, HD + ONES_ROWS),
                 tok_cols(3 * NSA_HEADS)]
    half64 = HD // ROT_FRACTION // 2
    in_specs = [row_blk(D), tok_cols(D), const((1, D)), per_batch((1, D)), per_batch((1, D)),
                const((D, tm)), per_batch((D, tm)), per_batch((D, tm)),
                const((D, ROW_COLS)), const((COL_ROWS, D)),
                const((1, DSA_KV_RANK)), const((DSA_KV_RANK, tm)),
                const((DSA_KV_RANK, HA * HD)), const((HA * HD, DSA_KV_RANK)),
                row_blk(LANES), row_blk(LANES), row_blk(LANES), row_blk(LANES), row_blk(LANES), row_blk(LANES),
                tok_cols(half64), tok_cols(half64), tok_cols(SUBLANES), tok_cols(SUBLANES)]
    return pl.pallas_call(
        _prep_kernel,
        grid=(T // tm,),
        in_specs=in_specs,
        out_specs=out_specs,
        out_shape=out_shape,
        compiler_params=_cparams(("arbitrary",)),
    )(x.reshape(T, D), x.transpose(0, 2, 1), g_norm1.reshape(1, D), mod_sc, mod_sh,
      wide(g_norm1), wide(mod_sc[:, 0]), wide(mod_sh[:, 0]), w_row, w_col,
      g_kv.reshape(1, DSA_KV_RANK), wide(g_kv), w_kk, w_kv,
      c64, sa64, sb64, c32, sa32, sb32, cos_t, sin_t, c32t, s32t)


def _cmp_kernel(f_ref, w1_ref, w2_ref, o_ref):
    hid = _dot(f_ref[...], w1_ref[...])
    hid = hid * jax.nn.sigmoid(hid)
    o_ref[...] = _dot(hid, w2_ref[...])


def _compress(tok, pe, w1, w2):
    B, S, G, HD = tok.shape
    r = CMP_LEN // CMP_STRIDE
    n_chunks = S // CMP_STRIDE
    n_cmp = n_chunks - r + 1
    chunks = tok.reshape(B, n_chunks, CMP_STRIDE, G, HD)
    blocks = jnp.concatenate([chunks[:, i:n_cmp + i] for i in range(r)], axis=2)
    blocks = blocks + pe[:, None, :]
    flat = blocks.transpose(0, 1, 3, 2, 4).reshape(B * n_cmp * G, CMP_LEN * HD)
    rows = flat.shape[0]
    tm = 512
    rows_p = -(-rows // tm) * tm
    flat = jnp.pad(flat, ((0, rows_p - rows), (0, 0))).astype(MXU_DTYPE)
    out = pl.pallas_call(
        _cmp_kernel,
        grid=(rows_p // tm,),
        in_specs=[pl.BlockSpec((tm, CMP_LEN * HD), lambda i: (i, 0)),
                  pl.BlockSpec((CMP_LEN * HD, CMP_HIDDEN), lambda i: (0, 0)),
                  pl.BlockSpec((CMP_HIDDEN, HD), lambda i: (0, 0))],
        out_specs=pl.BlockSpec((tm, HD), lambda i: (i, 0)),
        out_shape=jax.ShapeDtypeStruct((rows_p, HD), F32),
        compiler_params=_cparams(("arbitrary",)),
    )(flat, w1.astype(MXU_DTYPE), w2.astype(MXU_DTYPE))
    return out[:rows].reshape(B, n_cmp, G, HD)


def _select_kernel(ik_ref, iqt_ref, iwt_ref, bias_ref, sc_ref, *, TQ, KC, S, K, NBIS):
    qi = pl.program_id(1)
    q0 = qi * TQ
    n_ch = (q0 + TQ + KC - 1) // KC
    n_all = S // KC
    t_row = q0 + lax.broadcasted_iota(jnp.int32, (1, TQ), 1)
    key_iota = lax.broadcasted_iota(jnp.int32, (KC, TQ), 0)
    kf = float(K)
    SUB = LANES
    sub_iota = lax.broadcasted_iota(jnp.int32, (SUB, TQ), 0)

    def score_chunk(c, carry):
        mn, mx = carry
        for u in range(KC // SUB):
            off = pl.multiple_of(c * KC + u * SUB, SUB)
            ikc = ik_ref[0, pl.ds(off, SUB), :]
            acc = jnp.zeros((SUB, TQ), F32)
            for h in range(IDX_HEADS):
                lg = jnp.dot(ikc, iqt_ref[0, h], preferred_element_type=F32)
                acc = acc + jnp.maximum(lg, 0.0) * iwt_ref[0, h:h + 1, :]
            causal = off + sub_iota <= t_row
            sc_ref[pl.ds(off, SUB), :] = jnp.where(causal, acc, NEG_INF)
            rows = lambda x: x.reshape(SUB // SUBLANES, SUBLANES, TQ)
            mn = jnp.minimum(mn, jnp.min(rows(jnp.where(causal, acc, -LOWEST)), axis=0))
            mx = jnp.maximum(mx, jnp.max(rows(jnp.where(causal, acc, LOWEST)), axis=0))
        return mn, mx

    mn, mx = lax.fori_loop(0, n_ch, score_chunk, (jnp.full((SUBLANES, TQ), -LOWEST, F32),
                                                  jnp.full((SUBLANES, TQ), LOWEST, F32)))
    lo = jnp.min(mn, axis=0, keepdims=True)
    hi = jnp.max(mx, axis=0, keepdims=True)

    def chunk(c):
        off = pl.multiple_of(c * KC, KC)
        return sc_ref[pl.ds(off, KC), :], off

    AR = 4 * SUBLANES

    def fold(x, op):
        return op(x.reshape(KC // AR, AR, TQ), axis=0)

    def key_pass(fn, init):
        def body(c, acc):
            xs, off = chunk(c)
            return fn(acc, xs, off)
        return lax.fori_loop(0, n_ch, body, init)

    def count_ge(thr):
        acc = key_pass(lambda a, xs, off: a + fold(jnp.where(xs >= thr, 1.0, 0.0), jnp.sum),
                       jnp.zeros((AR, TQ), F32))
        return jnp.sum(acc, axis=0, keepdims=True)

    def count_gt(thr):
        acc = key_pass(lambda a, xs, off: a + fold(jnp.where(xs > thr, 1.0, 0.0), jnp.sum),
                       jnp.zeros((AR, TQ), F32))
        return jnp.sum(acc, axis=0, keepdims=True)

    def max_where(bound):
        fn = lambda a, xs, off: jnp.maximum(a, fold(jnp.where(xs <= bound, xs, LOWEST), jnp.max))
        acc = key_pass(fn, jnp.full((AR, TQ), LOWEST, F32))
        return jnp.max(acc, axis=0, keepdims=True)

    def count_and_next(v):
        def fn(a, xs, off):
            cnt, nxt = a
            return (cnt + fold(jnp.where(xs >= v, 1.0, 0.0), jnp.sum),
                    jnp.maximum(nxt, fold(jnp.where(xs < v, xs, LOWEST), jnp.max)))
        cnt, nxt = key_pass(fn, (jnp.zeros((AR, TQ), F32), jnp.full((AR, TQ), LOWEST, F32)))
        return jnp.sum(cnt, axis=0, keepdims=True), jnp.max(nxt, axis=0, keepdims=True)

    def bisect(_, carry):
        lo, hi = carry
        mid = 0.5 * (lo + hi)
        ge = count_ge(mid) >= kf
        return jnp.where(ge, mid, lo), jnp.where(ge, hi, mid)

    lo, hi = lax.fori_loop(0, NBIS, bisect, (lo, hi))

    all_keys = jnp.where(t_row < K, 1.0, 0.0)
    v = max_where(hi)
    n_ge, nxt = count_and_next(v)
    done = jnp.maximum(all_keys, jnp.where(n_ge >= kf, 1.0, 0.0))

    def peel_cond(st):
        return jnp.sum(st[1]) < float(TQ)

    def peel_body(st):
        v, done, n_ge, nxt = st
        v = jnp.where(done > 0.5, v, nxt)
        cnt, nxt = count_and_next(v)
        n_ge = jnp.where(done > 0.5, n_ge, cnt)
        done = jnp.maximum(done, jnp.where(cnt >= kf, 1.0, 0.0))
        return v, done, n_ge, nxt

    v, done, n_ge, _ = lax.while_loop(peel_cond, peel_body, (v, done, n_ge, nxt))
    thr = jnp.where(all_keys > 0.5, LOWEST, v)
    has_tie = jnp.max(jnp.where((n_ge > kf) & (all_keys < 0.5), 1.0, 0.0)) > 0.5

    def write(off, sel):
        bias_ref[0, pl.ds(off, KC), :] = jnp.where(sel, 0.0, NEG_INF).astype(bias_ref.dtype)

    def fill_tail():
        def body(c, carry):
            write(pl.multiple_of(c * KC, KC), jnp.zeros((KC, TQ), jnp.bool_))
            return carry
        lax.fori_loop(n_ch, n_all, body, 0)

    @pl.when(jnp.logical_not(has_tie))
    def _():
        def body(c, carry):
            xs, off = chunk(c)
            write(off, (xs >= thr) & ((off + key_iota) <= t_row))
            return carry
        lax.fori_loop(0, n_ch, body, 0)
        fill_tail()

    @pl.when(has_tie)
    def _():
        need = kf - count_gt(thr)
        r_i = lax.broadcasted_iota(jnp.int32, (KC, KC), 0)
        c_i = lax.broadcasted_iota(jnp.int32, (KC, KC), 1)
        lower = jnp.where(c_i <= r_i, 1.0, 0.0).astype(MXU_DTYPE)
        def body(c, seen):
            xs, off = chunk(c)
            causal = (off + key_iota) <= t_row
            eq = jnp.where((xs == thr) & causal, 1.0, 0.0)
            rank = jnp.dot(lower, eq.astype(MXU_DTYPE), preferred_element_type=F32) + seen
            write(off, ((xs > thr) & causal) | ((eq > 0.5) & (rank <= need)))
            return seen + jnp.sum(eq, axis=0, keepdims=True)
        lax.fori_loop(0, n_ch, body, jnp.zeros((1, TQ), F32))
        fill_tail()


def _dsa_select(ik, iqt, iwt, K, TQ, KC):
    B, S, DI = ik.shape
    H = iqt.shape[1]
    kern = functools.partial(_select_kernel, TQ=TQ, KC=KC, S=S, K=K, NBIS=SELECT_BISECTIONS)
    return pl.pallas_call(
        kern,
        grid=(B, S // TQ),
        in_specs=[pl.BlockSpec((1, S, DI), lambda b, q: (b, 0, 0)),
                  pl.BlockSpec((1, H, DI, TQ), lambda b, q: (b, 0, 0, q)),
                  pl.BlockSpec((1, H, TQ), lambda b, q: (b, 0, q))],
        out_specs=pl.BlockSpec((1, S, TQ), lambda b, q: (b, 0, q)),
        out_shape=jax.ShapeDtypeStruct((B, S, S), jnp.bfloat16),
        scratch_shapes=[pltpu.VMEM((S, TQ), F32)],
        compiler_params=_cparams(("arbitrary", "arbitrary")),
    )(ik, iqt, iwt)


def _cmp_attn_kernel(qt_ref, kc_ref, vct_ref, o_ref, bm_ref, *, TQ, NCP, NBP, N_SEL):
    qi = pl.program_id(1)
    q0 = qi * TQ
    G, J = NSA_KV_GROUPS, NSA_REP
    N = J * TQ
    t_lane = q0 + (lax.broadcasted_iota(jnp.int32, (NCP, N), 1) & (TQ - 1))
    cmp_end = lax.broadcasted_iota(jnp.int32, (NCP, N), 0) * CMP_STRIDE + (CMP_LEN - 1)
    vis = cmp_end <= t_lane
    n_i = lax.broadcasted_iota(jnp.int32, (NBP, NCP), 0) * SEL_BLOCK
    c_i = lax.broadcasted_iota(jnp.int32, (NBP, NCP), 1) * CMP_STRIDE
    overlap_t = jnp.where((c_i <= n_i + SEL_BLOCK - 1) & (c_i + CMP_LEN - 1 >= n_i), 1.0, 0.0)
    blk = lax.broadcasted_iota(jnp.int32, (NBP, TQ), 0)
    tq = q0 + lax.broadcasted_iota(jnp.int32, (NBP, TQ), 1)
    cur = tq // SEL_BLOCK
    admissible = blk * SEL_BLOCK <= tq
    forced = (blk == 0) | (blk == cur) | (blk == cur - 1)
    for g in range(G):
        qt = jnp.concatenate([qt_ref[0, g * J + j] for j in range(J)], axis=1)
        s = jnp.where(vis, _dot(kc_ref[0, g], qt), NEG_INF)
        p = jnp.exp(s - jnp.max(s, axis=0, keepdims=True))
        p = jnp.where(vis, p, 0.0)
        p = p / jnp.maximum(jnp.sum(p, axis=0, keepdims=True), TINY)
        o = _dot(vct_ref[0, g], p)
        for j in range(J):
            o_ref[0, g * J + j] = o[:, j * TQ:(j + 1) * TQ]
        psum = p[:, 0:TQ]
        for j in range(1, J):
            psum = psum + p[:, j * TQ:(j + 1) * TQ]
        imp = _dot_exact_lhs(overlap_t, psum)
        score = jnp.where(admissible & forced, FORCED, jnp.where(admissible, imp, NEG_INF))

        def pick(_, st):
            score, sel = st
            m = jnp.max(score, axis=0, keepdims=True)
            first = jnp.min(jnp.where(score == m, blk, NBP), axis=0, keepdims=True)
            hit = blk == first
            return jnp.where(hit, LOWEST, score), jnp.where(hit, 1.0, sel)

        _, sel = lax.fori_loop(0, N_SEL, pick, (score, jnp.zeros((NBP, TQ), F32)))
        bm_ref[0, g] = sel.astype(bm_ref.dtype)


def _cmp_attn(qt, kc, vct, n_sel, NBP, TQ):
    B, H, HD, S = qt.shape
    G, NCP = kc.shape[1], kc.shape[2]
    kern = functools.partial(_cmp_attn_kernel, TQ=TQ, NCP=NCP, NBP=NBP, N_SEL=n_sel)
    return pl.pallas_call(
        kern,
        grid=(B, S // TQ),
        in_specs=[pl.BlockSpec((1, H, HD, TQ), lambda b, q: (b, 0, 0, q)),
                  pl.BlockSpec((1, G, NCP, HD), lambda b, q: (b, 0, 0, 0)),
                  pl.BlockSpec((1, G, HD, NCP), lambda b, q: (b, 0, 0, 0))],
        out_specs=[pl.BlockSpec((1, H, HD, TQ), lambda b, q: (b, 0, 0, q)),
                   pl.BlockSpec((1, G, NBP, TQ), lambda b, q: (b, 0, 0, q))],
        out_shape=[jax.ShapeDtypeStruct((B, H, HD, S), F32),
                   jax.ShapeDtypeStruct((B, G, NBP, S), jnp.bfloat16)],
        compiler_params=_cparams(("arbitrary", "arbitrary")),
    )(qt, kc, vct)


def _flash_kernel(*refs, mode, G, J, TQ, TK, NWIN):
    if mode == "window":
        qt_ref, k_ref, vt_ref, o_ref, m_ref, acc_ref = refs
        x_ref = None
    else:
        qt_ref, k_ref, vt_ref, x_ref, o_ref, m_ref, acc_ref = refs
    HD = HEAD_DIM
    qi = pl.program_id(1)
    kk = pl.program_id(2)
    q0 = qi * TQ
    last = (q0 + TQ - 1) // TK
    if mode == "window":
        ki = last - (NWIN - 1) + kk
        valid = ki >= 0
    else:
        ki = kk
        valid = kk <= last

    @pl.when(kk == 0)
    def _():
        m_ref[...] = jnp.full(m_ref.shape, NEG_INF, F32)
        acc_ref[...] = jnp.zeros(acc_ref.shape, F32)

    @pl.when(valid)
    def _():
        k0 = ki * TK
        kidx = k0 + lax.broadcasted_iota(jnp.int32, (TK, TQ), 0)
        tq = q0 + lax.broadcasted_iota(jnp.int32, (TK, TQ), 1)
        if mode == "mask":
            bias = x_ref[0].astype(F32)
        elif mode == "window":
            bias = jnp.where((kidx <= tq) & (kidx > tq - WINDOW), 0.0, NEG_INF)
        else:
            causal = kidx <= tq
            nbp = x_ref.shape[2]
            blk_of_key = (k0 + lax.broadcasted_iota(jnp.int32, (TK, nbp), 0)) // SEL_BLOCK
            expand_t = jnp.where(lax.broadcasted_iota(jnp.int32, (TK, nbp), 1) == blk_of_key, 1.0, 0.0)
            expand_t = expand_t.astype(MXU_DTYPE)

        for g in range(G):
            if mode == "block":
                picked = jnp.dot(expand_t, x_ref[0, g].astype(MXU_DTYPE), preferred_element_type=F32)
                b = jnp.where(causal & (picked > 0.5), 0.0, NEG_INF)
            else:
                b = bias
            if J > 1:
                b = jnp.concatenate([b] * J, axis=1)
                qt = jnp.concatenate([qt_ref[0, g * J + j] for j in range(J)], axis=1)
            else:
                qt = qt_ref[0, g]
            s = _dot(k_ref[0, g], qt) + b
            m_prev = m_ref[g]
            m_new = jnp.maximum(m_prev, jnp.max(s, axis=0, keepdims=True))
            alpha = jnp.exp(m_prev - m_new)
            p = jnp.exp((s - m_new).astype(MXU_DTYPE))
            acc_ref[g] = alpha * acc_ref[g] + _dot(vt_ref[0, g], p)
            m_ref[g] = m_new

    @pl.when(kk == pl.num_programs(2) - 1)
    def _():
        for g in range(G):
            o = acc_ref[g, 0:HD] / jnp.maximum(acc_ref[g, HD:HD + 1], TINY)
            for j in range(J):
                o_ref[0, g * J + j] = o[:, j * TQ:(j + 1) * TQ].astype(o_ref.dtype)


def _flash(qt, k, vt, extra, mode, TQ, TK, out_dtype):
    B, H, HD, S = qt.shape
    G = k.shape[1]
    J = H // G
    N = J * TQ
    nq = S // TQ
    last_of = lambda qi: (qi * TQ + TQ - 1) // TK
    if mode == "window":
        NWIN = min((WINDOW - 1 + TK - 1) // TK + 1, S // TK)
        nk = NWIN
        kidx_of = lambda qi, kk: jnp.maximum(last_of(qi) - (NWIN - 1) + kk, 0)
    else:
        NWIN = 0
        nk = S // TK
        kidx_of = lambda qi, kk: jnp.minimum(kk, last_of(qi))
    in_specs = [pl.BlockSpec((1, H, HD, TQ), lambda b, qi, kk: (b, 0, 0, qi)),
                pl.BlockSpec((1, G, TK, HD), lambda b, qi, kk: (b, 0, kidx_of(qi, kk), 0)),
                pl.BlockSpec((1, G, HD + ONES_ROWS, TK), lambda b, qi, kk: (b, 0, 0, kidx_of(qi, kk)))]
    args = [qt, k, vt]
    if mode == "mask":
        in_specs.append(pl.BlockSpec((1, TK, TQ), lambda b, qi, kk: (b, kidx_of(qi, kk), qi)))
        args.append(extra)
    elif mode == "block":
        nbp = extra.shape[2]
        in_specs.append(pl.BlockSpec((1, G, nbp, TQ), lambda b, qi, kk: (b, 0, 0, qi)))
        args.append(extra)
    kern = functools.partial(_flash_kernel, mode=mode, G=G, J=J, TQ=TQ, TK=TK, NWIN=NWIN)
    return pl.pallas_call(
        kern,
        grid=(B, nq, nk),
        in_specs=in_specs,
        out_specs=pl.BlockSpec((1, H, HD, TQ), lambda b, qi, kk: (b, 0, 0, qi)),
        out_shape=jax.ShapeDtypeStruct((B, H, HD, S), out_dtype),
        scratch_shapes=[pltpu.VMEM((G, 1, N), F32),
                        pltpu.VMEM((G, HD + ONES_ROWS, N), F32)],
        compiler_params=_cparams(("arbitrary", "arbitrary", "arbitrary")),
    )(*args)


def _dot_tn(a_t, b):
    return lax.dot_general(a_t.astype(MXU_DTYPE), b.astype(MXU_DTYPE),
                           (((0,), (0,)), ((), ())), preferred_element_type=F32)


def _route(h, wr, br):
    tm = h.shape[0]
    lane = lax.broadcasted_iota(jnp.int32, (tm, LANES), 1)
    h_hi = h.astype(MXU_DTYPE)
    h_lo = (h - h_hi.astype(F32)).astype(MXU_DTYPE)
    w_hi = wr.astype(MXU_DTYPE)
    w_lo = (wr - w_hi.astype(F32)).astype(MXU_DTYPE)
    d = lambda a, b: jnp.dot(a, b, preferred_element_type=F32)
    logits = d(h_hi, w_hi) + d(h_hi, w_lo) + d(h_lo, w_hi) + br
    is_e = lane < N_EXPERTS
    is_g = (lane >= N_EXPERTS) & (lane < N_EXPERTS + N_GROUPS)
    lg = jnp.where(is_g, logits, LOWEST)
    mg = jnp.max(lg, axis=-1, keepdims=True)
    gsel = jnp.min(jnp.where(is_g & (lg == mg), lane - N_EXPERTS, N_GROUPS), axis=-1, keepdims=True)
    pg_sel = 1.0 / jnp.sum(jnp.where(is_g, jnp.exp(lg - mg), 0.0), axis=-1, keepdims=True)
    in_grp = is_e & ((lane // EXPERTS_PER_GROUP) == gsel)
    le = jnp.where(in_grp, logits, LOWEST)
    me = jnp.max(le, axis=-1, keepdims=True)
    ex = jnp.where(in_grp, jnp.exp(le - me), 0.0)
    pe = ex / jnp.sum(ex, axis=-1, keepdims=True)
    pe = jnp.where(in_grp, pe, -1.0)
    p1 = jnp.max(pe, axis=-1, keepdims=True)
    i1 = jnp.min(jnp.where(pe == p1, lane, LANES), axis=-1, keepdims=True)
    pe2 = jnp.where(lane == i1, -1.0, pe)
    p2 = jnp.max(pe2, axis=-1, keepdims=True)
    i2 = jnp.min(jnp.where(pe2 == p2, lane, LANES), axis=-1, keepdims=True)
    tot = p1 + p2
    comb = jnp.where(lane == i1, p1 / tot * pg_sel, 0.0) + jnp.where(lane == i2, p2 / tot * pg_sel, 0.0)
    return jnp.where(lane == N_EXPERTS, gsel.astype(F32), comb)


def _outproj_kernel(x_ref, oa_ref, oc_ref, os_ref, ow_ref, br_ref, mg_ref, wa_ref, wb_ref, wo_ref,
                    gt_ref, g2_ref, sc_ref, sh_ref, wr_ref, brt_ref, o_ref, hr_ref):
    HD = HEAD_DIM
    D = x_ref.shape[1]
    br = br_ref[0]
    parts = []
    for hh in range(NSA_HEADS):
        parts.append(br[3 * hh:3 * hh + 1] * oc_ref[0, hh] + br[3 * hh + 1:3 * hh + 2] * os_ref[0, hh]
                     + br[3 * hh + 2:3 * hh + 3] * ow_ref[0, hh])
    ob_t = jnp.concatenate(parts, axis=0)
    oa_t = jnp.concatenate([oa_ref[0, hh] for hh in range(DSA_HEADS)], axis=0)
    ua = _dot_tn(oa_t, wa_ref[...])
    ub = _dot_tn(ob_t, wb_ref[...])
    mg = mg_ref[...]
    merged = jax.nn.sigmoid(mg[:, :D]) * ua + jax.nn.sigmoid(mg[:, D:]) * ub
    x1 = x_ref[...] + gt_ref[0] * _dot(merged, wo_ref[...])
    o_ref[...] = x1
    h = _rms(x1, g2_ref[...]) * (1.0 + sc_ref[0]) + sh_ref[0]
    hr_ref[:, :D] = h
    hr_ref[:, D:] = _route(h, wr_ref[...], brt_ref[...])


def _out_proj(x2d, oa, oc, os_, ow, br, mg, wa, wb, wo, gt, g2, sc2, sh2, wr, brt, S, tm):
    T, D = x2d.shape
    B, H, HD, _ = oa.shape
    per_b = S // tm
    heads_cols = pl.BlockSpec((1, H, HD, tm), lambda i: (i // per_b, 0, 0, i % per_b))
    per_batch = pl.BlockSpec((1, 1, D), lambda i: (i // per_b, 0, 0))
    rows = lambda n: pl.BlockSpec((tm, n), lambda i: (i, 0))
    const = lambda a, b: pl.BlockSpec((a, b), lambda i: (0, 0))
    return pl.pallas_call(
        _outproj_kernel,
        grid=(T // tm,),
        in_specs=[rows(D), heads_cols, heads_cols, heads_cols, heads_cols,
                  pl.BlockSpec((1, br.shape[1], tm), lambda i: (i // per_b, 0, i % per_b)),
                  rows(2 * D), const(H * HD, D), const(H * HD, D), const(D, D), per_batch,
                  const(1, D), per_batch, per_batch, const(D, LANES), const(1, LANES)],
        out_specs=[rows(D), rows(D + LANES)],
        out_shape=[jax.ShapeDtypeStruct((T, D), F32), jax.ShapeDtypeStruct((T, D + LANES), F32)],
        compiler_params=_cparams(("arbitrary",)),
    )(x2d, oa, oc, os_, ow, br, mg, wa, wb, wo, gt, g2.reshape(1, D), sc2, sh2, wr, brt)


def _row_copy(src_hbm, t, dst, r, sem):
    return pltpu.make_async_copy(src_hbm.at[pl.ds(t, 1)], dst.at[pl.ds(r, 1)], sem)


def _start_rows(idx_ref, base, r0, n, src_hbm, dst, sem):
    for r in range(n):
        _row_copy(src_hbm, idx_ref[base + r0 + r], dst, r0 + r, sem).start()


def _wait_rows(src_hbm, dst, n, sem):
    pltpu.make_async_copy(src_hbm.at[pl.ds(0, n)], dst.at[pl.ds(0, n)], sem).wait()


def _experts_kernel(tg_ref, src_ref, h_hbm, wg_ref, wu_ref, wd_ref, o_ref,
                    hbuf, acc_ref, sems, *, TM):
    i = pl.program_id(0)
    e = pl.program_id(1)
    n_tiles = pl.num_programs(0)
    EPG = EXPERTS_PER_GROUP
    part = TM // EPG
    slot = i % 2

    @pl.when((i == 0) & (e == 0))
    def _():
        def start(c, carry):
            _start_rows(src_ref, 0, pl.multiple_of(c * part, part), part, h_hbm, hbuf.at[0], sems.at[0])
            return carry
        lax.fori_loop(0, EPG, start, 0)

    @pl.when(e == 0)
    def _():
        _wait_rows(h_hbm, hbuf.at[slot], TM, sems.at[slot])
        acc_ref[...] = jnp.zeros(acc_ref.shape, F32)

    @pl.when(i + 1 < n_tiles)
    def _():
        _start_rows(src_ref, (i + 1) * TM, e * part, part, h_hbm, hbuf.at[1 - slot], sems.at[1 - slot])

    D = acc_ref.shape[1]
    h = hbuf[slot, :, :D].astype(MXU_DTYPE)
    a = jnp.dot(h, wg_ref[0], preferred_element_type=F32)
    u = jnp.dot(h, wu_ref[0], preferred_element_type=F32)
    y = _dot(a * jax.nn.sigmoid(a) * u, wd_ref[0])
    lane = lax.broadcasted_iota(jnp.int32, (TM, LANES), 1)
    expert = tg_ref[i] * EPG + e
    w_e = jnp.sum(jnp.where(lane == expert, hbuf[slot, :, D:], 0.0), axis=-1, keepdims=True)
    acc_ref[...] += w_e * y

    @pl.when(e == pl.num_programs(1) - 1)
    def _():
        o_ref[...] = acc_ref[...]


def _experts(hr, tile_group, src, wg, wu, wd, TM):
    D = wg.shape[1]
    DE = wg.shape[2]
    P = src.shape[0]
    EPG = EXPERTS_PER_GROUP
    w_idx = lambda i, e, tg, src: (tg[i] * EPG + e, 0, 0)
    kern = functools.partial(_experts_kernel, TM=TM)
    grid_spec = pltpu.PrefetchScalarGridSpec(
        num_scalar_prefetch=2,
        grid=(P // TM, EPG),
        in_specs=[pl.BlockSpec(memory_space=pl.ANY),
                  pl.BlockSpec((1, D, DE), w_idx),
                  pl.BlockSpec((1, D, DE), w_idx),
                  pl.BlockSpec((1, DE, D), w_idx)],
        out_specs=pl.BlockSpec((TM, D), lambda i, e, tg, src: (i, 0)),
        scratch_shapes=[pltpu.VMEM((2, TM, D + LANES), F32),
                        pltpu.VMEM((TM, D), F32),
                        pltpu.SemaphoreType.DMA((2,))])
    return pl.pallas_call(
        kern,
        grid_spec=grid_spec,
        out_shape=jax.ShapeDtypeStruct((P, D), F32),
        compiler_params=_cparams(("arbitrary", "arbitrary")),
    )(tile_group, src, hr, wg, wu, wd)


def _combine_kernel(pos_ref, x_ref, y_hbm, gt_ref, gf_ref, o_ref, ybuf, sem, *, TM):
    part = 64
    def start(c, carry):
        _start_rows(pos_ref, pl.program_id(0) * TM, pl.multiple_of(c * part, part), part, y_hbm, ybuf,
                    sem.at[0])
        return carry
    lax.fori_loop(0, TM // part, start, 0)
    _wait_rows(y_hbm, ybuf, TM, sem.at[0])
    o_ref[...] = _rms(x_ref[...] + gt_ref[0] * ybuf[...], gf_ref[...])


def _combine(x1, ys, pos, gt, gf, S, tm):
    T, D = x1.shape
    per_b = S // tm
    grid_spec = pltpu.PrefetchScalarGridSpec(
        num_scalar_prefetch=1,
        grid=(T // tm,),
        in_specs=[pl.BlockSpec((tm, D), lambda i, pos: (i, 0)),
                  pl.BlockSpec(memory_space=pl.ANY),
                  pl.BlockSpec((1, 1, D), lambda i, pos: (i // per_b, 0, 0)),
                  pl.BlockSpec((1, D), lambda i, pos: (0, 0))],
        out_specs=pl.BlockSpec((tm, D), lambda i, pos: (i, 0)),
        scratch_shapes=[pltpu.VMEM((tm, D), F32), pltpu.SemaphoreType.DMA((1,))])
    return pl.pallas_call(
        functools.partial(_combine_kernel, TM=tm),
        grid_spec=grid_spec,
        out_shape=jax.ShapeDtypeStruct((T, D), F32),
        compiler_params=_cparams(("arbitrary",)),
    )(pos, x1, ys, gt, gf.reshape(1, D))


def _routed_moe(x1, hr, gt, gf, wg, wu, wd, S, TM, combine_rows):
    T, D = x1.shape
    gsel = hr[:, D + N_EXPERTS].astype(jnp.int32)
    onehot = (gsel[:, None] == jnp.arange(N_GROUPS, dtype=jnp.int32)[None, :]).astype(jnp.int32)
    csum = jnp.cumsum(onehot, axis=0)
    rank = jnp.take_along_axis(csum, gsel[:, None], axis=1)[:, 0] - 1
    padded = -(-csum[-1] // TM) * TM
    ends = jnp.cumsum(padded)
    starts = ends - padded
    pos = (jnp.take(starts, gsel) + rank).astype(jnp.int32)
    P = T + N_GROUPS * TM
    src = jnp.zeros((P,), jnp.int32).at[pos].set(jnp.arange(T, dtype=jnp.int32))
    tile_start = jnp.arange(P // TM, dtype=jnp.int32) * TM
    tile_group = jnp.minimum(jnp.sum(tile_start[:, None] >= ends[None, :], axis=1), N_GROUPS - 1)
    ys = _experts(hr, tile_group.astype(jnp.int32), src, wg, wu, wd, TM)
    return _combine(x1, ys, pos, gt, gf, S, combine_rows)


def _layer(x, mod, positions, g_norm1, w_in, g_kv_latent, w_kv_up, pe_cmp_k, pe_cmp_v,
           w_cmp1_k, w_cmp2_k, w_cmp1_v, w_cmp2_v, w_up_a, w_up_b, w_out, g_norm2,
           w_router_group, b_router_group, w_router_expert, b_router_expert,
           w_expert_gate, w_expert_up, w_expert_down, g_out):
    B, S, D = x.shape
    T = B * S
    HD, G = HEAD_DIM, NSA_KV_GROUPS
    kvb = G * HD
    topk_a = min(DSA_TOPK_MAX, S // 4)
    n_sel = min(SEL_COUNT, S // SEL_BLOCK)
    mod6 = mod.reshape(B, 6, 1, D)
    sh1, sc1, gt1, sh2, sc2, gt2 = (mod6[:, i] for i in range(6))

    x2d = x.reshape(T, D)
    t = _tiles(S)
    (mg, qt_a, k_a, vt_a, iqt, iwt, ik, qt_b, kcvc, ks, kw, vst, vwt, br) = _prep(
        x, sc1, sh1, g_norm1, w_in, g_kv_latent, w_kv_up, positions, t.rows)

    kc = _compress(kcvc[..., :kvb].reshape(B, S, G, HD), pe_cmp_k, w_cmp1_k, w_cmp2_k)
    vc = _compress(kcvc[..., kvb:].reshape(B, S, G, HD), pe_cmp_v, w_cmp1_v, w_cmp2_v)
    n_cmp = kc.shape[1]
    ncp = -(-(n_cmp + 1) // LANES) * LANES
    pad_c = lambda t: jnp.pad(t, ((0, 0), (0, ncp - n_cmp), (0, 0), (0, 0))).astype(MXU_DTYPE)
    kc, vct = pad_c(kc).transpose(0, 2, 1, 3), pad_c(vc).transpose(0, 2, 3, 1)

    sel_bias = _dsa_select(ik, iqt, iwt, topk_a, t.select_q, t.select_k)
    o_a = _flash(qt_a, k_a, vt_a, sel_bias, "mask", t.mask_q, t.mask_k, MXU_DTYPE)

    nbp = -(-(S // SEL_BLOCK) // LANES) * LANES
    o_c, blk_mask = _cmp_attn(qt_b, kc, vct, n_sel, nbp, t.nsa_q)
    o_s = _flash(qt_b, ks, vst, blk_mask, "block", t.nsa_q, t.block_k, F32)
    o_w = _flash(qt_b, kw, vwt, None, "window", t.nsa_q, t.nsa_q, F32)

    wr = jnp.concatenate([w_router_expert, w_router_group], axis=1)
    wr = jnp.pad(wr, ((0, 0), (0, LANES - wr.shape[1])))
    brt = jnp.concatenate([b_router_expert, b_router_group])
    brt = jnp.pad(brt, (0, LANES - brt.shape[0])).reshape(1, LANES)
    x1, hr = _out_proj(x2d, o_a, o_c, o_s, o_w, br, mg, w_up_a.astype(MXU_DTYPE),
                              w_up_b.astype(MXU_DTYPE), w_out.astype(MXU_DTYPE), gt1,
                              g_norm2, sc2, sh2, wr, brt, S, t.rows)

    out = _routed_moe(x1, hr, gt2, g_out, w_expert_gate.astype(MXU_DTYPE),
                      w_expert_up.astype(MXU_DTYPE), w_expert_down.astype(MXU_DTYPE), S,
                      t.moe_rows, t.rows)
    return out.reshape(B, S, D)


def kernel(x, c, positions, w_ada, b_ada, g_norm1, w_in, g_kv_latent, w_kv_up, pe_cmp_k, pe_cmp_v,
           w_cmp1_k, w_cmp2_k, w_cmp1_v, w_cmp2_v, w_up_a, w_up_b, w_out, g_norm2, w_router_group,
           b_router_group, w_router_expert, b_router_expert, w_expert_gate, w_expert_up,
           w_expert_down, g_final):
    depth = w_ada.shape[0]
    assert depth == 1, "the fused final norm assumes a single layer"
    mod = _ada_mod(c, w_ada[0], b_ada[0])
    return _layer(x, mod, positions, g_norm1[0], w_in[0], g_kv_latent[0], w_kv_up[0], pe_cmp_k[0],
                  pe_cmp_v[0], w_cmp1_k[0], w_cmp2_k[0], w_cmp1_v[0], w_cmp2_v[0], w_up_a[0],
                  w_up_b[0], w_out[0], g_norm2[0], w_router_group[0], b_router_group[0],
                  w_router_expert[0], b_router_expert[0], w_expert_gate[0], w_expert_up[0],
                  w_expert_down[0], g_final)
```

```python
import functools
from typing import NamedTuple

import numpy as np
import jax
import jax.numpy as jnp
from jax import lax
from jax.experimental import pallas as pl
from jax.experimental.pallas import tpu as pltpu

HEAD_DIM = 64
ROT_FRACTION = 4
ROPE_THETA = 500000.0
DSA_HEADS = 8
DSA_KV_RANK = 128
IDX_HEADS = 8
IDX_DIM = 32
DSA_TOPK_MAX = 256
NSA_HEADS = 8
NSA_KV_GROUPS = 2
NSA_REP = NSA_HEADS // NSA_KV_GROUPS
CMP_LEN = 32
CMP_STRIDE = 16
CMP_HIDDEN = 256
SEL_BLOCK = 64
SEL_COUNT = 16
WINDOW = 512
N_GROUPS = 4
EXPERTS_PER_GROUP = 8
N_EXPERTS = N_GROUPS * EXPERTS_PER_GROUP
D_EXPERT = 256
NORM_EPS = 1e-6
NEG_INF = -1e30
TINY = 1e-30
LOWEST = -3.0e38
FORCED = 1e30

LANES = 128
SUBLANES = 8
ONES_ROWS = 16
MXU_DTYPE = jnp.bfloat16
VMEM_LIMIT = 56 * 1024 * 1024

F32 = jnp.float32
SELECT_BISECTIONS = 16


class _Tiles(NamedTuple):
    rows: int
    select_q: int
    select_k: int
    mask_q: int
    mask_k: int
    nsa_q: int
    block_k: int
    moe_rows: int


def _tiles(S):
    cap = lambda n: min(n, S)
    return _Tiles(rows=cap(256), select_q=cap(256), select_k=cap(256), mask_q=cap(1024),
                  mask_k=cap(1024), nsa_q=cap(512), block_k=cap(1024), moe_rows=512)


def _cparams(sem):
    return pltpu.CompilerParams(dimension_semantics=sem, vmem_limit_bytes=VMEM_LIMIT)


def _dot(a, b):
    return jnp.dot(a.astype(MXU_DTYPE), b.astype(MXU_DTYPE), preferred_element_type=F32)


def _dot_exact_lhs(a01, b):
    hi = b.astype(MXU_DTYPE)
    r1 = b - hi.astype(F32)
    mid = r1.astype(MXU_DTYPE)
    lo = (r1 - mid.astype(F32)).astype(MXU_DTYPE)
    a = a01.astype(MXU_DTYPE)
    d = lambda u: jnp.dot(a, u, preferred_element_type=F32)
    return d(hi) + d(mid) + d(lo)


def _rms(x, g):
    return x * lax.rsqrt(jnp.mean(x * x, axis=-1, keepdims=True) + NORM_EPS) * g


def _ada_kernel(c_ref, w_ref, b_ref, o_ref):
    c = c_ref[...]
    cond = c * jax.nn.sigmoid(c)
    o_ref[...] = _dot(cond, w_ref[...]) + b_ref[...]


def _ada_mod(c, w_ada, b_ada):
    B, D = c.shape
    n_out = w_ada.shape[1]
    rows = SUBLANES
    cp = jnp.zeros((rows, D), F32).at[:B].set(c)
    tn = 1024
    out = pl.pallas_call(
        _ada_kernel,
        grid=(n_out // tn,),
        in_specs=[pl.BlockSpec((rows, D), lambda j: (0, 0)),
                  pl.BlockSpec((D, tn), lambda j: (0, j)),
                  pl.BlockSpec((1, tn), lambda j: (0, j))],
        out_specs=pl.BlockSpec((rows, tn), lambda j: (0, j)),
        out_shape=jax.ShapeDtypeStruct((rows, n_out), F32),
        compiler_params=_cparams(("arbitrary",)),
    )(cp, w_ada, b_ada.reshape(1, n_out))
    return out[:B]


KVB = NSA_KV_GROUPS * HEAD_DIM
ROW_SEGS = (("mg", 2048), ("lat", DSA_KV_RANK), ("kc", KVB), ("vc", KVB), ("ks", KVB), ("kw", KVB),
            ("ik", LANES))
COL_SEGS = (("qa", DSA_HEADS * HEAD_DIM), ("qb", NSA_HEADS * HEAD_DIM), ("iq", IDX_HEADS * IDX_DIM),
            ("vs", KVB), ("vw", KVB), ("lat", DSA_KV_RANK), ("iw", IDX_HEADS), ("br", 3 * NSA_HEADS))


def _seg_offsets(segs):
    out, pos = {}, 0
    for name, n in segs:
        out[name] = pos
        pos += n
    return out, pos


ROW_OFF, ROW_COLS = _seg_offsets(ROW_SEGS)
COL_OFF, COL_ROWS = _seg_offsets(COL_SEGS)


def _rot_lanes(x, c, sa, sb, half):
    outs = []
    for j in range(x.shape[1] // LANES):
        xs = x[:, j * LANES:(j + 1) * LANES]
        outs.append(xs * c + pltpu.roll(xs, half, 1) * sa + pltpu.roll(xs, LANES - half, 1) * sb)
    return outs[0] if len(outs) == 1 else jnp.concatenate(outs, axis=1)


def _prep_kernel(x_ref, g_ref, sc_ref, sh_ref, gc_ref, scc_ref, shc_ref, wr_ref, wc_ref,
                 gkv_ref, gkvc_ref, wkk_ref, wkv_ref, posc_ref, posr_ref, f64l_ref, f32l_ref,
                 f64c_ref, f32c_ref,
                 mg_ref, qa_ref, ka_ref, va_ref, iq_ref, iw_ref, ik_ref, qb_ref, kcvc_ref,
                 ks_ref, kw_ref, vs_ref, vw_ref, br_ref):
    HD = HEAD_DIM
    half = HD // ROT_FRACTION // 2
    scale = HD ** -0.5
    h = _rms(x_ref[...], g_ref[...]) * (1.0 + sc_ref[0]) + sh_ref[0]
    pr = _dot(h, wr_ref[...])
    row = lambda name, n: pr[:, ROW_OFF[name]:ROW_OFF[name] + n]
    mg_ref[...] = row("mg", 2048)
    lane = lax.broadcasted_iota(jnp.int32, (x_ref.shape[0], LANES), 1)

    def lane_tables(freq_row, dim, hf):
        ang = posc_ref[...] * freq_row
        c, s = jnp.cos(ang), jnp.sin(ang)
        upper = (lane & (dim - 1)) >= hf
        return c, jnp.where(upper, s, 0.0), jnp.where(upper, 0.0, -s)

    c64, sa64, sb64 = lane_tables(f64l_ref[...], HD, half)
    c32, sa32, sb32 = lane_tables(f32l_ref[...], IDX_DIM, IDX_DIM // ROT_FRACTION // 2)
    ka = _rot_lanes(_dot(_rms(row("lat", DSA_KV_RANK), gkv_ref[...]), wkk_ref[...]), c64, sa64, sb64, half)
    for hh in range(DSA_HEADS):
        ka_ref[0, hh] = ka[:, hh * HD:(hh + 1) * HD].astype(ka_ref.dtype)
    kcvc_ref[0] = jnp.concatenate([_rot_lanes(row("kc", KVB), c64, sa64, sb64, half), row("vc", KVB)], axis=1)
    ks = _rot_lanes(row("ks", KVB), c64, sa64, sb64, half)
    kw = _rot_lanes(row("kw", KVB), c64, sa64, sb64, half)
    for g in range(NSA_KV_GROUPS):
        ks_ref[0, g] = ks[:, g * HD:(g + 1) * HD].astype(ks_ref.dtype)
        kw_ref[0, g] = kw[:, g * HD:(g + 1) * HD].astype(kw_ref.dtype)
    ik = _rot_lanes(row("ik", LANES), c32, sa32, sb32, IDX_DIM // ROT_FRACTION // 2)
    ik_ref[0] = ik[:, :IDX_DIM].astype(ik_ref.dtype)
    xt = x_ref[...].T
    ht = xt * lax.rsqrt(jnp.mean(xt * xt, axis=0, keepdims=True) + NORM_EPS) * gc_ref[...]
    ht = ht * (1.0 + scc_ref[0]) + shc_ref[0]
    pc = _dot(wc_ref[...], ht)
    col = lambda name, n: pc[COL_OFF[name]:COL_OFF[name] + n]
    ang_t = f64c_ref[...] * posr_ref[0]
    cos, sin = jnp.cos(ang_t), jnp.sin(ang_t)

    def rot_rows(blk):
        x1, x2 = blk[0:half], blk[half:2 * half]
        return jnp.concatenate([x1 * cos - x2 * sin, x2 * cos + x1 * sin, blk[2 * half:]], axis=0)

    qa, qb = col("qa", DSA_HEADS * HD), col("qb", NSA_HEADS * HD)
    for hh in range(DSA_HEADS):
        qa_ref[0, hh] = (rot_rows(qa[hh * HD:(hh + 1) * HD]) * scale).astype(qa_ref.dtype)
    for hh in range(NSA_HEADS):
        qb_ref[0, hh] = (rot_rows(qb[hh * HD:(hh + 1) * HD]) * scale).astype(qb_ref.dtype)
    iq = col("iq", IDX_HEADS * IDX_DIM)
    ang32 = f32c_ref[...] * posr_ref[0]
    first = lax.broadcasted_iota(jnp.int32, ang32.shape, 0) < SUBLANES // 2
    c32t, s32t = jnp.cos(ang32), jnp.where(first, -jnp.sin(ang32), jnp.sin(ang32))
    for hh in range(IDX_HEADS):
        blk = iq[hh * IDX_DIM:(hh + 1) * IDX_DIM]
        top = blk[0:SUBLANES]
        top = top * c32t + pltpu.roll(top, SUBLANES // 2, 0) * s32t
        iq_ref[0, hh] = jnp.concatenate([top, blk[SUBLANES:]], axis=0).astype(iq_ref.dtype)
    ones = jnp.ones((ONES_ROWS, xt.shape[1]), F32)
    with_ones = lambda v: jnp.concatenate([v, ones], axis=0)
    vs, vw = col("vs", KVB), col("vw", KVB)
    for g in range(NSA_KV_GROUPS):
        vs_ref[0, g] = with_ones(vs[g * HD:(g + 1) * HD]).astype(vs_ref.dtype)
        vw_ref[0, g] = with_ones(vw[g * HD:(g + 1) * HD]).astype(vw_ref.dtype)
    lat = col("lat", DSA_KV_RANK)
    lat = lat * lax.rsqrt(jnp.mean(lat * lat, axis=0, keepdims=True) + NORM_EPS) * gkvc_ref[...]
    va = _dot(wkv_ref[...], lat)
    for hh in range(DSA_HEADS):
        va_ref[0, hh] = with_ones(va[hh * HD:(hh + 1) * HD]).astype(va_ref.dtype)
    iw_ref[0] = col("iw", IDX_HEADS)
    br_ref[0] = jax.nn.sigmoid(col("br", 3 * NSA_HEADS))


def _prep(x, mod_sc, mod_sh, g_norm1, w_in, g_kv, w_kv_up, positions, tm):
    B, S, D = x.shape
    T = B * S
    HD, HA, G = HEAD_DIM, DSA_HEADS, NSA_KV_GROUPS
    per_b = S // tm
    sizes = [HA * HD, DSA_KV_RANK, IDX_HEADS * IDX_DIM, IDX_HEADS, IDX_DIM, NSA_HEADS * HD,
             KVB, KVB, KVB, KVB, KVB, KVB, 3 * NSA_HEADS, 2 * D]
    names = ["qa", "lat", "iq", "iw", "ik", "qb", "kc", "vc", "ks", "vs", "kw", "vw", "br", "mg"]
    starts = dict(zip(names, np.concatenate([[0], np.cumsum(sizes)[:-1]]).astype(int)))
    width = dict(zip(names, sizes))
    def seg(name, n):
        w = w_in[:, starts[name]:starts[name] + width[name]]
        return jnp.pad(w, ((0, 0), (0, n - width[name])))
    w_row = jnp.concatenate([seg(n, k) for n, k in ROW_SEGS], axis=1).astype(MXU_DTYPE)
    w_col = jnp.concatenate([seg(n, k) for n, k in COL_SEGS], axis=1).T.astype(MXU_DTYPE)
    w_kk = w_kv_up[:, :HA * HD].astype(MXU_DTYPE)
    w_kv = w_kv_up[:, HA * HD:].T.astype(MXU_DTYPE)
    wide = lambda v: jnp.broadcast_to(v[..., None], v.shape + (tm,))
    column = lambda v: v[..., None]

    def freqs(dim):
        half = dim // ROT_FRACTION // 2
        inv_freq = ROPE_THETA ** (-jnp.arange(half, dtype=F32) / half)
        per_dim = jnp.concatenate([inv_freq, inv_freq, jnp.zeros((dim - 2 * half,), F32)])
        return jnp.tile(per_dim, LANES // dim).reshape(1, LANES), jnp.concatenate([inv_freq, inv_freq])
    f64l, f64pair = freqs(HD)
    f32l, f32pair = freqs(IDX_DIM)
    half64 = HD // ROT_FRACTION // 2
    f64c = wide(f64pair[:half64])
    f32c = wide(f32pair)
    pos_f = positions.astype(F32)

    row_blk = lambda n: pl.BlockSpec((tm, n), lambda i: (i, 0))
    const = lambda shape: pl.BlockSpec(shape, lambda i: (0,) * len(shape))
    per_batch = lambda shape: pl.BlockSpec((1,) + shape, lambda i: (i // per_b,) + (0,) * len(shape))
    tok_cols = lambda rows: pl.BlockSpec((1, rows, tm), lambda i: (i // per_b, 0, i % per_b))
    heads_cols = lambda h, rows: pl.BlockSpec((1, h, rows, tm), lambda i: (i // per_b, 0, 0, i % per_b))
    heads_rows = lambda h, n: pl.BlockSpec((1, h, tm, n), lambda i: (i // per_b, 0, i % per_b, 0))
    bf = MXU_DTYPE
    out_shape = [jax.ShapeDtypeStruct((T, 2 * D), F32),
                 jax.ShapeDtypeStruct((B, HA, HD, S), bf),
                 jax.ShapeDtypeStruct((B, HA, S, HD), bf),
                 jax.ShapeDtypeStruct((B, HA, HD + ONES_ROWS, S), bf),
                 jax.ShapeDtypeStruct((B, IDX_HEADS, IDX_DIM, S), bf),
                 jax.ShapeDtypeStruct((B, IDX_HEADS, S), F32),
                 jax.ShapeDtypeStruct((B, S, IDX_DIM), bf),
                 jax.ShapeDtypeStruct((B, NSA_HEADS, HD, S), bf),
                 jax.ShapeDtypeStruct((B, S, 2 * KVB), F32),
                 jax.ShapeDtypeStruct((B, G, S, HD), bf),
                 jax.ShapeDtypeStruct((B, G, S, HD), bf),
                 jax.ShapeDtypeStruct((B, G, HD + ONES_ROWS, S), bf),
                 jax.ShapeDtypeStruct((B, G, HD + ONES_ROWS, S), bf),
                 jax.ShapeDtypeStruct((B, 3 * NSA_HEADS, S), F32)]
    out_specs = [row_blk(2 * D), heads_cols(HA, HD), heads_rows(HA, HD), heads_cols(HA, HD + ONES_ROWS),
                 heads_cols(IDX_HEADS, IDX_DIM), tok_cols(IDX_HEADS),
                 pl.BlockSpec((1, tm, IDX_DIM), lambda i: (i // per_b, i % per_b, 0)),
                 heads_cols(NSA_HEADS, HD),
                 pl.BlockSpec((1, tm, 2 * KVB), lambda i: (i // per_b, i % per_b, 0)),
                 heads_rows(G, HD), heads_rows(G, HD), heads_cols(G, HD + ONES_ROWS),
                 heads_cols(G, HD + ONES_ROWS),
                 tok_cols(3 * NSA_HEADS)]
    in_specs = [row_blk(D), const((1, D)), per_batch((1, D)), per_batch((1, D)),
                const((D, 1)), per_batch((D, 1)), per_batch((D, 1)),
                const((D, ROW_COLS)), const((COL_ROWS, D)),
                const((1, DSA_KV_RANK)), const((DSA_KV_RANK, 1)),
                const((DSA_KV_RANK, HA * HD)), const((HA * HD, DSA_KV_RANK)),
                row_blk(1), tok_cols(1), const((1, LANES)), const((1, LANES)),
                const((half64, tm)), const((SUBLANES, tm))]
    return pl.pallas_call(
        _prep_kernel,
        grid=(T // tm,),
        in_specs=in_specs,
        out_specs=out_specs,
        out_shape=out_shape,
        compiler_params=_cparams(("arbitrary",)),
    )(x.reshape(T, D), g_norm1.reshape(1, D), mod_sc, mod_sh,
      column(g_norm1), column(mod_sc[:, 0]), column(mod_sh[:, 0]), w_row, w_col,
      g_kv.reshape(1, DSA_KV_RANK), column(g_kv), w_kk, w_kv,
      pos_f.reshape(T, 1), pos_f.reshape(B, 1, S), f64l, f32l, f64c, f32c)


def _cmp_kernel(f_ref, w1_ref, w2_ref, o_ref):
    hid = _dot(f_ref[...], w1_ref[...])
    hid = hid * jax.nn.sigmoid(hid)
    o_ref[...] = _dot(hid, w2_ref[...])


def _compress(tok, pe, w1, w2):
    B, S, G, HD = tok.shape
    r = CMP_LEN // CMP_STRIDE
    n_chunks = S // CMP_STRIDE
    n_cmp = n_chunks - r + 1
    chunks = tok.reshape(B, n_chunks, CMP_STRIDE, G, HD)
    blocks = jnp.concatenate([chunks[:, i:n_cmp + i] for i in range(r)], axis=2)
    blocks = blocks + pe[:, None, :]
    flat = blocks.transpose(0, 1, 3, 2, 4).reshape(B * n_cmp * G, CMP_LEN * HD)
    rows = flat.shape[0]
    tm = 512
    rows_p = -(-rows // tm) * tm
    flat = jnp.pad(flat, ((0, rows_p - rows), (0, 0))).astype(MXU_DTYPE)
    out = pl.pallas_call(
        _cmp_kernel,
        grid=(rows_p // tm,),
        in_specs=[pl.BlockSpec((tm, CMP_LEN * HD), lambda i: (i, 0)),
                  pl.BlockSpec((CMP_LEN * HD, CMP_HIDDEN), lambda i: (0, 0)),
                  pl.BlockSpec((CMP_HIDDEN, HD), lambda i: (0, 0))],
        out_specs=pl.BlockSpec((tm, HD), lambda i: (i, 0)),
        out_shape=jax.ShapeDtypeStruct((rows_p, HD), F32),
        compiler_params=_cparams(("arbitrary",)),
    )(flat, w1.astype(MXU_DTYPE), w2.astype(MXU_DTYPE))
    return out[:rows].reshape(B, n_cmp, G, HD)


def _select_kernel(ik_ref, iqt_ref, iwt_ref, bias_ref, sc_ref, *, TQ, KC, S, K, NBIS):
    qi = pl.program_id(1)
    q0 = qi * TQ
    n_ch = (q0 + TQ + KC - 1) // KC
    n_all = S // KC
    t_row = q0 + lax.broadcasted_iota(jnp.int32, (1, TQ), 1)
    key_iota = lax.broadcasted_iota(jnp.int32, (KC, TQ), 0)
    kf = float(K)
    SUB = LANES
    sub_iota = lax.broadcasted_iota(jnp.int32, (SUB, TQ), 0)

    def score_chunk(c, carry):
        mn, mx = carry
        for u in range(KC // SUB):
            off = pl.multiple_of(c * KC + u * SUB, SUB)
            ikc = ik_ref[0, pl.ds(off, SUB), :]
            acc = jnp.zeros((SUB, TQ), F32)
            for h in range(IDX_HEADS):
                lg = jnp.dot(ikc, iqt_ref[0, h], preferred_element_type=F32)
                acc = acc + jnp.maximum(lg, 0.0) * iwt_ref[0, h:h + 1, :]
            causal = off + sub_iota <= t_row
            sc_ref[pl.ds(off, SUB), :] = jnp.where(causal, acc, NEG_INF)
            rows = lambda x: x.reshape(SUB // SUBLANES, SUBLANES, TQ)
            mn = jnp.minimum(mn, jnp.min(rows(jnp.where(causal, acc, -LOWEST)), axis=0))
            mx = jnp.maximum(mx, jnp.max(rows(jnp.where(causal, acc, LOWEST)), axis=0))
        return mn, mx

    mn, mx = lax.fori_loop(0, n_ch, score_chunk, (jnp.full((SUBLANES, TQ), -LOWEST, F32),
                                                  jnp.full((SUBLANES, TQ), LOWEST, F32)))
    lo = jnp.min(mn, axis=0, keepdims=True)
    hi = jnp.max(mx, axis=0, keepdims=True)

    def chunk(c):
        off = pl.multiple_of(c * KC, KC)
        return sc_ref[pl.ds(off, KC), :], off

    AR = 4 * SUBLANES

    def fold(x, op):
        return op(x.reshape(KC // AR, AR, TQ), axis=0)

    def key_pass(fn, init):
        def body(c, acc):
            xs, off = chunk(c)
            return fn(acc, xs, off)
        return lax.fori_loop(0, n_ch, body, init)

    def count_ge(thr):
        acc = key_pass(lambda a, xs, off: a + fold(jnp.where(xs >= thr, 1.0, 0.0), jnp.sum),
                       jnp.zeros((AR, TQ), F32))
        return jnp.sum(acc, axis=0, keepdims=True)

    def count_gt(thr):
        acc = key_pass(lambda a, xs, off: a + fold(jnp.where(xs > thr, 1.0, 0.0), jnp.sum),
                       jnp.zeros((AR, TQ), F32))
        return jnp.sum(acc, axis=0, keepdims=True)

    def max_where(bound):
        fn = lambda a, xs, off: jnp.maximum(a, fold(jnp.where(xs <= bound, xs, LOWEST), jnp.max))
        acc = key_pass(fn, jnp.full((AR, TQ), LOWEST, F32))
        return jnp.max(acc, axis=0, keepdims=True)

    def count_and_next(v):
        def fn(a, xs, off):
            cnt, nxt = a
            return (cnt + fold(jnp.where(xs >= v, 1.0, 0.0), jnp.sum),
                    jnp.maximum(nxt, fold(jnp.where(xs < v, xs, LOWEST), jnp.max)))
        cnt, nxt = key_pass(fn, (jnp.zeros((AR, TQ), F32), jnp.full((AR, TQ), LOWEST, F32)))
        return jnp.sum(cnt, axis=0, keepdims=True), jnp.max(nxt, axis=0, keepdims=True)

    def bisect(_, carry):
        lo, hi = carry
        mid = 0.5 * (lo + hi)
        ge = count_ge(mid) >= kf
        return jnp.where(ge, mid, lo), jnp.where(ge, hi, mid)

    lo, hi = lax.fori_loop(0, NBIS, bisect, (lo, hi))

    all_keys = jnp.where(t_row < K, 1.0, 0.0)
    v = max_where(hi)
    n_ge, nxt = count_and_next(v)
    done = jnp.maximum(all_keys, jnp.where(n_ge >= kf, 1.0, 0.0))

    def peel_cond(st):
        return jnp.sum(st[1]) < float(TQ)

    def peel_body(st):
        v, done, n_ge, nxt = st
        v = jnp.where(done > 0.5, v, nxt)
        cnt, nxt = count_and_next(v)
        n_ge = jnp.where(done > 0.5, n_ge, cnt)
        done = jnp.maximum(done, jnp.where(cnt >= kf, 1.0, 0.0))
        return v, done, n_ge, nxt

    v, done, n_ge, _ = lax.while_loop(peel_cond, peel_body, (v, done, n_ge, nxt))
    thr = jnp.where(all_keys > 0.5, LOWEST, v)
    has_tie = jnp.max(jnp.where((n_ge > kf) & (all_keys < 0.5), 1.0, 0.0)) > 0.5

    def write(off, sel):
        bias_ref[0, pl.ds(off, KC), :] = jnp.where(sel, 0.0, NEG_INF).astype(bias_ref.dtype)

    def fill_tail():
        def body(c, carry):
            write(pl.multiple_of(c * KC, KC), jnp.zeros((KC, TQ), jnp.bool_))
            return carry
        lax.fori_loop(n_ch, n_all, body, 0)

    @pl.when(jnp.logical_not(has_tie))
    def _():
        def body(c, carry):
            xs, off = chunk(c)
            write(off, (xs >= thr) & ((off + key_iota) <= t_row))
            return carry
        lax.fori_loop(0, n_ch, body, 0)
        fill_tail()

    @pl.when(has_tie)
    def _():
        need = kf - count_gt(thr)
        r_i = lax.broadcasted_iota(jnp.int32, (KC, KC), 0)
        c_i = lax.broadcasted_iota(jnp.int32, (KC, KC), 1)
        lower = jnp.where(c_i <= r_i, 1.0, 0.0).astype(MXU_DTYPE)
        def body(c, seen):
            xs, off = chunk(c)
            causal = (off + key_iota) <= t_row
            eq = jnp.where((xs == thr) & causal, 1.0, 0.0)
            rank = jnp.dot(lower, eq.astype(MXU_DTYPE), preferred_element_type=F32) + seen
            write(off, ((xs > thr) & causal) | ((eq > 0.5) & (rank <= need)))
            return seen + jnp.sum(eq, axis=0, keepdims=True)
        lax.fori_loop(0, n_ch, body, jnp.zeros((1, TQ), F32))
        fill_tail()


def _dsa_select(ik, iqt, iwt, K, TQ, KC):
    B, S, DI = ik.shape
    H = iqt.shape[1]
    kern = functools.partial(_select_kernel, TQ=TQ, KC=KC, S=S, K=K, NBIS=SELECT_BISECTIONS)
    return pl.pallas_call(
        kern,
        grid=(B, S // TQ),
        in_specs=[pl.BlockSpec((1, S, DI), lambda b, q: (b, 0, 0)),
                  pl.BlockSpec((1, H, DI, TQ), lambda b, q: (b, 0, 0, q)),
                  pl.BlockSpec((1, H, TQ), lambda b, q: (b, 0, q))],
        out_specs=pl.BlockSpec((1, S, TQ), lambda b, q: (b, 0, q)),
        out_shape=jax.ShapeDtypeStruct((B, S, S), jnp.bfloat16),
        scratch_shapes=[pltpu.VMEM((S, TQ), F32)],
        compiler_params=_cparams(("arbitrary", "arbitrary")),
    )(ik, iqt, iwt)


def _cmp_attn_kernel(qt_ref, kc_ref, vct_ref, o_ref, bm_ref, *, TQ, NCP, NBP, N_SEL):
    qi = pl.program_id(1)
    q0 = qi * TQ
    G, J = NSA_KV_GROUPS, NSA_REP
    N = J * TQ
    t_lane = q0 + (lax.broadcasted_iota(jnp.int32, (NCP, N), 1) & (TQ - 1))
    cmp_end = lax.broadcasted_iota(jnp.int32, (NCP, N), 0) * CMP_STRIDE + (CMP_LEN - 1)
    vis = cmp_end <= t_lane
    n_i = lax.broadcasted_iota(jnp.int32, (NBP, NCP), 0) * SEL_BLOCK
    c_i = lax.broadcasted_iota(jnp.int32, (NBP, NCP), 1) * CMP_STRIDE
    overlap_t = jnp.where((c_i <= n_i + SEL_BLOCK - 1) & (c_i + CMP_LEN - 1 >= n_i), 1.0, 0.0)
    blk = lax.broadcasted_iota(jnp.int32, (NBP, TQ), 0)
    tq = q0 + lax.broadcasted_iota(jnp.int32, (NBP, TQ), 1)
    cur = tq // SEL_BLOCK
    admissible = blk * SEL_BLOCK <= tq
    forced = (blk == 0) | (blk == cur) | (blk == cur - 1)
    for g in range(G):
        qt = jnp.concatenate([qt_ref[0, g * J + j] for j in range(J)], axis=1)
        s = jnp.where(vis, _dot(kc_ref[0, g], qt), NEG_INF)
        p = jnp.exp(s - jnp.max(s, axis=0, keepdims=True))
        p = jnp.where(vis, p, 0.0)
        p = p / jnp.maximum(jnp.sum(p, axis=0, keepdims=True), TINY)
        o = _dot(vct_ref[0, g], p)
        for j in range(J):
            o_ref[0, g * J + j] = o[:, j * TQ:(j + 1) * TQ]
        psum = p[:, 0:TQ]
        for j in range(1, J):
            psum = psum + p[:, j * TQ:(j + 1) * TQ]
        imp = _dot_exact_lhs(overlap_t, psum)
        score = jnp.where(admissible & forced, FORCED, jnp.where(admissible, imp, NEG_INF))

        def pick(_, st):
            score, sel = st
            m = jnp.max(score, axis=0, keepdims=True)
            first = jnp.min(jnp.where(score == m, blk, NBP), axis=0, keepdims=True)
            hit = blk == first
            return jnp.where(hit, LOWEST, score), jnp.where(hit, 1.0, sel)

        _, sel = lax.fori_loop(0, N_SEL, pick, (score, jnp.zeros((NBP, TQ), F32)))
        bm_ref[0, g] = sel.astype(bm_ref.dtype)


def _cmp_attn(qt, kc, vct, n_sel, NBP, TQ):
    B, H, HD, S = qt.shape
    G, NCP = kc.shape[1], kc.shape[2]
    kern = functools.partial(_cmp_attn_kernel, TQ=TQ, NCP=NCP, NBP=NBP, N_SEL=n_sel)
    return pl.pallas_call(
        kern,
        grid=(B, S // TQ),
        in_specs=[pl.BlockSpec((1, H, HD, TQ), lambda b, q: (b, 0, 0, q)),
                  pl.BlockSpec((1, G, NCP, HD), lambda b, q: (b, 0, 0, 0)),
                  pl.BlockSpec((1, G, HD, NCP), lambda b, q: (b, 0, 0, 0))],
        out_specs=[pl.BlockSpec((1, H, HD, TQ), lambda b, q: (b, 0, 0, q)),
                   pl.BlockSpec((1, G, NBP, TQ), lambda b, q: (b, 0, 0, q))],
        out_shape=[jax.ShapeDtypeStruct((B, H, HD, S), F32),
                   jax.ShapeDtypeStruct((B, G, NBP, S), jnp.bfloat16)],
        compiler_params=_cparams(("arbitrary", "arbitrary")),
    )(qt, kc, vct)


def _flash_kernel(*refs, mode, G, J, TQ, TK, NWIN):
    if mode == "window":
        qt_ref, k_ref, vt_ref, o_ref, m_ref, acc_ref = refs
        x_ref = None
    else:
        qt_ref, k_ref, vt_ref, x_ref, o_ref, m_ref, acc_ref = refs
    HD = HEAD_DIM
    qi = pl.program_id(1)
    kk = pl.program_id(2)
    q0 = qi * TQ
    last = (q0 + TQ - 1) // TK
    if mode == "window":
        ki = last - (NWIN - 1) + kk
        valid = ki >= 0
    else:
        ki = kk
        valid = kk <= last

    @pl.when(kk == 0)
    def _():
        m_ref[...] = jnp.full(m_ref.shape, NEG_INF, F32)
        acc_ref[...] = jnp.zeros(acc_ref.shape, F32)

    @pl.when(valid)
    def _():
        k0 = ki * TK
        kidx = k0 + lax.broadcasted_iota(jnp.int32, (TK, TQ), 0)
        tq = q0 + lax.broadcasted_iota(jnp.int32, (TK, TQ), 1)
        if mode == "mask":
            bias = x_ref[0].astype(F32)
        elif mode == "window":
            bias = jnp.where((kidx <= tq) & (kidx > tq - WINDOW), 0.0, NEG_INF)
        else:
            causal = kidx <= tq
            nbp = x_ref.shape[2]
            blk_of_key = (k0 + lax.broadcasted_iota(jnp.int32, (TK, nbp), 0)) // SEL_BLOCK
            expand_t = jnp.where(lax.broadcasted_iota(jnp.int32, (TK, nbp), 1) == blk_of_key, 1.0, 0.0)
            expand_t = expand_t.astype(MXU_DTYPE)

        for g in range(G):
            if mode == "block":
                picked = jnp.dot(expand_t, x_ref[0, g].astype(MXU_DTYPE), preferred_element_type=F32)
                b = jnp.where(causal & (picked > 0.5), 0.0, NEG_INF)
            else:
                b = bias
            if J > 1:
                b = jnp.concatenate([b] * J, axis=1)
                qt = jnp.concatenate([qt_ref[0, g * J + j] for j in range(J)], axis=1)
            else:
                qt = qt_ref[0, g]
            s = _dot(k_ref[0, g], qt) + b
            m_prev = m_ref[g]
            m_new = jnp.maximum(m_prev, jnp.max(s, axis=0, keepdims=True))
            alpha = jnp.exp(m_prev - m_new)
            p = jnp.exp((s - m_new).astype(MXU_DTYPE))
            acc_ref[g] = alpha * acc_ref[g] + _dot(vt_ref[0, g], p)
            m_ref[g] = m_new

    @pl.when(kk == pl.num_programs(2) - 1)
    def _():
        for g in range(G):
            o = acc_ref[g, 0:HD] / jnp.maximum(acc_ref[g, HD:HD + 1], TINY)
            for j in range(J):
                o_ref[0, g * J + j] = o[:, j * TQ:(j + 1) * TQ].astype(o_ref.dtype)


def _flash(qt, k, vt, extra, mode, TQ, TK, out_dtype):
    B, H, HD, S = qt.shape
    G = k.shape[1]
    J = H // G
    N = J * TQ
    nq = S // TQ
    last_of = lambda qi: (qi * TQ + TQ - 1) // TK
    if mode == "window":
        NWIN = min((WINDOW - 1 + TK - 1) // TK + 1, S // TK)
        nk = NWIN
        kidx_of = lambda qi, kk: jnp.maximum(last_of(qi) - (NWIN - 1) + kk, 0)
    else:
        NWIN = 0
        nk = S // TK
        kidx_of = lambda qi, kk: jnp.minimum(kk, last_of(qi))
    in_specs = [pl.BlockSpec((1, H, HD, TQ), lambda b, qi, kk: (b, 0, 0, qi)),
                pl.BlockSpec((1, G, TK, HD), lambda b, qi, kk: (b, 0, kidx_of(qi, kk), 0)),
                pl.BlockSpec((1, G, HD + ONES_ROWS, TK), lambda b, qi, kk: (b, 0, 0, kidx_of(qi, kk)))]
    args = [qt, k, vt]
    if mode == "mask":
        in_specs.append(pl.BlockSpec((1, TK, TQ), lambda b, qi, kk: (b, kidx_of(qi, kk), qi)))
        args.append(extra)
    elif mode == "block":
        nbp = extra.shape[2]
        in_specs.append(pl.BlockSpec((1, G, nbp, TQ), lambda b, qi, kk: (b, 0, 0, qi)))
        args.append(extra)
    kern = functools.partial(_flash_kernel, mode=mode, G=G, J=J, TQ=TQ, TK=TK, NWIN=NWIN)
    return pl.pallas_call(
        kern,
        grid=(B, nq, nk),
        in_specs=in_specs,
        out_specs=pl.BlockSpec((1, H, HD, TQ), lambda b, qi, kk: (b, 0, 0, qi)),
        out_shape=jax.ShapeDtypeStruct((B, H, HD, S), out_dtype),
        scratch_shapes=[pltpu.VMEM((G, 1, N), F32),
                        pltpu.VMEM((G, HD + ONES_ROWS, N), F32)],
        compiler_params=_cparams(("arbitrary", "arbitrary", "arbitrary")),
    )(*args)


def _dot_tn(a_t, b):
    return lax.dot_general(a_t.astype(MXU_DTYPE), b.astype(MXU_DTYPE),
                           (((0,), (0,)), ((), ())), preferred_element_type=F32)


def _route(h, wr, br):
    tm = h.shape[0]
    lane = lax.broadcasted_iota(jnp.int32, (tm, LANES), 1)
    h_hi = h.astype(MXU_DTYPE)
    h_lo = (h - h_hi.astype(F32)).astype(MXU_DTYPE)
    w_hi = wr.astype(MXU_DTYPE)
    w_lo = (wr - w_hi.astype(F32)).astype(MXU_DTYPE)
    d = lambda a, b: jnp.dot(a, b, preferred_element_type=F32)
    logits = d(h_hi, w_hi) + d(h_hi, w_lo) + d(h_lo, w_hi) + br
    is_e = lane < N_EXPERTS
    is_g = (lane >= N_EXPERTS) & (lane < N_EXPERTS + N_GROUPS)
    lg = jnp.where(is_g, logits, LOWEST)
    mg = jnp.max(lg, axis=-1, keepdims=True)
    gsel = jnp.min(jnp.where(is_g & (lg == mg), lane - N_EXPERTS, N_GROUPS), axis=-1, keepdims=True)
    pg_sel = 1.0 / jnp.sum(jnp.where(is_g, jnp.exp(lg - mg), 0.0), axis=-1, keepdims=True)
    in_grp = is_e & ((lane // EXPERTS_PER_GROUP) == gsel)
    le = jnp.where(in_grp, logits, LOWEST)
    me = jnp.max(le, axis=-1, keepdims=True)
    ex = jnp.where(in_grp, jnp.exp(le - me), 0.0)
    pe = ex / jnp.sum(ex, axis=-1, keepdims=True)
    pe = jnp.where(in_grp, pe, -1.0)
    p1 = jnp.max(pe, axis=-1, keepdims=True)
    i1 = jnp.min(jnp.where(pe == p1, lane, LANES), axis=-1, keepdims=True)
    pe2 = jnp.where(lane == i1, -1.0, pe)
    p2 = jnp.max(pe2, axis=-1, keepdims=True)
    i2 = jnp.min(jnp.where(pe2 == p2, lane, LANES), axis=-1, keepdims=True)
    tot = p1 + p2
    comb = jnp.where(lane == i1, p1 / tot * pg_sel, 0.0) + jnp.where(lane == i2, p2 / tot * pg_sel, 0.0)
    return jnp.where(lane == N_EXPERTS, gsel.astype(F32), comb)


def _outproj_kernel(x_ref, oa_ref, oc_ref, os_ref, ow_ref, br_ref, mg_ref, wa_ref, wb_ref, wo_ref,
                    gt_ref, g2_ref, sc_ref, sh_ref, wr_ref, brt_ref, o_ref, hr_ref):
    HD = HEAD_DIM
    D = x_ref.shape[1]
    br = br_ref[0]
    parts = []
    for hh in range(NSA_HEADS):
        parts.append(br[3 * hh:3 * hh + 1] * oc_ref[0, hh] + br[3 * hh + 1:3 * hh + 2] * os_ref[0, hh]
                     + br[3 * hh + 2:3 * hh + 3] * ow_ref[0, hh])
    ob_t = jnp.concatenate(parts, axis=0)
    oa_t = jnp.concatenate([oa_ref[0, hh] for hh in range(DSA_HEADS)], axis=0)
    ua = _dot_tn(oa_t, wa_ref[...])
    ub = _dot_tn(ob_t, wb_ref[...])
    mg = mg_ref[...]
    merged = jax.nn.sigmoid(mg[:, :D]) * ua + jax.nn.sigmoid(mg[:, D:]) * ub
    x1 = x_ref[...] + gt_ref[0] * _dot(merged, wo_ref[...])
    o_ref[...] = x1
    h = _rms(x1, g2_ref[...]) * (1.0 + sc_ref[0]) + sh_ref[0]
    hr_ref[:, :D] = h
    hr_ref[:, D:] = _route(h, wr_ref[...], brt_ref[...])


def _out_proj(x2d, oa, oc, os_, ow, br, mg, wa, wb, wo, gt, g2, sc2, sh2, wr, brt, S, tm):
    T, D = x2d.shape
    B, H, HD, _ = oa.shape
    per_b = S // tm
    heads_cols = pl.BlockSpec((1, H, HD, tm), lambda i: (i // per_b, 0, 0, i % per_b))
    per_batch = pl.BlockSpec((1, 1, D), lambda i: (i // per_b, 0, 0))
    rows = lambda n: pl.BlockSpec((tm, n), lambda i: (i, 0))
    const = lambda a, b: pl.BlockSpec((a, b), lambda i: (0, 0))
    return pl.pallas_call(
        _outproj_kernel,
        grid=(T // tm,),
        in_specs=[rows(D), heads_cols, heads_cols, heads_cols, heads_cols,
                  pl.BlockSpec((1, br.shape[1], tm), lambda i: (i // per_b, 0, i % per_b)),
                  rows(2 * D), const(H * HD, D), const(H * HD, D), const(D, D), per_batch,
                  const(1, D), per_batch, per_batch, const(D, LANES), const(1, LANES)],
        out_specs=[rows(D), rows(D + LANES)],
        out_shape=[jax.ShapeDtypeStruct((T, D), F32), jax.ShapeDtypeStruct((T, D + LANES), F32)],
        compiler_params=_cparams(("arbitrary",)),
    )(x2d, oa, oc, os_, ow, br, mg, wa, wb, wo, gt, g2.reshape(1, D), sc2, sh2, wr, brt)


def _row_copy(src_hbm, t, dst, r, sem):
    return pltpu.make_async_copy(src_hbm.at[pl.ds(t, 1)], dst.at[pl.ds(r, 1)], sem)


def _start_rows(idx_ref, base, r0, n, src_hbm, dst, sem):
    for r in range(n):
        _row_copy(src_hbm, idx_ref[base + r0 + r], dst, r0 + r, sem).start()


def _wait_rows(src_hbm, dst, n, sem):
    pltpu.make_async_copy(src_hbm.at[pl.ds(0, n)], dst.at[pl.ds(0, n)], sem).wait()


def _experts_kernel(tg_ref, src_ref, h_hbm, wg_ref, wu_ref, wd_ref, o_ref,
                    hbuf, acc_ref, sems, *, TM):
    i = pl.program_id(0)
    e = pl.program_id(1)
    n_tiles = pl.num_programs(0)
    EPG = EXPERTS_PER_GROUP
    part = TM // EPG
    slot = i % 2

    @pl.when((i == 0) & (e == 0))
    def _():
        def start(c, carry):
            _start_rows(src_ref, 0, pl.multiple_of(c * part, part), part, h_hbm, hbuf.at[0], sems.at[0])
            return carry
        lax.fori_loop(0, EPG, start, 0)

    @pl.when(e == 0)
    def _():
        _wait_rows(h_hbm, hbuf.at[slot], TM, sems.at[slot])
        acc_ref[...] = jnp.zeros(acc_ref.shape, F32)

    @pl.when(i + 1 < n_tiles)
    def _():
        _start_rows(src_ref, (i + 1) * TM, e * part, part, h_hbm, hbuf.at[1 - slot], sems.at[1 - slot])

    D = acc_ref.shape[1]
    h = hbuf[slot, :, :D].astype(MXU_DTYPE)
    a = jnp.dot(h, wg_ref[0], preferred_element_type=F32)
    u = jnp.dot(h, wu_ref[0], preferred_element_type=F32)
    y = _dot(a * jax.nn.sigmoid(a) * u, wd_ref[0])
    lane = lax.broadcasted_iota(jnp.int32, (TM, LANES), 1)
    expert = tg_ref[i] * EPG + e
    w_e = jnp.sum(jnp.where(lane == expert, hbuf[slot, :, D:], 0.0), axis=-1, keepdims=True)
    acc_ref[...] += w_e * y

    @pl.when(e == pl.num_programs(1) - 1)
    def _():
        o_ref[...] = acc_ref[...]


def _experts(hr, tile_group, src, wg, wu, wd, TM):
    D = wg.shape[1]
    DE = wg.shape[2]
    P = src.shape[0]
    EPG = EXPERTS_PER_GROUP
    w_idx = lambda i, e, tg, src: (tg[i] * EPG + e, 0, 0)
    kern = functools.partial(_experts_kernel, TM=TM)
    grid_spec = pltpu.PrefetchScalarGridSpec(
        num_scalar_prefetch=2,
        grid=(P // TM, EPG),
        in_specs=[pl.BlockSpec(memory_space=pl.ANY),
                  pl.BlockSpec((1, D, DE), w_idx),
                  pl.BlockSpec((1, D, DE), w_idx),
                  pl.BlockSpec((1, DE, D), w_idx)],
        out_specs=pl.BlockSpec((TM, D), lambda i, e, tg, src: (i, 0)),
        scratch_shapes=[pltpu.VMEM((2, TM, D + LANES), F32),
                        pltpu.VMEM((TM, D), F32),
                        pltpu.SemaphoreType.DMA((2,))])
    return pl.pallas_call(
        kern,
        grid_spec=grid_spec,
        out_shape=jax.ShapeDtypeStruct((P, D), F32),
        compiler_params=_cparams(("arbitrary", "arbitrary")),
    )(tile_group, src, hr, wg, wu, wd)


def _combine_kernel(pos_ref, x_ref, y_hbm, gt_ref, gf_ref, o_ref, ybuf, sem, *, TM):
    part = 64
    def start(c, carry):
        _start_rows(pos_ref, pl.program_id(0) * TM, pl.multiple_of(c * part, part), part, y_hbm, ybuf,
                    sem.at[0])
        return carry
    lax.fori_loop(0, TM // part, start, 0)
    _wait_rows(y_hbm, ybuf, TM, sem.at[0])
    o_ref[...] = _rms(x_ref[...] + gt_ref[0] * ybuf[...], gf_ref[...])


def _combine(x1, ys, pos, gt, gf, S, tm):
    T, D = x1.shape
    per_b = S // tm
    grid_spec = pltpu.PrefetchScalarGridSpec(
        num_scalar_prefetch=1,
        grid=(T // tm,),
        in_specs=[pl.BlockSpec((tm, D), lambda i, pos: (i, 0)),
                  pl.BlockSpec(memory_space=pl.ANY),
                  pl.BlockSpec((1, 1, D), lambda i, pos: (i // per_b, 0, 0)),
                  pl.BlockSpec((1, D), lambda i, pos: (0, 0))],
        out_specs=pl.BlockSpec((tm, D), lambda i, pos: (i, 0)),
        scratch_shapes=[pltpu.VMEM((tm, D), F32), pltpu.SemaphoreType.DMA((1,))])
    return pl.pallas_call(
        functools.partial(_combine_kernel, TM=tm),
        grid_spec=grid_spec,
        out_shape=jax.ShapeDtypeStruct((T, D), F32),
        compiler_params=_cparams(("arbitrary",)),
    )(pos, x1, ys, gt, gf.reshape(1, D))


def _routed_moe(x1, hr, gt, gf, wg, wu, wd, S, TM, combine_rows):
    T, D = x1.shape
    gsel = hr[:, D + N_EXPERTS].astype(jnp.int32)
    onehot = (gsel[:, None] == jnp.arange(N_GROUPS, dtype=jnp.int32)[None, :]).astype(jnp.int32)
    csum = jnp.cumsum(onehot, axis=0)
    rank = jnp.take_along_axis(csum, gsel[:, None], axis=1)[:, 0] - 1
    padded = -(-csum[-1] // TM) * TM
    ends = jnp.cumsum(padded)
    starts = ends - padded
    pos = (jnp.take(starts, gsel) + rank).astype(jnp.int32)
    P = T + N_GROUPS * TM
    src = jnp.zeros((P,), jnp.int32).at[pos].set(jnp.arange(T, dtype=jnp.int32))
    tile_start = jnp.arange(P // TM, dtype=jnp.int32) * TM
    tile_group = jnp.minimum(jnp.sum(tile_start[:, None] >= ends[None, :], axis=1), N_GROUPS - 1)
    ys = _experts(hr, tile_group.astype(jnp.int32), src, wg, wu, wd, TM)
    return _combine(x1, ys, pos, gt, gf, S, combine_rows)


def _layer(x, mod, positions, g_norm1, w_in, g_kv_latent, w_kv_up, pe_cmp_k, pe_cmp_v,
           w_cmp1_k, w_cmp2_k, w_cmp1_v, w_cmp2_v, w_up_a, w_up_b, w_out, g_norm2,
           w_router_group, b_router_group, w_router_expert, b_router_expert,
           w_expert_gate, w_expert_up, w_expert_down, g_out):
    B, S, D = x.shape
    T = B * S
    HD, G = HEAD_DIM, NSA_KV_GROUPS
    kvb = G * HD
    topk_a = min(DSA_TOPK_MAX, S // 4)
    n_sel = min(SEL_COUNT, S // SEL_BLOCK)
    mod6 = mod.reshape(B, 6, 1, D)
    sh1, sc1, gt1, sh2, sc2, gt2 = (mod6[:, i] for i in range(6))

    x2d = x.reshape(T, D)
    t = _tiles(S)
    (mg, qt_a, k_a, vt_a, iqt, iwt, ik, qt_b, kcvc, ks, kw, vst, vwt, br) = _prep(
        x, sc1, sh1, g_norm1, w_in, g_kv_latent, w_kv_up, positions, t.rows)

    kc = _compress(kcvc[..., :kvb].reshape(B, S, G, HD), pe_cmp_k, w_cmp1_k, w_cmp2_k)
    vc = _compress(kcvc[..., kvb:].reshape(B, S, G, HD), pe_cmp_v, w_cmp1_v, w_cmp2_v)
    n_cmp = kc.shape[1]
    ncp = -(-(n_cmp + 1) // LANES) * LANES
    pad_c = lambda t: jnp.pad(t, ((0, 0), (0, ncp - n_cmp), (0, 0), (0, 0))).astype(MXU_DTYPE)
    kc, vct = pad_c(kc).transpose(0, 2, 1, 3), pad_c(vc).transpose(0, 2, 3, 1)

    sel_bias = _dsa_select(ik, iqt, iwt, topk_a, t.select_q, t.select_k)
    o_a = _flash(qt_a, k_a, vt_a, sel_bias, "mask", t.mask_q, t.mask_k, MXU_DTYPE)

    nbp = -(-(S // SEL_BLOCK) // LANES) * LANES
    o_c, blk_mask = _cmp_attn(qt_b, kc, vct, n_sel, nbp, t.nsa_q)
    o_s = _flash(qt_b, ks, vst, blk_mask, "block", t.nsa_q, t.block_k, F32)
    o_w = _flash(qt_b, kw, vwt, None, "window", t.nsa_q, t.nsa_q, F32)

    wr = jnp.concatenate([w_router_expert, w_router_group], axis=1)
    wr = jnp.pad(wr, ((0, 0), (0, LANES - wr.shape[1])))
    brt = jnp.concatenate([b_router_expert, b_router_group])
    brt = jnp.pad(brt, (0, LANES - brt.shape[0])).reshape(1, LANES)
    x1, hr = _out_proj(x2d, o_a, o_c, o_s, o_w, br, mg, w_up_a.astype(MXU_DTYPE),
                              w_up_b.astype(MXU_DTYPE), w_out.astype(MXU_DTYPE), gt1,
                              g_norm2, sc2, sh2, wr, brt, S, t.rows)

    out = _routed_moe(x1, hr, gt2, g_out, w_expert_gate.astype(MXU_DTYPE),
                      w_expert_up.astype(MXU_DTYPE), w_expert_down.astype(MXU_DTYPE), S,
                      t.moe_rows, t.rows)
    return out.reshape(B, S, D)


def kernel(x, c, positions, w_ada, b_ada, g_norm1, w_in, g_kv_latent, w_kv_up, pe_cmp_k, pe_cmp_v,
           w_cmp1_k, w_cmp2_k, w_cmp1_v, w_cmp2_v, w_up_a, w_up_b, w_out, g_norm2, w_router_group,
           b_router_group, w_router_expert, b_router_expert, w_expert_gate, w_expert_up,
           w_expert_down, g_final):
    depth = w_ada.shape[0]
    assert depth == 1, "the fused final norm assumes a single layer"
    mod = _ada_mod(c, w_ada[0], b_ada[0])
    return _layer(x, mod, positions, g_norm1[0], w_in[0], g_kv_latent[0], w_kv_up[0], pe_cmp_k[0],
                  pe_cmp_v[0], w_cmp1_k[0], w_cmp2_k[0], w_cmp1_v[0], w_cmp2_v[0], w_up_a[0],
                  w_up_b[0], w_out[0], g_norm2[0], w_router_group[0], b_router_group[0],
                  w_router_expert[0], b_router_expert[0], w_expert_gate[0], w_expert_up[0],
                  w_expert_down[0], g_final)
```

```python
import functools
from typing import NamedTuple

import numpy as np
import jax
import jax.numpy as jnp
from jax import lax
from jax.experimental import pallas as pl
from jax.experimental.pallas import tpu as pltpu

HEAD_DIM = 64
ROT_FRACTION = 4
ROPE_THETA = 500000.0
DSA_HEADS = 8
DSA_KV_RANK = 128
IDX_HEADS = 8
IDX_DIM = 32
DSA_TOPK_MAX = 256
NSA_HEADS = 8
NSA_KV_GROUPS = 2
NSA_REP = NSA_HEADS // NSA_KV_GROUPS
CMP_LEN = 32
CMP_STRIDE = 16
CMP_HIDDEN = 256
SEL_BLOCK = 64
SEL_COUNT = 16
WINDOW = 512
N_GROUPS = 4
EXPERTS_PER_GROUP = 8
N_EXPERTS = N_GROUPS * EXPERTS_PER_GROUP
D_EXPERT = 256
EXPERTS_PER_STEP = 8
NORM_EPS = 1e-6
NEG_INF = -1e30
TINY = 1e-30
LOWEST = -3.0e38
FORCED = 1e30

LANES = 128
SUBLANES = 8
ONES_ROWS = 16
MXU_DTYPE = jnp.bfloat16
VMEM_LIMIT = 56 * 1024 * 1024

F32 = jnp.float32
SELECT_BISECTIONS = 16


class _Tiles(NamedTuple):
    rows: int
    select_q: int
    select_k: int
    mask_q: int
    mask_k: int
    nsa_q: int
    block_k: int
    moe_rows: int


def _tiles(S):
    cap = lambda n: min(n, S)
    return _Tiles(rows=cap(256), select_q=cap(256), select_k=cap(256), mask_q=cap(1024),
                  mask_k=cap(1024), nsa_q=cap(512), block_k=cap(1024), moe_rows=512)


def _cparams(sem):
    return pltpu.CompilerParams(dimension_semantics=sem, vmem_limit_bytes=VMEM_LIMIT)


def _dot(a, b):
    return jnp.dot(a.astype(MXU_DTYPE), b.astype(MXU_DTYPE), preferred_element_type=F32)


def _dot_exact_lhs(a01, b):
    hi = b.astype(MXU_DTYPE)
    r1 = b - hi.astype(F32)
    mid = r1.astype(MXU_DTYPE)
    lo = (r1 - mid.astype(F32)).astype(MXU_DTYPE)
    a = a01.astype(MXU_DTYPE)
    d = lambda u: jnp.dot(a, u, preferred_element_type=F32)
    return d(hi) + d(mid) + d(lo)


def _rms(x, g):
    return x * lax.rsqrt(jnp.mean(x * x, axis=-1, keepdims=True) + NORM_EPS) * g


def _ada_kernel(c_ref, w_ref, b_ref, o_ref):
    c = c_ref[...]
    cond = c * jax.nn.sigmoid(c)
    o_ref[...] = _dot(cond, w_ref[...]) + b_ref[...]


def _ada_mod(c, w_ada, b_ada):
    B, D = c.shape
    n_out = w_ada.shape[1]
    rows = SUBLANES
    cp = jnp.zeros((rows, D), F32).at[:B].set(c)
    tn = 1024
    out = pl.pallas_call(
        _ada_kernel,
        grid=(n_out // tn,),
        in_specs=[pl.BlockSpec((rows, D), lambda j: (0, 0)),
                  pl.BlockSpec((D, tn), lambda j: (0, j)),
                  pl.BlockSpec((1, tn), lambda j: (0, j))],
        out_specs=pl.BlockSpec((rows, tn), lambda j: (0, j)),
        out_shape=jax.ShapeDtypeStruct((rows, n_out), F32),
        compiler_params=_cparams(("arbitrary",)),
    )(cp, w_ada, b_ada.reshape(1, n_out))
    return out[:B]


KVB = NSA_KV_GROUPS * HEAD_DIM
ROW_SEGS = (("mg", 2048), ("lat", DSA_KV_RANK), ("kc", KVB), ("vc", KVB), ("ks", KVB), ("kw", KVB),
            ("ik", LANES))
COL_SEGS = (("qa", DSA_HEADS * HEAD_DIM), ("qb", NSA_HEADS * HEAD_DIM), ("iq", IDX_HEADS * IDX_DIM),
            ("vs", KVB), ("vw", KVB), ("lat", DSA_KV_RANK), ("iw", IDX_HEADS), ("br", 3 * NSA_HEADS))


def _seg_offsets(segs):
    out, pos = {}, 0
    for name, n in segs:
        out[name] = pos
        pos += n
    return out, pos


ROW_OFF, ROW_COLS = _seg_offsets(ROW_SEGS)
COL_OFF, COL_ROWS = _seg_offsets(COL_SEGS)


def _rot_lanes(x, c, sa, sb, half):
    outs = []
    for j in range(x.shape[1] // LANES):
        xs = x[:, j * LANES:(j + 1) * LANES]
        outs.append(xs * c + pltpu.roll(xs, half, 1) * sa + pltpu.roll(xs, LANES - half, 1) * sb)
    return outs[0] if len(outs) == 1 else jnp.concatenate(outs, axis=1)


def _prep_kernel(x_ref, g_ref, sc_ref, sh_ref, gc_ref, scc_ref, shc_ref, wr_ref, wc_ref,
                 gkv_ref, gkvc_ref, wkk_ref, wkv_ref, posc_ref, posr_ref, f64l_ref, f32l_ref,
                 f64c_ref, f32c_ref,
                 mg_ref, qa_ref, ka_ref, va_ref, iq_ref, iw_ref, ik_ref, qb_ref, kcvc_ref,
                 ks_ref, kw_ref, vs_ref, vw_ref, br_ref):
    HD = HEAD_DIM
    half = HD // ROT_FRACTION // 2
    scale = HD ** -0.5
    h = _rms(x_ref[...], g_ref[...]) * (1.0 + sc_ref[0]) + sh_ref[0]
    pr = _dot(h, wr_ref[...])
    row = lambda name, n: pr[:, ROW_OFF[name]:ROW_OFF[name] + n]
    mg_ref[...] = row("mg", 2048)
    lane = lax.broadcasted_iota(jnp.int32, (x_ref.shape[0], LANES), 1)

    def lane_tables(freq_row, dim, hf):
        ang = posc_ref[...] * freq_row
        c, s = jnp.cos(ang), jnp.sin(ang)
        upper = (lane & (dim - 1)) >= hf
        return c, jnp.where(upper, s, 0.0), jnp.where(upper, 0.0, -s)

    c64, sa64, sb64 = lane_tables(f64l_ref[...], HD, half)
    c32, sa32, sb32 = lane_tables(f32l_ref[...], IDX_DIM, IDX_DIM // ROT_FRACTION // 2)
    ka = _rot_lanes(_dot(_rms(row("lat", DSA_KV_RANK), gkv_ref[...]), wkk_ref[...]), c64, sa64, sb64, half)
    for hh in range(DSA_HEADS):
        ka_ref[0, hh] = ka[:, hh * HD:(hh + 1) * HD].astype(ka_ref.dtype)
    kcvc_ref[0] = jnp.concatenate([_rot_lanes(row("kc", KVB), c64, sa64, sb64, half), row("vc", KVB)], axis=1)
    ks = _rot_lanes(row("ks", KVB), c64, sa64, sb64, half)
    kw = _rot_lanes(row("kw", KVB), c64, sa64, sb64, half)
    for g in range(NSA_KV_GROUPS):
        ks_ref[0, g] = ks[:, g * HD:(g + 1) * HD].astype(ks_ref.dtype)
        kw_ref[0, g] = kw[:, g * HD:(g + 1) * HD].astype(kw_ref.dtype)
    ik = _rot_lanes(row("ik", LANES), c32, sa32, sb32, IDX_DIM // ROT_FRACTION // 2)
    ik_ref[0] = ik[:, :IDX_DIM].astype(ik_ref.dtype)
    xt = x_ref[...].T
    ht = xt * lax.rsqrt(jnp.mean(xt * xt, axis=0, keepdims=True) + NORM_EPS) * gc_ref[...]
    ht = ht * (1.0 + scc_ref[0]) + shc_ref[0]
    pc = _dot(wc_ref[...], ht)
    col = lambda name, n: pc[COL_OFF[name]:COL_OFF[name] + n]
    ang_t = f64c_ref[...] * posr_ref[0]
    cos, sin = jnp.cos(ang_t), jnp.sin(ang_t)

    def rot_rows(blk):
        x1, x2 = blk[0:half], blk[half:2 * half]
        return jnp.concatenate([x1 * cos - x2 * sin, x2 * cos + x1 * sin, blk[2 * half:]], axis=0)

    qa, qb = col("qa", DSA_HEADS * HD), col("qb", NSA_HEADS * HD)
    for hh in range(DSA_HEADS):
        qa_ref[0, hh] = (rot_rows(qa[hh * HD:(hh + 1) * HD]) * scale).astype(qa_ref.dtype)
    for hh in range(NSA_HEADS):
        qb_ref[0, hh] = (rot_rows(qb[hh * HD:(hh + 1) * HD]) * scale).astype(qb_ref.dtype)
    iq = col("iq", IDX_HEADS * IDX_DIM)
    ang32 = f32c_ref[...] * posr_ref[0]
    first = lax.broadcasted_iota(jnp.int32, ang32.shape, 0) < SUBLANES // 2
    c32t, s32t = jnp.cos(ang32), jnp.where(first, -jnp.sin(ang32), jnp.sin(ang32))
    for hh in range(IDX_HEADS):
        blk = iq[hh * IDX_DIM:(hh + 1) * IDX_DIM]
        top = blk[0:SUBLANES]
        top = top * c32t + pltpu.roll(top, SUBLANES // 2, 0) * s32t
        iq_ref[0, hh] = jnp.concatenate([top, blk[SUBLANES:]], axis=0).astype(iq_ref.dtype)
    ones = jnp.ones((ONES_ROWS, xt.shape[1]), F32)
    with_ones = lambda v: jnp.concatenate([v, ones], axis=0)
    vs, vw = col("vs", KVB), col("vw", KVB)
    for g in range(NSA_KV_GROUPS):
        vs_ref[0, g] = with_ones(vs[g * HD:(g + 1) * HD]).astype(vs_ref.dtype)
        vw_ref[0, g] = with_ones(vw[g * HD:(g + 1) * HD]).astype(vw_ref.dtype)
    lat = col("lat", DSA_KV_RANK)
    lat = lat * lax.rsqrt(jnp.mean(lat * lat, axis=0, keepdims=True) + NORM_EPS) * gkvc_ref[...]
    va = _dot(wkv_ref[...], lat)
    for hh in range(DSA_HEADS):
        va_ref[0, hh] = with_ones(va[hh * HD:(hh + 1) * HD]).astype(va_ref.dtype)
    iw_ref[0] = col("iw", IDX_HEADS)
    br_ref[0] = jax.nn.sigmoid(col("br", 3 * NSA_HEADS))


def _prep(x, mod_sc, mod_sh, g_norm1, w_in, g_kv, w_kv_up, positions, tm):
    B, S, D = x.shape
    T = B * S
    HD, HA, G = HEAD_DIM, DSA_HEADS, NSA_KV_GROUPS
    per_b = S // tm
    sizes = [HA * HD, DSA_KV_RANK, IDX_HEADS * IDX_DIM, IDX_HEADS, IDX_DIM, NSA_HEADS * HD,
             KVB, KVB, KVB, KVB, KVB, KVB, 3 * NSA_HEADS, 2 * D]
    names = ["qa", "lat", "iq", "iw", "ik", "qb", "kc", "vc", "ks", "vs", "kw", "vw", "br", "mg"]
    starts = dict(zip(names, np.concatenate([[0], np.cumsum(sizes)[:-1]]).astype(int)))
    width = dict(zip(names, sizes))
    def seg(name, n):
        w = w_in[:, starts[name]:starts[name] + width[name]]
        return jnp.pad(w, ((0, 0), (0, n - width[name])))
    w_row = jnp.concatenate([seg(n, k) for n, k in ROW_SEGS], axis=1).astype(MXU_DTYPE)
    w_col = jnp.concatenate([seg(n, k) for n, k in COL_SEGS], axis=1).T.astype(MXU_DTYPE)
    w_kk = w_kv_up[:, :HA * HD].astype(MXU_DTYPE)
    w_kv = w_kv_up[:, HA * HD:].T.astype(MXU_DTYPE)
    wide = lambda v: jnp.broadcast_to(v[..., None], v.shape + (tm,))
    column = lambda v: v[..., None]

    def freqs(dim):
        half = dim // ROT_FRACTION // 2
        inv_freq = ROPE_THETA ** (-jnp.arange(half, dtype=F32) / half)
        per_dim = jnp.concatenate([inv_freq, inv_freq, jnp.zeros((dim - 2 * half,), F32)])
        return jnp.tile(per_dim, LANES // dim).reshape(1, LANES), jnp.concatenate([inv_freq, inv_freq])
    f64l, f64pair = freqs(HD)
    f32l, f32pair = freqs(IDX_DIM)
    half64 = HD // ROT_FRACTION // 2
    f64c = wide(f64pair[:half64])
    f32c = wide(f32pair)
    pos_f = positions.astype(F32)

    row_blk = lambda n: pl.BlockSpec((tm, n), lambda i: (i, 0))
    const = lambda shape: pl.BlockSpec(shape, lambda i: (0,) * len(shape))
    per_batch = lambda shape: pl.BlockSpec((1,) + shape, lambda i: (i // per_b,) + (0,) * len(shape))
    tok_cols = lambda rows: pl.BlockSpec((1, rows, tm), lambda i: (i // per_b, 0, i % per_b))
    heads_cols = lambda h, rows: pl.BlockSpec((1, h, rows, tm), lambda i: (i // per_b, 0, 0, i % per_b))
    heads_rows = lambda h, n: pl.BlockSpec((1, h, tm, n), lambda i: (i // per_b, 0, i % per_b, 0))
    bf = MXU_DTYPE
    out_shape = [jax.ShapeDtypeStruct((T, 2 * D), F32),
                 jax.ShapeDtypeStruct((B, HA, HD, S), bf),
                 jax.ShapeDtypeStruct((B, HA, S, HD), bf),
                 jax.ShapeDtypeStruct((B, HA, HD + ONES_ROWS, S), bf),
                 jax.ShapeDtypeStruct((B, IDX_HEADS, IDX_DIM, S), bf),
                 jax.ShapeDtypeStruct((B, IDX_HEADS, S), F32),
                 jax.ShapeDtypeStruct((B, S, IDX_DIM), bf),
                 jax.ShapeDtypeStruct((B, NSA_HEADS, HD, S), bf),
                 jax.ShapeDtypeStruct((B, S, 2 * KVB), F32),
                 jax.ShapeDtypeStruct((B, G, S, HD), bf),
                 jax.ShapeDtypeStruct((B, G, S, HD), bf),
                 jax.ShapeDtypeStruct((B, G, HD + ONES_ROWS, S), bf),
                 jax.ShapeDtypeStruct((B, G, HD + ONES_ROWS, S), bf),
                 jax.ShapeDtypeStruct((B, 3 * NSA_HEADS, S), F32)]
    out_specs = [row_blk(2 * D), heads_cols(HA, HD), heads_rows(HA, HD), heads_cols(HA, HD + ONES_ROWS),
                 heads_cols(IDX_HEADS, IDX_DIM), tok_cols(IDX_HEADS),
                 pl.BlockSpec((1, tm, IDX_DIM), lambda i: (i // per_b, i % per_b, 0)),
                 heads_cols(NSA_HEADS, HD),
                 pl.BlockSpec((1, tm, 2 * KVB), lambda i: (i // per_b, i % per_b, 0)),
                 heads_rows(G, HD), heads_rows(G, HD), heads_cols(G, HD + ONES_ROWS),
                 heads_cols(G, HD + ONES_ROWS),
                 tok_cols(3 * NSA_HEADS)]
    in_specs = [row_blk(D), const((1, D)), per_batch((1, D)), per_batch((1, D)),
                const((D, 1)), per_batch((D, 1)), per_batch((D, 1)),
                const((D, ROW_COLS)), const((COL_ROWS, D)),
                const((1, DSA_KV_RANK)), const((DSA_KV_RANK, 1)),
                const((DSA_KV_RANK, HA * HD)), const((HA * HD, DSA_KV_RANK)),
                row_blk(1), tok_cols(1), const((1, LANES)), const((1, LANES)),
                const((half64, tm)), const((SUBLANES, tm))]
    return pl.pallas_call(
        _prep_kernel,
        grid=(T // tm,),
        in_specs=in_specs,
        out_specs=out_specs,
        out_shape=out_shape,
        compiler_params=_cparams(("arbitrary",)),
    )(x.reshape(T, D), g_norm1.reshape(1, D), mod_sc, mod_sh,
      column(g_norm1), column(mod_sc[:, 0]), column(mod_sh[:, 0]), w_row, w_col,
      g_kv.reshape(1, DSA_KV_RANK), column(g_kv), w_kk, w_kv,
      pos_f.reshape(T, 1), pos_f.reshape(B, 1, S), f64l, f32l, f64c, f32c)


def _cmp_kernel(f_ref, w1_ref, w2_ref, o_ref):
    hid = _dot(f_ref[...], w1_ref[...])
    hid = hid * jax.nn.sigmoid(hid)
    o_ref[...] = _dot(hid, w2_ref[...])


def _compress(tok, pe, w1, w2):
    B, S, G, HD = tok.shape
    r = CMP_LEN // CMP_STRIDE
    n_chunks = S // CMP_STRIDE
    n_cmp = n_chunks - r + 1
    chunks = tok.reshape(B, n_chunks, CMP_STRIDE, G, HD)
    blocks = jnp.concatenate([chunks[:, i:n_cmp + i] for i in range(r)], axis=2)
    blocks = blocks + pe[:, None, :]
    flat = blocks.transpose(0, 1, 3, 2, 4).reshape(B * n_cmp * G, CMP_LEN * HD)
    rows = flat.shape[0]
    tm = 512
    rows_p = -(-rows // tm) * tm
    flat = jnp.pad(flat, ((0, rows_p - rows), (0, 0))).astype(MXU_DTYPE)
    out = pl.pallas_call(
        _cmp_kernel,
        grid=(rows_p // tm,),
        in_specs=[pl.BlockSpec((tm, CMP_LEN * HD), lambda i: (i, 0)),
                  pl.BlockSpec((CMP_LEN * HD, CMP_HIDDEN), lambda i: (0, 0)),
                  pl.BlockSpec((CMP_HIDDEN, HD), lambda i: (0, 0))],
        out_specs=pl.BlockSpec((tm, HD), lambda i: (i, 0)),
        out_shape=jax.ShapeDtypeStruct((rows_p, HD), F32),
        compiler_params=_cparams(("arbitrary",)),
    )(flat, w1.astype(MXU_DTYPE), w2.astype(MXU_DTYPE))
    return out[:rows].reshape(B, n_cmp, G, HD)


def _select_kernel(ik_ref, iqt_ref, iwt_ref, bias_ref, sc_ref, *, TQ, KC, S, K, NBIS):
    qi = pl.program_id(1)
    q0 = qi * TQ
    n_ch = (q0 + TQ + KC - 1) // KC
    n_all = S // KC
    t_row = q0 + lax.broadcasted_iota(jnp.int32, (1, TQ), 1)
    key_iota = lax.broadcasted_iota(jnp.int32, (KC, TQ), 0)
    kf = float(K)
    SUB = LANES
    sub_iota = lax.broadcasted_iota(jnp.int32, (SUB, TQ), 0)

    def score_chunk(c, carry):
        mn, mx = carry
        for u in range(KC // SUB):
            off = pl.multiple_of(c * KC + u * SUB, SUB)
            ikc = ik_ref[0, pl.ds(off, SUB), :]
            acc = jnp.zeros((SUB, TQ), F32)
            for h in range(IDX_HEADS):
                lg = jnp.dot(ikc, iqt_ref[0, h], preferred_element_type=F32)
                acc = acc + jnp.maximum(lg, 0.0) * iwt_ref[0, h:h + 1, :]
            causal = off + sub_iota <= t_row
            sc_ref[pl.ds(off, SUB), :] = jnp.where(causal, acc, NEG_INF)
            rows = lambda x: x.reshape(SUB // SUBLANES, SUBLANES, TQ)
            mn = jnp.minimum(mn, jnp.min(rows(jnp.where(causal, acc, -LOWEST)), axis=0))
            mx = jnp.maximum(mx, jnp.max(rows(jnp.where(causal, acc, LOWEST)), axis=0))
        return mn, mx

    mn, mx = lax.fori_loop(0, n_ch, score_chunk, (jnp.full((SUBLANES, TQ), -LOWEST, F32),
                                                  jnp.full((SUBLANES, TQ), LOWEST, F32)))
    lo = jnp.min(mn, axis=0, keepdims=True)
    hi = jnp.max(mx, axis=0, keepdims=True)

    def chunk(c):
        off = pl.multiple_of(c * KC, KC)
        return sc_ref[pl.ds(off, KC), :], off

    AR = 4 * SUBLANES

    def fold(x, op):
        return op(x.reshape(KC // AR, AR, TQ), axis=0)

    def key_pass(fn, init):
        def body(c, acc):
            xs, off = chunk(c)
            return fn(acc, xs, off)
        return lax.fori_loop(0, n_ch, body, init)

    def count_ge(thr):
        acc = key_pass(lambda a, xs, off: a + fold(jnp.where(xs >= thr, 1.0, 0.0), jnp.sum),
                       jnp.zeros((AR, TQ), F32))
        return jnp.sum(acc, axis=0, keepdims=True)

    def count_gt(thr):
        acc = key_pass(lambda a, xs, off: a + fold(jnp.where(xs > thr, 1.0, 0.0), jnp.sum),
                       jnp.zeros((AR, TQ), F32))
        return jnp.sum(acc, axis=0, keepdims=True)

    def max_where(bound):
        fn = lambda a, xs, off: jnp.maximum(a, fold(jnp.where(xs <= bound, xs, LOWEST), jnp.max))
        acc = key_pass(fn, jnp.full((AR, TQ), LOWEST, F32))
        return jnp.max(acc, axis=0, keepdims=True)

    def count_and_next(v):
        def fn(a, xs, off):
            cnt, nxt = a
            return (cnt + fold(jnp.where(xs >= v, 1.0, 0.0), jnp.sum),
                    jnp.maximum(nxt, fold(jnp.where(xs < v, xs, LOWEST), jnp.max)))
        cnt, nxt = key_pass(fn, (jnp.zeros((AR, TQ), F32), jnp.full((AR, TQ), LOWEST, F32)))
        return jnp.sum(cnt, axis=0, keepdims=True), jnp.max(nxt, axis=0, keepdims=True)

    def bisect(_, carry):
        lo, hi = carry
        mid = 0.5 * (lo + hi)
        ge = count_ge(mid) >= kf
        return jnp.where(ge, mid, lo), jnp.where(ge, hi, mid)

    lo, hi = lax.fori_loop(0, NBIS, bisect, (lo, hi))

    all_keys = jnp.where(t_row < K, 1.0, 0.0)
    v = max_where(hi)
    n_ge, nxt = count_and_next(v)
    done = jnp.maximum(all_keys, jnp.where(n_ge >= kf, 1.0, 0.0))

    def peel_cond(st):
        return jnp.sum(st[1]) < float(TQ)

    def peel_body(st):
        v, done, n_ge, nxt = st
        v = jnp.where(done > 0.5, v, nxt)
        cnt, nxt = count_and_next(v)
        n_ge = jnp.where(done > 0.5, n_ge, cnt)
        done = jnp.maximum(done, jnp.where(cnt >= kf, 1.0, 0.0))
        return v, done, n_ge, nxt

    v, done, n_ge, _ = lax.while_loop(peel_cond, peel_body, (v, done, n_ge, nxt))
    thr = jnp.where(all_keys > 0.5, LOWEST, v)
    has_tie = jnp.max(jnp.where((n_ge > kf) & (all_keys < 0.5), 1.0, 0.0)) > 0.5

    def write(off, sel):
        bias_ref[0, pl.ds(off, KC), :] = jnp.where(sel, 0.0, NEG_INF).astype(bias_ref.dtype)

    def fill_tail():
        def body(c, carry):
            write(pl.multiple_of(c * KC, KC), jnp.zeros((KC, TQ), jnp.bool_))
            return carry
        lax.fori_loop(n_ch, n_all, body, 0)

    @pl.when(jnp.logical_not(has_tie))
    def _():
        def body(c, carry):
            xs, off = chunk(c)
            write(off, (xs >= thr) & ((off + key_iota) <= t_row))
            return carry
        lax.fori_loop(0, n_ch, body, 0)
        fill_tail()

    @pl.when(has_tie)
    def _():
        need = kf - count_gt(thr)
        r_i = lax.broadcasted_iota(jnp.int32, (KC, KC), 0)
        c_i = lax.broadcasted_iota(jnp.int32, (KC, KC), 1)
        lower = jnp.where(c_i <= r_i, 1.0, 0.0).astype(MXU_DTYPE)
        def body(c, seen):
            xs, off = chunk(c)
            causal = (off + key_iota) <= t_row
            eq = jnp.where((xs == thr) & causal, 1.0, 0.0)
            rank = jnp.dot(lower, eq.astype(MXU_DTYPE), preferred_element_type=F32) + seen
            write(off, ((xs > thr) & causal) | ((eq > 0.5) & (rank <= need)))
            return seen + jnp.sum(eq, axis=0, keepdims=True)
        lax.fori_loop(0, n_ch, body, jnp.zeros((1, TQ), F32))
        fill_tail()


def _dsa_select(ik, iqt, iwt, K, TQ, KC):
    B, S, DI = ik.shape
    H = iqt.shape[1]
    kern = functools.partial(_select_kernel, TQ=TQ, KC=KC, S=S, K=K, NBIS=SELECT_BISECTIONS)
    return pl.pallas_call(
        kern,
        grid=(B, S // TQ),
        in_specs=[pl.BlockSpec((1, S, DI), lambda b, q: (b, 0, 0)),
                  pl.BlockSpec((1, H, DI, TQ), lambda b, q: (b, 0, 0, q)),
                  pl.BlockSpec((1, H, TQ), lambda b, q: (b, 0, q))],
        out_specs=pl.BlockSpec((1, S, TQ), lambda b, q: (b, 0, q)),
        out_shape=jax.ShapeDtypeStruct((B, S, S), jnp.bfloat16),
        scratch_shapes=[pltpu.VMEM((S, TQ), F32)],
        compiler_params=_cparams(("arbitrary", "arbitrary")),
    )(ik, iqt, iwt)


def _cmp_attn_kernel(qt_ref, kc_ref, vct_ref, o_ref, bm_ref, *, TQ, NCP, NBP, N_SEL):
    qi = pl.program_id(1)
    q0 = qi * TQ
    G, J = NSA_KV_GROUPS, NSA_REP
    N = J * TQ
    t_lane = q0 + (lax.broadcasted_iota(jnp.int32, (NCP, N), 1) & (TQ - 1))
    cmp_end = lax.broadcasted_iota(jnp.int32, (NCP, N), 0) * CMP_STRIDE + (CMP_LEN - 1)
    vis = cmp_end <= t_lane
    n_i = lax.broadcasted_iota(jnp.int32, (NBP, NCP), 0) * SEL_BLOCK
    c_i = lax.broadcasted_iota(jnp.int32, (NBP, NCP), 1) * CMP_STRIDE
    overlap_t = jnp.where((c_i <= n_i + SEL_BLOCK - 1) & (c_i + CMP_LEN - 1 >= n_i), 1.0, 0.0)
    blk = lax.broadcasted_iota(jnp.int32, (NBP, TQ), 0)
    tq = q0 + lax.broadcasted_iota(jnp.int32, (NBP, TQ), 1)
    cur = tq // SEL_BLOCK
    admissible = blk * SEL_BLOCK <= tq
    forced = (blk == 0) | (blk == cur) | (blk == cur - 1)
    for g in range(G):
        qt = jnp.concatenate([qt_ref[0, g * J + j] for j in range(J)], axis=1)
        s = jnp.where(vis, _dot(kc_ref[0, g], qt), NEG_INF)
        p = jnp.exp(s - jnp.max(s, axis=0, keepdims=True))
        p = jnp.where(vis, p, 0.0)
        p = p / jnp.maximum(jnp.sum(p, axis=0, keepdims=True), TINY)
        o = _dot(vct_ref[0, g], p)
        for j in range(J):
            o_ref[0, g * J + j] = o[:, j * TQ:(j + 1) * TQ]
        psum = p[:, 0:TQ]
        for j in range(1, J):
            psum = psum + p[:, j * TQ:(j + 1) * TQ]
        imp = _dot_exact_lhs(overlap_t, psum)
        score = jnp.where(admissible & forced, FORCED, jnp.where(admissible, imp, NEG_INF))

        def pick(_, st):
            score, sel = st
            m = jnp.max(score, axis=0, keepdims=True)
            first = jnp.min(jnp.where(score == m, blk, NBP), axis=0, keepdims=True)
            hit = blk == first
            return jnp.where(hit, LOWEST, score), jnp.where(hit, 1.0, sel)

        _, sel = lax.fori_loop(0, N_SEL, pick, (score, jnp.zeros((NBP, TQ), F32)))
        bm_ref[0, g] = sel.astype(bm_ref.dtype)


def _cmp_attn(qt, kc, vct, n_sel, NBP, TQ):
    B, H, HD, S = qt.shape
    G, NCP = kc.shape[1], kc.shape[2]
    kern = functools.partial(_cmp_attn_kernel, TQ=TQ, NCP=NCP, NBP=NBP, N_SEL=n_sel)
    return pl.pallas_call(
        kern,
        grid=(B, S // TQ),
        in_specs=[pl.BlockSpec((1, H, HD, TQ), lambda b, q: (b, 0, 0, q)),
                  pl.BlockSpec((1, G, NCP, HD), lambda b, q: (b, 0, 0, 0)),
                  pl.BlockSpec((1, G, HD, NCP), lambda b, q: (b, 0, 0, 0))],
        out_specs=[pl.BlockSpec((1, H, HD, TQ), lambda b, q: (b, 0, 0, q)),
                   pl.BlockSpec((1, G, NBP, TQ), lambda b, q: (b, 0, 0, q))],
        out_shape=[jax.ShapeDtypeStruct((B, H, HD, S), F32),
                   jax.ShapeDtypeStruct((B, G, NBP, S), jnp.bfloat16)],
        compiler_params=_cparams(("arbitrary", "arbitrary")),
    )(qt, kc, vct)


def _flash_kernel(*refs, mode, G, J, TQ, TK, NWIN):
    if mode == "window":
        qt_ref, k_ref, vt_ref, o_ref, m_ref, acc_ref = refs
        x_ref = None
    else:
        qt_ref, k_ref, vt_ref, x_ref, o_ref, m_ref, acc_ref = refs
    HD = HEAD_DIM
    qi = pl.program_id(1)
    kk = pl.program_id(2)
    q0 = qi * TQ
    last = (q0 + TQ - 1) // TK
    if mode == "window":
        ki = last - (NWIN - 1) + kk
        valid = ki >= 0
    else:
        ki = kk
        valid = kk <= last

    @pl.when(kk == 0)
    def _():
        m_ref[...] = jnp.full(m_ref.shape, NEG_INF, F32)
        acc_ref[...] = jnp.zeros(acc_ref.shape, F32)

    @pl.when(valid)
    def _():
        k0 = ki * TK
        kidx = k0 + lax.broadcasted_iota(jnp.int32, (TK, TQ), 0)
        tq = q0 + lax.broadcasted_iota(jnp.int32, (TK, TQ), 1)
        if mode == "mask":
            bias = x_ref[0].astype(F32)
        elif mode == "window":
            bias = jnp.where((kidx <= tq) & (kidx > tq - WINDOW), 0.0, NEG_INF)
        else:
            causal = kidx <= tq
            nbp = x_ref.shape[2]
            blk_of_key = (k0 + lax.broadcasted_iota(jnp.int32, (TK, nbp), 0)) // SEL_BLOCK
            expand_t = jnp.where(lax.broadcasted_iota(jnp.int32, (TK, nbp), 1) == blk_of_key, 1.0, 0.0)
            expand_t = expand_t.astype(MXU_DTYPE)

        def scores(g):
            if mode == "block":
                picked = jnp.dot(expand_t, x_ref[0, g].astype(MXU_DTYPE), preferred_element_type=F32)
                b = jnp.where(causal & (picked > 0.5), 0.0, NEG_INF)
            else:
                b = bias
            if J > 1:
                b = jnp.concatenate([b] * J, axis=1)
                qt = jnp.concatenate([qt_ref[0, g * J + j] for j in range(J)], axis=1)
            else:
                qt = qt_ref[0, g]
            return _dot(k_ref[0, g], qt) + b

        for g in range(G):
            s = scores(g)
            m_prev = m_ref[g]
            m_new = jnp.maximum(m_prev, jnp.max(s, axis=0, keepdims=True))
            alpha = jnp.exp(m_prev - m_new)
            p = jnp.exp((s - m_new).astype(MXU_DTYPE))
            acc_ref[g] = alpha * acc_ref[g] + _dot(vt_ref[0, g], p)
            m_ref[g] = m_new

    @pl.when(kk == pl.num_programs(2) - 1)
    def _():
        for g in range(G):
            o = acc_ref[g, 0:HD] / jnp.maximum(acc_ref[g, HD:HD + 1], TINY)
            for j in range(J):
                o_ref[0, g * J + j] = o[:, j * TQ:(j + 1) * TQ].astype(o_ref.dtype)


def _flash(qt, k, vt, extra, mode, TQ, TK, out_dtype):
    B, H, HD, S = qt.shape
    G = k.shape[1]
    J = H // G
    N = J * TQ
    nq = S // TQ
    last_of = lambda qi: (qi * TQ + TQ - 1) // TK
    if mode == "window":
        NWIN = min((WINDOW - 1 + TK - 1) // TK + 1, S // TK)
        nk = NWIN
        kidx_of = lambda qi, kk: jnp.maximum(last_of(qi) - (NWIN - 1) + kk, 0)
    else:
        NWIN = 0
        nk = S // TK
        kidx_of = lambda qi, kk: jnp.minimum(kk, last_of(qi))
    in_specs = [pl.BlockSpec((1, H, HD, TQ), lambda b, qi, kk: (b, 0, 0, qi)),
                pl.BlockSpec((1, G, TK, HD), lambda b, qi, kk: (b, 0, kidx_of(qi, kk), 0)),
                pl.BlockSpec((1, G, HD + ONES_ROWS, TK), lambda b, qi, kk: (b, 0, 0, kidx_of(qi, kk)))]
    args = [qt, k, vt]
    if mode == "mask":
        in_specs.append(pl.BlockSpec((1, TK, TQ), lambda b, qi, kk: (b, kidx_of(qi, kk), qi)))
        args.append(extra)
    elif mode == "block":
        nbp = extra.shape[2]
        in_specs.append(pl.BlockSpec((1, G, nbp, TQ), lambda b, qi, kk: (b, 0, 0, qi)))
        args.append(extra)
    kern = functools.partial(_flash_kernel, mode=mode, G=G, J=J, TQ=TQ, TK=TK, NWIN=NWIN)
    return pl.pallas_call(
        kern,
        grid=(B, nq, nk),
        in_specs=in_specs,
        out_specs=pl.BlockSpec((1, H, HD, TQ), lambda b, qi, kk: (b, 0, 0, qi)),
        out_shape=jax.ShapeDtypeStruct((B, H, HD, S), out_dtype),
        scratch_shapes=[pltpu.VMEM((G, 1, N), F32),
                        pltpu.VMEM((G, HD + ONES_ROWS, N), F32)],
        compiler_params=_cparams(("arbitrary", "arbitrary", "arbitrary")),
    )(*args)


def _dot_tn(a_t, b):
    return lax.dot_general(a_t.astype(MXU_DTYPE), b.astype(MXU_DTYPE),
                           (((0,), (0,)), ((), ())), preferred_element_type=F32)


def _route(h, wr, br):
    tm = h.shape[0]
    lane = lax.broadcasted_iota(jnp.int32, (tm, LANES), 1)
    h_hi = h.astype(MXU_DTYPE)
    h_lo = (h - h_hi.astype(F32)).astype(MXU_DTYPE)
    w_hi = wr.astype(MXU_DTYPE)
    w_lo = (wr - w_hi.astype(F32)).astype(MXU_DTYPE)
    d = lambda a, b: jnp.dot(a, b, preferred_element_type=F32)
    logits = d(h_hi, w_hi) + d(h_hi, w_lo) + d(h_lo, w_hi) + br
    is_e = lane < N_EXPERTS
    is_g = (lane >= N_EXPERTS) & (lane < N_EXPERTS + N_GROUPS)
    lg = jnp.where(is_g, logits, LOWEST)
    mg = jnp.max(lg, axis=-1, keepdims=True)
    gsel = jnp.min(jnp.where(is_g & (lg == mg), lane - N_EXPERTS, N_GROUPS), axis=-1, keepdims=True)
    pg_sel = 1.0 / jnp.sum(jnp.where(is_g, jnp.exp(lg - mg), 0.0), axis=-1, keepdims=True)
    in_grp = is_e & ((lane // EXPERTS_PER_GROUP) == gsel)
    le = jnp.where(in_grp, logits, LOWEST)
    me = jnp.max(le, axis=-1, keepdims=True)
    ex = jnp.where(in_grp, jnp.exp(le - me), 0.0)
    pe = ex / jnp.sum(ex, axis=-1, keepdims=True)
    pe = jnp.where(in_grp, pe, -1.0)
    p1 = jnp.max(pe, axis=-1, keepdims=True)
    i1 = jnp.min(jnp.where(pe == p1, lane, LANES), axis=-1, keepdims=True)
    pe2 = jnp.where(lane == i1, -1.0, pe)
    p2 = jnp.max(pe2, axis=-1, keepdims=True)
    i2 = jnp.min(jnp.where(pe2 == p2, lane, LANES), axis=-1, keepdims=True)
    tot = p1 + p2
    comb = jnp.where(lane == i1, p1 / tot * pg_sel, 0.0) + jnp.where(lane == i2, p2 / tot * pg_sel, 0.0)
    return jnp.where(lane == N_EXPERTS, gsel.astype(F32), comb)


def _outproj_kernel(x_ref, oa_ref, oc_ref, os_ref, ow_ref, br_ref, mg_ref, wa_ref, wb_ref, wo_ref,
                    gt_ref, g2_ref, sc_ref, sh_ref, wr_ref, brt_ref, o_ref, hr_ref):
    HD = HEAD_DIM
    D = x_ref.shape[1]
    br = br_ref[0]
    parts = []
    for hh in range(NSA_HEADS):
        parts.append(br[3 * hh:3 * hh + 1] * oc_ref[0, hh] + br[3 * hh + 1:3 * hh + 2] * os_ref[0, hh]
                     + br[3 * hh + 2:3 * hh + 3] * ow_ref[0, hh])
    ob_t = jnp.concatenate(parts, axis=0)
    oa_t = jnp.concatenate([oa_ref[0, hh] for hh in range(DSA_HEADS)], axis=0)
    ua = _dot_tn(oa_t, wa_ref[...])
    ub = _dot_tn(ob_t, wb_ref[...])
    mg = mg_ref[...]
    merged = jax.nn.sigmoid(mg[:, :D]) * ua + jax.nn.sigmoid(mg[:, D:]) * ub
    x1 = x_ref[...] + gt_ref[0] * _dot(merged, wo_ref[...])
    o_ref[...] = x1
    h = _rms(x1, g2_ref[...]) * (1.0 + sc_ref[0]) + sh_ref[0]
    hr_ref[:, :D] = h
    hr_ref[:, D:] = _route(h, wr_ref[...], brt_ref[...])


def _out_proj(x2d, oa, oc, os_, ow, br, mg, wa, wb, wo, gt, g2, sc2, sh2, wr, brt, S, tm):
    T, D = x2d.shape
    B, H, HD, _ = oa.shape
    per_b = S // tm
    heads_cols = pl.BlockSpec((1, H, HD, tm), lambda i: (i // per_b, 0, 0, i % per_b))
    per_batch = pl.BlockSpec((1, 1, D), lambda i: (i // per_b, 0, 0))
    rows = lambda n: pl.BlockSpec((tm, n), lambda i: (i, 0))
    const = lambda a, b: pl.BlockSpec((a, b), lambda i: (0, 0))
    return pl.pallas_call(
        _outproj_kernel,
        grid=(T // tm,),
        in_specs=[rows(D), heads_cols, heads_cols, heads_cols, heads_cols,
                  pl.BlockSpec((1, br.shape[1], tm), lambda i: (i // per_b, 0, i % per_b)),
                  rows(2 * D), const(H * HD, D), const(H * HD, D), const(D, D), per_batch,
                  const(1, D), per_batch, per_batch, const(D, LANES), const(1, LANES)],
        out_specs=[rows(D), rows(D + LANES)],
        out_shape=[jax.ShapeDtypeStruct((T, D), F32), jax.ShapeDtypeStruct((T, D + LANES), F32)],
        compiler_params=_cparams(("arbitrary",)),
    )(x2d, oa, oc, os_, ow, br, mg, wa, wb, wo, gt, g2.reshape(1, D), sc2, sh2, wr, brt)


def _row_copy(src_hbm, t, dst, r, sem):
    return pltpu.make_async_copy(src_hbm.at[pl.ds(t, 1)], dst.at[pl.ds(r, 1)], sem)


def _start_rows(idx_ref, base, r0, n, src_hbm, dst, sem):
    for r in range(n):
        _row_copy(src_hbm, idx_ref[base + r0 + r], dst, r0 + r, sem).start()


def _wait_rows(src_hbm, dst, n, sem):
    pltpu.make_async_copy(src_hbm.at[pl.ds(0, n)], dst.at[pl.ds(0, n)], sem).wait()


def _experts_kernel(tg_ref, src_ref, h_hbm, wg_ref, wu_ref, wd_ref, o_ref,
                    hbuf, acc_ref, sems, *, TM):
    i = pl.program_id(0)
    e = pl.program_id(1)
    n_tiles = pl.num_programs(0)
    EPG, EPS = EXPERTS_PER_GROUP, EXPERTS_PER_STEP
    n_steps = EPG // EPS
    part = TM // n_steps
    slot = i % 2

    @pl.when((i == 0) & (e == 0))
    def _():
        def start(c, carry):
            _start_rows(src_ref, 0, pl.multiple_of(c * part, part), part, h_hbm, hbuf.at[0], sems.at[0])
            return carry
        lax.fori_loop(0, n_steps, start, 0)

    @pl.when(e == 0)
    def _():
        _wait_rows(h_hbm, hbuf.at[slot], TM, sems.at[slot])
        acc_ref[...] = jnp.zeros(acc_ref.shape, F32)

    @pl.when(i + 1 < n_tiles)
    def _():
        _start_rows(src_ref, (i + 1) * TM, e * part, part, h_hbm, hbuf.at[1 - slot], sems.at[1 - slot])

    D = acc_ref.shape[1]
    h = hbuf[slot, :, :D].astype(MXU_DTYPE)
    lane = lax.broadcasted_iota(jnp.int32, (TM, LANES), 1)
    route = hbuf[slot, :, D:]
    out = acc_ref[...]
    for q in range(EPS):
        a = jnp.dot(h, wg_ref[q], preferred_element_type=F32)
        u = jnp.dot(h, wu_ref[q], preferred_element_type=F32)
        y = _dot(a * jax.nn.sigmoid(a) * u, wd_ref[q])
        expert = tg_ref[i] * EPG + e * EPS + q
        out = out + jnp.sum(jnp.where(lane == expert, route, 0.0), axis=-1, keepdims=True) * y
    acc_ref[...] = out

    @pl.when(e == pl.num_programs(1) - 1)
    def _():
        o_ref[...] = acc_ref[...]


def _experts(hr, tile_group, src, wg, wu, wd, TM):
    D = wg.shape[1]
    DE = wg.shape[2]
    P = src.shape[0]
    EPG = EXPERTS_PER_GROUP
    EPS = EXPERTS_PER_STEP
    w_idx = lambda i, e, tg, src: (tg[i] * (EPG // EPS) + e, 0, 0)
    kern = functools.partial(_experts_kernel, TM=TM)
    grid_spec = pltpu.PrefetchScalarGridSpec(
        num_scalar_prefetch=2,
        grid=(P // TM, EPG // EPS),
        in_specs=[pl.BlockSpec(memory_space=pl.ANY),
                  pl.BlockSpec((EPS, D, DE), w_idx),
                  pl.BlockSpec((EPS, D, DE), w_idx),
                  pl.BlockSpec((EPS, DE, D), w_idx)],
        out_specs=pl.BlockSpec((TM, D), lambda i, e, tg, src: (i, 0)),
        scratch_shapes=[pltpu.VMEM((2, TM, D + LANES), F32),
                        pltpu.VMEM((TM, D), F32),
                        pltpu.SemaphoreType.DMA((2,))])
    return pl.pallas_call(
        kern,
        grid_spec=grid_spec,
        out_shape=jax.ShapeDtypeStruct((P, D), F32),
        compiler_params=_cparams(("arbitrary", "arbitrary")),
    )(tile_group, src, hr, wg, wu, wd)


def _combine_kernel(pos_ref, x_ref, y_hbm, gt_ref, gf_ref, o_ref, ybuf, sem, *, TM):
    part = 64
    def start(c, carry):
        _start_rows(pos_ref, pl.program_id(0) * TM, pl.multiple_of(c * part, part), part, y_hbm, ybuf,
                    sem.at[0])
        return carry
    lax.fori_loop(0, TM // part, start, 0)
    _wait_rows(y_hbm, ybuf, TM, sem.at[0])
    o_ref[...] = _rms(x_ref[...] + gt_ref[0] * ybuf[...], gf_ref[...])


def _combine(x1, ys, pos, gt, gf, S, tm):
    T, D = x1.shape
    per_b = S // tm
    grid_spec = pltpu.PrefetchScalarGridSpec(
        num_scalar_prefetch=1,
        grid=(T // tm,),
        in_specs=[pl.BlockSpec((tm, D), lambda i, pos: (i, 0)),
                  pl.BlockSpec(memory_space=pl.ANY),
                  pl.BlockSpec((1, 1, D), lambda i, pos: (i // per_b, 0, 0)),
                  pl.BlockSpec((1, D), lambda i, pos: (0, 0))],
        out_specs=pl.BlockSpec((tm, D), lambda i, pos: (i, 0)),
        scratch_shapes=[pltpu.VMEM((tm, D), F32), pltpu.SemaphoreType.DMA((1,))])
    return pl.pallas_call(
        functools.partial(_combine_kernel, TM=tm),
        grid_spec=grid_spec,
        out_shape=jax.ShapeDtypeStruct((T, D), F32),
        compiler_params=_cparams(("arbitrary",)),
    )(pos, x1, ys, gt, gf.reshape(1, D))


def _routed_moe(x1, hr, gt, gf, wg, wu, wd, S, TM, combine_rows):
    T, D = x1.shape
    gsel = hr[:, D + N_EXPERTS].astype(jnp.int32)
    onehot = (gsel[:, None] == jnp.arange(N_GROUPS, dtype=jnp.int32)[None, :]).astype(jnp.int32)
    csum = jnp.cumsum(onehot, axis=0)
    rank = jnp.take_along_axis(csum, gsel[:, None], axis=1)[:, 0] - 1
    padded = -(-csum[-1] // TM) * TM
    ends = jnp.cumsum(padded)
    starts = ends - padded
    pos = (jnp.take(starts, gsel) + rank).astype(jnp.int32)
    P = T + N_GROUPS * TM
    src = jnp.zeros((P,), jnp.int32).at[pos].set(jnp.arange(T, dtype=jnp.int32))
    tile_start = jnp.arange(P // TM, dtype=jnp.int32) * TM
    tile_group = jnp.minimum(jnp.sum(tile_start[:, None] >= ends[None, :], axis=1), N_GROUPS - 1)
    ys = _experts(hr, tile_group.astype(jnp.int32), src, wg, wu, wd, TM)
    return _combine(x1, ys, pos, gt, gf, S, combine_rows)


def _layer(x, mod, positions, g_norm1, w_in, g_kv_latent, w_kv_up, pe_cmp_k, pe_cmp_v,
           w_cmp1_k, w_cmp2_k, w_cmp1_v, w_cmp2_v, w_up_a, w_up_b, w_out, g_norm2,
           w_router_group, b_router_group, w_router_expert, b_router_expert,
           w_expert_gate, w_expert_up, w_expert_down, g_out):
    B, S, D = x.shape
    T = B * S
    HD, G = HEAD_DIM, NSA_KV_GROUPS
    kvb = G * HD
    topk_a = min(DSA_TOPK_MAX, S // 4)
    n_sel = min(SEL_COUNT, S // SEL_BLOCK)
    mod6 = mod.reshape(B, 6, 1, D)
    sh1, sc1, gt1, sh2, sc2, gt2 = (mod6[:, i] for i in range(6))

    x2d = x.reshape(T, D)
    t = _tiles(S)
    (mg, qt_a, k_a, vt_a, iqt, iwt, ik, qt_b, kcvc, ks, kw, vst, vwt, br) = _prep(
        x, sc1, sh1, g_norm1, w_in, g_kv_latent, w_kv_up, positions, t.rows)

    kc = _compress(kcvc[..., :kvb].reshape(B, S, G, HD), pe_cmp_k, w_cmp1_k, w_cmp2_k)
    vc = _compress(kcvc[..., kvb:].reshape(B, S, G, HD), pe_cmp_v, w_cmp1_v, w_cmp2_v)
    n_cmp = kc.shape[1]
    ncp = -(-(n_cmp + 1) // LANES) * LANES
    pad_c = lambda t: jnp.pad(t, ((0, 0), (0, ncp - n_cmp), (0, 0), (0, 0))).astype(MXU_DTYPE)
    kc, vct = pad_c(kc).transpose(0, 2, 1, 3), pad_c(vc).transpose(0, 2, 3, 1)

    sel_bias = _dsa_select(ik, iqt, iwt, topk_a, t.select_q, t.select_k)
    o_a = _flash(qt_a, k_a, vt_a, sel_bias, "mask", t.mask_q, t.mask_k, MXU_DTYPE)

    nbp = -(-(S // SEL_BLOCK) // LANES) * LANES
    o_c, blk_mask = _cmp_attn(qt_b, kc, vct, n_sel, nbp, t.nsa_q)
    o_s = _flash(qt_b, ks, vst, blk_mask, "block", t.nsa_q, t.block_k, F32)
    o_w = _flash(qt_b, kw, vwt, None, "window", t.nsa_q, t.nsa_q, F32)

    wr = jnp.concatenate([w_router_expert, w_router_group], axis=1)
    wr = jnp.pad(wr, ((0, 0), (0, LANES - wr.shape[1])))
    brt = jnp.concatenate([b_router_expert, b_router_group])
    brt = jnp.pad(brt, (0, LANES - brt.shape[0])).reshape(1, LANES)
    x1, hr = _out_proj(x2d, o_a, o_c, o_s, o_w, br, mg, w_up_a.astype(MXU_DTYPE),
                              w_up_b.astype(MXU_DTYPE), w_out.astype(MXU_DTYPE), gt1,
                              g_norm2, sc2, sh2, wr, brt, S, t.rows)

    out = _routed_moe(x1, hr, gt2, g_out, w_expert_gate.astype(MXU_DTYPE),
                      w_expert_up.astype(MXU_DTYPE), w_expert_down.astype(MXU_DTYPE), S,
                      t.moe_rows, t.rows)
    return out.reshape(B, S, D)


def kernel(x, c, positions, w_ada, b_ada, g_norm1, w_in, g_kv_latent, w_kv_up, pe_cmp_k, pe_cmp_v,
           w_cmp1_k, w_cmp2_k, w_cmp1_v, w_cmp2_v, w_up_a, w_up_b, w_out, g_norm2, w_router_group,
           b_router_group, w_router_expert, b_router_expert, w_expert_gate, w_expert_up,
           w_expert_down, g_final):
    depth = w_ada.shape[0]
    assert depth == 1, "the fused final norm assumes a single layer"
    mod = _ada_mod(c, w_ada[0], b_ada[0])
    return _layer(x, mod, positions, g_norm1[0], w_in[0], g_kv_latent[0], w_kv_up[0], pe_cmp_k[0],
                  pe_cmp_v[0], w_cmp1_k[0], w_cmp2_k[0], w_cmp1_v[0], w_cmp2_v[0], w_up_a[0],
                  w_up_b[0], w_out[0], g_norm2[0], w_router_group[0], b_router_group[0],
                  w_router_expert[0], b_router_expert[0], w_expert_gate[0], w_expert_up[0],
                  w_expert_down[0], g_final)
```

```python
import functools
from typing import NamedTuple

import numpy as np
import jax
import jax.numpy as jnp
from jax import lax
from jax.experimental import pallas as pl
from jax.experimental.pallas import tpu as pltpu

HEAD_DIM = 64
ROT_FRACTION = 4
ROPE_THETA = 500000.0
DSA_HEADS = 8
DSA_KV_RANK = 128
IDX_HEADS = 8
IDX_DIM = 32
DSA_TOPK_MAX = 256
NSA_HEADS = 8
NSA_KV_GROUPS = 2
NSA_REP = NSA_HEADS // NSA_KV_GROUPS
CMP_LEN = 32
CMP_STRIDE = 16
CMP_HIDDEN = 256
SEL_BLOCK = 64
SEL_COUNT = 16
WINDOW = 512
N_GROUPS = 4
EXPERTS_PER_GROUP = 8
N_EXPERTS = N_GROUPS * EXPERTS_PER_GROUP
D_EXPERT = 256
EXPERTS_PER_STEP = 8
NORM_EPS = 1e-6
NEG_INF = -1e30
TINY = 1e-30
LOWEST = -3.0e38
FORCED = 1e30

LANES = 128
SUBLANES = 8
ONES_ROWS = 16
MXU_DTYPE = jnp.bfloat16
VMEM_LIMIT = 56 * 1024 * 1024

F32 = jnp.float32
SELECT_BISECTIONS = 16


class _Tiles(NamedTuple):
    rows: int
    select_q: int
    select_k: int
    mask_q: int
    mask_k: int
    nsa_q: int
    block_k: int
    moe_rows: int


def _tiles(S):
    cap = lambda n: min(n, S)
    return _Tiles(rows=cap(256), select_q=cap(256), select_k=cap(256), mask_q=cap(1024),
                  mask_k=cap(1024), nsa_q=cap(512), block_k=cap(1024), moe_rows=512)


def _cparams(sem):
    return pltpu.CompilerParams(dimension_semantics=sem, vmem_limit_bytes=VMEM_LIMIT)


def _dot(a, b):
    return jnp.dot(a.astype(MXU_DTYPE), b.astype(MXU_DTYPE), preferred_element_type=F32)


def _dot_exact_lhs(a01, b):
    hi = b.astype(MXU_DTYPE)
    r1 = b - hi.astype(F32)
    mid = r1.astype(MXU_DTYPE)
    lo = (r1 - mid.astype(F32)).astype(MXU_DTYPE)
    a = a01.astype(MXU_DTYPE)
    d = lambda u: jnp.dot(a, u, preferred_element_type=F32)
    return d(hi) + d(mid) + d(lo)


def _rms(x, g):
    return x * lax.rsqrt(jnp.mean(x * x, axis=-1, keepdims=True) + NORM_EPS) * g


def _ada_kernel(c_ref, w_ref, b_ref, o_ref):
    c = c_ref[...]
    cond = c * jax.nn.sigmoid(c)
    o_ref[...] = _dot(cond, w_ref[...]) + b_ref[...]


def _ada_mod(c, w_ada, b_ada):
    B, D = c.shape
    n_out = w_ada.shape[1]
    rows = SUBLANES
    cp = jnp.zeros((rows, D), F32).at[:B].set(c)
    tn = 1024
    out = pl.pallas_call(
        _ada_kernel,
        grid=(n_out // tn,),
        in_specs=[pl.BlockSpec((rows, D), lambda j: (0, 0)),
                  pl.BlockSpec((D, tn), lambda j: (0, j)),
                  pl.BlockSpec((1, tn), lambda j: (0, j))],
        out_specs=pl.BlockSpec((rows, tn), lambda j: (0, j)),
        out_shape=jax.ShapeDtypeStruct((rows, n_out), F32),
        compiler_params=_cparams(("arbitrary",)),
    )(cp, w_ada, b_ada.reshape(1, n_out))
    return out[:B]


KVB = NSA_KV_GROUPS * HEAD_DIM
ROW_SEGS = (("mg", 2048), ("lat", DSA_KV_RANK), ("kc", KVB), ("vc", KVB), ("ks", KVB), ("kw", KVB),
            ("ik", LANES))
COL_SEGS = (("qa", DSA_HEADS * HEAD_DIM), ("qb", NSA_HEADS * HEAD_DIM), ("iq", IDX_HEADS * IDX_DIM),
            ("vs", KVB), ("vw", KVB), ("lat", DSA_KV_RANK), ("iw", IDX_HEADS), ("br", 3 * NSA_HEADS))


def _seg_offsets(segs):
    out, pos = {}, 0
    for name, n in segs:
        out[name] = pos
        pos += n
    return out, pos


ROW_OFF, ROW_COLS = _seg_offsets(ROW_SEGS)
COL_OFF, COL_ROWS = _seg_offsets(COL_SEGS)


def _rot_lanes(x, c, sa, sb, half):
    outs = []
    for j in range(x.shape[1] // LANES):
        xs = x[:, j * LANES:(j + 1) * LANES]
        outs.append(xs * c + pltpu.roll(xs, half, 1) * sa + pltpu.roll(xs, LANES - half, 1) * sb)
    return outs[0] if len(outs) == 1 else jnp.concatenate(outs, axis=1)


def _prep_kernel(x_ref, g_ref, sc_ref, sh_ref, gc_ref, scc_ref, shc_ref, wr_ref, wc_ref,
                 gkv_ref, gkvc_ref, wkk_ref, wkv_ref, posc_ref, posr_ref, f64l_ref, f32l_ref,
                 f64c_ref, f32c_ref,
                 mg_ref, qa_ref, ka_ref, va_ref, iq_ref, iw_ref, ik_ref, qb_ref, kcvc_ref,
                 ks_ref, kw_ref, vs_ref, vw_ref, br_ref):
    HD = HEAD_DIM
    half = HD // ROT_FRACTION // 2
    scale = HD ** -0.5
    h = _rms(x_ref[...], g_ref[...]) * (1.0 + sc_ref[0]) + sh_ref[0]
    pr = _dot(h, wr_ref[...])
    row = lambda name, n: pr[:, ROW_OFF[name]:ROW_OFF[name] + n]
    mg_ref[...] = row("mg", 2048)
    lane = lax.broadcasted_iota(jnp.int32, (x_ref.shape[0], LANES), 1)

    def lane_tables(freq_row, dim, hf):
        ang = posc_ref[...] * freq_row
        c, s = jnp.cos(ang), jnp.sin(ang)
        upper = (lane & (dim - 1)) >= hf
        return c, jnp.where(upper, s, 0.0), jnp.where(upper, 0.0, -s)

    c64, sa64, sb64 = lane_tables(f64l_ref[...], HD, half)
    c32, sa32, sb32 = lane_tables(f32l_ref[...], IDX_DIM, IDX_DIM // ROT_FRACTION // 2)
    ka = _rot_lanes(_dot(_rms(row("lat", DSA_KV_RANK), gkv_ref[...]), wkk_ref[...]), c64, sa64, sb64, half)
    for hh in range(DSA_HEADS):
        ka_ref[0, hh] = ka[:, hh * HD:(hh + 1) * HD].astype(ka_ref.dtype)
    kcvc_ref[0] = jnp.concatenate([_rot_lanes(row("kc", KVB), c64, sa64, sb64, half), row("vc", KVB)], axis=1)
    ks = _rot_lanes(row("ks", KVB), c64, sa64, sb64, half)
    kw = _rot_lanes(row("kw", KVB), c64, sa64, sb64, half)
    for g in range(NSA_KV_GROUPS):
        ks_ref[0, g] = ks[:, g * HD:(g + 1) * HD].astype(ks_ref.dtype)
        kw_ref[0, g] = kw[:, g * HD:(g + 1) * HD].astype(kw_ref.dtype)
    ik = _rot_lanes(row("ik", LANES), c32, sa32, sb32, IDX_DIM // ROT_FRACTION // 2)
    ik_ref[0] = ik[:, :IDX_DIM].astype(ik_ref.dtype)
    xt = x_ref[...].T
    ht = xt * lax.rsqrt(jnp.mean(xt * xt, axis=0, keepdims=True) + NORM_EPS) * gc_ref[...]
    ht = ht * (1.0 + scc_ref[0]) + shc_ref[0]
    pc = _dot(wc_ref[...], ht)
    col = lambda name, n: pc[COL_OFF[name]:COL_OFF[name] + n]
    ang_t = f64c_ref[...] * posr_ref[0]
    cos, sin = jnp.cos(ang_t), jnp.sin(ang_t)

    def rot_rows(blk):
        x1, x2 = blk[0:half], blk[half:2 * half]
        return jnp.concatenate([x1 * cos - x2 * sin, x2 * cos + x1 * sin, blk[2 * half:]], axis=0)

    qa, qb = col("qa", DSA_HEADS * HD), col("qb", NSA_HEADS * HD)
    for hh in range(DSA_HEADS):
        qa_ref[0, hh] = (rot_rows(qa[hh * HD:(hh + 1) * HD]) * scale).astype(qa_ref.dtype)
    for hh in range(NSA_HEADS):
        qb_ref[0, hh] = (rot_rows(qb[hh * HD:(hh + 1) * HD]) * scale).astype(qb_ref.dtype)
    iq = col("iq", IDX_HEADS * IDX_DIM)
    ang32 = f32c_ref[...] * posr_ref[0]
    first = lax.broadcasted_iota(jnp.int32, ang32.shape, 0) < SUBLANES // 2
    c32t, s32t = jnp.cos(ang32), jnp.where(first, -jnp.sin(ang32), jnp.sin(ang32))
    for hh in range(IDX_HEADS):
        blk = iq[hh * IDX_DIM:(hh + 1) * IDX_DIM]
        top = blk[0:SUBLANES]
        top = top * c32t + pltpu.roll(top, SUBLANES // 2, 0) * s32t
        iq_ref[0, hh] = jnp.concatenate([top, blk[SUBLANES:]], axis=0).astype(iq_ref.dtype)
    ones = jnp.ones((ONES_ROWS, xt.shape[1]), F32)
    with_ones = lambda v: jnp.concatenate([v, ones], axis=0)
    vs, vw = col("vs", KVB), col("vw", KVB)
    for g in range(NSA_KV_GROUPS):
        vs_ref[0, g] = with_ones(vs[g * HD:(g + 1) * HD]).astype(vs_ref.dtype)
        vw_ref[0, g] = with_ones(vw[g * HD:(g + 1) * HD]).astype(vw_ref.dtype)
    lat = col("lat", DSA_KV_RANK)
    lat = lat * lax.rsqrt(jnp.mean(lat * lat, axis=0, keepdims=True) + NORM_EPS) * gkvc_ref[...]
    va = _dot(wkv_ref[...], lat)
    for hh in range(DSA_HEADS):
        va_ref[0, hh] = with_ones(va[hh * HD:(hh + 1) * HD]).astype(va_ref.dtype)
    iw_ref[0] = col("iw", IDX_HEADS)
    br_ref[0] = jax.nn.sigmoid(col("br", 3 * NSA_HEADS))


def _prep(x, mod_sc, mod_sh, g_norm1, w_in, g_kv, w_kv_up, positions, tm):
    B, S, D = x.shape
    T = B * S
    HD, HA, G = HEAD_DIM, DSA_HEADS, NSA_KV_GROUPS
    per_b = S // tm
    sizes = [HA * HD, DSA_KV_RANK, IDX_HEADS * IDX_DIM, IDX_HEADS, IDX_DIM, NSA_HEADS * HD,
             KVB, KVB, KVB, KVB, KVB, KVB, 3 * NSA_HEADS, 2 * D]
    names = ["qa", "lat", "iq", "iw", "ik", "qb", "kc", "vc", "ks", "vs", "kw", "vw", "br", "mg"]
    starts = dict(zip(names, np.concatenate([[0], np.cumsum(sizes)[:-1]]).astype(int)))
    width = dict(zip(names, sizes))
    def seg(name, n):
        w = w_in[:, starts[name]:starts[name] + width[name]]
        return jnp.pad(w, ((0, 0), (0, n - width[name])))
    w_row = jnp.concatenate([seg(n, k) for n, k in ROW_SEGS], axis=1).astype(MXU_DTYPE)
    w_col = jnp.concatenate([seg(n, k) for n, k in COL_SEGS], axis=1).T.astype(MXU_DTYPE)
    w_kk = w_kv_up[:, :HA * HD].astype(MXU_DTYPE)
    w_kv = w_kv_up[:, HA * HD:].T.astype(MXU_DTYPE)
    wide = lambda v: jnp.broadcast_to(v[..., None], v.shape + (tm,))
    column = lambda v: v[..., None]

    def freqs(dim):
        half = dim // ROT_FRACTION // 2
        inv_freq = ROPE_THETA ** (-jnp.arange(half, dtype=F32) / half)
        per_dim = jnp.concatenate([inv_freq, inv_freq, jnp.zeros((dim - 2 * half,), F32)])
        return jnp.tile(per_dim, LANES // dim).reshape(1, LANES), jnp.concatenate([inv_freq, inv_freq])
    f64l, f64pair = freqs(HD)
    f32l, f32pair = freqs(IDX_DIM)
    half64 = HD // ROT_FRACTION // 2
    f64c = wide(f64pair[:half64])
    f32c = wide(f32pair)
    pos_f = positions.astype(F32)

    row_blk = lambda n: pl.BlockSpec((tm, n), lambda i: (i, 0))
    const = lambda shape: pl.BlockSpec(shape, lambda i: (0,) * len(shape))
    per_batch = lambda shape: pl.BlockSpec((1,) + shape, lambda i: (i // per_b,) + (0,) * len(shape))
    tok_cols = lambda rows: pl.BlockSpec((1, rows, tm), lambda i: (i // per_b, 0, i % per_b))
    heads_cols = lambda h, rows: pl.BlockSpec((1, h, rows, tm), lambda i: (i // per_b, 0, 0, i % per_b))
    heads_rows = lambda h, n: pl.BlockSpec((1, h, tm, n), lambda i: (i // per_b, 0, i % per_b, 0))
    bf = MXU_DTYPE
    out_shape = [jax.ShapeDtypeStruct((T, 2 * D), F32),
                 jax.ShapeDtypeStruct((B, HA, HD, S), bf),
                 jax.ShapeDtypeStruct((B, HA, S, HD), bf),
                 jax.ShapeDtypeStruct((B, HA, HD + ONES_ROWS, S), bf),
                 jax.ShapeDtypeStruct((B, IDX_HEADS, IDX_DIM, S), bf),
                 jax.ShapeDtypeStruct((B, IDX_HEADS, S), F32),
                 jax.ShapeDtypeStruct((B, S, IDX_DIM), bf),
                 jax.ShapeDtypeStruct((B, NSA_HEADS, HD, S), bf),
                 jax.ShapeDtypeStruct((B, S, 2 * KVB), F32),
                 jax.ShapeDtypeStruct((B, G, S, HD), bf),
                 jax.ShapeDtypeStruct((B, G, S, HD), bf),
                 jax.ShapeDtypeStruct((B, G, HD + ONES_ROWS, S), bf),
                 jax.ShapeDtypeStruct((B, G, HD + ONES_ROWS, S), bf),
                 jax.ShapeDtypeStruct((B, 3 * NSA_HEADS, S), F32)]
    out_specs = [row_blk(2 * D), heads_cols(HA, HD), heads_rows(HA, HD), heads_cols(HA, HD + ONES_ROWS),
                 heads_cols(IDX_HEADS, IDX_DIM), tok_cols(IDX_HEADS),
                 pl.BlockSpec((1, tm, IDX_DIM), lambda i: (i // per_b, i % per_b, 0)),
                 heads_cols(NSA_HEADS, HD),
                 pl.BlockSpec((1, tm, 2 * KVB), lambda i: (i // per_b, i % per_b, 0)),
                 heads_rows(G, HD), heads_rows(G, HD), heads_cols(G, HD + ONES_ROWS),
                 heads_cols(G, HD + ONES_ROWS),
                 tok_cols(3 * NSA_HEADS)]
    in_specs = [row_blk(D), const((1, D)), per_batch((1, D)), per_batch((1, D)),
                const((D, 1)), per_batch((D, 1)), per_batch((D, 1)),
                const((D, ROW_COLS)), const((COL_ROWS, D)),
                const((1, DSA_KV_RANK)), const((DSA_KV_RANK, 1)),
                const((DSA_KV_RANK, HA * HD)), const((HA * HD, DSA_KV_RANK)),
                row_blk(1), tok_cols(1), const((1, LANES)), const((1, LANES)),
                const((half64, tm)), const((SUBLANES, tm))]
    return pl.pallas_call(
        _prep_kernel,
        grid=(T // tm,),
        in_specs=in_specs,
        out_specs=out_specs,
        out_shape=out_shape,
        compiler_params=_cparams(("arbitrary",)),
    )(x.reshape(T, D), g_norm1.reshape(1, D), mod_sc, mod_sh,
      column(g_norm1), column(mod_sc[:, 0]), column(mod_sh[:, 0]), w_row, w_col,
      g_kv.reshape(1, DSA_KV_RANK), column(g_kv), w_kk, w_kv,
      pos_f.reshape(T, 1), pos_f.reshape(B, 1, S), f64l, f32l, f64c, f32c)


def _cmp_kernel(f_ref, w1_ref, w2_ref, o_ref):
    hid = _dot(f_ref[...], w1_ref[...])
    hid = hid * jax.nn.sigmoid(hid)
    o_ref[...] = _dot(hid, w2_ref[...])


def _compress(tok, pe, w1, w2):
    B, S, G, HD = tok.shape
    r = CMP_LEN // CMP_STRIDE
    n_chunks = S // CMP_STRIDE
    n_cmp = n_chunks - r + 1
    chunks = tok.reshape(B, n_chunks, CMP_STRIDE, G, HD)
    blocks = jnp.concatenate([chunks[:, i:n_cmp + i] for i in range(r)], axis=2)
    blocks = blocks + pe[:, None, :]
    flat = blocks.transpose(0, 1, 3, 2, 4).reshape(B * n_cmp * G, CMP_LEN * HD)
    rows = flat.shape[0]
    tm = 512
    rows_p = -(-rows // tm) * tm
    flat = jnp.pad(flat, ((0, rows_p - rows), (0, 0))).astype(MXU_DTYPE)
    out = pl.pallas_call(
        _cmp_kernel,
        grid=(rows_p // tm,),
        in_specs=[pl.BlockSpec((tm, CMP_LEN * HD), lambda i: (i, 0)),
                  pl.BlockSpec((CMP_LEN * HD, CMP_HIDDEN), lambda i: (0, 0)),
                  pl.BlockSpec((CMP_HIDDEN, HD), lambda i: (0, 0))],
        out_specs=pl.BlockSpec((tm, HD), lambda i: (i, 0)),
        out_shape=jax.ShapeDtypeStruct((rows_p, HD), F32),
        compiler_params=_cparams(("arbitrary",)),
    )(flat, w1.astype(MXU_DTYPE), w2.astype(MXU_DTYPE))
    return out[:rows].reshape(B, n_cmp, G, HD)


def _select_kernel(ik_ref, iqt_ref, iwt_ref, bias_ref, sc_ref, *, TQ, KC, S, K, NBIS):
    qi = pl.program_id(1)
    q0 = qi * TQ
    n_ch = (q0 + TQ + KC - 1) // KC
    n_all = S // KC
    t_row = q0 + lax.broadcasted_iota(jnp.int32, (1, TQ), 1)
    key_iota = lax.broadcasted_iota(jnp.int32, (KC, TQ), 0)
    kf = float(K)
    SUB = LANES
    sub_iota = lax.broadcasted_iota(jnp.int32, (SUB, TQ), 0)

    def score_chunk(c, carry):
        mn, mx = carry
        for u in range(KC // SUB):
            off = pl.multiple_of(c * KC + u * SUB, SUB)
            ikc = ik_ref[0, pl.ds(off, SUB), :]
            acc = jnp.zeros((SUB, TQ), F32)
            for h in range(IDX_HEADS):
                lg = jnp.dot(ikc, iqt_ref[0, h], preferred_element_type=F32)
                acc = acc + jnp.maximum(lg, 0.0) * iwt_ref[0, h:h + 1, :]
            causal = off + sub_iota <= t_row
            sc_ref[pl.ds(off, SUB), :] = jnp.where(causal, acc, NEG_INF)
            rows = lambda x: x.reshape(SUB // SUBLANES, SUBLANES, TQ)
            mn = jnp.minimum(mn, jnp.min(rows(jnp.where(causal, acc, -LOWEST)), axis=0))
            mx = jnp.maximum(mx, jnp.max(rows(jnp.where(causal, acc, LOWEST)), axis=0))
        return mn, mx

    mn, mx = lax.fori_loop(0, n_ch, score_chunk, (jnp.full((SUBLANES, TQ), -LOWEST, F32),
                                                  jnp.full((SUBLANES, TQ), LOWEST, F32)))
    lo = jnp.min(mn, axis=0, keepdims=True)
    hi = jnp.max(mx, axis=0, keepdims=True)

    def chunk(c):
        off = pl.multiple_of(c * KC, KC)
        return sc_ref[pl.ds(off, KC), :], off

    AR = 4 * SUBLANES

    def fold(x, op):
        return op(x.reshape(KC // AR, AR, TQ), axis=0)

    def key_pass(fn, init):
        def body(c, acc):
            xs, off = chunk(c)
            return fn(acc, xs, off)
        return lax.fori_loop(0, n_ch, body, init)

    def count_ge(thr):
        acc = key_pass(lambda a, xs, off: a + fold(jnp.where(xs >= thr, 1.0, 0.0), jnp.sum),
                       jnp.zeros((AR, TQ), F32))
        return jnp.sum(acc, axis=0, keepdims=True)

    def count_gt(thr):
        acc = key_pass(lambda a, xs, off: a + fold(jnp.where(xs > thr, 1.0, 0.0), jnp.sum),
                       jnp.zeros((AR, TQ), F32))
        return jnp.sum(acc, axis=0, keepdims=True)

    def max_where(bound):
        fn = lambda a, xs, off: jnp.maximum(a, fold(jnp.where(xs <= bound, xs, LOWEST), jnp.max))
        acc = key_pass(fn, jnp.full((AR, TQ), LOWEST, F32))
        return jnp.max(acc, axis=0, keepdims=True)

    def count_and_next(v):
        def fn(a, xs, off):
            cnt, nxt = a
            return (cnt + fold(jnp.where(xs >= v, 1.0, 0.0), jnp.sum),
                    jnp.maximum(nxt, fold(jnp.where(xs < v, xs, LOWEST), jnp.max)))
        cnt, nxt = key_pass(fn, (jnp.zeros((AR, TQ), F32), jnp.full((AR, TQ), LOWEST, F32)))
        return jnp.sum(cnt, axis=0, keepdims=True), jnp.max(nxt, axis=0, keepdims=True)

    def bisect(_, carry):
        lo, hi = carry
        mid = 0.5 * (lo + hi)
        ge = count_ge(mid) >= kf
        return jnp.where(ge, mid, lo), jnp.where(ge, hi, mid)

    lo, hi = lax.fori_loop(0, NBIS, bisect, (lo, hi))

    all_keys = jnp.where(t_row < K, 1.0, 0.0)
    v = max_where(hi)
    n_ge, nxt = count_and_next(v)
    done = jnp.maximum(all_keys, jnp.where(n_ge >= kf, 1.0, 0.0))

    def peel_cond(st):
        return jnp.sum(st[1]) < float(TQ)

    def peel_body(st):
        v, done, n_ge, nxt = st
        v = jnp.where(done > 0.5, v, nxt)
        cnt, nxt = count_and_next(v)
        n_ge = jnp.where(done > 0.5, n_ge, cnt)
        done = jnp.maximum(done, jnp.where(cnt >= kf, 1.0, 0.0))
        return v, done, n_ge, nxt

    v, done, n_ge, _ = lax.while_loop(peel_cond, peel_body, (v, done, n_ge, nxt))
    thr = jnp.where(all_keys > 0.5, LOWEST, v)
    has_tie = jnp.max(jnp.where((n_ge > kf) & (all_keys < 0.5), 1.0, 0.0)) > 0.5

    def write(off, sel):
        bias_ref[0, pl.ds(off, KC), :] = jnp.where(sel, 0.0, NEG_INF).astype(bias_ref.dtype)

    def fill_tail():
        def body(c, carry):
            write(pl.multiple_of(c * KC, KC), jnp.zeros((KC, TQ), jnp.bool_))
            return carry
        lax.fori_loop(n_ch, n_all, body, 0)

    @pl.when(jnp.logical_not(has_tie))
    def _():
        def body(c, carry):
            xs, off = chunk(c)
            write(off, (xs >= thr) & ((off + key_iota) <= t_row))
            return carry
        lax.fori_loop(0, n_ch, body, 0)
        fill_tail()

    @pl.when(has_tie)
    def _():
        need = kf - count_gt(thr)
        r_i = lax.broadcasted_iota(jnp.int32, (KC, KC), 0)
        c_i = lax.broadcasted_iota(jnp.int32, (KC, KC), 1)
        lower = jnp.where(c_i <= r_i, 1.0, 0.0).astype(MXU_DTYPE)
        def body(c, seen):
            xs, off = chunk(c)
            causal = (off + key_iota) <= t_row
            eq = jnp.where((xs == thr) & causal, 1.0, 0.0)
            rank = jnp.dot(lower, eq.astype(MXU_DTYPE), preferred_element_type=F32) + seen
            write(off, ((xs > thr) & causal) | ((eq > 0.5) & (rank <= need)))
            return seen + jnp.sum(eq, axis=0, keepdims=True)
        lax.fori_loop(0, n_ch, body, jnp.zeros((1, TQ), F32))
        fill_tail()


def _dsa_select(ik, iqt, iwt, K, TQ, KC):
    B, S, DI = ik.shape
    H = iqt.shape[1]
    kern = functools.partial(_select_kernel, TQ=TQ, KC=KC, S=S, K=K, NBIS=SELECT_BISECTIONS)
    return pl.pallas_call(
        kern,
        grid=(B, S // TQ),
        in_specs=[pl.BlockSpec((1, S, DI), lambda b, q: (b, 0, 0)),
                  pl.BlockSpec((1, H, DI, TQ), lambda b, q: (b, 0, 0, q)),
                  pl.BlockSpec((1, H, TQ), lambda b, q: (b, 0, q))],
        out_specs=pl.BlockSpec((1, S, TQ), lambda b, q: (b, 0, q)),
        out_shape=jax.ShapeDtypeStruct((B, S, S), jnp.bfloat16),
        scratch_shapes=[pltpu.VMEM((S, TQ), F32)],
        compiler_params=_cparams(("arbitrary", "arbitrary")),
    )(ik, iqt, iwt)


def _cmp_attn_kernel(qt_ref, kc_ref, vct_ref, o_ref, bm_ref, *, TQ, NCP, NBP, N_SEL):
    qi = pl.program_id(1)
    q0 = qi * TQ
    G, J = NSA_KV_GROUPS, NSA_REP
    N = J * TQ
    t_lane = q0 + (lax.broadcasted_iota(jnp.int32, (NCP, N), 1) & (TQ - 1))
    cmp_end = lax.broadcasted_iota(jnp.int32, (NCP, N), 0) * CMP_STRIDE + (CMP_LEN - 1)
    vis = cmp_end <= t_lane
    n_i = lax.broadcasted_iota(jnp.int32, (NBP, NCP), 0) * SEL_BLOCK
    c_i = lax.broadcasted_iota(jnp.int32, (NBP, NCP), 1) * CMP_STRIDE
    overlap_t = jnp.where((c_i <= n_i + SEL_BLOCK - 1) & (c_i + CMP_LEN - 1 >= n_i), 1.0, 0.0)
    blk = lax.broadcasted_iota(jnp.int32, (NBP, TQ), 0)
    tq = q0 + lax.broadcasted_iota(jnp.int32, (NBP, TQ), 1)
    cur = tq // SEL_BLOCK
    admissible = blk * SEL_BLOCK <= tq
    forced = (blk == 0) | (blk == cur) | (blk == cur - 1)
    for g in range(G):
        qt = jnp.concatenate([qt_ref[0, g * J + j] for j in range(J)], axis=1)
        s = jnp.where(vis, _dot(kc_ref[0, g], qt), NEG_INF)
        p = jnp.exp(s - jnp.max(s, axis=0, keepdims=True))
        p = jnp.where(vis, p, 0.0)
        p = p / jnp.maximum(jnp.sum(p, axis=0, keepdims=True), TINY)
        o = _dot(vct_ref[0, g], p)
        for j in range(J):
            o_ref[0, g * J + j] = o[:, j * TQ:(j + 1) * TQ]
        psum = p[:, 0:TQ]
        for j in range(1, J):
            psum = psum + p[:, j * TQ:(j + 1) * TQ]
        imp = _dot_exact_lhs(overlap_t, psum)
        score = jnp.where(admissible & forced, FORCED, jnp.where(admissible, imp, NEG_INF))

        def pick(_, st):
            score, sel = st
            m = jnp.max(score, axis=0, keepdims=True)
            first = jnp.min(jnp.where(score == m, blk, NBP), axis=0, keepdims=True)
            hit = blk == first
            return jnp.where(hit, LOWEST, score), jnp.where(hit, 1.0, sel)

        _, sel = lax.fori_loop(0, N_SEL, pick, (score, jnp.zeros((NBP, TQ), F32)))
        bm_ref[0, g] = sel.astype(bm_ref.dtype)


def _cmp_attn(qt, kc, vct, n_sel, NBP, TQ):
    B, H, HD, S = qt.shape
    G, NCP = kc.shape[1], kc.shape[2]
    kern = functools.partial(_cmp_attn_kernel, TQ=TQ, NCP=NCP, NBP=NBP, N_SEL=n_sel)
    return pl.pallas_call(
        kern,
        grid=(B, S // TQ),
        in_specs=[pl.BlockSpec((1, H, HD, TQ), lambda b, q: (b, 0, 0, q)),
                  pl.BlockSpec((1, G, NCP, HD), lambda b, q: (b, 0, 0, 0)),
                  pl.BlockSpec((1, G, HD, NCP), lambda b, q: (b, 0, 0, 0))],
        out_specs=[pl.BlockSpec((1, H, HD, TQ), lambda b, q: (b, 0, 0, q)),
                   pl.BlockSpec((1, G, NBP, TQ), lambda b, q: (b, 0, 0, q))],
        out_shape=[jax.ShapeDtypeStruct((B, H, HD, S), F32),
                   jax.ShapeDtypeStruct((B, G, NBP, S), jnp.bfloat16)],
        compiler_params=_cparams(("arbitrary", "arbitrary")),
    )(qt, kc, vct)


def _flash_kernel(*refs, mode, G, J, TQ, TK, NWIN):
    if mode == "window":
        qt_ref, k_ref, vt_ref, o_ref, m_ref, acc_ref = refs
        x_ref = None
    else:
        qt_ref, k_ref, vt_ref, x_ref, o_ref, m_ref, acc_ref = refs
    HD = HEAD_DIM
    qi = pl.program_id(1)
    kk = pl.program_id(2)
    q0 = qi * TQ
    last = (q0 + TQ - 1) // TK
    if mode == "window":
        ki = last - (NWIN - 1) + kk
        valid = ki >= 0
    else:
        ki = kk
        valid = kk <= last

    @pl.when(kk == 0)
    def _():
        m_ref[...] = jnp.full(m_ref.shape, NEG_INF, F32)
        acc_ref[...] = jnp.zeros(acc_ref.shape, F32)

    @pl.when(valid)
    def _():
        k0 = ki * TK
        kidx = k0 + lax.broadcasted_iota(jnp.int32, (TK, TQ), 0)
        tq = q0 + lax.broadcasted_iota(jnp.int32, (TK, TQ), 1)
        if mode == "mask":
            bias = x_ref[0].astype(F32)
        elif mode == "window":
            bias = jnp.where((kidx <= tq) & (kidx > tq - WINDOW), 0.0, NEG_INF)
        else:
            causal = kidx <= tq
            nbp = x_ref.shape[2]
            blk_of_key = (k0 + lax.broadcasted_iota(jnp.int32, (TK, nbp), 0)) // SEL_BLOCK
            expand_t = jnp.where(lax.broadcasted_iota(jnp.int32, (TK, nbp), 1) == blk_of_key, 1.0, 0.0)
            expand_t = expand_t.astype(MXU_DTYPE)

        def scores(g):
            if mode == "block":
                picked = jnp.dot(expand_t, x_ref[0, g].astype(MXU_DTYPE), preferred_element_type=F32)
                b = jnp.where(causal & (picked > 0.5), 0.0, NEG_INF)
            else:
                b = bias
            if J > 1:
                b = jnp.concatenate([b] * J, axis=1)
                qt = jnp.concatenate([qt_ref[0, g * J + j] for j in range(J)], axis=1)
            else:
                qt = qt_ref[0, g]
            return _dot(k_ref[0, g], qt) + b

        for g in range(G):
            s = scores(g)
            m_prev = m_ref[g]
            m_new = jnp.maximum(m_prev, jnp.max(s, axis=0, keepdims=True))
            alpha = jnp.exp(m_prev - m_new)
            p = jnp.exp((s - m_new).astype(MXU_DTYPE))
            acc_ref[g] = alpha * acc_ref[g] + _dot(vt_ref[0, g], p)
            m_ref[g] = m_new

    @pl.when(kk == pl.num_programs(2) - 1)
    def _():
        for g in range(G):
            o = acc_ref[g, 0:HD] / jnp.maximum(acc_ref[g, HD:HD + 1], TINY)
            for j in range(J):
                o_ref[0, g * J + j] = o[:, j * TQ:(j + 1) * TQ].astype(o_ref.dtype)


def _flash(qt, k, vt, extra, mode, TQ, TK, out_dtype):
    B, H, HD, S = qt.shape
    G = k.shape[1]
    J = H // G
    N = J * TQ
    nq = S // TQ
    last_of = lambda qi: (qi * TQ + TQ - 1) // TK
    if mode == "window":
        NWIN = min((WINDOW - 1 + TK - 1) // TK + 1, S // TK)
        nk = NWIN
        kidx_of = lambda qi, kk: jnp.maximum(last_of(qi) - (NWIN - 1) + kk, 0)
    else:
        NWIN = 0
        nk = S // TK
        kidx_of = lambda qi, kk: jnp.minimum(kk, last_of(qi))
    in_specs = [pl.BlockSpec((1, H, HD, TQ), lambda b, qi, kk: (b, 0, 0, qi)),
                pl.BlockSpec((1, G, TK, HD), lambda b, qi, kk: (b, 0, kidx_of(qi, kk), 0)),
                pl.BlockSpec((1, G, HD + ONES_ROWS, TK), lambda b, qi, kk: (b, 0, 0, kidx_of(qi, kk)))]
    args = [qt, k, vt]
    if mode == "mask":
        in_specs.append(pl.BlockSpec((1, TK, TQ), lambda b, qi, kk: (b, kidx_of(qi, kk), qi)))
        args.append(extra)
    elif mode == "block":
        nbp = extra.shape[2]
        in_specs.append(pl.BlockSpec((1, G, nbp, TQ), lambda b, qi, kk: (b, 0, 0, qi)))
        args.append(extra)
    kern = functools.partial(_flash_kernel, mode=mode, G=G, J=J, TQ=TQ, TK=TK, NWIN=NWIN)
    return pl.pallas_call(
        kern,
        grid=(B, nq, nk),
        in_specs=in_specs,
        out_specs=pl.BlockSpec((1, H, HD, TQ), lambda b, qi, kk: (b, 0, 0, qi)),
        out_shape=jax.ShapeDtypeStruct((B, H, HD, S), out_dtype),
        scratch_shapes=[pltpu.VMEM((G, 1, N), F32),
                        pltpu.VMEM((G, HD + ONES_ROWS, N), F32)],
        compiler_params=_cparams(("arbitrary", "arbitrary", "arbitrary")),
    )(*args)


def _dot_tn(a_t, b):
    return lax.dot_general(a_t.astype(MXU_DTYPE), b.astype(MXU_DTYPE),
                           (((0,), (0,)), ((), ())), preferred_element_type=F32)


def _route(h, wr, br):
    tm = h.shape[0]
    lane = lax.broadcasted_iota(jnp.int32, (tm, LANES), 1)
    h_hi = h.astype(MXU_DTYPE)
    h_lo = (h - h_hi.astype(F32)).astype(MXU_DTYPE)
    w_hi = wr.astype(MXU_DTYPE)
    w_lo = (wr - w_hi.astype(F32)).astype(MXU_DTYPE)
    d = lambda a, b: jnp.dot(a, b, preferred_element_type=F32)
    logits = d(h_hi, w_hi) + d(h_hi, w_lo) + d(h_lo, w_hi) + br
    is_e = lane < N_EXPERTS
    is_g = (lane >= N_EXPERTS) & (lane < N_EXPERTS + N_GROUPS)
    lg = jnp.where(is_g, logits, LOWEST)
    mg = jnp.max(lg, axis=-1, keepdims=True)
    gsel = jnp.min(jnp.where(is_g & (lg == mg), lane - N_EXPERTS, N_GROUPS), axis=-1, keepdims=True)
    pg_sel = 1.0 / jnp.sum(jnp.where(is_g, jnp.exp(lg - mg), 0.0), axis=-1, keepdims=True)
    in_grp = is_e & ((lane // EXPERTS_PER_GROUP) == gsel)
    le = jnp.where(in_grp, logits, LOWEST)
    me = jnp.max(le, axis=-1, keepdims=True)
    ex = jnp.where(in_grp, jnp.exp(le - me), 0.0)
    pe = ex / jnp.sum(ex, axis=-1, keepdims=True)
    pe = jnp.where(in_grp, pe, -1.0)
    p1 = jnp.max(pe, axis=-1, keepdims=True)
    i1 = jnp.min(jnp.where(pe == p1, lane, LANES), axis=-1, keepdims=True)
    pe2 = jnp.where(lane == i1, -1.0, pe)
    p2 = jnp.max(pe2, axis=-1, keepdims=True)
    i2 = jnp.min(jnp.where(pe2 == p2, lane, LANES), axis=-1, keepdims=True)
    tot = p1 + p2
    comb = jnp.where(lane == i1, p1 / tot * pg_sel, 0.0) + jnp.where(lane == i2, p2 / tot * pg_sel, 0.0)
    return jnp.where(lane == N_EXPERTS, gsel.astype(F32), comb)


def _outproj_kernel(x_ref, oa_ref, oc_ref, os_ref, ow_ref, br_ref, mg_ref, wa_ref, wb_ref, wo_ref,
                    gt_ref, g2_ref, sc_ref, sh_ref, wr_ref, brt_ref, o_ref, hr_ref):
    HD = HEAD_DIM
    D = x_ref.shape[1]
    br = br_ref[0]
    parts = []
    for hh in range(NSA_HEADS):
        parts.append(br[3 * hh:3 * hh + 1] * oc_ref[0, hh] + br[3 * hh + 1:3 * hh + 2] * os_ref[0, hh]
                     + br[3 * hh + 2:3 * hh + 3] * ow_ref[0, hh])
    ob_t = jnp.concatenate(parts, axis=0)
    oa_t = jnp.concatenate([oa_ref[0, hh] for hh in range(DSA_HEADS)], axis=0)
    ua = _dot_tn(oa_t, wa_ref[...])
    ub = _dot_tn(ob_t, wb_ref[...])
    mg = mg_ref[...]
    merged = jax.nn.sigmoid(mg[:, :D]) * ua + jax.nn.sigmoid(mg[:, D:]) * ub
    x1 = x_ref[...] + gt_ref[0] * _dot(merged, wo_ref[...])
    o_ref[...] = x1
    h = _rms(x1, g2_ref[...]) * (1.0 + sc_ref[0]) + sh_ref[0]
    hr_ref[:, :D] = h
    hr_ref[:, D:] = _route(h, wr_ref[...], brt_ref[...])


def _out_proj(x2d, oa, oc, os_, ow, br, mg, wa, wb, wo, gt, g2, sc2, sh2, wr, brt, S, tm):
    T, D = x2d.shape
    B, H, HD, _ = oa.shape
    per_b = S // tm
    heads_cols = pl.BlockSpec((1, H, HD, tm), lambda i: (i // per_b, 0, 0, i % per_b))
    per_batch = pl.BlockSpec((1, 1, D), lambda i: (i // per_b, 0, 0))
    rows = lambda n: pl.BlockSpec((tm, n), lambda i: (i, 0))
    const = lambda a, b: pl.BlockSpec((a, b), lambda i: (0, 0))
    return pl.pallas_call(
        _outproj_kernel,
        grid=(T // tm,),
        in_specs=[rows(D), heads_cols, heads_cols, heads_cols, heads_cols,
                  pl.BlockSpec((1, br.shape[1], tm), lambda i: (i // per_b, 0, i % per_b)),
                  rows(2 * D), const(H * HD, D), const(H * HD, D), const(D, D), per_batch,
                  const(1, D), per_batch, per_batch, const(D, LANES), const(1, LANES)],
        out_specs=[rows(D), rows(D + LANES)],
        out_shape=[jax.ShapeDtypeStruct((T, D), F32), jax.ShapeDtypeStruct((T, D + LANES), F32)],
        compiler_params=_cparams(("arbitrary",)),
    )(x2d, oa, oc, os_, ow, br, mg, wa, wb, wo, gt, g2.reshape(1, D), sc2, sh2, wr, brt)


def _row_copy(src_hbm, t, dst, r, sem):
    return pltpu.make_async_copy(src_hbm.at[pl.ds(t, 1)], dst.at[pl.ds(r, 1)], sem)


def _start_rows(idx_ref, base, r0, n, src_hbm, dst, sem):
    for r in range(n):
        _row_copy(src_hbm, idx_ref[base + r0 + r], dst, r0 + r, sem).start()


def _wait_rows(src_hbm, dst, n, sem):
    pltpu.make_async_copy(src_hbm.at[pl.ds(0, n)], dst.at[pl.ds(0, n)], sem).wait()


def _experts_kernel(tg_ref, src_ref, h_hbm, wg_ref, wu_ref, wd_ref, o_ref,
                    hbuf, acc_ref, sems, *, TM):
    i = pl.program_id(0)
    e = pl.program_id(1)
    n_tiles = pl.num_programs(0)
    EPG, EPS = EXPERTS_PER_GROUP, EXPERTS_PER_STEP
    n_steps = EPG // EPS
    part = TM // n_steps
    slot = i % 2

    @pl.when((i == 0) & (e == 0))
    def _():
        def start(c, carry):
            _start_rows(src_ref, 0, pl.multiple_of(c * part, part), part, h_hbm, hbuf.at[0], sems.at[0])
            return carry
        lax.fori_loop(0, n_steps, start, 0)

    @pl.when(e == 0)
    def _():
        _wait_rows(h_hbm, hbuf.at[slot], TM, sems.at[slot])
        acc_ref[...] = jnp.zeros(acc_ref.shape, F32)

    @pl.when(i + 1 < n_tiles)
    def _():
        _start_rows(src_ref, (i + 1) * TM, e * part, part, h_hbm, hbuf.at[1 - slot], sems.at[1 - slot])

    D = acc_ref.shape[1]
    h = hbuf[slot, :, :D].astype(MXU_DTYPE)
    lane = lax.broadcasted_iota(jnp.int32, (TM, LANES), 1)
    route = hbuf[slot, :, D:]
    out = acc_ref[...]
    for q in range(EPS):
        a = jnp.dot(h, wg_ref[q], preferred_element_type=F32)
        u = jnp.dot(h, wu_ref[q], preferred_element_type=F32)
        y = _dot(a * jax.nn.sigmoid(a) * u, wd_ref[q])
        expert = tg_ref[i] * EPG + e * EPS + q
        out = out + jnp.sum(jnp.where(lane == expert, route, 0.0), axis=-1, keepdims=True) * y
    acc_ref[...] = out

    @pl.when(e == pl.num_programs(1) - 1)
    def _():
        o_ref[...] = acc_ref[...]


def _experts(hr, tile_group, src, wg, wu, wd, TM):
    D = wg.shape[1]
    DE = wg.shape[2]
    P = src.shape[0]
    EPG = EXPERTS_PER_GROUP
    EPS = EXPERTS_PER_STEP
    w_idx = lambda i, e, tg, src: (tg[i] * (EPG // EPS) + e, 0, 0)
    kern = functools.partial(_experts_kernel, TM=TM)
    grid_spec = pltpu.PrefetchScalarGridSpec(
        num_scalar_prefetch=2,
        grid=(P // TM, EPG // EPS),
        in_specs=[pl.BlockSpec(memory_space=pl.ANY),
                  pl.BlockSpec((EPS, D, DE), w_idx),
                  pl.BlockSpec((EPS, D, DE), w_idx),
                  pl.BlockSpec((EPS, DE, D), w_idx)],
        out_specs=pl.BlockSpec((TM, D), lambda i, e, tg, src: (i, 0)),
        scratch_shapes=[pltpu.VMEM((2, TM, D + LANES), F32),
                        pltpu.VMEM((TM, D), F32),
                        pltpu.SemaphoreType.DMA((2,))])
    return pl.pallas_call(
        kern,
        grid_spec=grid_spec,
        out_shape=jax.ShapeDtypeStruct((P, D), F32),
        compiler_params=_cparams(("arbitrary", "arbitrary")),
    )(tile_group, src, hr, wg, wu, wd)


def _combine_kernel(pos_ref, x_ref, y_hbm, gt_ref, gf_ref, o_ref, ybuf, sems, *, TM):
    i = pl.program_id(0)
    slot = i % 2
    part = 64

    def start_tile(tile, half):
        def start(c, carry):
            _start_rows(pos_ref, tile * TM, pl.multiple_of(c * part, part), part, y_hbm, ybuf.at[half],
                        sems.at[half])
            return carry
        lax.fori_loop(0, TM // part, start, 0)

    @pl.when(i == 0)
    def _():
        start_tile(0, 0)

    @pl.when(i + 1 < pl.num_programs(0))
    def _():
        start_tile(i + 1, 1 - slot)

    _wait_rows(y_hbm, ybuf.at[slot], TM, sems.at[slot])
    o_ref[...] = _rms(x_ref[...] + gt_ref[0] * ybuf[slot], gf_ref[...])


def _combine(x1, ys, pos, gt, gf, S, tm):
    T, D = x1.shape
    per_b = S // tm
    grid_spec = pltpu.PrefetchScalarGridSpec(
        num_scalar_prefetch=1,
        grid=(T // tm,),
        in_specs=[pl.BlockSpec((tm, D), lambda i, pos: (i, 0)),
                  pl.BlockSpec(memory_space=pl.ANY),
                  pl.BlockSpec((1, 1, D), lambda i, pos: (i // per_b, 0, 0)),
                  pl.BlockSpec((1, D), lambda i, pos: (0, 0))],
        out_specs=pl.BlockSpec((tm, D), lambda i, pos: (i, 0)),
        scratch_shapes=[pltpu.VMEM((2, tm, D), F32), pltpu.SemaphoreType.DMA((2,))])
    return pl.pallas_call(
        functools.partial(_combine_kernel, TM=tm),
        grid_spec=grid_spec,
        out_shape=jax.ShapeDtypeStruct((T, D), F32),
        compiler_params=_cparams(("arbitrary",)),
    )(pos, x1, ys, gt, gf.reshape(1, D))


def _routed_moe(x1, hr, gt, gf, wg, wu, wd, S, TM, combine_rows):
    T, D = x1.shape
    gsel = hr[:, D + N_EXPERTS].astype(jnp.int32)
    onehot = (gsel[:, None] == jnp.arange(N_GROUPS, dtype=jnp.int32)[None, :]).astype(jnp.int32)
    csum = jnp.cumsum(onehot, axis=0)
    rank = jnp.take_along_axis(csum, gsel[:, None], axis=1)[:, 0] - 1
    padded = -(-csum[-1] // TM) * TM
    ends = jnp.cumsum(padded)
    starts = ends - padded
    pos = (jnp.take(starts, gsel) + rank).astype(jnp.int32)
    P = T + N_GROUPS * TM
    src = jnp.zeros((P,), jnp.int32).at[pos].set(jnp.arange(T, dtype=jnp.int32))
    tile_start = jnp.arange(P // TM, dtype=jnp.int32) * TM
    tile_group = jnp.minimum(jnp.sum(tile_start[:, None] >= ends[None, :], axis=1), N_GROUPS - 1)
    ys = _experts(hr, tile_group.astype(jnp.int32), src, wg, wu, wd, TM)
    return _combine(x1, ys, pos, gt, gf, S, combine_rows)


def _layer(x, mod, positions, g_norm1, w_in, g_kv_latent, w_kv_up, pe_cmp_k, pe_cmp_v,
           w_cmp1_k, w_cmp2_k, w_cmp1_v, w_cmp2_v, w_up_a, w_up_b, w_out, g_norm2,
           w_router_group, b_router_group, w_router_expert, b_router_expert,
           w_expert_gate, w_expert_up, w_expert_down, g_out):
    B, S, D = x.shape
    T = B * S
    HD, G = HEAD_DIM, NSA_KV_GROUPS
    kvb = G * HD
    topk_a = min(DSA_TOPK_MAX, S // 4)
    n_sel = min(SEL_COUNT, S // SEL_BLOCK)
    mod6 = mod.reshape(B, 6, 1, D)
    sh1, sc1, gt1, sh2, sc2, gt2 = (mod6[:, i] for i in range(6))

    x2d = x.reshape(T, D)
    t = _tiles(S)
    (mg, qt_a, k_a, vt_a, iqt, iwt, ik, qt_b, kcvc, ks, kw, vst, vwt, br) = _prep(
        x, sc1, sh1, g_norm1, w_in, g_kv_latent, w_kv_up, positions, t.rows)

    kc = _compress(kcvc[..., :kvb].reshape(B, S, G, HD), pe_cmp_k, w_cmp1_k, w_cmp2_k)
    vc = _compress(kcvc[..., kvb:].reshape(B, S, G, HD), pe_cmp_v, w_cmp1_v, w_cmp2_v)
    n_cmp = kc.shape[1]
    ncp = -(-(n_cmp + 1) // LANES) * LANES
    pad_c = lambda t: jnp.pad(t, ((0, 0), (0, ncp - n_cmp), (0, 0), (0, 0))).astype(MXU_DTYPE)
    kc, vct = pad_c(kc).transpose(0, 2, 1, 3), pad_c(vc).transpose(0, 2, 3, 1)

    sel_bias = _dsa_select(ik, iqt, iwt, topk_a, t.select_q, t.select_k)
    o_a = _flash(qt_a, k_a, vt_a, sel_bias, "mask", t.mask_q, t.mask_k, MXU_DTYPE)

    nbp = -(-(S // SEL_BLOCK) // LANES) * LANES
    o_c, blk_mask = _cmp_attn(qt_b, kc, vct, n_sel, nbp, t.nsa_q)
    o_s = _flash(qt_b, ks, vst, blk_mask, "block", t.nsa_q, t.block_k, F32)
    o_w = _flash(qt_b, kw, vwt, None, "window", t.nsa_q, t.nsa_q, F32)

    wr = jnp.concatenate([w_router_expert, w_router_group], axis=1)
    wr = jnp.pad(wr, ((0, 0), (0, LANES - wr.shape[1])))
    brt = jnp.concatenate([b_router_expert, b_router_group])
    brt = jnp.pad(brt, (0, LANES - brt.shape[0])).reshape(1, LANES)
    x1, hr = _out_proj(x2d, o_a, o_c, o_s, o_w, br, mg, w_up_a.astype(MXU_DTYPE),
                              w_up_b.astype(MXU_DTYPE), w_out.astype(MXU_DTYPE), gt1,
                              g_norm2, sc2, sh2, wr, brt, S, t.rows)

    out = _routed_moe(x1, hr, gt2, g_out, w_expert_gate.astype(MXU_DTYPE),
                      w_expert_up.astype(MXU_DTYPE), w_expert_down.astype(MXU_DTYPE), S,
                      t.moe_rows, t.rows)
    return out.reshape(B, S, D)


def kernel(x, c, positions, w_ada, b_ada, g_norm1, w_in, g_kv_latent, w_kv_up, pe_cmp_k, pe_cmp_v,
           w_cmp1_k, w_cmp2_k, w_cmp1_v, w_cmp2_v, w_up_a, w_up_b, w_out, g_norm2, w_router_group,
           b_router_group, w_router_expert, b_router_expert, w_expert_gate, w_expert_up,
           w_expert_down, g_final):
    depth = w_ada.shape[0]
    assert depth == 1, "the fused final norm assumes a single layer"
    mod = _ada_mod(c, w_ada[0], b_ada[0])
    return _layer(x, mod, positions, g_norm1[0], w_in[0], g_kv_latent[0], w_kv_up[0], pe_cmp_k[0],
                  pe_cmp_v[0], w_cmp1_k[0], w_cmp2_k[0], w_cmp1_v[0], w_cmp2_v[0], w_up_a[0],
                  w_up_b[0], w_out[0], g_norm2[0], w_router_group[0], b_router_group[0],
                  w_router_expert[0], b_router_expert[0], w_expert_gate[0], w_expert_up[0],
                  w_expert_down[0], g_final)
```

```python
import functools
from typing import NamedTuple

import numpy as np
import jax
import jax.numpy as jnp
from jax import lax
from jax.experimental import pallas as pl
from jax.experimental.pallas import tpu as pltpu

HEAD_DIM = 64
ROT_FRACTION = 4
ROPE_THETA = 500000.0
DSA_HEADS = 8
DSA_KV_RANK = 128
IDX_HEADS = 8
IDX_DIM = 32
DSA_TOPK_MAX = 256
NSA_HEADS = 8
NSA_KV_GROUPS = 2
NSA_REP = NSA_HEADS // NSA_KV_GROUPS
CMP_LEN = 32
CMP_STRIDE = 16
CMP_HIDDEN = 256
SEL_BLOCK = 64
SEL_COUNT = 16
WINDOW = 512
N_GROUPS = 4
EXPERTS_PER_GROUP = 8
N_EXPERTS = N_GROUPS * EXPERTS_PER_GROUP
D_EXPERT = 256
EXPERTS_PER_STEP = 8
NORM_EPS = 1e-6
NEG_INF = -1e30
TINY = 1e-30
LOWEST = -3.0e38
FORCED = 1e30

LANES = 128
SUBLANES = 8
ONES_ROWS = 16
MXU_DTYPE = jnp.bfloat16
VMEM_LIMIT = 56 * 1024 * 1024

F32 = jnp.float32
SELECT_BISECTIONS = 16


class _Tiles(NamedTuple):
    rows: int
    select_q: int
    select_k: int
    mask_q: int
    mask_k: int
    nsa_q: int
    block_k: int
    moe_rows: int


def _tiles(S):
    cap = lambda n: min(n, S)
    return _Tiles(rows=cap(512), select_q=cap(256), select_k=cap(256), mask_q=cap(1024),
                  mask_k=cap(1024), nsa_q=cap(512), block_k=cap(1024), moe_rows=512)


def _cparams(sem):
    return pltpu.CompilerParams(dimension_semantics=sem, vmem_limit_bytes=VMEM_LIMIT)


def _dot(a, b):
    return jnp.dot(a.astype(MXU_DTYPE), b.astype(MXU_DTYPE), preferred_element_type=F32)


def _dot_exact_lhs(a01, b):
    hi = b.astype(MXU_DTYPE)
    r1 = b - hi.astype(F32)
    mid = r1.astype(MXU_DTYPE)
    lo = (r1 - mid.astype(F32)).astype(MXU_DTYPE)
    a = a01.astype(MXU_DTYPE)
    d = lambda u: jnp.dot(a, u, preferred_element_type=F32)
    return d(hi) + d(mid) + d(lo)


def _rms(x, g):
    return x * lax.rsqrt(jnp.mean(x * x, axis=-1, keepdims=True) + NORM_EPS) * g


def _ada_kernel(c_ref, w_ref, b_ref, o_ref):
    c = c_ref[...]
    cond = c * jax.nn.sigmoid(c)
    o_ref[...] = _dot(cond, w_ref[...]) + b_ref[...]


def _ada_mod(c, w_ada, b_ada):
    B, D = c.shape
    n_out = w_ada.shape[1]
    rows = SUBLANES
    cp = jnp.zeros((rows, D), F32).at[:B].set(c)
    tn = 1024
    out = pl.pallas_call(
        _ada_kernel,
        grid=(n_out // tn,),
        in_specs=[pl.BlockSpec((rows, D), lambda j: (0, 0)),
                  pl.BlockSpec((D, tn), lambda j: (0, j)),
                  pl.BlockSpec((1, tn), lambda j: (0, j))],
        out_specs=pl.BlockSpec((rows, tn), lambda j: (0, j)),
        out_shape=jax.ShapeDtypeStruct((rows, n_out), F32),
        compiler_params=_cparams(("arbitrary",)),
    )(cp, w_ada, b_ada.reshape(1, n_out))
    return out[:B]


KVB = NSA_KV_GROUPS * HEAD_DIM
ROW_SEGS = (("mg", 2048), ("lat", DSA_KV_RANK), ("kc", KVB), ("vc", KVB), ("ks", KVB), ("kw", KVB),
            ("ik", LANES))
COL_SEGS = (("qa", DSA_HEADS * HEAD_DIM), ("qb", NSA_HEADS * HEAD_DIM), ("iq", IDX_HEADS * IDX_DIM),
            ("vs", KVB), ("vw", KVB), ("lat", DSA_KV_RANK), ("iw", IDX_HEADS), ("br", 3 * NSA_HEADS))


def _seg_offsets(segs):
    out, pos = {}, 0
    for name, n in segs:
        out[name] = pos
        pos += n
    return out, pos


ROW_OFF, ROW_COLS = _seg_offsets(ROW_SEGS)
COL_OFF, COL_ROWS = _seg_offsets(COL_SEGS)


def _rot_lanes(x, c, sa, sb, half):
    outs = []
    for j in range(x.shape[1] // LANES):
        xs = x[:, j * LANES:(j + 1) * LANES]
        outs.append(xs * c + pltpu.roll(xs, half, 1) * sa + pltpu.roll(xs, LANES - half, 1) * sb)
    return outs[0] if len(outs) == 1 else jnp.concatenate(outs, axis=1)


def _prep_kernel(x_ref, g_ref, sc_ref, sh_ref, gc_ref, scc_ref, shc_ref, wr_ref, wc_ref,
                 gkv_ref, gkvc_ref, wkk_ref, wkv_ref, posc_ref, posr_ref, f64l_ref, f32l_ref,
                 f64c_ref, f32c_ref,
                 mg_ref, qa_ref, ka_ref, va_ref, iq_ref, iw_ref, ik_ref, qb_ref, kcvc_ref,
                 ks_ref, kw_ref, vs_ref, vw_ref, br_ref):
    HD = HEAD_DIM
    half = HD // ROT_FRACTION // 2
    scale = HD ** -0.5
    h = _rms(x_ref[...], g_ref[...]) * (1.0 + sc_ref[0]) + sh_ref[0]
    pr = _dot(h, wr_ref[...])
    row = lambda name, n: pr[:, ROW_OFF[name]:ROW_OFF[name] + n]
    mg_ref[...] = row("mg", 2048)
    lane = lax.broadcasted_iota(jnp.int32, (x_ref.shape[0], LANES), 1)

    def lane_tables(freq_row, dim, hf):
        ang = posc_ref[...] * freq_row
        c, s = jnp.cos(ang), jnp.sin(ang)
        upper = (lane & (dim - 1)) >= hf
        return c, jnp.where(upper, s, 0.0), jnp.where(upper, 0.0, -s)

    c64, sa64, sb64 = lane_tables(f64l_ref[...], HD, half)
    c32, sa32, sb32 = lane_tables(f32l_ref[...], IDX_DIM, IDX_DIM // ROT_FRACTION // 2)
    ka = _rot_lanes(_dot(_rms(row("lat", DSA_KV_RANK), gkv_ref[...]), wkk_ref[...]), c64, sa64, sb64, half)
    for hh in range(DSA_HEADS):
        ka_ref[0, hh] = ka[:, hh * HD:(hh + 1) * HD].astype(ka_ref.dtype)
    kcvc_ref[0] = jnp.concatenate([_rot_lanes(row("kc", KVB), c64, sa64, sb64, half), row("vc", KVB)], axis=1)
    ks = _rot_lanes(row("ks", KVB), c64, sa64, sb64, half)
    kw = _rot_lanes(row("kw", KVB), c64, sa64, sb64, half)
    for g in range(NSA_KV_GROUPS):
        ks_ref[0, g] = ks[:, g * HD:(g + 1) * HD].astype(ks_ref.dtype)
        kw_ref[0, g] = kw[:, g * HD:(g + 1) * HD].astype(kw_ref.dtype)
    ik = _rot_lanes(row("ik", LANES), c32, sa32, sb32, IDX_DIM // ROT_FRACTION // 2)
    ik_ref[0] = ik[:, :IDX_DIM].astype(ik_ref.dtype)
    xt = x_ref[...].T
    ht = xt * lax.rsqrt(jnp.mean(xt * xt, axis=0, keepdims=True) + NORM_EPS) * gc_ref[...]
    ht = ht * (1.0 + scc_ref[0]) + shc_ref[0]
    pc = _dot(wc_ref[...], ht)
    col = lambda name, n: pc[COL_OFF[name]:COL_OFF[name] + n]
    ang_t = f64c_ref[...] * posr_ref[0]
    cos, sin = jnp.cos(ang_t), jnp.sin(ang_t)

    def rot_rows(blk):
        x1, x2 = blk[0:half], blk[half:2 * half]
        return jnp.concatenate([x1 * cos - x2 * sin, x2 * cos + x1 * sin, blk[2 * half:]], axis=0)

    qa, qb = col("qa", DSA_HEADS * HD), col("qb", NSA_HEADS * HD)
    for hh in range(DSA_HEADS):
        qa_ref[0, hh] = (rot_rows(qa[hh * HD:(hh + 1) * HD]) * scale).astype(qa_ref.dtype)
    for hh in range(NSA_HEADS):
        qb_ref[0, hh] = (rot_rows(qb[hh * HD:(hh + 1) * HD]) * scale).astype(qb_ref.dtype)
    iq = col("iq", IDX_HEADS * IDX_DIM)
    ang32 = f32c_ref[...] * posr_ref[0]
    first = lax.broadcasted_iota(jnp.int32, ang32.shape, 0) < SUBLANES // 2
    c32t, s32t = jnp.cos(ang32), jnp.where(first, -jnp.sin(ang32), jnp.sin(ang32))
    for hh in range(IDX_HEADS):
        blk = iq[hh * IDX_DIM:(hh + 1) * IDX_DIM]
        top = blk[0:SUBLANES]
        top = top * c32t + pltpu.roll(top, SUBLANES // 2, 0) * s32t
        iq_ref[0, hh] = jnp.concatenate([top, blk[SUBLANES:]], axis=0).astype(iq_ref.dtype)
    ones = jnp.ones((ONES_ROWS, xt.shape[1]), F32)
    with_ones = lambda v: jnp.concatenate([v, ones], axis=0)
    vs, vw = col("vs", KVB), col("vw", KVB)
    for g in range(NSA_KV_GROUPS):
        vs_ref[0, g] = with_ones(vs[g * HD:(g + 1) * HD]).astype(vs_ref.dtype)
        vw_ref[0, g] = with_ones(vw[g * HD:(g + 1) * HD]).astype(vw_ref.dtype)
    lat = col("lat", DSA_KV_RANK)
    lat = lat * lax.rsqrt(jnp.mean(lat * lat, axis=0, keepdims=True) + NORM_EPS) * gkvc_ref[...]
    va = _dot(wkv_ref[...], lat)
    for hh in range(DSA_HEADS):
        va_ref[0, hh] = with_ones(va[hh * HD:(hh + 1) * HD]).astype(va_ref.dtype)
    iw_ref[0] = col("iw", IDX_HEADS)
    br_ref[0] = jax.nn.sigmoid(col("br", 3 * NSA_HEADS))


def _prep(x, mod_sc, mod_sh, g_norm1, w_in, g_kv, w_kv_up, positions, tm):
    B, S, D = x.shape
    T = B * S
    HD, HA, G = HEAD_DIM, DSA_HEADS, NSA_KV_GROUPS
    per_b = S // tm
    sizes = [HA * HD, DSA_KV_RANK, IDX_HEADS * IDX_DIM, IDX_HEADS, IDX_DIM, NSA_HEADS * HD,
             KVB, KVB, KVB, KVB, KVB, KVB, 3 * NSA_HEADS, 2 * D]
    names = ["qa", "lat", "iq", "iw", "ik", "qb", "kc", "vc", "ks", "vs", "kw", "vw", "br", "mg"]
    starts = dict(zip(names, np.concatenate([[0], np.cumsum(sizes)[:-1]]).astype(int)))
    width = dict(zip(names, sizes))
    def seg(name, n):
        w = w_in[:, starts[name]:starts[name] + width[name]]
        return jnp.pad(w, ((0, 0), (0, n - width[name])))
    w_row = jnp.concatenate([seg(n, k) for n, k in ROW_SEGS], axis=1).astype(MXU_DTYPE)
    w_col = jnp.concatenate([seg(n, k) for n, k in COL_SEGS], axis=1).T.astype(MXU_DTYPE)
    w_kk = w_kv_up[:, :HA * HD].astype(MXU_DTYPE)
    w_kv = w_kv_up[:, HA * HD:].T.astype(MXU_DTYPE)
    wide = lambda v: jnp.broadcast_to(v[..., None], v.shape + (tm,))
    column = lambda v: v[..., None]

    def freqs(dim):
        half = dim // ROT_FRACTION // 2
        inv_freq = ROPE_THETA ** (-jnp.arange(half, dtype=F32) / half)
        per_dim = jnp.concatenate([inv_freq, inv_freq, jnp.zeros((dim - 2 * half,), F32)])
        return jnp.tile(per_dim, LANES // dim).reshape(1, LANES), jnp.concatenate([inv_freq, inv_freq])
    f64l, f64pair = freqs(HD)
    f32l, f32pair = freqs(IDX_DIM)
    half64 = HD // ROT_FRACTION // 2
    f64c = wide(f64pair[:half64])
    f32c = wide(f32pair)
    pos_f = positions.astype(F32)

    row_blk = lambda n: pl.BlockSpec((tm, n), lambda i: (i, 0))
    const = lambda shape: pl.BlockSpec(shape, lambda i: (0,) * len(shape))
    per_batch = lambda shape: pl.BlockSpec((1,) + shape, lambda i: (i // per_b,) + (0,) * len(shape))
    tok_cols = lambda rows: pl.BlockSpec((1, rows, tm), lambda i: (i // per_b, 0, i % per_b))
    heads_cols = lambda h, rows: pl.BlockSpec((1, h, rows, tm), lambda i: (i // per_b, 0, 0, i % per_b))
    heads_rows = lambda h, n: pl.BlockSpec((1, h, tm, n), lambda i: (i // per_b, 0, i % per_b, 0))
    bf = MXU_DTYPE
    out_shape = [jax.ShapeDtypeStruct((T, 2 * D), F32),
                 jax.ShapeDtypeStruct((B, HA, HD, S), bf),
                 jax.ShapeDtypeStruct((B, HA, S, HD), bf),
                 jax.ShapeDtypeStruct((B, HA, HD + ONES_ROWS, S), bf),
                 jax.ShapeDtypeStruct((B, IDX_HEADS, IDX_DIM, S), bf),
                 jax.ShapeDtypeStruct((B, IDX_HEADS, S), F32),
                 jax.ShapeDtypeStruct((B, S, IDX_DIM), bf),
                 jax.ShapeDtypeStruct((B, NSA_HEADS, HD, S), bf),
                 jax.ShapeDtypeStruct((B, S, 2 * KVB), F32),
                 jax.ShapeDtypeStruct((B, G, S, HD), bf),
                 jax.ShapeDtypeStruct((B, G, S, HD), bf),
                 jax.ShapeDtypeStruct((B, G, HD + ONES_ROWS, S), bf),
                 jax.ShapeDtypeStruct((B, G, HD + ONES_ROWS, S), bf),
                 jax.ShapeDtypeStruct((B, 3 * NSA_HEADS, S), F32)]
    out_specs = [row_blk(2 * D), heads_cols(HA, HD), heads_rows(HA, HD), heads_cols(HA, HD + ONES_ROWS),
                 heads_cols(IDX_HEADS, IDX_DIM), tok_cols(IDX_HEADS),
                 pl.BlockSpec((1, tm, IDX_DIM), lambda i: (i // per_b, i % per_b, 0)),
                 heads_cols(NSA_HEADS, HD),
                 pl.BlockSpec((1, tm, 2 * KVB), lambda i: (i // per_b, i % per_b, 0)),
                 heads_rows(G, HD), heads_rows(G, HD), heads_cols(G, HD + ONES_ROWS),
                 heads_cols(G, HD + ONES_ROWS),
                 tok_cols(3 * NSA_HEADS)]
    in_specs = [row_blk(D), const((1, D)), per_batch((1, D)), per_batch((1, D)),
                const((D, 1)), per_batch((D, 1)), per_batch((D, 1)),
                const((D, ROW_COLS)), const((COL_ROWS, D)),
                const((1, DSA_KV_RANK)), const((DSA_KV_RANK, 1)),
                const((DSA_KV_RANK, HA * HD)), const((HA * HD, DSA_KV_RANK)),
                row_blk(1), tok_cols(1), const((1, LANES)), const((1, LANES)),
                const((half64, tm)), const((SUBLANES, tm))]
    return pl.pallas_call(
        _prep_kernel,
        grid=(T // tm,),
        in_specs=in_specs,
        out_specs=out_specs,
        out_shape=out_shape,
        compiler_params=_cparams(("arbitrary",)),
    )(x.reshape(T, D), g_norm1.reshape(1, D), mod_sc, mod_sh,
      column(g_norm1), column(mod_sc[:, 0]), column(mod_sh[:, 0]), w_row, w_col,
      g_kv.reshape(1, DSA_KV_RANK), column(g_kv), w_kk, w_kv,
      pos_f.reshape(T, 1), pos_f.reshape(B, 1, S), f64l, f32l, f64c, f32c)


def _cmp_kernel(f_ref, w1_ref, w2_ref, o_ref):
    hid = _dot(f_ref[...], w1_ref[...])
    hid = hid * jax.nn.sigmoid(hid)
    o_ref[...] = _dot(hid, w2_ref[...])


def _compress(tok, pe, w1, w2):
    B, S, G, HD = tok.shape
    r = CMP_LEN // CMP_STRIDE
    n_chunks = S // CMP_STRIDE
    n_cmp = n_chunks - r + 1
    chunks = tok.reshape(B, n_chunks, CMP_STRIDE, G, HD)
    blocks = jnp.concatenate([chunks[:, i:n_cmp + i] for i in range(r)], axis=2)
    blocks = blocks + pe[:, None, :]
    flat = blocks.transpose(0, 1, 3, 2, 4).reshape(B * n_cmp * G, CMP_LEN * HD)
    rows = flat.shape[0]
    tm = 512
    rows_p = -(-rows // tm) * tm
    flat = jnp.pad(flat, ((0, rows_p - rows), (0, 0))).astype(MXU_DTYPE)
    out = pl.pallas_call(
        _cmp_kernel,
        grid=(rows_p // tm,),
        in_specs=[pl.BlockSpec((tm, CMP_LEN * HD), lambda i: (i, 0)),
                  pl.BlockSpec((CMP_LEN * HD, CMP_HIDDEN), lambda i: (0, 0)),
                  pl.BlockSpec((CMP_HIDDEN, HD), lambda i: (0, 0))],
        out_specs=pl.BlockSpec((tm, HD), lambda i: (i, 0)),
        out_shape=jax.ShapeDtypeStruct((rows_p, HD), F32),
        compiler_params=_cparams(("arbitrary",)),
    )(flat, w1.astype(MXU_DTYPE), w2.astype(MXU_DTYPE))
    return out[:rows].reshape(B, n_cmp, G, HD)


def _select_kernel(ik_ref, iqt_ref, iwt_ref, bias_ref, sc_ref, *, TQ, KC, S, K, NBIS):
    qi = pl.program_id(1)
    q0 = qi * TQ
    n_ch = (q0 + TQ + KC - 1) // KC
    n_all = S // KC
    t_row = q0 + lax.broadcasted_iota(jnp.int32, (1, TQ), 1)
    key_iota = lax.broadcasted_iota(jnp.int32, (KC, TQ), 0)
    kf = float(K)
    SUB = LANES
    sub_iota = lax.broadcasted_iota(jnp.int32, (SUB, TQ), 0)

    def score_chunk(c, carry):
        mn, mx = carry
        for u in range(KC // SUB):
            off = pl.multiple_of(c * KC + u * SUB, SUB)
            ikc = ik_ref[0, pl.ds(off, SUB), :]
            acc = jnp.zeros((SUB, TQ), F32)
            for h in range(IDX_HEADS):
                lg = jnp.dot(ikc, iqt_ref[0, h], preferred_element_type=F32)
                acc = acc + jnp.maximum(lg, 0.0) * iwt_ref[0, h:h + 1, :]
            causal = off + sub_iota <= t_row
            sc_ref[pl.ds(off, SUB), :] = jnp.where(causal, acc, NEG_INF)
            rows = lambda x: x.reshape(SUB // SUBLANES, SUBLANES, TQ)
            mn = jnp.minimum(mn, jnp.min(rows(jnp.where(causal, acc, -LOWEST)), axis=0))
            mx = jnp.maximum(mx, jnp.max(rows(jnp.where(causal, acc, LOWEST)), axis=0))
        return mn, mx

    mn, mx = lax.fori_loop(0, n_ch, score_chunk, (jnp.full((SUBLANES, TQ), -LOWEST, F32),
                                                  jnp.full((SUBLANES, TQ), LOWEST, F32)))
    lo = jnp.min(mn, axis=0, keepdims=True)
    hi = jnp.max(mx, axis=0, keepdims=True)

    def chunk(c):
        off = pl.multiple_of(c * KC, KC)
        return sc_ref[pl.ds(off, KC), :], off

    AR = 4 * SUBLANES

    def fold(x, op):
        return op(x.reshape(KC // AR, AR, TQ), axis=0)

    def key_pass(fn, init):
        def body(c, acc):
            xs, off = chunk(c)
            return fn(acc, xs, off)
        return lax.fori_loop(0, n_ch, body, init)

    def count_ge(thr):
        acc = key_pass(lambda a, xs, off: a + fold(jnp.where(xs >= thr, 1.0, 0.0), jnp.sum),
                       jnp.zeros((AR, TQ), F32))
        return jnp.sum(acc, axis=0, keepdims=True)

    def count_gt(thr):
        acc = key_pass(lambda a, xs, off: a + fold(jnp.where(xs > thr, 1.0, 0.0), jnp.sum),
                       jnp.zeros((AR, TQ), F32))
        return jnp.sum(acc, axis=0, keepdims=True)

    def max_where(bound):
        fn = lambda a, xs, off: jnp.maximum(a, fold(jnp.where(xs <= bound, xs, LOWEST), jnp.max))
        acc = key_pass(fn, jnp.full((AR, TQ), LOWEST, F32))
        return jnp.max(acc, axis=0, keepdims=True)

    def count_and_next(v):
        def fn(a, xs, off):
            cnt, nxt = a
            return (cnt + fold(jnp.where(xs >= v, 1.0, 0.0), jnp.sum),
                    jnp.maximum(nxt, fold(jnp.where(xs < v, xs, LOWEST), jnp.max)))
        cnt, nxt = key_pass(fn, (jnp.zeros((AR, TQ), F32), jnp.full((AR, TQ), LOWEST, F32)))
        return jnp.sum(cnt, axis=0, keepdims=True), jnp.max(nxt, axis=0, keepdims=True)

    def bisect(_, carry):
        lo, hi = carry
        mid = 0.5 * (lo + hi)
        ge = count_ge(mid) >= kf
        return jnp.where(ge, mid, lo), jnp.where(ge, hi, mid)

    lo, hi = lax.fori_loop(0, NBIS, bisect, (lo, hi))

    all_keys = jnp.where(t_row < K, 1.0, 0.0)
    v = max_where(hi)
    n_ge, nxt = count_and_next(v)
    done = jnp.maximum(all_keys, jnp.where(n_ge >= kf, 1.0, 0.0))

    def peel_cond(st):
        return jnp.sum(st[1]) < float(TQ)

    def peel_body(st):
        v, done, n_ge, nxt = st
        v = jnp.where(done > 0.5, v, nxt)
        cnt, nxt = count_and_next(v)
        n_ge = jnp.where(done > 0.5, n_ge, cnt)
        done = jnp.maximum(done, jnp.where(cnt >= kf, 1.0, 0.0))
        return v, done, n_ge, nxt

    v, done, n_ge, _ = lax.while_loop(peel_cond, peel_body, (v, done, n_ge, nxt))
    thr = jnp.where(all_keys > 0.5, LOWEST, v)
    has_tie = jnp.max(jnp.where((n_ge > kf) & (all_keys < 0.5), 1.0, 0.0)) > 0.5

    def write(off, sel):
        bias_ref[0, pl.ds(off, KC), :] = jnp.where(sel, 0.0, NEG_INF).astype(bias_ref.dtype)

    def fill_tail():
        def body(c, carry):
            write(pl.multiple_of(c * KC, KC), jnp.zeros((KC, TQ), jnp.bool_))
            return carry
        lax.fori_loop(n_ch, n_all, body, 0)

    @pl.when(jnp.logical_not(has_tie))
    def _():
        def body(c, carry):
            xs, off = chunk(c)
            write(off, (xs >= thr) & ((off + key_iota) <= t_row))
            return carry
        lax.fori_loop(0, n_ch, body, 0)
        fill_tail()

    @pl.when(has_tie)
    def _():
        need = kf - count_gt(thr)
        r_i = lax.broadcasted_iota(jnp.int32, (KC, KC), 0)
        c_i = lax.broadcasted_iota(jnp.int32, (KC, KC), 1)
        lower = jnp.where(c_i <= r_i, 1.0, 0.0).astype(MXU_DTYPE)
        def body(c, seen):
            xs, off = chunk(c)
            causal = (off + key_iota) <= t_row
            eq = jnp.where((xs == thr) & causal, 1.0, 0.0)
            rank = jnp.dot(lower, eq.astype(MXU_DTYPE), preferred_element_type=F32) + seen
            write(off, ((xs > thr) & causal) | ((eq > 0.5) & (rank <= need)))
            return seen + jnp.sum(eq, axis=0, keepdims=True)
        lax.fori_loop(0, n_ch, body, jnp.zeros((1, TQ), F32))
        fill_tail()


def _dsa_select(ik, iqt, iwt, K, TQ, KC):
    B, S, DI = ik.shape
    H = iqt.shape[1]
    kern = functools.partial(_select_kernel, TQ=TQ, KC=KC, S=S, K=K, NBIS=SELECT_BISECTIONS)
    return pl.pallas_call(
        kern,
        grid=(B, S // TQ),
        in_specs=[pl.BlockSpec((1, S, DI), lambda b, q: (b, 0, 0)),
                  pl.BlockSpec((1, H, DI, TQ), lambda b, q: (b, 0, 0, q)),
                  pl.BlockSpec((1, H, TQ), lambda b, q: (b, 0, q))],
        out_specs=pl.BlockSpec((1, S, TQ), lambda b, q: (b, 0, q)),
        out_shape=jax.ShapeDtypeStruct((B, S, S), jnp.bfloat16),
        scratch_shapes=[pltpu.VMEM((S, TQ), F32)],
        compiler_params=_cparams(("arbitrary", "arbitrary")),
    )(ik, iqt, iwt)


def _cmp_attn_kernel(qt_ref, kc_ref, vct_ref, o_ref, bm_ref, *, TQ, NCP, NBP, N_SEL):
    qi = pl.program_id(1)
    q0 = qi * TQ
    G, J = NSA_KV_GROUPS, NSA_REP
    N = J * TQ
    t_lane = q0 + (lax.broadcasted_iota(jnp.int32, (NCP, N), 1) & (TQ - 1))
    cmp_end = lax.broadcasted_iota(jnp.int32, (NCP, N), 0) * CMP_STRIDE + (CMP_LEN - 1)
    vis = cmp_end <= t_lane
    n_i = lax.broadcasted_iota(jnp.int32, (NBP, NCP), 0) * SEL_BLOCK
    c_i = lax.broadcasted_iota(jnp.int32, (NBP, NCP), 1) * CMP_STRIDE
    overlap_t = jnp.where((c_i <= n_i + SEL_BLOCK - 1) & (c_i + CMP_LEN - 1 >= n_i), 1.0, 0.0)
    blk = lax.broadcasted_iota(jnp.int32, (NBP, TQ), 0)
    tq = q0 + lax.broadcasted_iota(jnp.int32, (NBP, TQ), 1)
    cur = tq // SEL_BLOCK
    admissible = blk * SEL_BLOCK <= tq
    forced = (blk == 0) | (blk == cur) | (blk == cur - 1)
    for g in range(G):
        qt = jnp.concatenate([qt_ref[0, g * J + j] for j in range(J)], axis=1)
        s = jnp.where(vis, _dot(kc_ref[0, g], qt), NEG_INF)
        p = jnp.exp(s - jnp.max(s, axis=0, keepdims=True))
        p = jnp.where(vis, p, 0.0)
        p = p / jnp.maximum(jnp.sum(p, axis=0, keepdims=True), TINY)
        o = _dot(vct_ref[0, g], p)
        for j in range(J):
            o_ref[0, g * J + j] = o[:, j * TQ:(j + 1) * TQ]
        psum = p[:, 0:TQ]
        for j in range(1, J):
            psum = psum + p[:, j * TQ:(j + 1) * TQ]
        imp = _dot_exact_lhs(overlap_t, psum)
        score = jnp.where(admissible & forced, FORCED, jnp.where(admissible, imp, NEG_INF))

        def pick(_, st):
            score, sel = st
            m = jnp.max(score, axis=0, keepdims=True)
            first = jnp.min(jnp.where(score == m, blk, NBP), axis=0, keepdims=True)
            hit = blk == first
            return jnp.where(hit, LOWEST, score), jnp.where(hit, 1.0, sel)

        _, sel = lax.fori_loop(0, N_SEL, pick, (score, jnp.zeros((NBP, TQ), F32)))
        bm_ref[0, g] = sel.astype(bm_ref.dtype)


def _cmp_attn(qt, kc, vct, n_sel, NBP, TQ):
    B, H, HD, S = qt.shape
    G, NCP = kc.shape[1], kc.shape[2]
    kern = functools.partial(_cmp_attn_kernel, TQ=TQ, NCP=NCP, NBP=NBP, N_SEL=n_sel)
    return pl.pallas_call(
        kern,
        grid=(B, S // TQ),
        in_specs=[pl.BlockSpec((1, H, HD, TQ), lambda b, q: (b, 0, 0, q)),
                  pl.BlockSpec((1, G, NCP, HD), lambda b, q: (b, 0, 0, 0)),
                  pl.BlockSpec((1, G, HD, NCP), lambda b, q: (b, 0, 0, 0))],
        out_specs=[pl.BlockSpec((1, H, HD, TQ), lambda b, q: (b, 0, 0, q)),
                   pl.BlockSpec((1, G, NBP, TQ), lambda b, q: (b, 0, 0, q))],
        out_shape=[jax.ShapeDtypeStruct((B, H, HD, S), F32),
                   jax.ShapeDtypeStruct((B, G, NBP, S), jnp.bfloat16)],
        compiler_params=_cparams(("arbitrary", "arbitrary")),
    )(qt, kc, vct)


def _flash_kernel(*refs, mode, G, J, TQ, TK, NWIN):
    if mode == "window":
        qt_ref, k_ref, vt_ref, o_ref, m_ref, acc_ref = refs
        x_ref = None
    else:
        qt_ref, k_ref, vt_ref, x_ref, o_ref, m_ref, acc_ref = refs
    HD = HEAD_DIM
    qi = pl.program_id(1)
    kk = pl.program_id(2)
    q0 = qi * TQ
    last = (q0 + TQ - 1) // TK
    if mode == "window":
        ki = last - (NWIN - 1) + kk
        valid = ki >= 0
    else:
        ki = kk
        valid = kk <= last

    @pl.when(kk == 0)
    def _():
        m_ref[...] = jnp.full(m_ref.shape, NEG_INF, F32)
        acc_ref[...] = jnp.zeros(acc_ref.shape, F32)

    @pl.when(valid)
    def _():
        k0 = ki * TK
        kidx = k0 + lax.broadcasted_iota(jnp.int32, (TK, TQ), 0)
        tq = q0 + lax.broadcasted_iota(jnp.int32, (TK, TQ), 1)
        if mode == "mask":
            bias = x_ref[0].astype(F32)
        elif mode == "window":
            bias = jnp.where((kidx <= tq) & (kidx > tq - WINDOW), 0.0, NEG_INF)
        else:
            causal = kidx <= tq
            nbp = x_ref.shape[2]
            blk_of_key = (k0 + lax.broadcasted_iota(jnp.int32, (TK, nbp), 0)) // SEL_BLOCK
            expand_t = jnp.where(lax.broadcasted_iota(jnp.int32, (TK, nbp), 1) == blk_of_key, 1.0, 0.0)
            expand_t = expand_t.astype(MXU_DTYPE)

        def scores(g):
            if mode == "block":
                picked = jnp.dot(expand_t, x_ref[0, g].astype(MXU_DTYPE), preferred_element_type=F32)
                b = jnp.where(causal & (picked > 0.5), 0.0, NEG_INF)
            else:
                b = bias
            if J > 1:
                b = jnp.concatenate([b] * J, axis=1)
                qt = jnp.concatenate([qt_ref[0, g * J + j] for j in range(J)], axis=1)
            else:
                qt = qt_ref[0, g]
            return _dot(k_ref[0, g], qt) + b

        for g in range(G):
            s = scores(g)
            m_prev = m_ref[g]
            m_new = jnp.maximum(m_prev, jnp.max(s, axis=0, keepdims=True))
            alpha = jnp.exp(m_prev - m_new)
            p = jnp.exp((s - m_new).astype(MXU_DTYPE))
            acc_ref[g] = alpha * acc_ref[g] + _dot(vt_ref[0, g], p)
            m_ref[g] = m_new

    @pl.when(kk == pl.num_programs(2) - 1)
    def _():
        for g in range(G):
            o = acc_ref[g, 0:HD] / jnp.maximum(acc_ref[g, HD:HD + 1], TINY)
            for j in range(J):
                o_ref[0, g * J + j] = o[:, j * TQ:(j + 1) * TQ].astype(o_ref.dtype)


def _flash(qt, k, vt, extra, mode, TQ, TK, out_dtype):
    B, H, HD, S = qt.shape
    G = k.shape[1]
    J = H // G
    N = J * TQ
    nq = S // TQ
    last_of = lambda qi: (qi * TQ + TQ - 1) // TK
    if mode == "window":
        NWIN = min((WINDOW - 1 + TK - 1) // TK + 1, S // TK)
        nk = NWIN
        kidx_of = lambda qi, kk: jnp.maximum(last_of(qi) - (NWIN - 1) + kk, 0)
    else:
        NWIN = 0
        nk = S // TK
        kidx_of = lambda qi, kk: jnp.minimum(kk, last_of(qi))
    in_specs = [pl.BlockSpec((1, H, HD, TQ), lambda b, qi, kk: (b, 0, 0, qi)),
                pl.BlockSpec((1, G, TK, HD), lambda b, qi, kk: (b, 0, kidx_of(qi, kk), 0)),
                pl.BlockSpec((1, G, HD + ONES_ROWS, TK), lambda b, qi, kk: (b, 0, 0, kidx_of(qi, kk)))]
    args = [qt, k, vt]
    if mode == "mask":
        in_specs.append(pl.BlockSpec((1, TK, TQ), lambda b, qi, kk: (b, kidx_of(qi, kk), qi)))
        args.append(extra)
    elif mode == "block":
        nbp = extra.shape[2]
        in_specs.append(pl.BlockSpec((1, G, nbp, TQ), lambda b, qi, kk: (b, 0, 0, qi)))
        args.append(extra)
    kern = functools.partial(_flash_kernel, mode=mode, G=G, J=J, TQ=TQ, TK=TK, NWIN=NWIN)
    return pl.pallas_call(
        kern,
        grid=(B, nq, nk),
        in_specs=in_specs,
        out_specs=pl.BlockSpec((1, H, HD, TQ), lambda b, qi, kk: (b, 0, 0, qi)),
        out_shape=jax.ShapeDtypeStruct((B, H, HD, S), out_dtype),
        scratch_shapes=[pltpu.VMEM((G, 1, N), F32),
                        pltpu.VMEM((G, HD + ONES_ROWS, N), F32)],
        compiler_params=_cparams(("arbitrary", "arbitrary", "arbitrary")),
    )(*args)


def _dot_tn(a_t, b):
    return lax.dot_general(a_t.astype(MXU_DTYPE), b.astype(MXU_DTYPE),
                           (((0,), (0,)), ((), ())), preferred_element_type=F32)


def _route(h, wr, br):
    tm = h.shape[0]
    lane = lax.broadcasted_iota(jnp.int32, (tm, LANES), 1)
    h_hi = h.astype(MXU_DTYPE)
    h_lo = (h - h_hi.astype(F32)).astype(MXU_DTYPE)
    w_hi = wr.astype(MXU_DTYPE)
    w_lo = (wr - w_hi.astype(F32)).astype(MXU_DTYPE)
    d = lambda a, b: jnp.dot(a, b, preferred_element_type=F32)
    logits = d(h_hi, w_hi) + d(h_hi, w_lo) + d(h_lo, w_hi) + br
    is_e = lane < N_EXPERTS
    is_g = (lane >= N_EXPERTS) & (lane < N_EXPERTS + N_GROUPS)
    lg = jnp.where(is_g, logits, LOWEST)
    mg = jnp.max(lg, axis=-1, keepdims=True)
    gsel = jnp.min(jnp.where(is_g & (lg == mg), lane - N_EXPERTS, N_GROUPS), axis=-1, keepdims=True)
    pg_sel = 1.0 / jnp.sum(jnp.where(is_g, jnp.exp(lg - mg), 0.0), axis=-1, keepdims=True)
    in_grp = is_e & ((lane // EXPERTS_PER_GROUP) == gsel)
    le = jnp.where(in_grp, logits, LOWEST)
    me = jnp.max(le, axis=-1, keepdims=True)
    ex = jnp.where(in_grp, jnp.exp(le - me), 0.0)
    pe = ex / jnp.sum(ex, axis=-1, keepdims=True)
    pe = jnp.where(in_grp, pe, -1.0)
    p1 = jnp.max(pe, axis=-1, keepdims=True)
    i1 = jnp.min(jnp.where(pe == p1, lane, LANES), axis=-1, keepdims=True)
    pe2 = jnp.where(lane == i1, -1.0, pe)
    p2 = jnp.max(pe2, axis=-1, keepdims=True)
    i2 = jnp.min(jnp.where(pe2 == p2, lane, LANES), axis=-1, keepdims=True)
    tot = p1 + p2
    comb = jnp.where(lane == i1, p1 / tot * pg_sel, 0.0) + jnp.where(lane == i2, p2 / tot * pg_sel, 0.0)
    return jnp.where(lane == N_EXPERTS, gsel.astype(F32), comb)


def _outproj_kernel(x_ref, oa_ref, oc_ref, os_ref, ow_ref, br_ref, mg_ref, wa_ref, wb_ref, wo_ref,
                    gt_ref, g2_ref, sc_ref, sh_ref, wr_ref, brt_ref, o_ref, hr_ref):
    HD = HEAD_DIM
    D = x_ref.shape[1]
    br = br_ref[0]
    parts = []
    for hh in range(NSA_HEADS):
        parts.append(br[3 * hh:3 * hh + 1] * oc_ref[0, hh] + br[3 * hh + 1:3 * hh + 2] * os_ref[0, hh]
                     + br[3 * hh + 2:3 * hh + 3] * ow_ref[0, hh])
    ob_t = jnp.concatenate(parts, axis=0)
    oa_t = jnp.concatenate([oa_ref[0, hh] for hh in range(DSA_HEADS)], axis=0)
    ua = _dot_tn(oa_t, wa_ref[...])
    ub = _dot_tn(ob_t, wb_ref[...])
    mg = mg_ref[...]
    merged = jax.nn.sigmoid(mg[:, :D]) * ua + jax.nn.sigmoid(mg[:, D:]) * ub
    x1 = x_ref[...] + gt_ref[0] * _dot(merged, wo_ref[...])
    o_ref[...] = x1
    h = _rms(x1, g2_ref[...]) * (1.0 + sc_ref[0]) + sh_ref[0]
    hr_ref[:, :D] = h
    hr_ref[:, D:] = _route(h, wr_ref[...], brt_ref[...])


def _out_proj(x2d, oa, oc, os_, ow, br, mg, wa, wb, wo, gt, g2, sc2, sh2, wr, brt, S, tm):
    T, D = x2d.shape
    B, H, HD, _ = oa.shape
    per_b = S // tm
    heads_cols = pl.BlockSpec((1, H, HD, tm), lambda i: (i // per_b, 0, 0, i % per_b))
    per_batch = pl.BlockSpec((1, 1, D), lambda i: (i // per_b, 0, 0))
    rows = lambda n: pl.BlockSpec((tm, n), lambda i: (i, 0))
    const = lambda a, b: pl.BlockSpec((a, b), lambda i: (0, 0))
    return pl.pallas_call(
        _outproj_kernel,
        grid=(T // tm,),
        in_specs=[rows(D), heads_cols, heads_cols, heads_cols, heads_cols,
                  pl.BlockSpec((1, br.shape[1], tm), lambda i: (i // per_b, 0, i % per_b)),
                  rows(2 * D), const(H * HD, D), const(H * HD, D), const(D, D), per_batch,
                  const(1, D), per_batch, per_batch, const(D, LANES), const(1, LANES)],
        out_specs=[rows(D), rows(D + LANES)],
        out_shape=[jax.ShapeDtypeStruct((T, D), F32), jax.ShapeDtypeStruct((T, D + LANES), F32)],
        compiler_params=_cparams(("arbitrary",)),
    )(x2d, oa, oc, os_, ow, br, mg, wa, wb, wo, gt, g2.reshape(1, D), sc2, sh2, wr, brt)


def _row_copy(src_hbm, t, dst, r, sem):
    return pltpu.make_async_copy(src_hbm.at[pl.ds(t, 1)], dst.at[pl.ds(r, 1)], sem)


def _start_rows(idx_ref, base, r0, n, src_hbm, dst, sem):
    for r in range(n):
        _row_copy(src_hbm, idx_ref[base + r0 + r], dst, r0 + r, sem).start()


def _wait_rows(src_hbm, dst, n, sem):
    pltpu.make_async_copy(src_hbm.at[pl.ds(0, n)], dst.at[pl.ds(0, n)], sem).wait()


def _experts_kernel(tg_ref, src_ref, h_hbm, wg_ref, wu_ref, wd_ref, o_ref,
                    hbuf, acc_ref, sems, *, TM):
    i = pl.program_id(0)
    e = pl.program_id(1)
    n_tiles = pl.num_programs(0)
    EPG, EPS = EXPERTS_PER_GROUP, EXPERTS_PER_STEP
    n_steps = EPG // EPS
    part = TM // n_steps
    slot = i % 2

    @pl.when((i == 0) & (e == 0))
    def _():
        def start(c, carry):
            _start_rows(src_ref, 0, pl.multiple_of(c * part, part), part, h_hbm, hbuf.at[0], sems.at[0])
            return carry
        lax.fori_loop(0, n_steps, start, 0)

    @pl.when(e == 0)
    def _():
        _wait_rows(h_hbm, hbuf.at[slot], TM, sems.at[slot])
        acc_ref[...] = jnp.zeros(acc_ref.shape, F32)

    @pl.when(i + 1 < n_tiles)
    def _():
        _start_rows(src_ref, (i + 1) * TM, e * part, part, h_hbm, hbuf.at[1 - slot], sems.at[1 - slot])

    D = acc_ref.shape[1]
    h = hbuf[slot, :, :D].astype(MXU_DTYPE)
    lane = lax.broadcasted_iota(jnp.int32, (TM, LANES), 1)
    route = hbuf[slot, :, D:]
    out = acc_ref[...]
    for q in range(EPS):
        a = jnp.dot(h, wg_ref[q], preferred_element_type=F32)
        u = jnp.dot(h, wu_ref[q], preferred_element_type=F32)
        y = _dot(a * jax.nn.sigmoid(a) * u, wd_ref[q])
        expert = tg_ref[i] * EPG + e * EPS + q
        out = out + jnp.sum(jnp.where(lane == expert, route, 0.0), axis=-1, keepdims=True) * y
    acc_ref[...] = out

    @pl.when(e == pl.num_programs(1) - 1)
    def _():
        o_ref[...] = acc_ref[...]


def _experts(hr, tile_group, src, wg, wu, wd, TM):
    D = wg.shape[1]
    DE = wg.shape[2]
    P = src.shape[0]
    EPG = EXPERTS_PER_GROUP
    EPS = EXPERTS_PER_STEP
    w_idx = lambda i, e, tg, src: (tg[i] * (EPG // EPS) + e, 0, 0)
    kern = functools.partial(_experts_kernel, TM=TM)
    grid_spec = pltpu.PrefetchScalarGridSpec(
        num_scalar_prefetch=2,
        grid=(P // TM, EPG // EPS),
        in_specs=[pl.BlockSpec(memory_space=pl.ANY),
                  pl.BlockSpec((EPS, D, DE), w_idx),
                  pl.BlockSpec((EPS, D, DE), w_idx),
                  pl.BlockSpec((EPS, DE, D), w_idx)],
        out_specs=pl.BlockSpec((TM, D), lambda i, e, tg, src: (i, 0)),
        scratch_shapes=[pltpu.VMEM((2, TM, D + LANES), F32),
                        pltpu.VMEM((TM, D), F32),
                        pltpu.SemaphoreType.DMA((2,))])
    return pl.pallas_call(
        kern,
        grid_spec=grid_spec,
        out_shape=jax.ShapeDtypeStruct((P, D), F32),
        compiler_params=_cparams(("arbitrary", "arbitrary")),
    )(tile_group, src, hr, wg, wu, wd)


def _combine_kernel(pos_ref, x_ref, y_hbm, gt_ref, gf_ref, o_ref, ybuf, sems, *, TM):
    i = pl.program_id(0)
    slot = i % 2
    part = 64

    def start_tile(tile, half):
        def start(c, carry):
            _start_rows(pos_ref, tile * TM, pl.multiple_of(c * part, part), part, y_hbm, ybuf.at[half],
                        sems.at[half])
            return carry
        lax.fori_loop(0, TM // part, start, 0)

    @pl.when(i == 0)
    def _():
        start_tile(0, 0)

    @pl.when(i + 1 < pl.num_programs(0))
    def _():
        start_tile(i + 1, 1 - slot)

    _wait_rows(y_hbm, ybuf.at[slot], TM, sems.at[slot])
    o_ref[...] = _rms(x_ref[...] + gt_ref[0] * ybuf[slot], gf_ref[...])


def _combine(x1, ys, pos, gt, gf, S, tm):
    T, D = x1.shape
    per_b = S // tm
    grid_spec = pltpu.PrefetchScalarGridSpec(
        num_scalar_prefetch=1,
        grid=(T // tm,),
        in_specs=[pl.BlockSpec((tm, D), lambda i, pos: (i, 0)),
                  pl.BlockSpec(memory_space=pl.ANY),
                  pl.BlockSpec((1, 1, D), lambda i, pos: (i // per_b, 0, 0)),
                  pl.BlockSpec((1, D), lambda i, pos: (0, 0))],
        out_specs=pl.BlockSpec((tm, D), lambda i, pos: (i, 0)),
        scratch_shapes=[pltpu.VMEM((2, tm, D), F32), pltpu.SemaphoreType.DMA((2,))])
    return pl.pallas_call(
        functools.partial(_combine_kernel, TM=tm),
        grid_spec=grid_spec,
        out_shape=jax.ShapeDtypeStruct((T, D), F32),
        compiler_params=_cparams(("arbitrary",)),
    )(pos, x1, ys, gt, gf.reshape(1, D))


def _routed_moe(x1, hr, gt, gf, wg, wu, wd, S, TM, combine_rows):
    T, D = x1.shape
    gsel = hr[:, D + N_EXPERTS].astype(jnp.int32)
    onehot = (gsel[:, None] == jnp.arange(N_GROUPS, dtype=jnp.int32)[None, :]).astype(jnp.int32)
    csum = jnp.cumsum(onehot, axis=0)
    rank = jnp.take_along_axis(csum, gsel[:, None], axis=1)[:, 0] - 1
    padded = -(-csum[-1] // TM) * TM
    ends = jnp.cumsum(padded)
    starts = ends - padded
    pos = (jnp.take(starts, gsel) + rank).astype(jnp.int32)
    P = T + N_GROUPS * TM
    src = jnp.zeros((P,), jnp.int32).at[pos].set(jnp.arange(T, dtype=jnp.int32))
    tile_start = jnp.arange(P // TM, dtype=jnp.int32) * TM
    tile_group = jnp.minimum(jnp.sum(tile_start[:, None] >= ends[None, :], axis=1), N_GROUPS - 1)
    ys = _experts(hr, tile_group.astype(jnp.int32), src, wg, wu, wd, TM)
    return _combine(x1, ys, pos, gt, gf, S, combine_rows)


def _layer(x, mod, positions, g_norm1, w_in, g_kv_latent, w_kv_up, pe_cmp_k, pe_cmp_v,
           w_cmp1_k, w_cmp2_k, w_cmp1_v, w_cmp2_v, w_up_a, w_up_b, w_out, g_norm2,
           w_router_group, b_router_group, w_router_expert, b_router_expert,
           w_expert_gate, w_expert_up, w_expert_down, g_out):
    B, S, D = x.shape
    T = B * S
    HD, G = HEAD_DIM, NSA_KV_GROUPS
    kvb = G * HD
    topk_a = min(DSA_TOPK_MAX, S // 4)
    n_sel = min(SEL_COUNT, S // SEL_BLOCK)
    mod6 = mod.reshape(B, 6, 1, D)
    sh1, sc1, gt1, sh2, sc2, gt2 = (mod6[:, i] for i in range(6))

    x2d = x.reshape(T, D)
    t = _tiles(S)
    (mg, qt_a, k_a, vt_a, iqt, iwt, ik, qt_b, kcvc, ks, kw, vst, vwt, br) = _prep(
        x, sc1, sh1, g_norm1, w_in, g_kv_latent, w_kv_up, positions, t.rows)

    kc = _compress(kcvc[..., :kvb].reshape(B, S, G, HD), pe_cmp_k, w_cmp1_k, w_cmp2_k)
    vc = _compress(kcvc[..., kvb:].reshape(B, S, G, HD), pe_cmp_v, w_cmp1_v, w_cmp2_v)
    n_cmp = kc.shape[1]
    ncp = -(-(n_cmp + 1) // LANES) * LANES
    pad_c = lambda t: jnp.pad(t, ((0, 0), (0, ncp - n_cmp), (0, 0), (0, 0))).astype(MXU_DTYPE)
    kc, vct = pad_c(kc).transpose(0, 2, 1, 3), pad_c(vc).transpose(0, 2, 3, 1)

    sel_bias = _dsa_select(ik, iqt, iwt, topk_a, t.select_q, t.select_k)
    o_a = _flash(qt_a, k_a, vt_a, sel_bias, "mask", t.mask_q, t.mask_k, MXU_DTYPE)

    nbp = -(-(S // SEL_BLOCK) // LANES) * LANES
    o_c, blk_mask = _cmp_attn(qt_b, kc, vct, n_sel, nbp, t.nsa_q)
    o_s = _flash(qt_b, ks, vst, blk_mask, "block", t.nsa_q, t.block_k, F32)
    o_w = _flash(qt_b, kw, vwt, None, "window", t.nsa_q, t.nsa_q, F32)

    wr = jnp.concatenate([w_router_expert, w_router_group], axis=1)
    wr = jnp.pad(wr, ((0, 0), (0, LANES - wr.shape[1])))
    brt = jnp.concatenate([b_router_expert, b_router_group])
    brt = jnp.pad(brt, (0, LANES - brt.shape[0])).reshape(1, LANES)
    x1, hr = _out_proj(x2d, o_a, o_c, o_s, o_w, br, mg, w_up_a.astype(MXU_DTYPE),
                              w_up_b.astype(MXU_DTYPE), w_out.astype(MXU_DTYPE), gt1,
                              g_norm2, sc2, sh2, wr, brt, S, t.rows)

    out = _routed_moe(x1, hr, gt2, g_out, w_expert_gate.astype(MXU_DTYPE),
                      w_expert_up.astype(MXU_DTYPE), w_expert_down.astype(MXU_DTYPE), S,
                      t.moe_rows, t.rows)
    return out.reshape(B, S, D)


def kernel(x, c, positions, w_ada, b_ada, g_norm1, w_in, g_kv_latent, w_kv_up, pe_cmp_k, pe_cmp_v,
           w_cmp1_k, w_cmp2_k, w_cmp1_v, w_cmp2_v, w_up_a, w_up_b, w_out, g_norm2, w_router_group,
           b_router_group, w_router_expert, b_router_expert, w_expert_gate, w_expert_up,
           w_expert_down, g_final):
    depth = w_ada.shape[0]
    assert depth == 1, "the fused final norm assumes a single layer"
    mod = _ada_mod(c, w_ada[0], b_ada[0])
    return _layer(x, mod, positions, g_norm1[0], w_in[0], g_kv_latent[0], w_kv_up[0], pe_cmp_k[0],
                  pe_cmp_v[0], w_cmp1_k[0], w_cmp2_k[0], w_cmp1_v[0], w_cmp2_v[0], w_up_a[0],
                  w_up_b[0], w_out[0], g_norm2[0], w_router_group[0], b_router_group[0],
                  w_router_expert[0], b_router_expert[0], w_expert_gate[0], w_expert_up[0],
                  w_expert_down[0], g_final)
```

```python
import functools
from typing import NamedTuple

import numpy as np
import jax
import jax.numpy as jnp
from jax import lax
from jax.experimental import pallas as pl
from jax.experimental.pallas import tpu as pltpu

HEAD_DIM = 64
ROT_FRACTION = 4
ROPE_THETA = 500000.0
DSA_HEADS = 8
DSA_KV_RANK = 128
IDX_HEADS = 8
IDX_DIM = 32
DSA_TOPK_MAX = 256
NSA_HEADS = 8
NSA_KV_GROUPS = 2
NSA_REP = NSA_HEADS // NSA_KV_GROUPS
CMP_LEN = 32
CMP_STRIDE = 16
CMP_HIDDEN = 256
SEL_BLOCK = 64
SEL_COUNT = 16
WINDOW = 512
N_GROUPS = 4
EXPERTS_PER_GROUP = 8
N_EXPERTS = N_GROUPS * EXPERTS_PER_GROUP
D_EXPERT = 256
EXPERTS_PER_STEP = 8
NORM_EPS = 1e-6
NEG_INF = -1e30
TINY = 1e-30
LOWEST = -3.0e38
FORCED = 1e30

LANES = 128
SUBLANES = 8
ONES_ROWS = 16
MXU_DTYPE = jnp.bfloat16
VMEM_LIMIT = 56 * 1024 * 1024

F32 = jnp.float32
SELECT_BISECTIONS = 16


class _Tiles(NamedTuple):
    rows: int
    select_q: int
    select_k: int
    mask_q: int
    mask_k: int
    nsa_q: int
    block_k: int
    moe_rows: int


def _tiles(S):
    cap = lambda n: min(n, S)
    return _Tiles(rows=cap(512), select_q=cap(256), select_k=cap(256), mask_q=cap(1024),
                  mask_k=cap(1024), nsa_q=cap(512), block_k=cap(1024), moe_rows=512)


def _cparams(sem):
    return pltpu.CompilerParams(dimension_semantics=sem, vmem_limit_bytes=VMEM_LIMIT)


def _dot(a, b):
    return jnp.dot(a.astype(MXU_DTYPE), b.astype(MXU_DTYPE), preferred_element_type=F32)


def _dot_exact_lhs(a01, b):
    hi = b.astype(MXU_DTYPE)
    r1 = b - hi.astype(F32)
    mid = r1.astype(MXU_DTYPE)
    lo = (r1 - mid.astype(F32)).astype(MXU_DTYPE)
    a = a01.astype(MXU_DTYPE)
    d = lambda u: jnp.dot(a, u, preferred_element_type=F32)
    return d(hi) + d(mid) + d(lo)


def _rms(x, g):
    return x * lax.rsqrt(jnp.mean(x * x, axis=-1, keepdims=True) + NORM_EPS) * g


def _ada_kernel(c_ref, w_ref, b_ref, o_ref):
    c = c_ref[...]
    cond = c * jax.nn.sigmoid(c)
    o_ref[...] = _dot(cond, w_ref[...]) + b_ref[...]


def _ada_mod(c, w_ada, b_ada):
    B, D = c.shape
    n_out = w_ada.shape[1]
    rows = SUBLANES
    cp = jnp.zeros((rows, D), F32).at[:B].set(c)
    tn = 1024
    out = pl.pallas_call(
        _ada_kernel,
        grid=(n_out // tn,),
        in_specs=[pl.BlockSpec((rows, D), lambda j: (0, 0)),
                  pl.BlockSpec((D, tn), lambda j: (0, j)),
                  pl.BlockSpec((1, tn), lambda j: (0, j))],
        out_specs=pl.BlockSpec((rows, tn), lambda j: (0, j)),
        out_shape=jax.ShapeDtypeStruct((rows, n_out), F32),
        compiler_params=_cparams(("arbitrary",)),
    )(cp, w_ada, b_ada.reshape(1, n_out))
    return out[:B]


KVB = NSA_KV_GROUPS * HEAD_DIM
ROW_SEGS = (("mg", 2048), ("lat", DSA_KV_RANK), ("kc", KVB), ("vc", KVB), ("ks", KVB), ("kw", KVB),
            ("ik", LANES))
COL_SEGS = (("qa", DSA_HEADS * HEAD_DIM), ("qb", NSA_HEADS * HEAD_DIM), ("iq", IDX_HEADS * IDX_DIM),
            ("vs", KVB), ("vw", KVB), ("lat", DSA_KV_RANK), ("iw", IDX_HEADS), ("br", 3 * NSA_HEADS))


def _seg_offsets(segs):
    out, pos = {}, 0
    for name, n in segs:
        out[name] = pos
        pos += n
    return out, pos


ROW_OFF, ROW_COLS = _seg_offsets(ROW_SEGS)
COL_OFF, COL_ROWS = _seg_offsets(COL_SEGS)


def _rot_lanes(x, c, sa, sb, half):
    outs = []
    for j in range(x.shape[1] // LANES):
        xs = x[:, j * LANES:(j + 1) * LANES]
        outs.append(xs * c + pltpu.roll(xs, half, 1) * sa + pltpu.roll(xs, LANES - half, 1) * sb)
    return outs[0] if len(outs) == 1 else jnp.concatenate(outs, axis=1)


def _prep_kernel(x_ref, g_ref, sc_ref, sh_ref, gc_ref, scc_ref, shc_ref, wr_ref, wc_ref,
                 gkv_ref, gkvc_ref, wkk_ref, wkv_ref, posc_ref, posr_ref, f64l_ref, f32l_ref,
                 f64c_ref, f32c_ref,
                 mg_ref, qa_ref, ka_ref, va_ref, iq_ref, iw_ref, ik_ref, qb_ref, kcvc_ref,
                 ks_ref, kw_ref, vs_ref, vw_ref, br_ref):
    HD = HEAD_DIM
    half = HD // ROT_FRACTION // 2
    scale = HD ** -0.5
    h = _rms(x_ref[...], g_ref[...]) * (1.0 + sc_ref[0]) + sh_ref[0]
    pr = _dot(h, wr_ref[...])
    row = lambda name, n: pr[:, ROW_OFF[name]:ROW_OFF[name] + n]
    mg_ref[...] = row("mg", 2048)
    lane = lax.broadcasted_iota(jnp.int32, (x_ref.shape[0], LANES), 1)

    def lane_tables(freq_row, dim, hf):
        ang = posc_ref[...] * freq_row
        c, s = jnp.cos(ang), jnp.sin(ang)
        upper = (lane & (dim - 1)) >= hf
        return c, jnp.where(upper, s, 0.0), jnp.where(upper, 0.0, -s)

    c64, sa64, sb64 = lane_tables(f64l_ref[...], HD, half)
    c32, sa32, sb32 = lane_tables(f32l_ref[...], IDX_DIM, IDX_DIM // ROT_FRACTION // 2)
    ka = _rot_lanes(_dot(_rms(row("lat", DSA_KV_RANK), gkv_ref[...]), wkk_ref[...]), c64, sa64, sb64, half)
    for hh in range(DSA_HEADS):
        ka_ref[0, hh] = ka[:, hh * HD:(hh + 1) * HD].astype(ka_ref.dtype)
    kcvc_ref[0] = jnp.concatenate([_rot_lanes(row("kc", KVB), c64, sa64, sb64, half), row("vc", KVB)], axis=1)
    ks = _rot_lanes(row("ks", KVB), c64, sa64, sb64, half)
    kw = _rot_lanes(row("kw", KVB), c64, sa64, sb64, half)
    for g in range(NSA_KV_GROUPS):
        ks_ref[0, g] = ks[:, g * HD:(g + 1) * HD].astype(ks_ref.dtype)
        kw_ref[0, g] = kw[:, g * HD:(g + 1) * HD].astype(kw_ref.dtype)
    ik = _rot_lanes(row("ik", LANES), c32, sa32, sb32, IDX_DIM // ROT_FRACTION // 2)
    ik_ref[0] = ik[:, :IDX_DIM].astype(ik_ref.dtype)
    xt = x_ref[...].T
    ht = xt * lax.rsqrt(jnp.mean(xt * xt, axis=0, keepdims=True) + NORM_EPS) * gc_ref[...]
    ht = ht * (1.0 + scc_ref[0]) + shc_ref[0]
    pc = _dot(wc_ref[...], ht)
    col = lambda name, n: pc[COL_OFF[name]:COL_OFF[name] + n]
    ang_t = f64c_ref[...] * posr_ref[0]
    cos, sin = jnp.cos(ang_t), jnp.sin(ang_t)

    def rot_rows(blk):
        x1, x2 = blk[0:half], blk[half:2 * half]
        return jnp.concatenate([x1 * cos - x2 * sin, x2 * cos + x1 * sin, blk[2 * half:]], axis=0)

    qa, qb = col("qa", DSA_HEADS * HD), col("qb", NSA_HEADS * HD)
    for hh in range(DSA_HEADS):
        qa_ref[0, hh] = (rot_rows(qa[hh * HD:(hh + 1) * HD]) * scale).astype(qa_ref.dtype)
    for hh in range(NSA_HEADS):
        qb_ref[0, hh] = (rot_rows(qb[hh * HD:(hh + 1) * HD]) * scale).astype(qb_ref.dtype)
    iq = col("iq", IDX_HEADS * IDX_DIM)
    ang32 = f32c_ref[...] * posr_ref[0]
    first = lax.broadcasted_iota(jnp.int32, ang32.shape, 0) < SUBLANES // 2
    c32t, s32t = jnp.cos(ang32), jnp.where(first, -jnp.sin(ang32), jnp.sin(ang32))
    for hh in range(IDX_HEADS):
        blk = iq[hh * IDX_DIM:(hh + 1) * IDX_DIM]
        top = blk[0:SUBLANES]
        top = top * c32t + pltpu.roll(top, SUBLANES // 2, 0) * s32t
        iq_ref[0, hh] = jnp.concatenate([top, blk[SUBLANES:]], axis=0).astype(iq_ref.dtype)
    ones = jnp.ones((ONES_ROWS, xt.shape[1]), F32)
    with_ones = lambda v: jnp.concatenate([v, ones], axis=0)
    vs, vw = col("vs", KVB), col("vw", KVB)
    for g in range(NSA_KV_GROUPS):
        vs_ref[0, g] = with_ones(vs[g * HD:(g + 1) * HD]).astype(vs_ref.dtype)
        vw_ref[0, g] = with_ones(vw[g * HD:(g + 1) * HD]).astype(vw_ref.dtype)
    lat = col("lat", DSA_KV_RANK)
    lat = lat * lax.rsqrt(jnp.mean(lat * lat, axis=0, keepdims=True) + NORM_EPS) * gkvc_ref[...]
    va = _dot(wkv_ref[...], lat)
    for hh in range(DSA_HEADS):
        va_ref[0, hh] = with_ones(va[hh * HD:(hh + 1) * HD]).astype(va_ref.dtype)
    iw_ref[0] = col("iw", IDX_HEADS)
    br_ref[0] = jax.nn.sigmoid(col("br", 3 * NSA_HEADS))


def _prep(x, mod_sc, mod_sh, g_norm1, w_in, g_kv, w_kv_up, positions, tm):
    B, S, D = x.shape
    T = B * S
    HD, HA, G = HEAD_DIM, DSA_HEADS, NSA_KV_GROUPS
    per_b = S // tm
    sizes = [HA * HD, DSA_KV_RANK, IDX_HEADS * IDX_DIM, IDX_HEADS, IDX_DIM, NSA_HEADS * HD,
             KVB, KVB, KVB, KVB, KVB, KVB, 3 * NSA_HEADS, 2 * D]
    names = ["qa", "lat", "iq", "iw", "ik", "qb", "kc", "vc", "ks", "vs", "kw", "vw", "br", "mg"]
    starts = dict(zip(names, np.concatenate([[0], np.cumsum(sizes)[:-1]]).astype(int)))
    width = dict(zip(names, sizes))
    def seg(name, n):
        w = w_in[:, starts[name]:starts[name] + width[name]]
        return jnp.pad(w, ((0, 0), (0, n - width[name])))
    w_row = jnp.concatenate([seg(n, k) for n, k in ROW_SEGS], axis=1).astype(MXU_DTYPE)
    w_col = jnp.concatenate([seg(n, k) for n, k in COL_SEGS], axis=1).T.astype(MXU_DTYPE)
    w_kk = w_kv_up[:, :HA * HD].astype(MXU_DTYPE)
    w_kv = w_kv_up[:, HA * HD:].T.astype(MXU_DTYPE)
    wide = lambda v: jnp.broadcast_to(v[..., None], v.shape + (tm,))
    column = lambda v: v[..., None]

    def freqs(dim):
        half = dim // ROT_FRACTION // 2
        inv_freq = ROPE_THETA ** (-jnp.arange(half, dtype=F32) / half)
        per_dim = jnp.concatenate([inv_freq, inv_freq, jnp.zeros((dim - 2 * half,), F32)])
        return jnp.tile(per_dim, LANES // dim).reshape(1, LANES), jnp.concatenate([inv_freq, inv_freq])
    f64l, f64pair = freqs(HD)
    f32l, f32pair = freqs(IDX_DIM)
    half64 = HD // ROT_FRACTION // 2
    f64c = wide(f64pair[:half64])
    f32c = wide(f32pair)
    pos_f = positions.astype(F32)

    row_blk = lambda n: pl.BlockSpec((tm, n), lambda i: (i, 0))
    const = lambda shape: pl.BlockSpec(shape, lambda i: (0,) * len(shape))
    per_batch = lambda shape: pl.BlockSpec((1,) + shape, lambda i: (i // per_b,) + (0,) * len(shape))
    tok_cols = lambda rows: pl.BlockSpec((1, rows, tm), lambda i: (i // per_b, 0, i % per_b))
    heads_cols = lambda h, rows: pl.BlockSpec((1, h, rows, tm), lambda i: (i // per_b, 0, 0, i % per_b))
    heads_rows = lambda h, n: pl.BlockSpec((1, h, tm, n), lambda i: (i // per_b, 0, i % per_b, 0))
    bf = MXU_DTYPE
    out_shape = [jax.ShapeDtypeStruct((T, 2 * D), F32),
                 jax.ShapeDtypeStruct((B, HA, HD, S), bf),
                 jax.ShapeDtypeStruct((B, HA, S, HD), bf),
                 jax.ShapeDtypeStruct((B, HA, HD + ONES_ROWS, S), bf),
                 jax.ShapeDtypeStruct((B, IDX_HEADS, IDX_DIM, S), bf),
                 jax.ShapeDtypeStruct((B, IDX_HEADS, S), F32),
                 jax.ShapeDtypeStruct((B, S, IDX_DIM), bf),
                 jax.ShapeDtypeStruct((B, NSA_HEADS, HD, S), bf),
                 jax.ShapeDtypeStruct((B, S, 2 * KVB), F32),
                 jax.ShapeDtypeStruct((B, G, S, HD), bf),
                 jax.ShapeDtypeStruct((B, G, S, HD), bf),
                 jax.ShapeDtypeStruct((B, G, HD + ONES_ROWS, S), bf),
                 jax.ShapeDtypeStruct((B, G, HD + ONES_ROWS, S), bf),
                 jax.ShapeDtypeStruct((B, 3 * NSA_HEADS, S), F32)]
    out_specs = [row_blk(2 * D), heads_cols(HA, HD), heads_rows(HA, HD), heads_cols(HA, HD + ONES_ROWS),
                 heads_cols(IDX_HEADS, IDX_DIM), tok_cols(IDX_HEADS),
                 pl.BlockSpec((1, tm, IDX_DIM), lambda i: (i // per_b, i % per_b, 0)),
                 heads_cols(NSA_HEADS, HD),
                 pl.BlockSpec((1, tm, 2 * KVB), lambda i: (i // per_b, i % per_b, 0)),
                 heads_rows(G, HD), heads_rows(G, HD), heads_cols(G, HD + ONES_ROWS),
                 heads_cols(G, HD + ONES_ROWS),
                 tok_cols(3 * NSA_HEADS)]
    in_specs = [row_blk(D), const((1, D)), per_batch((1, D)), per_batch((1, D)),
                const((D, 1)), per_batch((D, 1)), per_batch((D, 1)),
                const((D, ROW_COLS)), const((COL_ROWS, D)),
                const((1, DSA_KV_RANK)), const((DSA_KV_RANK, 1)),
                const((DSA_KV_RANK, HA * HD)), const((HA * HD, DSA_KV_RANK)),
                row_blk(1), tok_cols(1), const((1, LANES)), const((1, LANES)),
                const((half64, tm)), const((SUBLANES, tm))]
    return pl.pallas_call(
        _prep_kernel,
        grid=(T // tm,),
        in_specs=in_specs,
        out_specs=out_specs,
        out_shape=out_shape,
        compiler_params=_cparams(("arbitrary",)),
    )(x.reshape(T, D), g_norm1.reshape(1, D), mod_sc, mod_sh,
      column(g_norm1), column(mod_sc[:, 0]), column(mod_sh[:, 0]), w_row, w_col,
      g_kv.reshape(1, DSA_KV_RANK), column(g_kv), w_kk, w_kv,
      pos_f.reshape(T, 1), pos_f.reshape(B, 1, S), f64l, f32l, f64c, f32c)


def _cmp_kernel(f_ref, w1_ref, w2_ref, o_ref):
    hid = _dot(f_ref[...], w1_ref[...])
    hid = hid * jax.nn.sigmoid(hid)
    o_ref[...] = _dot(hid, w2_ref[...])


def _compress(tok, pe, w1, w2):
    B, S, G, HD = tok.shape
    r = CMP_LEN // CMP_STRIDE
    n_chunks = S // CMP_STRIDE
    n_cmp = n_chunks - r + 1
    chunks = tok.reshape(B, n_chunks, CMP_STRIDE, G, HD)
    blocks = jnp.concatenate([chunks[:, i:n_cmp + i] for i in range(r)], axis=2)
    blocks = blocks + pe[:, None, :]
    flat = blocks.transpose(0, 1, 3, 2, 4).reshape(B * n_cmp * G, CMP_LEN * HD)
    rows = flat.shape[0]
    tm = 512
    rows_p = -(-rows // tm) * tm
    flat = jnp.pad(flat, ((0, rows_p - rows), (0, 0))).astype(MXU_DTYPE)
    out = pl.pallas_call(
        _cmp_kernel,
        grid=(rows_p // tm,),
        in_specs=[pl.BlockSpec((tm, CMP_LEN * HD), lambda i: (i, 0)),
                  pl.BlockSpec((CMP_LEN * HD, CMP_HIDDEN), lambda i: (0, 0)),
                  pl.BlockSpec((CMP_HIDDEN, HD), lambda i: (0, 0))],
        out_specs=pl.BlockSpec((tm, HD), lambda i: (i, 0)),
        out_shape=jax.ShapeDtypeStruct((rows_p, HD), F32),
        compiler_params=_cparams(("arbitrary",)),
    )(flat, w1.astype(MXU_DTYPE), w2.astype(MXU_DTYPE))
    return out[:rows].reshape(B, n_cmp, G, HD)


def _select_kernel(ik_ref, iqt_ref, iwt_ref, bias_ref, sc_ref, *, TQ, KC, S, K, NBIS):
    qi = pl.program_id(1)
    q0 = qi * TQ
    n_ch = (q0 + TQ + KC - 1) // KC
    n_all = S // KC
    t_row = q0 + lax.broadcasted_iota(jnp.int32, (1, TQ), 1)
    key_iota = lax.broadcasted_iota(jnp.int32, (KC, TQ), 0)
    kf = float(K)
    SUB = LANES
    sub_iota = lax.broadcasted_iota(jnp.int32, (SUB, TQ), 0)

    def score_chunk(c, carry):
        mn, mx = carry
        for u in range(KC // SUB):
            off = pl.multiple_of(c * KC + u * SUB, SUB)
            ikc = ik_ref[0, pl.ds(off, SUB), :]
            acc = jnp.zeros((SUB, TQ), F32)
            for h in range(IDX_HEADS):
                lg = jnp.dot(ikc, iqt_ref[0, h], preferred_element_type=F32)
                acc = acc + jnp.maximum(lg, 0.0) * iwt_ref[0, h:h + 1, :]
            causal = off + sub_iota <= t_row
            sc_ref[pl.ds(off, SUB), :] = jnp.where(causal, acc, NEG_INF)
            rows = lambda x: x.reshape(SUB // SUBLANES, SUBLANES, TQ)
            mn = jnp.minimum(mn, jnp.min(rows(jnp.where(causal, acc, -LOWEST)), axis=0))
            mx = jnp.maximum(mx, jnp.max(rows(jnp.where(causal, acc, LOWEST)), axis=0))
        return mn, mx

    mn, mx = lax.fori_loop(0, n_ch, score_chunk, (jnp.full((SUBLANES, TQ), -LOWEST, F32),
                                                  jnp.full((SUBLANES, TQ), LOWEST, F32)))
    lo = jnp.min(mn, axis=0, keepdims=True)
    hi = jnp.max(mx, axis=0, keepdims=True)

    def chunk(c):
        off = pl.multiple_of(c * KC, KC)
        return sc_ref[pl.ds(off, KC), :], off

    AR = 2 * SUBLANES

    def fold(x, op):
        return op(x.reshape(KC // AR, AR, TQ), axis=0)

    def key_pass(fn, init):
        def body(c, acc):
            xs, off = chunk(c)
            return fn(acc, xs, off)
        return lax.fori_loop(0, n_ch, body, init)

    def count_ge(thr):
        acc = key_pass(lambda a, xs, off: a + fold(jnp.where(xs >= thr, 1.0, 0.0), jnp.sum),
                       jnp.zeros((AR, TQ), F32))
        return jnp.sum(acc, axis=0, keepdims=True)

    def count_gt(thr):
        acc = key_pass(lambda a, xs, off: a + fold(jnp.where(xs > thr, 1.0, 0.0), jnp.sum),
                       jnp.zeros((AR, TQ), F32))
        return jnp.sum(acc, axis=0, keepdims=True)

    def max_where(bound):
        fn = lambda a, xs, off: jnp.maximum(a, fold(jnp.where(xs <= bound, xs, LOWEST), jnp.max))
        acc = key_pass(fn, jnp.full((AR, TQ), LOWEST, F32))
        return jnp.max(acc, axis=0, keepdims=True)

    def count_and_next(v):
        def fn(a, xs, off):
            cnt, nxt = a
            return (cnt + fold(jnp.where(xs >= v, 1.0, 0.0), jnp.sum),
                    jnp.maximum(nxt, fold(jnp.where(xs < v, xs, LOWEST), jnp.max)))
        cnt, nxt = key_pass(fn, (jnp.zeros((AR, TQ), F32), jnp.full((AR, TQ), LOWEST, F32)))
        return jnp.sum(cnt, axis=0, keepdims=True), jnp.max(nxt, axis=0, keepdims=True)

    def bisect(_, carry):
        lo, hi = carry
        mid = 0.5 * (lo + hi)
        ge = count_ge(mid) >= kf
        return jnp.where(ge, mid, lo), jnp.where(ge, hi, mid)

    lo, hi = lax.fori_loop(0, NBIS, bisect, (lo, hi))

    all_keys = jnp.where(t_row < K, 1.0, 0.0)
    v = max_where(hi)
    n_ge, nxt = count_and_next(v)
    done = jnp.maximum(all_keys, jnp.where(n_ge >= kf, 1.0, 0.0))

    def peel_cond(st):
        return jnp.sum(st[1]) < float(TQ)

    def peel_body(st):
        v, done, n_ge, nxt = st
        v = jnp.where(done > 0.5, v, nxt)
        cnt, nxt = count_and_next(v)
        n_ge = jnp.where(done > 0.5, n_ge, cnt)
        done = jnp.maximum(done, jnp.where(cnt >= kf, 1.0, 0.0))
        return v, done, n_ge, nxt

    v, done, n_ge, _ = lax.while_loop(peel_cond, peel_body, (v, done, n_ge, nxt))
    thr = jnp.where(all_keys > 0.5, LOWEST, v)
    has_tie = jnp.max(jnp.where((n_ge > kf) & (all_keys < 0.5), 1.0, 0.0)) > 0.5

    def write(off, sel):
        bias_ref[0, pl.ds(off, KC), :] = jnp.where(sel, 0.0, NEG_INF).astype(bias_ref.dtype)

    def fill_tail():
        def body(c, carry):
            write(pl.multiple_of(c * KC, KC), jnp.zeros((KC, TQ), jnp.bool_))
            return carry
        lax.fori_loop(n_ch, n_all, body, 0)

    @pl.when(jnp.logical_not(has_tie))
    def _():
        def body(c, carry):
            xs, off = chunk(c)
            write(off, (xs >= thr) & ((off + key_iota) <= t_row))
            return carry
        lax.fori_loop(0, n_ch, body, 0)
        fill_tail()

    @pl.when(has_tie)
    def _():
        need = kf - count_gt(thr)
        r_i = lax.broadcasted_iota(jnp.int32, (KC, KC), 0)
        c_i = lax.broadcasted_iota(jnp.int32, (KC, KC), 1)
        lower = jnp.where(c_i <= r_i, 1.0, 0.0).astype(MXU_DTYPE)
        def body(c, seen):
            xs, off = chunk(c)
            causal = (off + key_iota) <= t_row
            eq = jnp.where((xs == thr) & causal, 1.0, 0.0)
            rank = jnp.dot(lower, eq.astype(MXU_DTYPE), preferred_element_type=F32) + seen
            write(off, ((xs > thr) & causal) | ((eq > 0.5) & (rank <= need)))
            return seen + jnp.sum(eq, axis=0, keepdims=True)
        lax.fori_loop(0, n_ch, body, jnp.zeros((1, TQ), F32))
        fill_tail()


def _dsa_select(ik, iqt, iwt, K, TQ, KC):
    B, S, DI = ik.shape
    H = iqt.shape[1]
    kern = functools.partial(_select_kernel, TQ=TQ, KC=KC, S=S, K=K, NBIS=SELECT_BISECTIONS)
    return pl.pallas_call(
        kern,
        grid=(B, S // TQ),
        in_specs=[pl.BlockSpec((1, S, DI), lambda b, q: (b, 0, 0)),
                  pl.BlockSpec((1, H, DI, TQ), lambda b, q: (b, 0, 0, q)),
                  pl.BlockSpec((1, H, TQ), lambda b, q: (b, 0, q))],
        out_specs=pl.BlockSpec((1, S, TQ), lambda b, q: (b, 0, q)),
        out_shape=jax.ShapeDtypeStruct((B, S, S), jnp.bfloat16),
        scratch_shapes=[pltpu.VMEM((S, TQ), F32)],
        compiler_params=_cparams(("arbitrary", "arbitrary")),
    )(ik, iqt, iwt)


def _cmp_attn_kernel(qt_ref, kc_ref, vct_ref, o_ref, bm_ref, *, TQ, NCP, NBP, N_SEL):
    qi = pl.program_id(1)
    q0 = qi * TQ
    G, J = NSA_KV_GROUPS, NSA_REP
    N = J * TQ
    t_lane = q0 + (lax.broadcasted_iota(jnp.int32, (NCP, N), 1) & (TQ - 1))
    cmp_end = lax.broadcasted_iota(jnp.int32, (NCP, N), 0) * CMP_STRIDE + (CMP_LEN - 1)
    vis = cmp_end <= t_lane
    n_i = lax.broadcasted_iota(jnp.int32, (NBP, NCP), 0) * SEL_BLOCK
    c_i = lax.broadcasted_iota(jnp.int32, (NBP, NCP), 1) * CMP_STRIDE
    overlap_t = jnp.where((c_i <= n_i + SEL_BLOCK - 1) & (c_i + CMP_LEN - 1 >= n_i), 1.0, 0.0)
    blk = lax.broadcasted_iota(jnp.int32, (NBP, TQ), 0)
    tq = q0 + lax.broadcasted_iota(jnp.int32, (NBP, TQ), 1)
    cur = tq // SEL_BLOCK
    admissible = blk * SEL_BLOCK <= tq
    forced = (blk == 0) | (blk == cur) | (blk == cur - 1)
    for g in range(G):
        qt = jnp.concatenate([qt_ref[0, g * J + j] for j in range(J)], axis=1)
        s = jnp.where(vis, _dot(kc_ref[0, g], qt), NEG_INF)
        p = jnp.exp(s - jnp.max(s, axis=0, keepdims=True))
        p = jnp.where(vis, p, 0.0)
        p = p / jnp.maximum(jnp.sum(p, axis=0, keepdims=True), TINY)
        o = _dot(vct_ref[0, g], p)
        for j in range(J):
            o_ref[0, g * J + j] = o[:, j * TQ:(j + 1) * TQ]
        psum = p[:, 0:TQ]
        for j in range(1, J):
            psum = psum + p[:, j * TQ:(j + 1) * TQ]
        imp = _dot_exact_lhs(overlap_t, psum)
        score = jnp.where(admissible & forced, FORCED, jnp.where(admissible, imp, NEG_INF))

        def pick(_, st):
            score, sel = st
            m = jnp.max(score, axis=0, keepdims=True)
            first = jnp.min(jnp.where(score == m, blk, NBP), axis=0, keepdims=True)
            hit = blk == first
            return jnp.where(hit, LOWEST, score), jnp.where(hit, 1.0, sel)

        _, sel = lax.fori_loop(0, N_SEL, pick, (score, jnp.zeros((NBP, TQ), F32)))
        bm_ref[0, g] = sel.astype(bm_ref.dtype)


def _cmp_attn(qt, kc, vct, n_sel, NBP, TQ):
    B, H, HD, S = qt.shape
    G, NCP = kc.shape[1], kc.shape[2]
    kern = functools.partial(_cmp_attn_kernel, TQ=TQ, NCP=NCP, NBP=NBP, N_SEL=n_sel)
    return pl.pallas_call(
        kern,
        grid=(B, S // TQ),
        in_specs=[pl.BlockSpec((1, H, HD, TQ), lambda b, q: (b, 0, 0, q)),
                  pl.BlockSpec((1, G, NCP, HD), lambda b, q: (b, 0, 0, 0)),
                  pl.BlockSpec((1, G, HD, NCP), lambda b, q: (b, 0, 0, 0))],
        out_specs=[pl.BlockSpec((1, H, HD, TQ), lambda b, q: (b, 0, 0, q)),
                   pl.BlockSpec((1, G, NBP, TQ), lambda b, q: (b, 0, 0, q))],
        out_shape=[jax.ShapeDtypeStruct((B, H, HD, S), F32),
                   jax.ShapeDtypeStruct((B, G, NBP, S), jnp.bfloat16)],
        compiler_params=_cparams(("arbitrary", "arbitrary")),
    )(qt, kc, vct)


def _flash_kernel(*refs, mode, G, J, TQ, TK, NWIN):
    if mode == "window":
        qt_ref, k_ref, vt_ref, o_ref, m_ref, acc_ref = refs
        x_ref = None
    else:
        qt_ref, k_ref, vt_ref, x_ref, o_ref, m_ref, acc_ref = refs
    HD = HEAD_DIM
    qi = pl.program_id(1)
    kk = pl.program_id(2)
    q0 = qi * TQ
    last = (q0 + TQ - 1) // TK
    if mode == "window":
        ki = last - (NWIN - 1) + kk
        valid = ki >= 0
    else:
        ki = kk
        valid = kk <= last

    @pl.when(kk == 0)
    def _():
        m_ref[...] = jnp.full(m_ref.shape, NEG_INF, F32)
        acc_ref[...] = jnp.zeros(acc_ref.shape, F32)

    @pl.when(valid)
    def _():
        k0 = ki * TK
        kidx = k0 + lax.broadcasted_iota(jnp.int32, (TK, TQ), 0)
        tq = q0 + lax.broadcasted_iota(jnp.int32, (TK, TQ), 1)
        if mode == "mask":
            bias = x_ref[0].astype(F32)
        elif mode == "window":
            bias = jnp.where((kidx <= tq) & (kidx > tq - WINDOW), 0.0, NEG_INF)
        else:
            causal = kidx <= tq
            nbp = x_ref.shape[2]
            blk_of_key = (k0 + lax.broadcasted_iota(jnp.int32, (TK, nbp), 0)) // SEL_BLOCK
            expand_t = jnp.where(lax.broadcasted_iota(jnp.int32, (TK, nbp), 1) == blk_of_key, 1.0, 0.0)
            expand_t = expand_t.astype(MXU_DTYPE)

        def scores(g):
            if mode == "block":
                picked = jnp.dot(expand_t, x_ref[0, g].astype(MXU_DTYPE), preferred_element_type=F32)
                b = jnp.where(causal & (picked > 0.5), 0.0, NEG_INF)
            else:
                b = bias
            if J > 1:
                b = jnp.concatenate([b] * J, axis=1)
                qt = jnp.concatenate([qt_ref[0, g * J + j] for j in range(J)], axis=1)
            else:
                qt = qt_ref[0, g]
            return _dot(k_ref[0, g], qt) + b

        for g in range(G):
            s = scores(g)
            m_prev = m_ref[g]
            m_new = jnp.maximum(m_prev, jnp.max(s, axis=0, keepdims=True))
            alpha = jnp.exp(m_prev - m_new)
            p = jnp.exp((s - m_new).astype(MXU_DTYPE))
            acc_ref[g] = alpha * acc_ref[g] + _dot(vt_ref[0, g], p)
            m_ref[g] = m_new

    @pl.when(kk == pl.num_programs(2) - 1)
    def _():
        for g in range(G):
            o = acc_ref[g, 0:HD] / jnp.maximum(acc_ref[g, HD:HD + 1], TINY)
            for j in range(J):
                o_ref[0, g * J + j] = o[:, j * TQ:(j + 1) * TQ].astype(o_ref.dtype)


def _flash(qt, k, vt, extra, mode, TQ, TK, out_dtype):
    B, H, HD, S = qt.shape
    G = k.shape[1]
    J = H // G
    N = J * TQ
    nq = S // TQ
    last_of = lambda qi: (qi * TQ + TQ - 1) // TK
    if mode == "window":
        NWIN = min((WINDOW - 1 + TK - 1) // TK + 1, S // TK)
        nk = NWIN
        kidx_of = lambda qi, kk: jnp.maximum(last_of(qi) - (NWIN - 1) + kk, 0)
    else:
        NWIN = 0
        nk = S // TK
        kidx_of = lambda qi, kk: jnp.minimum(kk, last_of(qi))
    in_specs = [pl.BlockSpec((1, H, HD, TQ), lambda b, qi, kk: (b, 0, 0, qi)),
                pl.BlockSpec((1, G, TK, HD), lambda b, qi, kk: (b, 0, kidx_of(qi, kk), 0)),
                pl.BlockSpec((1, G, HD + ONES_ROWS, TK), lambda b, qi, kk: (b, 0, 0, kidx_of(qi, kk)))]
    args = [qt, k, vt]
    if mode == "mask":
        in_specs.append(pl.BlockSpec((1, TK, TQ), lambda b, qi, kk: (b, kidx_of(qi, kk), qi)))
        args.append(extra)
    elif mode == "block":
        nbp = extra.shape[2]
        in_specs.append(pl.BlockSpec((1, G, nbp, TQ), lambda b, qi, kk: (b, 0, 0, qi)))
        args.append(extra)
    kern = functools.partial(_flash_kernel, mode=mode, G=G, J=J, TQ=TQ, TK=TK, NWIN=NWIN)
    return pl.pallas_call(
        kern,
        grid=(B, nq, nk),
        in_specs=in_specs,
        out_specs=pl.BlockSpec((1, H, HD, TQ), lambda b, qi, kk: (b, 0, 0, qi)),
        out_shape=jax.ShapeDtypeStruct((B, H, HD, S), out_dtype),
        scratch_shapes=[pltpu.VMEM((G, 1, N), F32),
                        pltpu.VMEM((G, HD + ONES_ROWS, N), F32)],
        compiler_params=_cparams(("arbitrary", "arbitrary", "arbitrary")),
    )(*args)


def _dot_tn(a_t, b):
    return lax.dot_general(a_t.astype(MXU_DTYPE), b.astype(MXU_DTYPE),
                           (((0,), (0,)), ((), ())), preferred_element_type=F32)


def _route(h, wr, br):
    tm = h.shape[0]
    lane = lax.broadcasted_iota(jnp.int32, (tm, LANES), 1)
    h_hi = h.astype(MXU_DTYPE)
    h_lo = (h - h_hi.astype(F32)).astype(MXU_DTYPE)
    w_hi = wr.astype(MXU_DTYPE)
    w_lo = (wr - w_hi.astype(F32)).astype(MXU_DTYPE)
    d = lambda a, b: jnp.dot(a, b, preferred_element_type=F32)
    logits = d(h_hi, w_hi) + d(h_hi, w_lo) + d(h_lo, w_hi) + br
    is_e = lane < N_EXPERTS
    is_g = (lane >= N_EXPERTS) & (lane < N_EXPERTS + N_GROUPS)
    lg = jnp.where(is_g, logits, LOWEST)
    mg = jnp.max(lg, axis=-1, keepdims=True)
    gsel = jnp.min(jnp.where(is_g & (lg == mg), lane - N_EXPERTS, N_GROUPS), axis=-1, keepdims=True)
    pg_sel = 1.0 / jnp.sum(jnp.where(is_g, jnp.exp(lg - mg), 0.0), axis=-1, keepdims=True)
    in_grp = is_e & ((lane // EXPERTS_PER_GROUP) == gsel)
    le = jnp.where(in_grp, logits, LOWEST)
    me = jnp.max(le, axis=-1, keepdims=True)
    ex = jnp.where(in_grp, jnp.exp(le - me), 0.0)
    pe = ex / jnp.sum(ex, axis=-1, keepdims=True)
    pe = jnp.where(in_grp, pe, -1.0)
    p1 = jnp.max(pe, axis=-1, keepdims=True)
    i1 = jnp.min(jnp.where(pe == p1, lane, LANES), axis=-1, keepdims=True)
    pe2 = jnp.where(lane == i1, -1.0, pe)
    p2 = jnp.max(pe2, axis=-1, keepdims=True)
    i2 = jnp.min(jnp.where(pe2 == p2, lane, LANES), axis=-1, keepdims=True)
    tot = p1 + p2
    comb = jnp.where(lane == i1, p1 / tot * pg_sel, 0.0) + jnp.where(lane == i2, p2 / tot * pg_sel, 0.0)
    return jnp.where(lane == N_EXPERTS, gsel.astype(F32), comb)


def _outproj_kernel(x_ref, oa_ref, oc_ref, os_ref, ow_ref, br_ref, mg_ref, wa_ref, wb_ref, wo_ref,
                    gt_ref, g2_ref, sc_ref, sh_ref, wr_ref, brt_ref, o_ref, hr_ref):
    HD = HEAD_DIM
    D = x_ref.shape[1]
    br = br_ref[0]
    parts = []
    for hh in range(NSA_HEADS):
        parts.append(br[3 * hh:3 * hh + 1] * oc_ref[0, hh] + br[3 * hh + 1:3 * hh + 2] * os_ref[0, hh]
                     + br[3 * hh + 2:3 * hh + 3] * ow_ref[0, hh])
    ob_t = jnp.concatenate(parts, axis=0)
    oa_t = jnp.concatenate([oa_ref[0, hh] for hh in range(DSA_HEADS)], axis=0)
    ua = _dot_tn(oa_t, wa_ref[...])
    ub = _dot_tn(ob_t, wb_ref[...])
    mg = mg_ref[...]
    merged = jax.nn.sigmoid(mg[:, :D]) * ua + jax.nn.sigmoid(mg[:, D:]) * ub
    x1 = x_ref[...] + gt_ref[0] * _dot(merged, wo_ref[...])
    o_ref[...] = x1
    h = _rms(x1, g2_ref[...]) * (1.0 + sc_ref[0]) + sh_ref[0]
    hr_ref[:, :D] = h
    hr_ref[:, D:] = _route(h, wr_ref[...], brt_ref[...])


def _out_proj(x2d, oa, oc, os_, ow, br, mg, wa, wb, wo, gt, g2, sc2, sh2, wr, brt, S, tm):
    T, D = x2d.shape
    B, H, HD, _ = oa.shape
    per_b = S // tm
    heads_cols = pl.BlockSpec((1, H, HD, tm), lambda i: (i // per_b, 0, 0, i % per_b))
    per_batch = pl.BlockSpec((1, 1, D), lambda i: (i // per_b, 0, 0))
    rows = lambda n: pl.BlockSpec((tm, n), lambda i: (i, 0))
    const = lambda a, b: pl.BlockSpec((a, b), lambda i: (0, 0))
    return pl.pallas_call(
        _outproj_kernel,
        grid=(T // tm,),
        in_specs=[rows(D), heads_cols, heads_cols, heads_cols, heads_cols,
                  pl.BlockSpec((1, br.shape[1], tm), lambda i: (i // per_b, 0, i % per_b)),
                  rows(2 * D), const(H * HD, D), const(H * HD, D), const(D, D), per_batch,
                  const(1, D), per_batch, per_batch, const(D, LANES), const(1, LANES)],
        out_specs=[rows(D), rows(D + LANES)],
        out_shape=[jax.ShapeDtypeStruct((T, D), F32), jax.ShapeDtypeStruct((T, D + LANES), F32)],
        compiler_params=_cparams(("arbitrary",)),
    )(x2d, oa, oc, os_, ow, br, mg, wa, wb, wo, gt, g2.reshape(1, D), sc2, sh2, wr, brt)


def _row_copy(src_hbm, t, dst, r, sem):
    return pltpu.make_async_copy(src_hbm.at[pl.ds(t, 1)], dst.at[pl.ds(r, 1)], sem)


def _start_rows(idx_ref, base, r0, n, src_hbm, dst, sem):
    for r in range(n):
        _row_copy(src_hbm, idx_ref[base + r0 + r], dst, r0 + r, sem).start()


def _wait_rows(src_hbm, dst, n, sem):
    pltpu.make_async_copy(src_hbm.at[pl.ds(0, n)], dst.at[pl.ds(0, n)], sem).wait()


def _experts_kernel(tg_ref, src_ref, h_hbm, wg_ref, wu_ref, wd_ref, o_ref,
                    hbuf, acc_ref, sems, *, TM):
    i = pl.program_id(0)
    e = pl.program_id(1)
    n_tiles = pl.num_programs(0)
    EPG, EPS = EXPERTS_PER_GROUP, EXPERTS_PER_STEP
    n_steps = EPG // EPS
    part = TM // n_steps
    slot = i % 2

    @pl.when((i == 0) & (e == 0))
    def _():
        def start(c, carry):
            _start_rows(src_ref, 0, pl.multiple_of(c * part, part), part, h_hbm, hbuf.at[0], sems.at[0])
            return carry
        lax.fori_loop(0, n_steps, start, 0)

    @pl.when(e == 0)
    def _():
        _wait_rows(h_hbm, hbuf.at[slot], TM, sems.at[slot])
        acc_ref[...] = jnp.zeros(acc_ref.shape, F32)

    @pl.when(i + 1 < n_tiles)
    def _():
        _start_rows(src_ref, (i + 1) * TM, e * part, part, h_hbm, hbuf.at[1 - slot], sems.at[1 - slot])

    D = acc_ref.shape[1]
    h = hbuf[slot, :, :D].astype(MXU_DTYPE)
    lane = lax.broadcasted_iota(jnp.int32, (TM, LANES), 1)
    route = hbuf[slot, :, D:]
    out = acc_ref[...]
    for q in range(EPS):
        a = jnp.dot(h, wg_ref[q], preferred_element_type=F32)
        u = jnp.dot(h, wu_ref[q], preferred_element_type=F32)
        y = _dot(a * jax.nn.sigmoid(a) * u, wd_ref[q])
        expert = tg_ref[i] * EPG + e * EPS + q
        out = out + jnp.sum(jnp.where(lane == expert, route, 0.0), axis=-1, keepdims=True) * y
    acc_ref[...] = out

    @pl.when(e == pl.num_programs(1) - 1)
    def _():
        o_ref[...] = acc_ref[...]


def _experts(hr, tile_group, src, wg, wu, wd, TM):
    D = wg.shape[1]
    DE = wg.shape[2]
    P = src.shape[0]
    EPG = EXPERTS_PER_GROUP
    EPS = EXPERTS_PER_STEP
    w_idx = lambda i, e, tg, src: (tg[i] * (EPG // EPS) + e, 0, 0)
    kern = functools.partial(_experts_kernel, TM=TM)
    grid_spec = pltpu.PrefetchScalarGridSpec(
        num_scalar_prefetch=2,
        grid=(P // TM, EPG // EPS),
        in_specs=[pl.BlockSpec(memory_space=pl.ANY),
                  pl.BlockSpec((EPS, D, DE), w_idx),
                  pl.BlockSpec((EPS, D, DE), w_idx),
                  pl.BlockSpec((EPS, DE, D), w_idx)],
        out_specs=pl.BlockSpec((TM, D), lambda i, e, tg, src: (i, 0)),
        scratch_shapes=[pltpu.VMEM((2, TM, D + LANES), F32),
                        pltpu.VMEM((TM, D), F32),
                        pltpu.SemaphoreType.DMA((2,))])
    return pl.pallas_call(
        kern,
        grid_spec=grid_spec,
        out_shape=jax.ShapeDtypeStruct((P, D), F32),
        compiler_params=_cparams(("arbitrary", "arbitrary")),
    )(tile_group, src, hr, wg, wu, wd)


def _combine_kernel(pos_ref, x_ref, y_hbm, gt_ref, gf_ref, o_ref, ybuf, sems, *, TM):
    i = pl.program_id(0)
    slot = i % 2
    part = 64

    def start_tile(tile, half):
        def start(c, carry):
            _start_rows(pos_ref, tile * TM, pl.multiple_of(c * part, part), part, y_hbm, ybuf.at[half],
                        sems.at[half])
            return carry
        lax.fori_loop(0, TM // part, start, 0)

    @pl.when(i == 0)
    def _():
        start_tile(0, 0)

    @pl.when(i + 1 < pl.num_programs(0))
    def _():
        start_tile(i + 1, 1 - slot)

    _wait_rows(y_hbm, ybuf.at[slot], TM, sems.at[slot])
    o_ref[...] = _rms(x_ref[...] + gt_ref[0] * ybuf[slot], gf_ref[...])


def _combine(x1, ys, pos, gt, gf, S, tm):
    T, D = x1.shape
    per_b = S // tm
    grid_spec = pltpu.PrefetchScalarGridSpec(
        num_scalar_prefetch=1,
        grid=(T // tm,),
        in_specs=[pl.BlockSpec((tm, D), lambda i, pos: (i, 0)),
                  pl.BlockSpec(memory_space=pl.ANY),
                  pl.BlockSpec((1, 1, D), lambda i, pos: (i // per_b, 0, 0)),
                  pl.BlockSpec((1, D), lambda i, pos: (0, 0))],
        out_specs=pl.BlockSpec((tm, D), lambda i, pos: (i, 0)),
        scratch_shapes=[pltpu.VMEM((2, tm, D), F32), pltpu.SemaphoreType.DMA((2,))])
    return pl.pallas_call(
        functools.partial(_combine_kernel, TM=tm),
        grid_spec=grid_spec,
        out_shape=jax.ShapeDtypeStruct((T, D), F32),
        compiler_params=_cparams(("arbitrary",)),
    )(pos, x1, ys, gt, gf.reshape(1, D))


def _routed_moe(x1, hr, gt, gf, wg, wu, wd, S, TM, combine_rows):
    T, D = x1.shape
    gsel = hr[:, D + N_EXPERTS].astype(jnp.int32)
    onehot = (gsel[:, None] == jnp.arange(N_GROUPS, dtype=jnp.int32)[None, :]).astype(jnp.int32)
    csum = jnp.cumsum(onehot, axis=0)
    rank = jnp.take_along_axis(csum, gsel[:, None], axis=1)[:, 0] - 1
    padded = -(-csum[-1] // TM) * TM
    ends = jnp.cumsum(padded)
    starts = ends - padded
    pos = (jnp.take(starts, gsel) + rank).astype(jnp.int32)
    P = T + N_GROUPS * TM
    src = jnp.zeros((P,), jnp.int32).at[pos].set(jnp.arange(T, dtype=jnp.int32))
    tile_start = jnp.arange(P // TM, dtype=jnp.int32) * TM
    tile_group = jnp.minimum(jnp.sum(tile_start[:, None] >= ends[None, :], axis=1), N_GROUPS - 1)
    ys = _experts(hr, tile_group.astype(jnp.int32), src, wg, wu, wd, TM)
    return _combine(x1, ys, pos, gt, gf, S, combine_rows)


def _layer(x, mod, positions, g_norm1, w_in, g_kv_latent, w_kv_up, pe_cmp_k, pe_cmp_v,
           w_cmp1_k, w_cmp2_k, w_cmp1_v, w_cmp2_v, w_up_a, w_up_b, w_out, g_norm2,
           w_router_group, b_router_group, w_router_expert, b_router_expert,
           w_expert_gate, w_expert_up, w_expert_down, g_out):
    B, S, D = x.shape
    T = B * S
    HD, G = HEAD_DIM, NSA_KV_GROUPS
    kvb = G * HD
    topk_a = min(DSA_TOPK_MAX, S // 4)
    n_sel = min(SEL_COUNT, S // SEL_BLOCK)
    mod6 = mod.reshape(B, 6, 1, D)
    sh1, sc1, gt1, sh2, sc2, gt2 = (mod6[:, i] for i in range(6))

    x2d = x.reshape(T, D)
    t = _tiles(S)
    (mg, qt_a, k_a, vt_a, iqt, iwt, ik, qt_b, kcvc, ks, kw, vst, vwt, br) = _prep(
        x, sc1, sh1, g_norm1, w_in, g_kv_latent, w_kv_up, positions, t.rows)

    kc = _compress(kcvc[..., :kvb].reshape(B, S, G, HD), pe_cmp_k, w_cmp1_k, w_cmp2_k)
    vc = _compress(kcvc[..., kvb:].reshape(B, S, G, HD), pe_cmp_v, w_cmp1_v, w_cmp2_v)
    n_cmp = kc.shape[1]
    ncp = -(-(n_cmp + 1) // LANES) * LANES
    pad_c = lambda t: jnp.pad(t, ((0, 0), (0, ncp - n_cmp), (0, 0), (0, 0))).astype(MXU_DTYPE)
    kc, vct = pad_c(kc).transpose(0, 2, 1, 3), pad_c(vc).transpose(0, 2, 3, 1)

    sel_bias = _dsa_select(ik, iqt, iwt, topk_a, t.select_q, t.select_k)
    o_a = _flash(qt_a, k_a, vt_a, sel_bias, "mask", t.mask_q, t.mask_k, MXU_DTYPE)

    nbp = -(-(S // SEL_BLOCK) // LANES) * LANES
    o_c, blk_mask = _cmp_attn(qt_b, kc, vct, n_sel, nbp, t.nsa_q)
    o_s = _flash(qt_b, ks, vst, blk_mask, "block", t.nsa_q, t.block_k, F32)
    o_w = _flash(qt_b, kw, vwt, None, "window", t.nsa_q, t.nsa_q, F32)

    wr = jnp.concatenate([w_router_expert, w_router_group], axis=1)
    wr = jnp.pad(wr, ((0, 0), (0, LANES - wr.shape[1])))
    brt = jnp.concatenate([b_router_expert, b_router_group])
    brt = jnp.pad(brt, (0, LANES - brt.shape[0])).reshape(1, LANES)
    x1, hr = _out_proj(x2d, o_a, o_c, o_s, o_w, br, mg, w_up_a.astype(MXU_DTYPE),
                              w_up_b.astype(MXU_DTYPE), w_out.astype(MXU_DTYPE), gt1,
                              g_norm2, sc2, sh2, wr, brt, S, t.rows)

    out = _routed_moe(x1, hr, gt2, g_out, w_expert_gate.astype(MXU_DTYPE),
                      w_expert_up.astype(MXU_DTYPE), w_expert_down.astype(MXU_DTYPE), S,
                      t.moe_rows, t.rows)
    return out.reshape(B, S, D)


def kernel(x, c, positions, w_ada, b_ada, g_norm1, w_in, g_kv_latent, w_kv_up, pe_cmp_k, pe_cmp_v,
           w_cmp1_k, w_cmp2_k, w_cmp1_v, w_cmp2_v, w_up_a, w_up_b, w_out, g_norm2, w_router_group,
           b_router_group, w_router_expert, b_router_expert, w_expert_gate, w_expert_up,
           w_expert_down, g_final):
    depth = w_ada.shape[0]
    assert depth == 1, "the fused final norm assumes a single layer"
    mod = _ada_mod(c, w_ada[0], b_ada[0])
    return _layer(x, mod, positions, g_norm1[0], w_in[0], g_kv_latent[0], w_kv_up[0], pe_cmp_k[0],
                  pe_cmp_v[0], w_cmp1_k[0], w_cmp2_k[0], w_cmp1_v[0], w_cmp2_v[0], w_up_a[0],
                  w_up_b[0], w_out[0], g_norm2[0], w_router_group[0], b_router_group[0],
                  w_router_expert[0], b_router_expert[0], w_expert_gate[0], w_expert_up[0],
                  w_expert_down[0], g_final)
```

```python
import functools
from typing import NamedTuple

import numpy as np
import jax
import jax.numpy as jnp
from jax import lax
from jax.experimental import pallas as pl
from jax.experimental.pallas import tpu as pltpu

HEAD_DIM = 64
ROT_FRACTION = 4
ROPE_THETA = 500000.0
DSA_HEADS = 8
DSA_KV_RANK = 128
IDX_HEADS = 8
IDX_DIM = 32
DSA_TOPK_MAX = 256
NSA_HEADS = 8
NSA_KV_GROUPS = 2
NSA_REP = NSA_HEADS // NSA_KV_GROUPS
CMP_LEN = 32
CMP_STRIDE = 16
CMP_HIDDEN = 256
SEL_BLOCK = 64
SEL_COUNT = 16
WINDOW = 512
N_GROUPS = 4
EXPERTS_PER_GROUP = 8
N_EXPERTS = N_GROUPS * EXPERTS_PER_GROUP
D_EXPERT = 256
EXPERTS_PER_STEP = 8
NORM_EPS = 1e-6
NEG_INF = -1e30
TINY = 1e-30
LOWEST = -3.0e38
FORCED = 1e30

LANES = 128
SUBLANES = 8
ONES_ROWS = 16
MXU_DTYPE = jnp.bfloat16
VMEM_LIMIT = 56 * 1024 * 1024

F32 = jnp.float32
SELECT_BISECTIONS = 16


class _Tiles(NamedTuple):
    rows: int
    select_q: int
    select_k: int
    mask_q: int
    mask_k: int
    nsa_q: int
    block_k: int
    moe_rows: int


def _tiles(S):
    cap = lambda n: min(n, S)
    return _Tiles(rows=cap(512), select_q=cap(256), select_k=cap(256), mask_q=cap(1024),
                  mask_k=cap(1024), nsa_q=cap(512), block_k=cap(1024), moe_rows=512)


def _cparams(sem):
    return pltpu.CompilerParams(dimension_semantics=sem, vmem_limit_bytes=VMEM_LIMIT)


def _dot(a, b):
    return jnp.dot(a.astype(MXU_DTYPE), b.astype(MXU_DTYPE), preferred_element_type=F32)


def _dot_exact_lhs(a01, b):
    hi = b.astype(MXU_DTYPE)
    r1 = b - hi.astype(F32)
    mid = r1.astype(MXU_DTYPE)
    lo = (r1 - mid.astype(F32)).astype(MXU_DTYPE)
    a = a01.astype(MXU_DTYPE)
    d = lambda u: jnp.dot(a, u, preferred_element_type=F32)
    return d(hi) + d(mid) + d(lo)


def _rms(x, g):
    return x * lax.rsqrt(jnp.mean(x * x, axis=-1, keepdims=True) + NORM_EPS) * g


def _ada_kernel(c_ref, w_ref, b_ref, o_ref):
    c = c_ref[...]
    cond = c * jax.nn.sigmoid(c)
    o_ref[...] = _dot(cond, w_ref[...]) + b_ref[...]


def _ada_mod(c, w_ada, b_ada):
    B, D = c.shape
    n_out = w_ada.shape[1]
    rows = SUBLANES
    cp = jnp.zeros((rows, D), F32).at[:B].set(c)
    tn = 1024
    out = pl.pallas_call(
        _ada_kernel,
        grid=(n_out // tn,),
        in_specs=[pl.BlockSpec((rows, D), lambda j: (0, 0)),
                  pl.BlockSpec((D, tn), lambda j: (0, j)),
                  pl.BlockSpec((1, tn), lambda j: (0, j))],
        out_specs=pl.BlockSpec((rows, tn), lambda j: (0, j)),
        out_shape=jax.ShapeDtypeStruct((rows, n_out), F32),
        compiler_params=_cparams(("arbitrary",)),
    )(cp, w_ada, b_ada.reshape(1, n_out))
    return out[:B]


KVB = NSA_KV_GROUPS * HEAD_DIM
ROW_SEGS = (("mg", 2048), ("lat", DSA_KV_RANK), ("kc", KVB), ("vc", KVB), ("ks", KVB), ("kw", KVB),
            ("ik", LANES))
COL_SEGS = (("qa", DSA_HEADS * HEAD_DIM), ("qb", NSA_HEADS * HEAD_DIM), ("iq", IDX_HEADS * IDX_DIM),
            ("vs", KVB), ("vw", KVB), ("lat", DSA_KV_RANK), ("iw", IDX_HEADS), ("br", 3 * NSA_HEADS))


def _seg_offsets(segs):
    out, pos = {}, 0
    for name, n in segs:
        out[name] = pos
        pos += n
    return out, pos


ROW_OFF, ROW_COLS = _seg_offsets(ROW_SEGS)
COL_OFF, COL_ROWS = _seg_offsets(COL_SEGS)


def _rot_lanes(x, c, sa, sb, half):
    outs = []
    for j in range(x.shape[1] // LANES):
        xs = x[:, j * LANES:(j + 1) * LANES]
        outs.append(xs * c + pltpu.roll(xs, half, 1) * sa + pltpu.roll(xs, LANES - half, 1) * sb)
    return outs[0] if len(outs) == 1 else jnp.concatenate(outs, axis=1)


def _prep_kernel(x_ref, g_ref, sc_ref, sh_ref, gc_ref, scc_ref, shc_ref, wr_ref, wc_ref,
                 gkv_ref, gkvc_ref, wkk_ref, wkv_ref, posc_ref, posr_ref, f64l_ref, f32l_ref,
                 f64c_ref, f32c_ref,
                 mg_ref, qa_ref, ka_ref, va_ref, iq_ref, iw_ref, ik_ref, qb_ref, kcvc_ref,
                 ks_ref, kw_ref, vs_ref, vw_ref, br_ref):
    HD = HEAD_DIM
    half = HD // ROT_FRACTION // 2
    scale = HD ** -0.5
    h = _rms(x_ref[...], g_ref[...]) * (1.0 + sc_ref[0]) + sh_ref[0]
    pr = _dot(h, wr_ref[...])
    row = lambda name, n: pr[:, ROW_OFF[name]:ROW_OFF[name] + n]
    mg_ref[...] = row("mg", 2048)
    lane = lax.broadcasted_iota(jnp.int32, (x_ref.shape[0], LANES), 1)

    def lane_tables(freq_row, dim, hf):
        ang = posc_ref[...] * freq_row
        c, s = jnp.cos(ang), jnp.sin(ang)
        upper = (lane & (dim - 1)) >= hf
        return c, jnp.where(upper, s, 0.0), jnp.where(upper, 0.0, -s)

    c64, sa64, sb64 = lane_tables(f64l_ref[...], HD, half)
    c32, sa32, sb32 = lane_tables(f32l_ref[...], IDX_DIM, IDX_DIM // ROT_FRACTION // 2)
    ka = _rot_lanes(_dot(_rms(row("lat", DSA_KV_RANK), gkv_ref[...]), wkk_ref[...]), c64, sa64, sb64, half)
    for hh in range(DSA_HEADS):
        ka_ref[0, hh] = ka[:, hh * HD:(hh + 1) * HD].astype(ka_ref.dtype)
    kcvc_ref[0] = jnp.concatenate([_rot_lanes(row("kc", KVB), c64, sa64, sb64, half), row("vc", KVB)], axis=1)
    ks = _rot_lanes(row("ks", KVB), c64, sa64, sb64, half)
    kw = _rot_lanes(row("kw", KVB), c64, sa64, sb64, half)
    for g in range(NSA_KV_GROUPS):
        ks_ref[0, g] = ks[:, g * HD:(g + 1) * HD].astype(ks_ref.dtype)
        kw_ref[0, g] = kw[:, g * HD:(g + 1) * HD].astype(kw_ref.dtype)
    ik = _rot_lanes(row("ik", LANES), c32, sa32, sb32, IDX_DIM // ROT_FRACTION // 2)
    ik_ref[0] = ik[:, :IDX_DIM].astype(ik_ref.dtype)
    xt = x_ref[...].T
    ht = xt * lax.rsqrt(jnp.mean(xt * xt, axis=0, keepdims=True) + NORM_EPS) * gc_ref[...]
    ht = ht * (1.0 + scc_ref[0]) + shc_ref[0]
    pc = _dot(wc_ref[...], ht)
    col = lambda name, n: pc[COL_OFF[name]:COL_OFF[name] + n]
    ang_t = f64c_ref[...] * posr_ref[0]
    cos, sin = jnp.cos(ang_t), jnp.sin(ang_t)

    def rot_rows(blk):
        x1, x2 = blk[0:half], blk[half:2 * half]
        return jnp.concatenate([x1 * cos - x2 * sin, x2 * cos + x1 * sin, blk[2 * half:]], axis=0)

    qa, qb = col("qa", DSA_HEADS * HD), col("qb", NSA_HEADS * HD)
    for hh in range(DSA_HEADS):
        qa_ref[0, hh] = (rot_rows(qa[hh * HD:(hh + 1) * HD]) * scale).astype(qa_ref.dtype)
    for hh in range(NSA_HEADS):
        qb_ref[0, hh] = (rot_rows(qb[hh * HD:(hh + 1) * HD]) * scale).astype(qb_ref.dtype)
    iq = col("iq", IDX_HEADS * IDX_DIM)
    ang32 = f32c_ref[...] * posr_ref[0]
    first = lax.broadcasted_iota(jnp.int32, ang32.shape, 0) < SUBLANES // 2
    c32t, s32t = jnp.cos(ang32), jnp.where(first, -jnp.sin(ang32), jnp.sin(ang32))
    for hh in range(IDX_HEADS):
        blk = iq[hh * IDX_DIM:(hh + 1) * IDX_DIM]
        top = blk[0:SUBLANES]
        top = top * c32t + pltpu.roll(top, SUBLANES // 2, 0) * s32t
        iq_ref[0, hh] = jnp.concatenate([top, blk[SUBLANES:]], axis=0).astype(iq_ref.dtype)
    ones = jnp.ones((ONES_ROWS, xt.shape[1]), F32)
    with_ones = lambda v: jnp.concatenate([v, ones], axis=0)
    vs, vw = col("vs", KVB), col("vw", KVB)
    for g in range(NSA_KV_GROUPS):
        vs_ref[0, g] = with_ones(vs[g * HD:(g + 1) * HD]).astype(vs_ref.dtype)
        vw_ref[0, g] = with_ones(vw[g * HD:(g + 1) * HD]).astype(vw_ref.dtype)
    lat = col("lat", DSA_KV_RANK)
    lat = lat * lax.rsqrt(jnp.mean(lat * lat, axis=0, keepdims=True) + NORM_EPS) * gkvc_ref[...]
    va = _dot(wkv_ref[...], lat)
    for hh in range(DSA_HEADS):
        va_ref[0, hh] = with_ones(va[hh * HD:(hh + 1) * HD]).astype(va_ref.dtype)
    iw_ref[0] = col("iw", IDX_HEADS)
    br_ref[0] = jax.nn.sigmoid(col("br", 3 * NSA_HEADS))


def _prep(x, mod_sc, mod_sh, g_norm1, w_in, g_kv, w_kv_up, positions, tm):
    B, S, D = x.shape
    T = B * S
    HD, HA, G = HEAD_DIM, DSA_HEADS, NSA_KV_GROUPS
    per_b = S // tm
    sizes = [HA * HD, DSA_KV_RANK, IDX_HEADS * IDX_DIM, IDX_HEADS, IDX_DIM, NSA_HEADS * HD,
             KVB, KVB, KVB, KVB, KVB, KVB, 3 * NSA_HEADS, 2 * D]
    names = ["qa", "lat", "iq", "iw", "ik", "qb", "kc", "vc", "ks", "vs", "kw", "vw", "br", "mg"]
    starts = dict(zip(names, np.concatenate([[0], np.cumsum(sizes)[:-1]]).astype(int)))
    width = dict(zip(names, sizes))
    def seg(name, n):
        w = w_in[:, starts[name]:starts[name] + width[name]]
        return jnp.pad(w, ((0, 0), (0, n - width[name])))
    w_row = jnp.concatenate([seg(n, k) for n, k in ROW_SEGS], axis=1).astype(MXU_DTYPE)
    w_col = jnp.concatenate([seg(n, k) for n, k in COL_SEGS], axis=1).T.astype(MXU_DTYPE)
    w_kk = w_kv_up[:, :HA * HD].astype(MXU_DTYPE)
    w_kv = w_kv_up[:, HA * HD:].T.astype(MXU_DTYPE)
    wide = lambda v: jnp.broadcast_to(v[..., None], v.shape + (tm,))
    column = lambda v: v[..., None]

    def freqs(dim):
        half = dim // ROT_FRACTION // 2
        inv_freq = ROPE_THETA ** (-jnp.arange(half, dtype=F32) / half)
        per_dim = jnp.concatenate([inv_freq, inv_freq, jnp.zeros((dim - 2 * half,), F32)])
        return jnp.tile(per_dim, LANES // dim).reshape(1, LANES), jnp.concatenate([inv_freq, inv_freq])
    f64l, f64pair = freqs(HD)
    f32l, f32pair = freqs(IDX_DIM)
    half64 = HD // ROT_FRACTION // 2
    f64c = wide(f64pair[:half64])
    f32c = wide(f32pair)
    pos_f = positions.astype(F32)

    row_blk = lambda n: pl.BlockSpec((tm, n), lambda i: (i, 0))
    const = lambda shape: pl.BlockSpec(shape, lambda i: (0,) * len(shape))
    per_batch = lambda shape: pl.BlockSpec((1,) + shape, lambda i: (i // per_b,) + (0,) * len(shape))
    tok_cols = lambda rows: pl.BlockSpec((1, rows, tm), lambda i: (i // per_b, 0, i % per_b))
    heads_cols = lambda h, rows: pl.BlockSpec((1, h, rows, tm), lambda i: (i // per_b, 0, 0, i % per_b))
    heads_rows = lambda h, n: pl.BlockSpec((1, h, tm, n), lambda i: (i // per_b, 0, i % per_b, 0))
    bf = MXU_DTYPE
    out_shape = [jax.ShapeDtypeStruct((T, 2 * D), F32),
                 jax.ShapeDtypeStruct((B, HA, HD, S), bf),
                 jax.ShapeDtypeStruct((B, HA, S, HD), bf),
                 jax.ShapeDtypeStruct((B, HA, HD + ONES_ROWS, S), bf),
                 jax.ShapeDtypeStruct((B, IDX_HEADS, IDX_DIM, S), bf),
                 jax.ShapeDtypeStruct((B, IDX_HEADS, S), F32),
                 jax.ShapeDtypeStruct((B, S, IDX_DIM), bf),
                 jax.ShapeDtypeStruct((B, NSA_HEADS, HD, S), bf),
                 jax.ShapeDtypeStruct((B, S, 2 * KVB), F32),
                 jax.ShapeDtypeStruct((B, G, S, HD), bf),
                 jax.ShapeDtypeStruct((B, G, S, HD), bf),
                 jax.ShapeDtypeStruct((B, G, HD + ONES_ROWS, S), bf),
                 jax.ShapeDtypeStruct((B, G, HD + ONES_ROWS, S), bf),
                 jax.ShapeDtypeStruct((B, 3 * NSA_HEADS, S), F32)]
    out_specs = [row_blk(2 * D), heads_cols(HA, HD), heads_rows(HA, HD), heads_cols(HA, HD + ONES_ROWS),
                 heads_cols(IDX_HEADS, IDX_DIM), tok_cols(IDX_HEADS),
                 pl.BlockSpec((1, tm, IDX_DIM), lambda i: (i // per_b, i % per_b, 0)),
                 heads_cols(NSA_HEADS, HD),
                 pl.BlockSpec((1, tm, 2 * KVB), lambda i: (i // per_b, i % per_b, 0)),
                 heads_rows(G, HD), heads_rows(G, HD), heads_cols(G, HD + ONES_ROWS),
                 heads_cols(G, HD + ONES_ROWS),
                 tok_cols(3 * NSA_HEADS)]
    in_specs = [row_blk(D), const((1, D)), per_batch((1, D)), per_batch((1, D)),
                const((D, 1)), per_batch((D, 1)), per_batch((D, 1)),
                const((D, ROW_COLS)), const((COL_ROWS, D)),
                const((1, DSA_KV_RANK)), const((DSA_KV_RANK, 1)),
                const((DSA_KV_RANK, HA * HD)), const((HA * HD, DSA_KV_RANK)),
                row_blk(1), tok_cols(1), const((1, LANES)), const((1, LANES)),
                const((half64, tm)), const((SUBLANES, tm))]
    return pl.pallas_call(
        _prep_kernel,
        grid=(T // tm,),
        in_specs=in_specs,
        out_specs=out_specs,
        out_shape=out_shape,
        compiler_params=_cparams(("arbitrary",)),
    )(x.reshape(T, D), g_norm1.reshape(1, D), mod_sc, mod_sh,
      column(g_norm1), column(mod_sc[:, 0]), column(mod_sh[:, 0]), w_row, w_col,
      g_kv.reshape(1, DSA_KV_RANK), column(g_kv), w_kk, w_kv,
      pos_f.reshape(T, 1), pos_f.reshape(B, 1, S), f64l, f32l, f64c, f32c)


def _cmp_kernel(f_ref, w1_ref, w2_ref, o_ref):
    hid = _dot(f_ref[...], w1_ref[...])
    hid = hid * jax.nn.sigmoid(hid)
    o_ref[...] = _dot(hid, w2_ref[...])


def _compress(tok, pe, w1, w2):
    B, S, G, HD = tok.shape
    r = CMP_LEN // CMP_STRIDE
    n_chunks = S // CMP_STRIDE
    n_cmp = n_chunks - r + 1
    chunks = tok.reshape(B, n_chunks, CMP_STRIDE, G, HD)
    blocks = jnp.concatenate([chunks[:, i:n_cmp + i] for i in range(r)], axis=2)
    blocks = blocks + pe[:, None, :]
    flat = blocks.transpose(0, 1, 3, 2, 4).reshape(B * n_cmp * G, CMP_LEN * HD)
    rows = flat.shape[0]
    tm = 512
    rows_p = -(-rows // tm) * tm
    flat = jnp.pad(flat, ((0, rows_p - rows), (0, 0))).astype(MXU_DTYPE)
    out = pl.pallas_call(
        _cmp_kernel,
        grid=(rows_p // tm,),
        in_specs=[pl.BlockSpec((tm, CMP_LEN * HD), lambda i: (i, 0)),
                  pl.BlockSpec((CMP_LEN * HD, CMP_HIDDEN), lambda i: (0, 0)),
                  pl.BlockSpec((CMP_HIDDEN, HD), lambda i: (0, 0))],
        out_specs=pl.BlockSpec((tm, HD), lambda i: (i, 0)),
        out_shape=jax.ShapeDtypeStruct((rows_p, HD), F32),
        compiler_params=_cparams(("arbitrary",)),
    )(flat, w1.astype(MXU_DTYPE), w2.astype(MXU_DTYPE))
    return out[:rows].reshape(B, n_cmp, G, HD)


def _select_kernel(ik_ref, iqt_ref, iwt_ref, bias_ref, sc_ref, *, TQ, KC, S, K, NBIS):
    qi = pl.program_id(1)
    q0 = qi * TQ
    n_ch = (q0 + TQ + KC - 1) // KC
    n_all = S // KC
    t_row = q0 + lax.broadcasted_iota(jnp.int32, (1, TQ), 1)
    key_iota = lax.broadcasted_iota(jnp.int32, (KC, TQ), 0)
    kf = float(K)
    SUB = LANES
    sub_iota = lax.broadcasted_iota(jnp.int32, (SUB, TQ), 0)

    def score_chunk(c, carry):
        mn, mx = carry
        for u in range(KC // SUB):
            off = pl.multiple_of(c * KC + u * SUB, SUB)
            ikc = ik_ref[0, pl.ds(off, SUB), :]
            acc = jnp.zeros((SUB, TQ), F32)
            for h in range(IDX_HEADS):
                lg = jnp.dot(ikc, iqt_ref[0, h], preferred_element_type=F32)
                acc = acc + jnp.maximum(lg, 0.0) * iwt_ref[0, h:h + 1, :]
            causal = off + sub_iota <= t_row
            sc_ref[pl.ds(off, SUB), :] = jnp.where(causal, acc, NEG_INF)
            rows = lambda x: x.reshape(SUB // SUBLANES, SUBLANES, TQ)
            mn = jnp.minimum(mn, jnp.min(rows(jnp.where(causal, acc, -LOWEST)), axis=0))
            mx = jnp.maximum(mx, jnp.max(rows(jnp.where(causal, acc, LOWEST)), axis=0))
        return mn, mx

    mn, mx = lax.fori_loop(0, n_ch, score_chunk, (jnp.full((SUBLANES, TQ), -LOWEST, F32),
                                                  jnp.full((SUBLANES, TQ), LOWEST, F32)))
    lo = jnp.min(mn, axis=0, keepdims=True)
    hi = jnp.max(mx, axis=0, keepdims=True)

    def chunk(c):
        off = pl.multiple_of(c * KC, KC)
        return sc_ref[pl.ds(off, KC), :], off

    AR = 2 * SUBLANES

    def fold(x, op):
        return op(x.reshape(KC // AR, AR, TQ), axis=0)

    def key_pass(fn, init):
        def body(c, acc):
            xs, off = chunk(c)
            return fn(acc, xs, off)
        return lax.fori_loop(0, n_ch, body, init)

    def count_ge(thr):
        acc = key_pass(lambda a, xs, off: a + fold(jnp.where(xs >= thr, 1.0, 0.0), jnp.sum),
                       jnp.zeros((AR, TQ), F32))
        return jnp.sum(acc, axis=0, keepdims=True)

    def count_gt(thr):
        acc = key_pass(lambda a, xs, off: a + fold(jnp.where(xs > thr, 1.0, 0.0), jnp.sum),
                       jnp.zeros((AR, TQ), F32))
        return jnp.sum(acc, axis=0, keepdims=True)

    def max_where(bound):
        fn = lambda a, xs, off: jnp.maximum(a, fold(jnp.where(xs <= bound, xs, LOWEST), jnp.max))
        acc = key_pass(fn, jnp.full((AR, TQ), LOWEST, F32))
        return jnp.max(acc, axis=0, keepdims=True)

    def count_and_next(v):
        def fn(a, xs, off):
            cnt, nxt = a
            return (cnt + fold(jnp.where(xs >= v, 1.0, 0.0), jnp.sum),
                    jnp.maximum(nxt, fold(jnp.where(xs < v, xs, LOWEST), jnp.max)))
        cnt, nxt = key_pass(fn, (jnp.zeros((AR, TQ), F32), jnp.full((AR, TQ), LOWEST, F32)))
        return jnp.sum(cnt, axis=0, keepdims=True), jnp.max(nxt, axis=0, keepdims=True)

    def bisect(_, carry):
        lo, hi = carry
        mid = 0.5 * (lo + hi)
        ge = count_ge(mid) >= kf
        return jnp.where(ge, mid, lo), jnp.where(ge, hi, mid)

    lo, hi = lax.fori_loop(0, NBIS, bisect, (lo, hi))

    all_keys = jnp.where(t_row < K, 1.0, 0.0)
    v = max_where(hi)
    n_ge, nxt = count_and_next(v)
    done = jnp.maximum(all_keys, jnp.where(n_ge >= kf, 1.0, 0.0))

    def peel_cond(st):
        return jnp.sum(st[1]) < float(TQ)

    def peel_body(st):
        v, done, n_ge, nxt = st
        v = jnp.where(done > 0.5, v, nxt)
        cnt, nxt = count_and_next(v)
        n_ge = jnp.where(done > 0.5, n_ge, cnt)
        done = jnp.maximum(done, jnp.where(cnt >= kf, 1.0, 0.0))
        return v, done, n_ge, nxt

    v, done, n_ge, _ = lax.while_loop(peel_cond, peel_body, (v, done, n_ge, nxt))
    thr = jnp.where(all_keys > 0.5, LOWEST, v)
    has_tie = jnp.max(jnp.where((n_ge > kf) & (all_keys < 0.5), 1.0, 0.0)) > 0.5

    def write(off, sel):
        bias_ref[0, pl.ds(off, KC), :] = jnp.where(sel, 0.0, NEG_INF).astype(bias_ref.dtype)

    def fill_tail():
        def body(c, carry):
            write(pl.multiple_of(c * KC, KC), jnp.zeros((KC, TQ), jnp.bool_))
            return carry
        lax.fori_loop(n_ch, n_all, body, 0)

    @pl.when(jnp.logical_not(has_tie))
    def _():
        def body(c, carry):
            xs, off = chunk(c)
            write(off, (xs >= thr) & ((off + key_iota) <= t_row))
            return carry
        lax.fori_loop(0, n_ch, body, 0)
        fill_tail()

    @pl.when(has_tie)
    def _():
        need = kf - count_gt(thr)
        r_i = lax.broadcasted_iota(jnp.int32, (KC, KC), 0)
        c_i = lax.broadcasted_iota(jnp.int32, (KC, KC), 1)
        lower = jnp.where(c_i <= r_i, 1.0, 0.0).astype(MXU_DTYPE)
        def body(c, seen):
            xs, off = chunk(c)
            causal = (off + key_iota) <= t_row
            eq = jnp.where((xs == thr) & causal, 1.0, 0.0)
            rank = jnp.dot(lower, eq.astype(MXU_DTYPE), preferred_element_type=F32) + seen
            write(off, ((xs > thr) & causal) | ((eq > 0.5) & (rank <= need)))
            return seen + jnp.sum(eq, axis=0, keepdims=True)
        lax.fori_loop(0, n_ch, body, jnp.zeros((1, TQ), F32))
        fill_tail()


def _dsa_select(ik, iqt, iwt, K, TQ, KC):
    B, S, DI = ik.shape
    H = iqt.shape[1]
    kern = functools.partial(_select_kernel, TQ=TQ, KC=KC, S=S, K=K, NBIS=SELECT_BISECTIONS)
    return pl.pallas_call(
        kern,
        grid=(B, S // TQ),
        in_specs=[pl.BlockSpec((1, S, DI), lambda b, q: (b, 0, 0)),
                  pl.BlockSpec((1, H, DI, TQ), lambda b, q: (b, 0, 0, q)),
                  pl.BlockSpec((1, H, TQ), lambda b, q: (b, 0, q))],
        out_specs=pl.BlockSpec((1, S, TQ), lambda b, q: (b, 0, q)),
        out_shape=jax.ShapeDtypeStruct((B, S, S), jnp.bfloat16),
        scratch_shapes=[pltpu.VMEM((S, TQ), F32)],
        compiler_params=_cparams(("arbitrary", "arbitrary")),
    )(ik, iqt, iwt)


def _cmp_attn_kernel(qt_ref, kc_ref, vct_ref, o_ref, bm_ref, *, TQ, NCP, NBP, N_SEL):
    qi = pl.program_id(1)
    q0 = qi * TQ
    G, J = NSA_KV_GROUPS, NSA_REP
    N = J * TQ
    t_lane = q0 + (lax.broadcasted_iota(jnp.int32, (NCP, N), 1) & (TQ - 1))
    cmp_end = lax.broadcasted_iota(jnp.int32, (NCP, N), 0) * CMP_STRIDE + (CMP_LEN - 1)
    vis = cmp_end <= t_lane
    n_i = lax.broadcasted_iota(jnp.int32, (NBP, NCP), 0) * SEL_BLOCK
    c_i = lax.broadcasted_iota(jnp.int32, (NBP, NCP), 1) * CMP_STRIDE
    overlap_t = jnp.where((c_i <= n_i + SEL_BLOCK - 1) & (c_i + CMP_LEN - 1 >= n_i), 1.0, 0.0)
    blk = lax.broadcasted_iota(jnp.int32, (NBP, TQ), 0)
    tq = q0 + lax.broadcasted_iota(jnp.int32, (NBP, TQ), 1)
    cur = tq // SEL_BLOCK
    admissible = blk * SEL_BLOCK <= tq
    forced = (blk == 0) | (blk == cur) | (blk == cur - 1)
    for g in range(G):
        qt = jnp.concatenate([qt_ref[0, g * J + j] for j in range(J)], axis=1)
        s = jnp.where(vis, _dot(kc_ref[0, g], qt), NEG_INF)
        p = jnp.exp(s - jnp.max(s, axis=0, keepdims=True))
        p = jnp.where(vis, p, 0.0)
        p = p / jnp.maximum(jnp.sum(p, axis=0, keepdims=True), TINY)
        o = _dot(vct_ref[0, g], p)
        for j in range(J):
            o_ref[0, g * J + j] = o[:, j * TQ:(j + 1) * TQ]
        psum = p[:, 0:TQ]
        for j in range(1, J):
            psum = psum + p[:, j * TQ:(j + 1) * TQ]
        imp = _dot_exact_lhs(overlap_t, psum)
        score = jnp.where(admissible & forced, FORCED, jnp.where(admissible, imp, NEG_INF))

        def pick(_, st):
            score, sel = st
            m = jnp.max(score, axis=0, keepdims=True)
            first = jnp.min(jnp.where(score == m, blk, NBP), axis=0, keepdims=True)
            hit = blk == first
            return jnp.where(hit, LOWEST, score), jnp.where(hit, 1.0, sel)

        _, sel = lax.fori_loop(0, N_SEL, pick, (score, jnp.zeros((NBP, TQ), F32)))
        bm_ref[0, g] = sel.astype(bm_ref.dtype)


def _cmp_attn(qt, kc, vct, n_sel, NBP, TQ):
    B, H, HD, S = qt.shape
    G, NCP = kc.shape[1], kc.shape[2]
    kern = functools.partial(_cmp_attn_kernel, TQ=TQ, NCP=NCP, NBP=NBP, N_SEL=n_sel)
    return pl.pallas_call(
        kern,
        grid=(B, S // TQ),
        in_specs=[pl.BlockSpec((1, H, HD, TQ), lambda b, q: (b, 0, 0, q)),
                  pl.BlockSpec((1, G, NCP, HD), lambda b, q: (b, 0, 0, 0)),
                  pl.BlockSpec((1, G, HD, NCP), lambda b, q: (b, 0, 0, 0))],
        out_specs=[pl.BlockSpec((1, H, HD, TQ), lambda b, q: (b, 0, 0, q)),
                   pl.BlockSpec((1, G, NBP, TQ), lambda b, q: (b, 0, 0, q))],
        out_shape=[jax.ShapeDtypeStruct((B, H, HD, S), F32),
                   jax.ShapeDtypeStruct((B, G, NBP, S), jnp.bfloat16)],
        compiler_params=_cparams(("arbitrary", "arbitrary")),
    )(qt, kc, vct)


def _flash_kernel(*refs, mode, G, J, TQ, TK, NWIN):
    if mode == "window":
        qt_ref, k_ref, vt_ref, o_ref, m_ref, acc_ref = refs
        x_ref = None
    else:
        qt_ref, k_ref, vt_ref, x_ref, o_ref, m_ref, acc_ref = refs
    HD = HEAD_DIM
    qi = pl.program_id(1)
    kk = pl.program_id(2)
    q0 = qi * TQ
    last = (q0 + TQ - 1) // TK
    if mode == "window":
        ki = last - (NWIN - 1) + kk
        valid = ki >= 0
    else:
        ki = kk
        valid = kk <= last

    @pl.when(kk == 0)
    def _():
        m_ref[...] = jnp.full(m_ref.shape, NEG_INF, F32)
        acc_ref[...] = jnp.zeros(acc_ref.shape, F32)

    @pl.when(valid)
    def _():
        k0 = ki * TK
        kidx = k0 + lax.broadcasted_iota(jnp.int32, (TK, TQ), 0)
        tq = q0 + lax.broadcasted_iota(jnp.int32, (TK, TQ), 1)
        if mode == "mask":
            bias = x_ref[0].astype(F32)
        elif mode == "window":
            bias = jnp.where((kidx <= tq) & (kidx > tq - WINDOW), 0.0, NEG_INF)
        else:
            causal = kidx <= tq
            nbp = x_ref.shape[2]
            blk_of_key = (k0 + lax.broadcasted_iota(jnp.int32, (TK, nbp), 0)) // SEL_BLOCK
            expand_t = jnp.where(lax.broadcasted_iota(jnp.int32, (TK, nbp), 1) == blk_of_key, 1.0, 0.0)
            expand_t = expand_t.astype(MXU_DTYPE)

        def scores(g):
            if mode == "block":
                picked = jnp.dot(expand_t, x_ref[0, g].astype(MXU_DTYPE), preferred_element_type=F32)
                b = jnp.where(causal & (picked > 0.5), 0.0, NEG_INF)
            else:
                b = bias
            if J > 1:
                b = jnp.concatenate([b] * J, axis=1)
                qt = jnp.concatenate([qt_ref[0, g * J + j] for j in range(J)], axis=1)
            else:
                qt = qt_ref[0, g]
            return _dot(k_ref[0, g], qt) + b

        for g in range(G):
            s = scores(g)
            m_prev = m_ref[g]
            m_new = jnp.maximum(m_prev, jnp.max(s, axis=0, keepdims=True))
            alpha = jnp.exp(m_prev - m_new)
            p = jnp.exp((s - m_new).astype(MXU_DTYPE))
            acc_ref[g] = alpha * acc_ref[g] + _dot(vt_ref[0, g], p)
            m_ref[g] = m_new

    @pl.when(kk == pl.num_programs(2) - 1)
    def _():
        for g in range(G):
            o = acc_ref[g, 0:HD] / jnp.maximum(acc_ref[g, HD:HD + 1], TINY)
            for j in range(J):
                o_ref[0, g * J + j] = o[:, j * TQ:(j + 1) * TQ].astype(o_ref.dtype)


def _flash(qt, k, vt, extra, mode, TQ, TK, out_dtype):
    B, H, HD, S = qt.shape
    G = k.shape[1]
    J = H // G
    N = J * TQ
    nq = S // TQ
    last_of = lambda qi: (qi * TQ + TQ - 1) // TK
    if mode == "window":
        NWIN = min((WINDOW - 1 + TK - 1) // TK + 1, S // TK)
        nk = NWIN
        kidx_of = lambda qi, kk: jnp.maximum(last_of(qi) - (NWIN - 1) + kk, 0)
    else:
        NWIN = 0
        nk = S // TK
        kidx_of = lambda qi, kk: jnp.minimum(kk, last_of(qi))
    in_specs = [pl.BlockSpec((1, H, HD, TQ), lambda b, qi, kk: (b, 0, 0, qi)),
                pl.BlockSpec((1, G, TK, HD), lambda b, qi, kk: (b, 0, kidx_of(qi, kk), 0)),
                pl.BlockSpec((1, G, HD + ONES_ROWS, TK), lambda b, qi, kk: (b, 0, 0, kidx_of(qi, kk)))]
    args = [qt, k, vt]
    if mode == "mask":
        in_specs.append(pl.BlockSpec((1, TK, TQ), lambda b, qi, kk: (b, kidx_of(qi, kk), qi)))
        args.append(extra)
    elif mode == "block":
        nbp = extra.shape[2]
        in_specs.append(pl.BlockSpec((1, G, nbp, TQ), lambda b, qi, kk: (b, 0, 0, qi)))
        args.append(extra)
    kern = functools.partial(_flash_kernel, mode=mode, G=G, J=J, TQ=TQ, TK=TK, NWIN=NWIN)
    return pl.pallas_call(
        kern,
        grid=(B, nq, nk),
        in_specs=in_specs,
        out_specs=pl.BlockSpec((1, H, HD, TQ), lambda b, qi, kk: (b, 0, 0, qi)),
        out_shape=jax.ShapeDtypeStruct((B, H, HD, S), out_dtype),
        scratch_shapes=[pltpu.VMEM((G, 1, N), F32),
                        pltpu.VMEM((G, HD + ONES_ROWS, N), F32)],
        compiler_params=_cparams(("arbitrary", "arbitrary", "arbitrary")),
    )(*args)


def _dot_tn(a_t, b):
    return lax.dot_general(a_t.astype(MXU_DTYPE), b.astype(MXU_DTYPE),
                           (((0,), (0,)), ((), ())), preferred_element_type=F32)


def _route(h, wr, br):
    tm = h.shape[0]
    lane = lax.broadcasted_iota(jnp.int32, (tm, LANES), 1)
    h_hi = h.astype(MXU_DTYPE)
    h_lo = (h - h_hi.astype(F32)).astype(MXU_DTYPE)
    w_hi = wr.astype(MXU_DTYPE)
    w_lo = (wr - w_hi.astype(F32)).astype(MXU_DTYPE)
    d = lambda a, b: jnp.dot(a, b, preferred_element_type=F32)
    logits = d(h_hi, w_hi) + d(h_hi, w_lo) + d(h_lo, w_hi) + br
    is_e = lane < N_EXPERTS
    is_g = (lane >= N_EXPERTS) & (lane < N_EXPERTS + N_GROUPS)
    lg = jnp.where(is_g, logits, LOWEST)
    mg = jnp.max(lg, axis=-1, keepdims=True)
    gsel = jnp.min(jnp.where(is_g & (lg == mg), lane - N_EXPERTS, N_GROUPS), axis=-1, keepdims=True)
    pg_sel = 1.0 / jnp.sum(jnp.where(is_g, jnp.exp(lg - mg), 0.0), axis=-1, keepdims=True)
    in_grp = is_e & ((lane // EXPERTS_PER_GROUP) == gsel)
    le = jnp.where(in_grp, logits, LOWEST)
    me = jnp.max(le, axis=-1, keepdims=True)
    ex = jnp.where(in_grp, jnp.exp(le - me), 0.0)
    pe = ex / jnp.sum(ex, axis=-1, keepdims=True)
    pe = jnp.where(in_grp, pe, -1.0)
    p1 = jnp.max(pe, axis=-1, keepdims=True)
    i1 = jnp.min(jnp.where(pe == p1, lane, LANES), axis=-1, keepdims=True)
    pe2 = jnp.where(lane == i1, -1.0, pe)
    p2 = jnp.max(pe2, axis=-1, keepdims=True)
    i2 = jnp.min(jnp.where(pe2 == p2, lane, LANES), axis=-1, keepdims=True)
    tot = p1 + p2
    comb = jnp.where(lane == i1, p1 / tot * pg_sel, 0.0) + jnp.where(lane == i2, p2 / tot * pg_sel, 0.0)
    return jnp.where(lane == N_EXPERTS, gsel.astype(F32), comb)


def _outproj_kernel(x_ref, oa_ref, oc_ref, os_ref, ow_ref, br_ref, mg_ref, wa_ref, wb_ref, wo_ref,
                    gt_ref, g2_ref, sc_ref, sh_ref, wr_ref, brt_ref, o_ref, hr_ref):
    HD = HEAD_DIM
    D = x_ref.shape[1]
    br = br_ref[0]
    parts = []
    for hh in range(NSA_HEADS):
        parts.append(br[3 * hh:3 * hh + 1] * oc_ref[0, hh] + br[3 * hh + 1:3 * hh + 2] * os_ref[0, hh]
                     + br[3 * hh + 2:3 * hh + 3] * ow_ref[0, hh])
    ob_t = jnp.concatenate(parts, axis=0)
    oa_t = jnp.concatenate([oa_ref[0, hh] for hh in range(DSA_HEADS)], axis=0)
    ua = _dot_tn(oa_t, wa_ref[...])
    ub = _dot_tn(ob_t, wb_ref[...])
    mg = mg_ref[...]
    merged = jax.nn.sigmoid(mg[:, :D]) * ua + jax.nn.sigmoid(mg[:, D:]) * ub
    x1 = x_ref[...] + gt_ref[0] * _dot(merged, wo_ref[...])
    o_ref[...] = x1
    h = _rms(x1, g2_ref[...]) * (1.0 + sc_ref[0]) + sh_ref[0]
    hr_ref[:, :D] = h
    hr_ref[:, D:] = _route(h, wr_ref[...], brt_ref[...])


def _out_proj(x2d, oa, oc, os_, ow, br, mg, wa, wb, wo, gt, g2, sc2, sh2, wr, brt, S, tm):
    T, D = x2d.shape
    B, H, HD, _ = oa.shape
    per_b = S // tm
    heads_cols = pl.BlockSpec((1, H, HD, tm), lambda i: (i // per_b, 0, 0, i % per_b))
    per_batch = pl.BlockSpec((1, 1, D), lambda i: (i // per_b, 0, 0))
    rows = lambda n: pl.BlockSpec((tm, n), lambda i: (i, 0))
    const = lambda a, b: pl.BlockSpec((a, b), lambda i: (0, 0))
    return pl.pallas_call(
        _outproj_kernel,
        grid=(T // tm,),
        in_specs=[rows(D), heads_cols, heads_cols, heads_cols, heads_cols,
                  pl.BlockSpec((1, br.shape[1], tm), lambda i: (i // per_b, 0, i % per_b)),
                  rows(2 * D), const(H * HD, D), const(H * HD, D), const(D, D), per_batch,
                  const(1, D), per_batch, per_batch, const(D, LANES), const(1, LANES)],
        out_specs=[rows(D), rows(D + LANES)],
        out_shape=[jax.ShapeDtypeStruct((T, D), F32), jax.ShapeDtypeStruct((T, D + LANES), F32)],
        compiler_params=_cparams(("arbitrary",)),
    )(x2d, oa, oc, os_, ow, br, mg, wa, wb, wo, gt, g2.reshape(1, D), sc2, sh2, wr, brt)


def _row_copy(src_hbm, t, dst, r, sem):
    return pltpu.make_async_copy(src_hbm.at[pl.ds(t, 1)], dst.at[pl.ds(r, 1)], sem)


def _start_rows(idx_ref, base, r0, n, src_hbm, dst, sem):
    for r in range(n):
        _row_copy(src_hbm, idx_ref[base + r0 + r], dst, r0 + r, sem).start(priority=r % 2)


def _wait_rows(src_hbm, dst, n, sem):
    pltpu.make_async_copy(src_hbm.at[pl.ds(0, n)], dst.at[pl.ds(0, n)], sem).wait()


def _experts_kernel(tg_ref, src_ref, h_hbm, wg_ref, wu_ref, wd_ref, o_ref,
                    hbuf, acc_ref, sems, *, TM):
    i = pl.program_id(0)
    e = pl.program_id(1)
    n_tiles = pl.num_programs(0)
    EPG, EPS = EXPERTS_PER_GROUP, EXPERTS_PER_STEP
    n_steps = EPG // EPS
    part = TM // n_steps
    slot = i % 2

    @pl.when((i == 0) & (e == 0))
    def _():
        def start(c, carry):
            _start_rows(src_ref, 0, pl.multiple_of(c * part, part), part, h_hbm, hbuf.at[0], sems.at[0])
            return carry
        lax.fori_loop(0, n_steps, start, 0)

    @pl.when(e == 0)
    def _():
        _wait_rows(h_hbm, hbuf.at[slot], TM, sems.at[slot])
        acc_ref[...] = jnp.zeros(acc_ref.shape, F32)

    @pl.when(i + 1 < n_tiles)
    def _():
        _start_rows(src_ref, (i + 1) * TM, e * part, part, h_hbm, hbuf.at[1 - slot], sems.at[1 - slot])

    D = acc_ref.shape[1]
    h = hbuf[slot, :, :D].astype(MXU_DTYPE)
    lane = lax.broadcasted_iota(jnp.int32, (TM, LANES), 1)
    route = hbuf[slot, :, D:]
    out = acc_ref[...]
    for q in range(EPS):
        a = jnp.dot(h, wg_ref[q], preferred_element_type=F32)
        u = jnp.dot(h, wu_ref[q], preferred_element_type=F32)
        y = _dot(a * jax.nn.sigmoid(a) * u, wd_ref[q])
        expert = tg_ref[i] * EPG + e * EPS + q
        out = out + jnp.sum(jnp.where(lane == expert, route, 0.0), axis=-1, keepdims=True) * y
    acc_ref[...] = out

    @pl.when(e == pl.num_programs(1) - 1)
    def _():
        o_ref[...] = acc_ref[...]


def _experts(hr, tile_group, src, wg, wu, wd, TM):
    D = wg.shape[1]
    DE = wg.shape[2]
    P = src.shape[0]
    EPG = EXPERTS_PER_GROUP
    EPS = EXPERTS_PER_STEP
    w_idx = lambda i, e, tg, src: (tg[i] * (EPG // EPS) + e, 0, 0)
    kern = functools.partial(_experts_kernel, TM=TM)
    grid_spec = pltpu.PrefetchScalarGridSpec(
        num_scalar_prefetch=2,
        grid=(P // TM, EPG // EPS),
        in_specs=[pl.BlockSpec(memory_space=pl.ANY),
                  pl.BlockSpec((EPS, D, DE), w_idx),
                  pl.BlockSpec((EPS, D, DE), w_idx),
                  pl.BlockSpec((EPS, DE, D), w_idx)],
        out_specs=pl.BlockSpec((TM, D), lambda i, e, tg, src: (i, 0)),
        scratch_shapes=[pltpu.VMEM((2, TM, D + LANES), F32),
                        pltpu.VMEM((TM, D), F32),
                        pltpu.SemaphoreType.DMA((2,))])
    return pl.pallas_call(
        kern,
        grid_spec=grid_spec,
        out_shape=jax.ShapeDtypeStruct((P, D), F32),
        compiler_params=_cparams(("arbitrary", "arbitrary")),
    )(tile_group, src, hr, wg, wu, wd)


def _combine_kernel(pos_ref, x_ref, y_hbm, gt_ref, gf_ref, o_ref, ybuf, sems, *, TM):
    i = pl.program_id(0)
    slot = i % 2
    part = 64

    def start_tile(tile, half):
        def start(c, carry):
            _start_rows(pos_ref, tile * TM, pl.multiple_of(c * part, part), part, y_hbm, ybuf.at[half],
                        sems.at[half])
            return carry
        lax.fori_loop(0, TM // part, start, 0)

    @pl.when(i == 0)
    def _():
        start_tile(0, 0)

    @pl.when(i + 1 < pl.num_programs(0))
    def _():
        start_tile(i + 1, 1 - slot)

    _wait_rows(y_hbm, ybuf.at[slot], TM, sems.at[slot])
    o_ref[...] = _rms(x_ref[...] + gt_ref[0] * ybuf[slot], gf_ref[...])


def _combine(x1, ys, pos, gt, gf, S, tm):
    T, D = x1.shape
    per_b = S // tm
    grid_spec = pltpu.PrefetchScalarGridSpec(
        num_scalar_prefetch=1,
        grid=(T // tm,),
        in_specs=[pl.BlockSpec((tm, D), lambda i, pos: (i, 0)),
                  pl.BlockSpec(memory_space=pl.ANY),
                  pl.BlockSpec((1, 1, D), lambda i, pos: (i // per_b, 0, 0)),
                  pl.BlockSpec((1, D), lambda i, pos: (0, 0))],
        out_specs=pl.BlockSpec((tm, D), lambda i, pos: (i, 0)),
        scratch_shapes=[pltpu.VMEM((2, tm, D), F32), pltpu.SemaphoreType.DMA((2,))])
    return pl.pallas_call(
        functools.partial(_combine_kernel, TM=tm),
        grid_spec=grid_spec,
        out_shape=jax.ShapeDtypeStruct((T, D), F32),
        compiler_params=_cparams(("arbitrary",)),
    )(pos, x1, ys, gt, gf.reshape(1, D))


def _routed_moe(x1, hr, gt, gf, wg, wu, wd, S, TM, combine_rows):
    T, D = x1.shape
    gsel = hr[:, D + N_EXPERTS].astype(jnp.int32)
    onehot = (gsel[:, None] == jnp.arange(N_GROUPS, dtype=jnp.int32)[None, :]).astype(jnp.int32)
    csum = jnp.cumsum(onehot, axis=0)
    rank = jnp.take_along_axis(csum, gsel[:, None], axis=1)[:, 0] - 1
    padded = -(-csum[-1] // TM) * TM
    ends = jnp.cumsum(padded)
    starts = ends - padded
    pos = (jnp.take(starts, gsel) + rank).astype(jnp.int32)
    P = T + N_GROUPS * TM
    src = jnp.zeros((P,), jnp.int32).at[pos].set(jnp.arange(T, dtype=jnp.int32))
    tile_start = jnp.arange(P // TM, dtype=jnp.int32) * TM
    tile_group = jnp.minimum(jnp.sum(tile_start[:, None] >= ends[None, :], axis=1), N_GROUPS - 1)
    ys = _experts(hr, tile_group.astype(jnp.int32), src, wg, wu, wd, TM)
    return _combine(x1, ys, pos, gt, gf, S, combine_rows)


def _layer(x, mod, positions, g_norm1, w_in, g_kv_latent, w_kv_up, pe_cmp_k, pe_cmp_v,
           w_cmp1_k, w_cmp2_k, w_cmp1_v, w_cmp2_v, w_up_a, w_up_b, w_out, g_norm2,
           w_router_group, b_router_group, w_router_expert, b_router_expert,
           w_expert_gate, w_expert_up, w_expert_down, g_out):
    B, S, D = x.shape
    T = B * S
    HD, G = HEAD_DIM, NSA_KV_GROUPS
    kvb = G * HD
    topk_a = min(DSA_TOPK_MAX, S // 4)
    n_sel = min(SEL_COUNT, S // SEL_BLOCK)
    mod6 = mod.reshape(B, 6, 1, D)
    sh1, sc1, gt1, sh2, sc2, gt2 = (mod6[:, i] for i in range(6))

    x2d = x.reshape(T, D)
    t = _tiles(S)
    (mg, qt_a, k_a, vt_a, iqt, iwt, ik, qt_b, kcvc, ks, kw, vst, vwt, br) = _prep(
        x, sc1, sh1, g_norm1, w_in, g_kv_latent, w_kv_up, positions, t.rows)

    kc = _compress(kcvc[..., :kvb].reshape(B, S, G, HD), pe_cmp_k, w_cmp1_k, w_cmp2_k)
    vc = _compress(kcvc[..., kvb:].reshape(B, S, G, HD), pe_cmp_v, w_cmp1_v, w_cmp2_v)
    n_cmp = kc.shape[1]
    ncp = -(-(n_cmp + 1) // LANES) * LANES
    pad_c = lambda t: jnp.pad(t, ((0, 0), (0, ncp - n_cmp), (0, 0), (0, 0))).astype(MXU_DTYPE)
    kc, vct = pad_c(kc).transpose(0, 2, 1, 3), pad_c(vc).transpose(0, 2, 3, 1)

    sel_bias = _dsa_select(ik, iqt, iwt, topk_a, t.select_q, t.select_k)
    o_a = _flash(qt_a, k_a, vt_a, sel_bias, "mask", t.mask_q, t.mask_k, MXU_DTYPE)

    nbp = -(-(S // SEL_BLOCK) // LANES) * LANES
    o_c, blk_mask = _cmp_attn(qt_b, kc, vct, n_sel, nbp, t.nsa_q)
    o_s = _flash(qt_b, ks, vst, blk_mask, "block", t.nsa_q, t.block_k, F32)
    o_w = _flash(qt_b, kw, vwt, None, "window", t.nsa_q, t.nsa_q, F32)

    wr = jnp.concatenate([w_router_expert, w_router_group], axis=1)
    wr = jnp.pad(wr, ((0, 0), (0, LANES - wr.shape[1])))
    brt = jnp.concatenate([b_router_expert, b_router_group])
    brt = jnp.pad(brt, (0, LANES - brt.shape[0])).reshape(1, LANES)
    x1, hr = _out_proj(x2d, o_a, o_c, o_s, o_w, br, mg, w_up_a.astype(MXU_DTYPE),
                              w_up_b.astype(MXU_DTYPE), w_out.astype(MXU_DTYPE), gt1,
                              g_norm2, sc2, sh2, wr, brt, S, t.rows)

    out = _routed_moe(x1, hr, gt2, g_out, w_expert_gate.astype(MXU_DTYPE),
                      w_expert_up.astype(MXU_DTYPE), w_expert_down.astype(MXU_DTYPE), S,
                      t.moe_rows, t.rows)
    return out.reshape(B, S, D)


def kernel(x, c, positions, w_ada, b_ada, g_norm1, w_in, g_kv_latent, w_kv_up, pe_cmp_k, pe_cmp_v,
           w_cmp1_k, w_cmp2_k, w_cmp1_v, w_cmp2_v, w_up_a, w_up_b, w_out, g_norm2, w_router_group,
           b_router_group, w_router_expert, b_router_expert, w_expert_gate, w_expert_up,
           w_expert_down, g_final):
    depth = w_ada.shape[0]
    assert depth == 1, "the fused final norm assumes a single layer"
    mod = _ada_mod(c, w_ada[0], b_ada[0])
    return _layer(x, mod, positions, g_norm1[0], w_in[0], g_kv_latent[0], w_kv_up[0], pe_cmp_k[0],
                  pe_cmp_v[0], w_cmp1_k[0], w_cmp2_k[0], w_cmp1_v[0], w_cmp2_v[0], w_up_a[0],
                  w_up_b[0], w_out[0], g_norm2[0], w_router_group[0], b_router_group[0],
                  w_router_expert[0], b_router_expert[0], w_expert_gate[0], w_expert_up[0],
                  w_expert_down[0], g_final)
```
